```python
import math
import jax, jax.numpy as jnp
from jax import lax
import numpy as np

D_MODEL = 1024
BATCH = 2
SEQ = 8192
DEPTH = 2
DEC_BATCH = 128
DEC_SEQ = 1
PAST_LEN = 2048
PAGE_SIZE = 128

D_MIX = D_MODEL
C_CONV = D_MIX // 2
N_HEADS = 8
HEAD_DIM = (D_MIX - C_CONV) // N_HEADS
N_KV_HEADS = 2
GROUP = N_HEADS // N_KV_HEADS
CONV_W = 31
N_IDX_HEADS = 4
IDX_DIM = 64
TOPK_MAX = 256
N_META = 16
NUM_BUCKETS = 32
MAX_DISTANCE = 128
Q_BLOCK = 128
EPS = 1e-6
LN_EPS = 1e-5

PROJ_SIZES = (C_CONV, C_CONV, C_CONV,
              N_HEADS * HEAD_DIM, N_KV_HEADS * HEAD_DIM,
              N_KV_HEADS * HEAD_DIM, N_HEADS * HEAD_DIM,
              N_IDX_HEADS * IDX_DIM, IDX_DIM, N_IDX_HEADS)
PROJ_SPLITS = tuple(int(s) for s in np.cumsum(PROJ_SIZES)[:-1])
D_IN = int(sum(PROJ_SIZES))

kernel_name = "hymba_conformer_dsa_decode_step"


def rmsnorm(x, g):
    xf = x.astype(jnp.float32)
    y = xf * lax.rsqrt(jnp.mean(xf * xf, axis=-1, keepdims=True) + EPS) * g.astype(jnp.float32)
    return y.astype(x.dtype)


def project(xn, w_in):
    B, T = xn.shape[:2]
    h = xn @ w_in
    ua, ub, gc, q, k, v, ga, iq, ik, iw = jnp.split(h, PROJ_SPLITS, axis=-1)
    u = ua * jax.nn.sigmoid(ub)
    q = q.reshape(B, T, N_HEADS, HEAD_DIM)
    k = k.reshape(B, T, N_KV_HEADS, HEAD_DIM)
    v = v.reshape(B, T, N_KV_HEADS, HEAD_DIM)
    iq = iq.reshape(B, T, N_IDX_HEADS, IDX_DIM) * (IDX_DIM ** -0.5)
    iw = iw * (N_IDX_HEADS ** -0.5)
    return u, gc, q, k, v, ga, iq, ik, iw


def conv_branch(u_ext, w, b, ln_g, ln_b):
    y = lax.conv_general_dilated(u_ext, w[:, None, :], window_strides=(1,), padding='VALID',
                                 dimension_numbers=('NWC', 'WIO', 'NWC'),
                                 feature_group_count=C_CONV) + b
    yf = y.astype(jnp.float32)
    mu = jnp.mean(yf, axis=-1, keepdims=True)
    var = jnp.mean(jnp.square(yf - mu), axis=-1, keepdims=True)
    yn = (yf - mu) * lax.rsqrt(var + LN_EPS) * ln_g.astype(jnp.float32) + ln_b.astype(jnp.float32)
    return jax.nn.silu(yn).astype(u_ext.dtype)


def t5_bucket(rel):
    n = jnp.maximum(rel, 0)
    max_exact = NUM_BUCKETS // 2
    large = max_exact + (jnp.log(jnp.maximum(n, 1).astype(jnp.float32) / max_exact)
                         / math.log(MAX_DISTANCE / max_exact)
                         * (NUM_BUCKETS - max_exact)).astype(jnp.int32)
    large = jnp.minimum(large, NUM_BUCKETS - 1)
    return jnp.where(n < max_exact, n, large)


def sparse_attend(q, iq, iw, qpos, k_all, v_all, ik_all, rel_bias, topk):
    B, T = q.shape[:2]
    L = k_all.shape[1]
    kpos = jnp.arange(L, dtype=jnp.int32)
    s = jnp.einsum('bthd,bsd->bths', iq.astype(jnp.float32), ik_all.astype(jnp.float32))
    score = jnp.einsum('bth,bths->bts', iw.astype(jnp.float32), jax.nn.relu(s))
    admissible = kpos[None, :] <= qpos[:, None]
    score = jnp.where(admissible[None], score, -jnp.inf)
    _, idx = lax.top_k(score, topk)
    valid = idx <= qpos[None, :, None]
    gather = jax.vmap(lambda a, i: a[i])
    k_sel = gather(k_all, idx)
    v_sel = gather(v_all, idx)
    qg = q.reshape(B, T, N_KV_HEADS, GROUP, HEAD_DIM)
    logits = jnp.einsum('btngd,btknd->btngk', qg, k_sel).astype(jnp.float32) * (HEAD_DIM ** -0.5)
    bias = rel_bias[t5_bucket(qpos[None, :, None] - idx)]
    bias = jnp.moveaxis(bias.reshape(B, T, topk, N_KV_HEADS, GROUP), 2, -1)
    logits = logits + bias.astype(jnp.float32)
    logits = jnp.where(valid[:, :, None, None, :], logits, -jnp.inf)
    p = jax.nn.softmax(logits, axis=-1)
    o = jnp.einsum('btngk,btknd->btngd', p.astype(v_sel.dtype), v_sel)
    return o.reshape(B, T, N_HEADS * HEAD_DIM)


def to_blocks(a, l_pad, n_blk):
    B, L = a.shape[:2]
    a = jnp.pad(a, ((0, 0), (0, l_pad - L)) + ((0, 0),) * (a.ndim - 2))
    return jnp.moveaxis(a.reshape((B, n_blk, Q_BLOCK) + a.shape[2:]), 1, 0)


def merge(conv_y, gc, attn_o, ga, w_out):
    mix = jnp.concatenate([conv_y * jax.nn.silu(gc), attn_o * jax.nn.silu(ga)], axis=-1)
    return mix @ w_out


def setup_inputs(seed: int = 0) -> dict:
    key = jax.random.key(seed)
    ks = jax.random.split(key, 17)
    n_pages = PAST_LEN // PAGE_SIZE
    n_pool = (DEC_BATCH * n_pages * 5) // 4
    nrm = jax.random.normal
    page_table = jax.random.permutation(ks[6], n_pool)[:DEC_BATCH * n_pages].reshape(DEC_BATCH, n_pages).astype(jnp.int32)
    return {
        "x_prompt": nrm(ks[0], (BATCH, SEQ, D_MODEL), jnp.float32),
        "x_sample": nrm(ks[1], (DEC_BATCH, DEC_SEQ, D_MODEL), jnp.float32),
        "cache_k": nrm(ks[2], (DEPTH, n_pool, PAGE_SIZE, N_KV_HEADS, HEAD_DIM), jnp.float32),
        "cache_v": nrm(ks[3], (DEPTH, n_pool, PAGE_SIZE, N_KV_HEADS, HEAD_DIM), jnp.float32),
        "cache_idx_k": nrm(ks[4], (DEPTH, n_pool, PAGE_SIZE, IDX_DIM), jnp.float32),
        "state_conv": 0.5 * nrm(ks[5], (DEPTH, DEC_BATCH, CONV_W - 1, C_CONV), jnp.float32),
        "page_table": page_table,
        "meta_tokens": nrm(ks[7], (N_META, D_MODEL), jnp.float32),
        "rel_bias": 0.5 * nrm(ks[8], (NUM_BUCKETS, N_HEADS), jnp.float32),
        "norm_g": 1.0 + 0.05 * nrm(ks[9], (DEPTH, D_MODEL), jnp.float32),
        "w_in": nrm(ks[10], (DEPTH, D_MODEL, D_IN), jnp.float32) * D_MODEL ** -0.5,
        "conv_w": nrm(ks[11], (DEPTH, CONV_W, C_CONV), jnp.float32) * CONV_W ** -0.5,
        "conv_b": 0.02 * nrm(ks[12], (DEPTH, C_CONV), jnp.float32),
        "conv_ln_g": 1.0 + 0.05 * nrm(ks[13], (DEPTH, C_CONV), jnp.float32),
        "conv_ln_b": 0.02 * nrm(ks[14], (DEPTH, C_CONV), jnp.float32),
        "w_out": nrm(ks[15], (DEPTH, D_MIX, D_MODEL), jnp.float32) * D_MIX ** -0.5,
        "final_norm_g": 1.0 + 0.05 * nrm(ks[16], (D_MODEL,), jnp.float32),
    }


def reference(x_prompt, x_sample, cache_k, cache_v, cache_idx_k, state_conv, page_table,
              meta_tokens, rel_bias, norm_g, w_in, conv_w, conv_b, conv_ln_g, conv_ln_b,
              w_out, final_norm_g):
    B, S, _ = x_prompt.shape
    L_p = S + N_META
    topk_p = min(TOPK_MAX, L_p // 4)
    n_blk = -(-L_p // Q_BLOCK)
    L_pad = n_blk * Q_BLOCK
    pos_blocks = jnp.arange(L_pad, dtype=jnp.int32).reshape(n_blk, Q_BLOCK)

    DB, T_s, _ = x_sample.shape
    n_pages = page_table.shape[1]
    past = n_pages * cache_k.shape[2]
    L_s = past + T_s
    topk_s = min(TOPK_MAX, L_s // 4)
    pos_s = past + jnp.arange(T_s, dtype=jnp.int32)

    xp = jnp.concatenate([jnp.broadcast_to(meta_tokens[None].astype(x_prompt.dtype), (B, N_META, D_MODEL)),
                          x_prompt], axis=1)
    xs = x_sample
    kp, vp, ikp, cp = [], [], [], []
    ksm, vsm, iks, cs = [], [], [], []
    for l in range(DEPTH):
        xn = rmsnorm(xp, norm_g[l])
        u, gc, q, k, v, ga, iq, ik, iw = project(xn, w_in[l])
        u_ext = jnp.pad(u, ((0, 0), (CONV_W - 1, 0), (0, 0)))
        cy = conv_branch(u_ext, conv_w[l], conv_b[l], conv_ln_g[l], conv_ln_b[l])
        o_blk = lax.map(lambda a: sparse_attend(a[0], a[1], a[2], a[3], k, v, ik, rel_bias, topk_p),
                        (to_blocks(q, L_pad, n_blk), to_blocks(iq, L_pad, n_blk),
                         to_blocks(iw, L_pad, n_blk), pos_blocks))
        o = jnp.moveaxis(o_blk, 0, 1).reshape(B, L_pad, N_HEADS * HEAD_DIM)[:, :L_p]
        xp = xp + merge(cy, gc, o, ga, w_out[l])
        kp.append(k); vp.append(v); ikp.append(ik); cp.append(u_ext[:, -(CONV_W - 1):])

        xn = rmsnorm(xs, norm_g[l])
        u, gc, q, k, v, ga, iq, ik, iw = project(xn, w_in[l])
        u_ext = jnp.concatenate([state_conv[l].astype(u.dtype), u], axis=1)
        cy = conv_branch(u_ext, conv_w[l], conv_b[l], conv_ln_g[l], conv_ln_b[l])
        k_past = cache_k[l][page_table].reshape(DB, past, N_KV_HEADS, HEAD_DIM)
        v_past = cache_v[l][page_table].reshape(DB, past, N_KV_HEADS, HEAD_DIM)
        ik_past = cache_idx_k[l][page_table].reshape(DB, past, IDX_DIM)
        k_all = jnp.concatenate([k_past.astype(k.dtype), k], axis=1)
        v_all = jnp.concatenate([v_past.astype(v.dtype), v], axis=1)
        ik_all = jnp.concatenate([ik_past.astype(ik.dtype), ik], axis=1)
        o = sparse_attend(q, iq, iw, pos_s, k_all, v_all, ik_all, rel_bias, topk_s)
        xs = xs + merge(cy, gc, o, ga, w_out[l])
        ksm.append(k); vsm.append(v); iks.append(ik); cs.append(u_ext[:, -(CONV_W - 1):])

    y_prompt = rmsnorm(xp, final_norm_g)[:, N_META:]
    y_sample = rmsnorm(xs, final_norm_g)
    return (y_prompt, y_sample,
            jnp.stack(kp), jnp.stack(vp), jnp.stack(ikp), jnp.stack(cp),
            jnp.stack(ksm), jnp.stack(vsm), jnp.stack(iks), jnp.stack(cs))
```

```python
import functools
import math

import numpy as np
import jax
import jax.numpy as jnp
from jax import lax
from jax.experimental import pallas as pl
from jax.experimental.pallas import tpu as pltpu

N_HEADS = 8
N_KV_HEADS = 2
GROUP = N_HEADS // N_KV_HEADS
HEAD_DIM = 64
N_IDX_HEADS = 4
IDX_DIM = 64
C_CONV = 512
CONV_W = 31
TOPK_MAX = 256
N_META = 16
NUM_BUCKETS = 32
MAX_DISTANCE = 128
EPS = 1e-6
LN_EPS = 1e-5

LANE = 128
TQ = 128
KC = 512
SUB = KC // LANE
HALO = 32
CONV_SUB = 64
MXU_DTYPE = jnp.bfloat16
NEG = -1e30
INT_MIN = -2 ** 31
KEY_NEG_INF = -0x7F800000
IDX_BIG = 0x3FFFFFFF
VMEM_LIMIT = 56 * 1024 * 1024

_Q_OFF = 3 * C_CONV
_KV_OFF = _Q_OFF + N_HEADS * LANE
_GA_OFF = _KV_OFF + 2 * N_KV_HEADS * HEAD_DIM
_IQ_OFF = _GA_OFF + N_HEADS * HEAD_DIM
_TAIL_OFF = _IQ_OFF + N_IDX_HEADS * LANE
_W_COLS = _TAIL_OFF + LANE

_HEAD_ORDER = [h for j in range(GROUP) for h in (j, j + GROUP)]
_PERM = np.concatenate([np.arange(h * HEAD_DIM, (h + 1) * HEAD_DIM) for h in _HEAD_ORDER])

_PROJ_SIZES = (C_CONV, C_CONV, C_CONV, N_HEADS * HEAD_DIM, N_KV_HEADS * HEAD_DIM, N_KV_HEADS * HEAD_DIM,
               N_HEADS * HEAD_DIM, N_IDX_HEADS * IDX_DIM, IDX_DIM, N_IDX_HEADS)
_PROJ_SPLITS = tuple(int(s) for s in np.cumsum(_PROJ_SIZES)[:-1])


def _cparams(n_axes):
    return pltpu.CompilerParams(dimension_semantics=("arbitrary",) * n_axes, vmem_limit_bytes=VMEM_LIMIT)


def _silu(x):
    return x * jax.nn.sigmoid(x)


def _lane_tile(x):
    return jnp.concatenate([x] * SUB, axis=1)


def _prep_w_in(w):
    ua, ub, gc, q, k, v, ga, iq, ik, iw = jnp.split(w, _PROJ_SPLITS, axis=1)
    d = w.shape[0]
    z = jnp.zeros((d, HEAD_DIM), w.dtype)
    qz = []
    for h in range(N_HEADS):
        qh = q[:, h * HEAD_DIM:(h + 1) * HEAD_DIM]
        qz += [qh, z] if h // GROUP == 0 else [z, qh]
    iqz = []
    for h in range(N_IDX_HEADS):
        iqz += [iq[:, h * IDX_DIM:(h + 1) * IDX_DIM], z]
    tail = jnp.concatenate([ik, iw, jnp.zeros((d, LANE - IDX_DIM - N_IDX_HEADS), w.dtype)], axis=1)
    out = jnp.concatenate([ua, ub, gc] + qz + [k, v, ga[:, _PERM]] + iqz + [tail], axis=1)
    assert out.shape[1] == _W_COLS
    return out.astype(MXU_DTYPE)


def _t5_bucket(rel):
    n = jnp.maximum(rel, 0)
    max_exact = NUM_BUCKETS // 2
    large = max_exact + (jnp.log(jnp.maximum(n, 1).astype(jnp.float32) / max_exact)
                         / math.log(MAX_DISTANCE / max_exact)
                         * (NUM_BUCKETS - max_exact)).astype(jnp.int32)
    large = jnp.minimum(large, NUM_BUCKETS - 1)
    return jnp.where(n < max_exact, n, large)


def _bias_tiles(rel_bias):
    i = jnp.arange(TQ, dtype=jnp.int32)
    tiles = []
    for delta in range(3):
        dist = delta * TQ + i[:, None] - i[None, :]
        tiles.append(jnp.moveaxis(rel_bias[_t5_bucket(dist)], -1, 0))
    return jnp.stack(tiles).astype(jnp.float32)


def _proj_kernel(x_ref, g_ref, w_ref, u_ref, gc_ref, qz_ref, kv_ref, ga_ref, iqz_ref, tail_ref):
    x = x_ref[...]
    ms = jnp.mean(x * x, axis=-1, keepdims=True)
    xn = (x * lax.rsqrt(ms + EPS) * g_ref[...]).astype(MXU_DTYPE)

    def mm(lo, hi):
        return jnp.dot(xn, w_ref[:, lo:hi], preferred_element_type=jnp.float32)

    u_ref[...] = mm(0, C_CONV) * jax.nn.sigmoid(mm(C_CONV, 2 * C_CONV))
    gc_ref[...] = mm(2 * C_CONV, 3 * C_CONV)
    qz_ref[...] = (mm(_Q_OFF, _KV_OFF) * (HEAD_DIM ** -0.5)).astype(qz_ref.dtype)
    kv_ref[...] = mm(_KV_OFF, _GA_OFF)
    ga_ref[...] = mm(_GA_OFF, _IQ_OFF)
    iqz_ref[...] = (mm(_IQ_OFF, _TAIL_OFF) * (IDX_DIM ** -0.5)).astype(iqz_ref.dtype)
    t = mm(_TAIL_OFF, _W_COLS)
    lane = lax.broadcasted_iota(jnp.int32, t.shape, 1)
    tail_ref[...] = t * jnp.where(lane >= IDX_DIM, N_IDX_HEADS ** -0.5, 1.0)


def _project(x, g, w, tm):
    r, d = x.shape
    assert r % tm == 0

    def rows(c):
        return pl.BlockSpec((tm, c), lambda i: (i, 0))

    widths = (C_CONV, C_CONV, N_HEADS * LANE, _GA_OFF - _KV_OFF, N_HEADS * HEAD_DIM, N_IDX_HEADS * LANE, LANE)
    dtypes = (jnp.float32, jnp.float32, MXU_DTYPE, jnp.float32, jnp.float32, MXU_DTYPE, jnp.float32)
    return pl.pallas_call(
        _proj_kernel,
        grid=(r // tm,),
        in_specs=[rows(d), pl.BlockSpec((1, d), lambda i: (0, 0)), pl.BlockSpec((d, _W_COLS), lambda i: (0, 0))],
        out_specs=[rows(c) for c in widths],
        out_shape=[jax.ShapeDtypeStruct((r, c), t) for c, t in zip(widths, dtypes)],
        compiler_params=_cparams(1),
        name="proj",
    )(x, g.reshape(1, d), w)


def _conv_kernel(prev_ref, cur_ref, gc_ref, w_ref, b_ref, lg_ref, lb_ref, o_ref, ext_ref):
    i = pl.program_id(1)
    ext_ref[0:HALO, :] = jnp.where(i > 0, prev_ref[...], 0.0)
    ext_ref[HALO:HALO + TQ, :] = cur_ref[...]
    off = HALO - (CONV_W - 1)
    for r0 in range(0, TQ, CONV_SUB):
        acc = jnp.zeros((CONV_SUB, C_CONV), jnp.float32)
        for j in range(CONV_W):
            acc = acc + w_ref[j:j + 1, :] * ext_ref[r0 + off + j:r0 + off + j + CONV_SUB, :]
        y = acc + b_ref[...]
        mu = jnp.mean(y, axis=-1, keepdims=True)
        dev = y - mu
        var = jnp.mean(dev * dev, axis=-1, keepdims=True)
        yn = dev * lax.rsqrt(var + LN_EPS) * lg_ref[...] + lb_ref[...]
        o_ref[r0:r0 + CONV_SUB, :] = (_silu(yn) * _silu(gc_ref[r0:r0 + CONV_SUB, :])).astype(o_ref.dtype)


def _conv_branch(u, gc, w, b, lg, lb):
    bsz, lq, c = u.shape
    per = TQ // HALO
    row = lambda a: a.reshape(1, c)
    vec = pl.BlockSpec((1, c), lambda bi, i: (0, 0))
    tile = pl.BlockSpec((None, TQ, c), lambda bi, i: (bi, i, 0))
    return pl.pallas_call(
        _conv_kernel,
        grid=(bsz, lq // TQ),
        in_specs=[pl.BlockSpec((None, HALO, c), lambda bi, i: (bi, jnp.maximum(i * per - 1, 0), 0)),
                  tile, tile, pl.BlockSpec((CONV_W, c), lambda bi, i: (0, 0)), vec, vec, vec],
        out_specs=tile,
        out_shape=jax.ShapeDtypeStruct((bsz, lq, c), MXU_DTYPE),
        scratch_shapes=[pltpu.VMEM((HALO + TQ, c), jnp.float32)],
        compiler_params=_cparams(2),
        name="conv",
    )(u, u, gc, w, row(b), row(lg), row(lb))


def _to_key(s):
    bits = lax.bitcast_convert_type(s, jnp.int32)
    return jnp.where(bits < 0, jnp.int32(INT_MIN) - bits, bits)


def _count(ref, nkc, pred):
    rows = ref.shape[1]

    def body(c, acc):
        blk = ref[c]
        for j in range(SUB):
            acc = acc + jnp.where(pred(blk[:, j * LANE:(j + 1) * LANE]), 1.0, 0.0)
        return acc

    acc = lax.fori_loop(0, nkc, body, jnp.zeros((rows, LANE), jnp.float32))
    return jnp.broadcast_to(jnp.sum(acc, axis=-1, keepdims=True), (rows, LANE))


def _select(keys_ref, eidx_ref, madd_ref, nkc, kk, idx_bits):
    rows = keys_ref.shape[1]
    kf = jnp.float32(kk)

    def value_bit(i, thr):
        cand = thr ^ lax.shift_left(jnp.int32(1), 31 - i)
        cnt = _count(keys_ref, nkc, lambda x: x >= cand)
        return jnp.where(cnt >= kf, cand, thr)

    thr = lax.fori_loop(0, 32, value_bit, jnp.full((rows, LANE), INT_MIN, jnp.int32))

    def mark_ties(c, n_gt):
        blk = keys_ref[c]
        lane = lax.broadcasted_iota(jnp.int32, (rows, LANE), 1)
        for j in range(SUB):
            x = blk[:, j * LANE:(j + 1) * LANE]
            n_gt = n_gt + jnp.where(x > thr, 1.0, 0.0)
            eidx_ref[c, :, j * LANE:(j + 1) * LANE] = jnp.where(x == thr, c * KC + j * LANE + lane, IDX_BIG)
        return n_gt

    n_gt = lax.fori_loop(0, nkc, mark_ties, jnp.zeros((rows, LANE), jnp.float32))
    n_gt = jnp.broadcast_to(jnp.sum(n_gt, axis=-1, keepdims=True), (rows, LANE))
    need = kf - n_gt

    def index_bit(i, cut):
        cand = cut | lax.shift_left(jnp.int32(1), idx_bits - 1 - i)
        cnt = _count(eidx_ref, nkc, lambda x: x < cand)
        return jnp.where(cnt < need, cand, cut)

    cut = lax.fori_loop(0, idx_bits, index_bit, jnp.zeros((rows, LANE), jnp.int32))
    cut = jnp.where(thr == KEY_NEG_INF, -1, cut)

    def write(c, carry):
        blk = keys_ref[c]
        eblk = eidx_ref[c]
        for j in range(SUB):
            sl = slice(j * LANE, (j + 1) * LANE)
            sel = (blk[:, sl] > thr) | (eblk[:, sl] <= cut)
            madd_ref[c, :, sl] = jnp.where(sel, 0.0, NEG)
        return carry

    lax.fori_loop(0, nkc, write, 0)


def _attend_kernel(iqz_ref, tail_ref, qz_ref, ga_ref, ikt_ref, kt_ref, v_ref, toep_ref, o_ref,
                   keys_ref, eidx_ref, madd_ref, wb_ref, m_ref, l_ref, acc_ref, *, kk, idx_bits):
    qi = pl.program_id(1)
    nkc = qi // SUB + 1

    tw = tail_ref[...]
    for h in range(N_IDX_HEADS):
        wb_ref[h] = jnp.broadcast_to(tw[:, IDX_DIM + h:IDX_DIM + h + 1], (TQ, LANE))

    def score_chunk(c, carry):
        ikt = ikt_ref[c]
        s = jnp.zeros((TQ, KC), jnp.float32)
        for h in range(N_IDX_HEADS):
            sh = jnp.dot(iqz_ref[:, h * LANE:(h + 1) * LANE], ikt, preferred_element_type=jnp.float32)
            s = s + _lane_tile(wb_ref[h]) * jnp.maximum(sh, 0.0)
        kpos = c * KC + lax.broadcasted_iota(jnp.int32, (TQ, KC), 1)
        qpos = qi * TQ + lax.broadcasted_iota(jnp.int32, (TQ, KC), 0)
        keys_ref[c] = _to_key(jnp.where(kpos <= qpos, s, -jnp.inf))
        return carry

    lax.fori_loop(0, nkc, score_chunk, 0)
    _select(keys_ref, eidx_ref, madd_ref, nkc, kk, idx_bits)

    m_ref[...] = jnp.full(m_ref.shape, NEG, jnp.float32)
    l_ref[...] = jnp.zeros(l_ref.shape, jnp.float32)
    acc_ref[...] = jnp.zeros(acc_ref.shape, jnp.float32)

    def attend_chunk(c, carry):
        kt = kt_ref[c]
        vv = v_ref[c]
        madd = madd_ref[c]
        delta = [jnp.clip(qi - (c * SUB + j), 0, 2) for j in range(SUB)]
        for h in range(N_HEADS):
            lg = jnp.dot(qz_ref[:, h * LANE:(h + 1) * LANE], kt, preferred_element_type=jnp.float32)
            bias = jnp.concatenate([toep_ref[delta[j], h] for j in range(SUB)], axis=1)
            lg = lg + bias + madd
            m_old = m_ref[h]
            m_new = jnp.maximum(m_old, jnp.max(lg, axis=-1, keepdims=True))
            alpha = jnp.exp(m_old - m_new)
            p = jnp.exp(lg - _lane_tile(m_new))
            l_ref[h] = alpha * l_ref[h] + jnp.sum(p, axis=-1, keepdims=True)
            acc_ref[h] = alpha * acc_ref[h] + jnp.dot(p.astype(MXU_DTYPE), vv, preferred_element_type=jnp.float32)
            m_ref[h] = m_new
        return carry

    lax.fori_loop(0, nkc, attend_chunk, 0)

    lane = lax.broadcasted_iota(jnp.int32, (TQ, LANE), 1)
    for j in range(GROUP):
        pair = jnp.where(lane < HEAD_DIM, acc_ref[j] / l_ref[j], acc_ref[j + GROUP] / l_ref[j + GROUP])
        sl = slice(j * LANE, (j + 1) * LANE)
        o_ref[:, sl] = (pair * _silu(ga_ref[:, sl])).astype(o_ref.dtype)


def _attend(iqz, tail, qz, ga, ikt, kt, v, toep, kk):
    bsz, lq, _ = qz.shape
    nc = kt.shape[1]
    idx_bits = max(1, int(nc * KC - 1).bit_length())
    tile = lambda c: pl.BlockSpec((None, TQ, c), lambda b, i: (b, i, 0))
    whole = lambda a: pl.BlockSpec((None,) + a.shape[1:], lambda b, i: (b,) + (0,) * (a.ndim - 1))
    return pl.pallas_call(
        functools.partial(_attend_kernel, kk=kk, idx_bits=idx_bits),
        grid=(bsz, lq // TQ),
        in_specs=[tile(N_IDX_HEADS * LANE), tile(LANE), tile(N_HEADS * LANE), tile(N_HEADS * HEAD_DIM),
                  whole(ikt), whole(kt), whole(v),
                  pl.BlockSpec(toep.shape, lambda b, i: (0, 0, 0, 0))],
        out_specs=tile(N_HEADS * HEAD_DIM),
        out_shape=jax.ShapeDtypeStruct((bsz, lq, N_HEADS * HEAD_DIM), MXU_DTYPE),
        scratch_shapes=[pltpu.VMEM((nc, TQ, KC), jnp.int32), pltpu.VMEM((nc, TQ, KC), jnp.int32),
                        pltpu.VMEM((nc, TQ, KC), jnp.float32), pltpu.VMEM((N_IDX_HEADS, TQ, LANE), jnp.float32),
                        pltpu.VMEM((N_HEADS, TQ, LANE), jnp.float32), pltpu.VMEM((N_HEADS, TQ, LANE), jnp.float32),
                        pltpu.VMEM((N_HEADS, TQ, LANE), jnp.float32)],
        compiler_params=_cparams(2),
        name="attend",
    )(iqz, tail, qz, ga, ikt, kt, v, toep)


def _out_kernel(x_ref, mc_ref, ma_ref, wc_ref, wa_ref, o_ref):
    o_ref[...] = (x_ref[...]
                  + jnp.dot(mc_ref[...], wc_ref[...], preferred_element_type=jnp.float32)
                  + jnp.dot(ma_ref[...], wa_ref[...], preferred_element_type=jnp.float32))


def _out_final_kernel(x_ref, mc_ref, ma_ref, wc_ref, wa_ref, g_ref, o_ref, y_ref):
    x = (x_ref[...]
         + jnp.dot(mc_ref[...], wc_ref[...], preferred_element_type=jnp.float32)
         + jnp.dot(ma_ref[...], wa_ref[...], preferred_element_type=jnp.float32))
    o_ref[...] = x
    ms = jnp.mean(x * x, axis=-1, keepdims=True)
    y_ref[...] = x * lax.rsqrt(ms + EPS) * g_ref[...]


def _out_proj(x, mc, ma, wc, wa, tm, final_g=None):
    r, d = x.shape
    c = mc.shape[1]
    rows = lambda w: pl.BlockSpec((tm, w), lambda i: (i, 0))
    full = lambda a: pl.BlockSpec(a.shape, lambda i: (0, 0))
    if final_g is None:
        return pl.pallas_call(
            _out_kernel, grid=(r // tm,),
            in_specs=[rows(d), rows(c), rows(c), full(wc), full(wa)],
            out_specs=rows(d), out_shape=jax.ShapeDtypeStruct((r, d), jnp.float32),
            compiler_params=_cparams(1), name="out",
        )(x, mc, ma, wc, wa)
    g = final_g.reshape(1, d)
    return pl.pallas_call(
        _out_final_kernel, grid=(r // tm,),
        in_specs=[rows(d), rows(c), rows(c), full(wc), full(wa), full(g)],
        out_specs=[rows(d), rows(d)], out_shape=[jax.ShapeDtypeStruct((r, d), jnp.float32)] * 2,
        compiler_params=_cparams(1), name="out_final",
    )(x, mc, ma, wc, wa, g)


def _sconv_kernel(st_ref, u_ref, gc_ref, w_ref, b_ref, lg_ref, lb_ref, o_ref):
    acc = jnp.zeros(u_ref.shape, jnp.float32)
    for j in range(CONV_W - 1):
        acc = acc + w_ref[j:j + 1, :] * st_ref[j]
    acc = acc + w_ref[CONV_W - 1:CONV_W, :] * u_ref[...]
    y = acc + b_ref[...]
    mu = jnp.mean(y, axis=-1, keepdims=True)
    dev = y - mu
    var = jnp.mean(dev * dev, axis=-1, keepdims=True)
    yn = dev * lax.rsqrt(var + LN_EPS) * lg_ref[...] + lb_ref[...]
    o_ref[...] = (_silu(yn) * _silu(gc_ref[...])).astype(o_ref.dtype)


def _sample_conv(state_t, u, gc, w, b, lg, lb):
    db, c = u.shape
    row = lambda a: a.reshape(1, c)
    full = lambda a: pl.BlockSpec(a.shape, lambda i: (0,) * a.ndim)
    args = (state_t, u, gc, w, row(b), row(lg), row(lb))
    return pl.pallas_call(
        _sconv_kernel, grid=(1,), in_specs=[full(a) for a in args],
        out_specs=pl.BlockSpec((db, c), lambda i: (0, 0)),
        out_shape=jax.ShapeDtypeStruct((db, c), MXU_DTYPE),
        compiler_params=_cparams(1), name="sconv",
    )(*args)


def _dec_score_kernel(pt_ref, iq_ref, iw_ref, ikn_ref, *refs, n_pages, width):
    del pt_ref
    pages, o_ref = refs[:n_pages], refs[n_pages]
    iq = iq_ref[...][:, :IDX_DIM]
    w = iw_ref[...]
    pieces = []
    for p in range(n_pages):
        s = lax.dot_general(iq, pages[p][...].astype(MXU_DTYPE), (((1,), (1,)), ((), ())),
                            preferred_element_type=jnp.float32)
        pieces.append(jnp.sum(w * jnp.maximum(s, 0.0), axis=0, keepdims=True))
    ikn = ikn_ref[...][:, :IDX_DIM].astype(MXU_DTYPE).astype(jnp.float32)
    sn = jnp.sum(iq.astype(jnp.float32) * ikn, axis=-1, keepdims=True)
    scn = jnp.sum(w * jnp.maximum(sn, 0.0), axis=0, keepdims=True)
    page = pieces[0].shape[1]
    lane = lax.broadcasted_iota(jnp.int32, (1, width - n_pages * page), 1)
    pieces.append(jnp.where(lane == 0, scn, -jnp.inf))
    o_ref[...] = jnp.concatenate(pieces, axis=1)


def _dec_scores(page_table, iq3, iw3, tail3, cache_ik, layer, width):
    db, n_pages = page_table.shape
    page, di = cache_ik.shape[2:]
    seq = lambda a: pl.BlockSpec((None,) + a.shape[1:], lambda b, pt: (b, 0, 0))
    page_spec = lambda p: pl.BlockSpec((None, None, page, di), lambda b, pt: (layer, pt[b, p], 0, 0))
    return pl.pallas_call(
        functools.partial(_dec_score_kernel, n_pages=n_pages, width=width),
        grid_spec=pltpu.PrefetchScalarGridSpec(
            num_scalar_prefetch=1, grid=(db,),
            in_specs=[seq(iq3), seq(iw3), seq(tail3)] + [page_spec(p) for p in range(n_pages)],
            out_specs=pl.BlockSpec((None, 1, width), lambda b, pt: (b, 0, 0))),
        out_shape=jax.ShapeDtypeStruct((db, 1, width), jnp.float32),
        compiler_params=_cparams(1), name="dec_score",
    )(page_table, iq3, iw3, tail3, *([cache_ik] * n_pages))


def _dec_select_kernel(s_ref, o_ref, keys_ref, eidx_ref, madd_ref, *, kk, idx_bits):
    nc = keys_ref.shape[0]
    for c in range(nc):
        keys_ref[c] = _to_key(s_ref[:, c * KC:(c + 1) * KC])
    _select(keys_ref, eidx_ref, madd_ref, nc, kk, idx_bits)
    for c in range(nc):
        o_ref[:, c * KC:(c + 1) * KC] = madd_ref[c]


def _dec_select(scores, kk):
    db, width = scores.shape
    nc = width // KC
    idx_bits = max(1, int(width - 1).bit_length())
    spec = pl.BlockSpec((db, width), lambda i: (0, 0))
    return pl.pallas_call(
        functools.partial(_dec_select_kernel, kk=kk, idx_bits=idx_bits),
        grid=(1,), in_specs=[spec], out_specs=spec,
        out_shape=jax.ShapeDtypeStruct((db, width), jnp.float32),
        scratch_shapes=[pltpu.VMEM((nc, db, KC), jnp.int32), pltpu.VMEM((nc, db, KC), jnp.int32),
                        pltpu.VMEM((nc, db, KC), jnp.float32)],
        compiler_params=_cparams(1), name="dec_select",
    )(scores)


def _dec_attend_kernel(pt_ref, qz_ref, kvn_ref, madd_ref, bias_ref, ga_ref, *refs, n_pages):
    del pt_ref
    kpages, vpages, o_ref = refs[:n_pages], refs[n_pages:2 * n_pages], refs[2 * n_pages]
    qz = qz_ref[...]
    page = kpages[0].shape[0]
    past = n_pages * page
    nt = (((1,), (1,)), ((), ()))
    logits = []
    for p in range(n_pages):
        sl = slice(p * page, (p + 1) * page)
        lg = lax.dot_general(qz, kpages[p][...].astype(MXU_DTYPE), nt, preferred_element_type=jnp.float32)
        logits.append(lg + bias_ref[:, sl] + madd_ref[:, sl])
    kvn = kvn_ref[...].astype(MXU_DTYPE).astype(jnp.float32)
    lgn = jnp.sum(qz.astype(jnp.float32) * kvn[:, :LANE], axis=-1, keepdims=True)
    lgn = lgn + bias_ref[:, past:past + 1] + madd_ref[:, past:past + 1]
    m = lgn
    for lg in logits:
        m = jnp.maximum(m, jnp.max(lg, axis=-1, keepdims=True))
    pn = jnp.exp(lgn - m)
    den = pn
    o = pn.astype(MXU_DTYPE).astype(jnp.float32) * kvn[:, LANE:]
    for p in range(n_pages):
        pp = jnp.exp(logits[p] - m)
        den = den + jnp.sum(pp, axis=-1, keepdims=True)
        o = o + jnp.dot(pp.astype(MXU_DTYPE), vpages[p][...].astype(MXU_DTYPE), preferred_element_type=jnp.float32)
    o = o / den
    lane = lax.broadcasted_iota(jnp.int32, (GROUP, LANE), 1)
    pair = jnp.where(lane < HEAD_DIM, o[:GROUP], o[GROUP:])
    o_ref[...] = (pair * _silu(ga_ref[...])).astype(o_ref.dtype)


def _dec_attend(page_table, qz3, kvn3, madd3, bias, ga3, cache_k, cache_v, layer):
    db, n_pages = page_table.shape
    page, feat = cache_k.shape[2:]
    seq = lambda a: pl.BlockSpec((None,) + a.shape[1:], lambda b, pt: (b, 0, 0))
    page_spec = lambda p: pl.BlockSpec((None, None, page, feat), lambda b, pt: (layer, pt[b, p], 0, 0))
    return pl.pallas_call(
        functools.partial(_dec_attend_kernel, n_pages=n_pages),
        grid_spec=pltpu.PrefetchScalarGridSpec(
            num_scalar_prefetch=1, grid=(db,),
            in_specs=[seq(qz3), seq(kvn3), seq(madd3), pl.BlockSpec(bias.shape, lambda b, pt: (0, 0)), seq(ga3)]
                     + [page_spec(p) for p in range(n_pages)] * 2,
            out_specs=pl.BlockSpec((None, GROUP, LANE), lambda b, pt: (b, 0, 0))),
        out_shape=jax.ShapeDtypeStruct((db, GROUP, LANE), MXU_DTYPE),
        compiler_params=_cparams(1), name="dec_attend",
    )(page_table, qz3, kvn3, madd3, bias, ga3, *([cache_k] * n_pages), *([cache_v] * n_pages))


def _round_up(x, m):
    return -(-x // m) * m


def _row_tile(rows):
    for f in (5, 4, 3, 2, 1):
        if rows % (f * TQ) == 0:
            return f * TQ
    return rows


def kernel(x_prompt, x_sample, cache_k, cache_v, cache_idx_k, state_conv, page_table, meta_tokens, rel_bias,
           norm_g, w_in, conv_w, conv_b, conv_ln_g, conv_ln_b, w_out, final_norm_g):
    bsz, seq, d = x_prompt.shape
    depth = w_in.shape[0]
    lp = seq + N_META
    lq = _round_up(lp, TQ)
    lk = _round_up(lp, KC)
    nc = lk // KC
    kk_p = min(TOPK_MAX, lp // 4)
    db = x_sample.shape[0]
    n_pool, page = cache_k.shape[1:3]
    n_pages = page_table.shape[1]
    past = n_pages * page
    kk_s = min(TOPK_MAX, (past + 1) // 4)
    width_s = _round_up(past + 1, KC)
    feat = N_KV_HEADS * HEAD_DIM
    tm = _row_tile(bsz * lq)

    xp = jnp.concatenate([jnp.broadcast_to(meta_tokens[None].astype(x_prompt.dtype), (bsz, N_META, d)), x_prompt,
                          jnp.zeros((bsz, lq - lp, d), x_prompt.dtype)], axis=1).reshape(bsz * lq, d)
    xs = x_sample.reshape(db, d)
    toep = _bias_tiles(rel_bias)
    bias_s = jnp.moveaxis(rel_bias[_t5_bucket(past - jnp.arange(width_s, dtype=jnp.int32))], -1, 0)
    cache_k4 = cache_k.reshape(depth, n_pool, page, feat)
    cache_v4 = cache_v.reshape(depth, n_pool, page, feat)

    kp, vp, ikp, cp, ksm, vsm, iks, cs = ([] for _ in range(8))
    yp = ys = None
    for l in range(depth):
        w = _prep_w_in(w_in[l])
        wc = w_out[l, :C_CONV].astype(MXU_DTYPE)
        wa = w_out[l, C_CONV:][_PERM].astype(MXU_DTYPE)
        last = l == depth - 1

        u, gc, qz, kv, ga, iqz, tail = _project(xp, norm_g[l], w, tm)
        b3 = lambda a: a.reshape(bsz, lq, a.shape[-1])
        u3, kv3, tail3 = b3(u), b3(kv), b3(tail)
        mixc = _conv_branch(u3, b3(gc), conv_w[l], conv_b[l], conv_ln_g[l], conv_ln_b[l])
        keypad = lambda a: jnp.pad(a, ((0, 0), (0, lk - lq), (0, 0))).astype(MXU_DTYPE)
        chunked_t = lambda a: jnp.transpose(a.reshape(bsz, nc, KC, a.shape[-1]), (0, 1, 3, 2))
        ikt = chunked_t(keypad(jnp.concatenate([tail3[..., :IDX_DIM], jnp.zeros_like(tail3[..., IDX_DIM:])], -1)))
        kt = chunked_t(keypad(kv3[..., :feat]))
        vch = keypad(kv3[..., feat:]).reshape(bsz, nc, KC, feat)
        mixa = _attend(b3(iqz), tail3, b3(qz), b3(ga), ikt, kt, vch, toep, kk_p)
        res = _out_proj(xp, mixc.reshape(bsz * lq, C_CONV), mixa.reshape(bsz * lq, -1), wc, wa, tm,
                        final_norm_g if last else None)
        xp, yp = res if last else (res, None)
        kp.append(kv3[:, :lp, :feat].reshape(bsz, lp, N_KV_HEADS, HEAD_DIM))
        vp.append(kv3[:, :lp, feat:].reshape(bsz, lp, N_KV_HEADS, HEAD_DIM))
        ikp.append(tail3[:, :lp, :IDX_DIM])
        cp.append(u3[:, lp - (CONV_W - 1):lp])

        u, gc, qz, kv, ga, iqz, tail = _project(xs, norm_g[l], w, db)
        mixc = _sample_conv(jnp.transpose(state_conv[l], (1, 0, 2)), u, gc,
                            conv_w[l], conv_b[l], conv_ln_g[l], conv_ln_b[l])
        iw3 = tail[:, IDX_DIM:IDX_DIM + N_IDX_HEADS].reshape(db, N_IDX_HEADS, 1)
        scores = _dec_scores(page_table, iqz.reshape(db, N_IDX_HEADS, LANE), iw3, tail.reshape(db, 1, LANE),
                             cache_idx_k, l, width_s)
        madd = _dec_select(scores.reshape(db, width_s), kk_s)
        mixa = _dec_attend(page_table, qz.reshape(db, N_HEADS, LANE), kv.reshape(db, 1, 2 * feat),
                           madd.reshape(db, 1, width_s), bias_s, ga.reshape(db, GROUP, LANE),
                           cache_k4, cache_v4, l)
        res = _out_proj(xs, mixc, mixa.reshape(db, -1), wc, wa, db, final_norm_g if last else None)
        xs, ys = res if last else (res, None)
        ksm.append(kv[:, :feat].reshape(db, 1, N_KV_HEADS, HEAD_DIM))
        vsm.append(kv[:, feat:].reshape(db, 1, N_KV_HEADS, HEAD_DIM))
        iks.append(tail[:, None, :IDX_DIM])
        cs.append(jnp.concatenate([state_conv[l][:, 1:], u[:, None]], axis=1))

    y_prompt = yp.reshape(bsz, lq, d)[:, N_META:lp]
    y_sample = ys.reshape(db, 1, d)
    return (y_prompt, y_sample, jnp.stack(kp), jnp.stack(vp), jnp.stack(ikp), jnp.stack(cp),
            jnp.stack(ksm), jnp.stack(vsm), jnp.stack(iks), jnp.stack(cs))
```

```python
import functools
import math

import numpy as np
import jax
import jax.numpy as jnp
from jax import lax
from jax.experimental import pallas as pl
from jax.experimental.pallas import tpu as pltpu

N_HEADS = 8
N_KV_HEADS = 2
GROUP = N_HEADS // N_KV_HEADS
HEAD_DIM = 64
N_IDX_HEADS = 4
IDX_DIM = 64
C_CONV = 512
CONV_W = 31
TOPK_MAX = 256
N_META = 16
NUM_BUCKETS = 32
MAX_DISTANCE = 128
EPS = 1e-6
LN_EPS = 1e-5

LANE = 128
ROWS = 8
TQ = 128
KC = 512
SUB = KC // TQ
N_ACC = 4
HALO = 32
CONV_SUB = 64
MXU_DTYPE = jnp.bfloat16
NEG = -1e30
INT_MIN = -2 ** 31
KEY_NEG_INF = -0x7F800000
IDX_BIG = 0x3FFFFFFF
VMEM_LIMIT = 56 * 1024 * 1024

_Q_OFF = 3 * C_CONV
_KV_OFF = _Q_OFF + N_HEADS * LANE
_GA_OFF = _KV_OFF + 2 * N_KV_HEADS * HEAD_DIM
_IQ_OFF = _GA_OFF + N_HEADS * HEAD_DIM
_TAIL_OFF = _IQ_OFF + N_IDX_HEADS * LANE
_W_COLS = _TAIL_OFF + LANE

_HEAD_ORDER = [h for j in range(GROUP) for h in (j, j + GROUP)]
_PERM = np.concatenate([np.arange(h * HEAD_DIM, (h + 1) * HEAD_DIM) for h in _HEAD_ORDER])

_PROJ_SIZES = (C_CONV, C_CONV, C_CONV, N_HEADS * HEAD_DIM, N_KV_HEADS * HEAD_DIM, N_KV_HEADS * HEAD_DIM,
               N_HEADS * HEAD_DIM, N_IDX_HEADS * IDX_DIM, IDX_DIM, N_IDX_HEADS)
_PROJ_SPLITS = tuple(int(s) for s in np.cumsum(_PROJ_SIZES)[:-1])


def _cparams(n_axes):
    return pltpu.CompilerParams(dimension_semantics=("arbitrary",) * n_axes, vmem_limit_bytes=VMEM_LIMIT)


def _silu(x):
    return x * jax.nn.sigmoid(x)


def _prep_w_in(w):
    ua, ub, gc, q, k, v, ga, iq, ik, iw = jnp.split(w, _PROJ_SPLITS, axis=1)
    d = w.shape[0]
    z = jnp.zeros((d, HEAD_DIM), w.dtype)
    qz = []
    for h in range(N_HEADS):
        qh = q[:, h * HEAD_DIM:(h + 1) * HEAD_DIM]
        qz += [qh, z] if h // GROUP == 0 else [z, qh]
    iqz = []
    for h in range(N_IDX_HEADS):
        iqz += [iq[:, h * IDX_DIM:(h + 1) * IDX_DIM], z]
    tail = jnp.concatenate([ik, iw, jnp.zeros((d, LANE - IDX_DIM - N_IDX_HEADS), w.dtype)], axis=1)
    out = jnp.concatenate([ua, ub, gc] + qz + [k, v, ga[:, _PERM]] + iqz + [tail], axis=1)
    assert out.shape[1] == _W_COLS
    return out.astype(MXU_DTYPE)


def _t5_bucket(rel):
    n = jnp.maximum(rel, 0)
    max_exact = NUM_BUCKETS // 2
    large = max_exact + (jnp.log(jnp.maximum(n, 1).astype(jnp.float32) / max_exact)
                         / math.log(MAX_DISTANCE / max_exact)
                         * (NUM_BUCKETS - max_exact)).astype(jnp.int32)
    large = jnp.minimum(large, NUM_BUCKETS - 1)
    return jnp.where(n < max_exact, n, large)


def _bias_tiles(rel_bias):
    i = jnp.arange(TQ, dtype=jnp.int32)
    tiles = []
    for delta in range(3):
        dist = delta * TQ + i[None, :] - i[:, None]
        tiles.append(jnp.moveaxis(rel_bias[_t5_bucket(dist)], -1, 0))
    return jnp.stack(tiles).astype(jnp.float32)


def _proj_kernel(x_ref, g_ref, w_ref, u_ref, gc_ref, qz_ref, kv_ref, ga_ref, iqz_ref, tail_ref):
    x = x_ref[...]
    ms = jnp.mean(x * x, axis=-1, keepdims=True)
    xn = (x * lax.rsqrt(ms + EPS) * g_ref[...]).astype(MXU_DTYPE)

    def mm(lo, hi):
        return jnp.dot(xn, w_ref[:, lo:hi], preferred_element_type=jnp.float32)

    u_ref[...] = mm(0, C_CONV) * jax.nn.sigmoid(mm(C_CONV, 2 * C_CONV))
    gc_ref[...] = mm(2 * C_CONV, 3 * C_CONV)
    qz_ref[...] = (mm(_Q_OFF, _KV_OFF) * (HEAD_DIM ** -0.5)).astype(qz_ref.dtype)
    kv_ref[...] = mm(_KV_OFF, _GA_OFF)
    ga_ref[...] = mm(_GA_OFF, _IQ_OFF)
    iqz_ref[...] = (mm(_IQ_OFF, _TAIL_OFF) * (IDX_DIM ** -0.5)).astype(iqz_ref.dtype)
    t = mm(_TAIL_OFF, _W_COLS)
    lane = lax.broadcasted_iota(jnp.int32, t.shape, 1)
    tail_ref[...] = t * jnp.where(lane >= IDX_DIM, N_IDX_HEADS ** -0.5, 1.0)


def _project(x, g, w, tm):
    r, d = x.shape
    assert r % tm == 0

    def rows(c):
        return pl.BlockSpec((tm, c), lambda i: (i, 0))

    widths = (C_CONV, C_CONV, N_HEADS * LANE, _GA_OFF - _KV_OFF, N_HEADS * HEAD_DIM, N_IDX_HEADS * LANE, LANE)
    dtypes = (jnp.float32, jnp.float32, MXU_DTYPE, jnp.float32, jnp.float32, MXU_DTYPE, jnp.float32)
    return pl.pallas_call(
        _proj_kernel,
        grid=(r // tm,),
        in_specs=[rows(d), pl.BlockSpec((1, d), lambda i: (0, 0)), pl.BlockSpec((d, _W_COLS), lambda i: (0, 0))],
        out_specs=[rows(c) for c in widths],
        out_shape=[jax.ShapeDtypeStruct((r, c), t) for c, t in zip(widths, dtypes)],
        compiler_params=_cparams(1),
        name="proj",
    )(x, g.reshape(1, d), w)


def _conv_kernel(prev_ref, cur_ref, gc_ref, w_ref, b_ref, lg_ref, lb_ref, o_ref, ext_ref):
    i = pl.program_id(1)
    ext_ref[0:HALO, :] = jnp.where(i > 0, prev_ref[...], 0.0)
    ext_ref[HALO:HALO + TQ, :] = cur_ref[...]
    off = HALO - (CONV_W - 1)
    for r0 in range(0, TQ, CONV_SUB):
        acc = jnp.zeros((CONV_SUB, C_CONV), jnp.float32)
        for j in range(CONV_W):
            acc = acc + w_ref[j:j + 1, :] * ext_ref[r0 + off + j:r0 + off + j + CONV_SUB, :]
        y = acc + b_ref[...]
        mu = jnp.mean(y, axis=-1, keepdims=True)
        dev = y - mu
        var = jnp.mean(dev * dev, axis=-1, keepdims=True)
        yn = dev * lax.rsqrt(var + LN_EPS) * lg_ref[...] + lb_ref[...]
        o_ref[r0:r0 + CONV_SUB, :] = (_silu(yn) * _silu(gc_ref[r0:r0 + CONV_SUB, :])).astype(o_ref.dtype)


def _conv_branch(u, gc, w, b, lg, lb):
    bsz, lq, c = u.shape
    per = TQ // HALO
    row = lambda a: a.reshape(1, c)
    vec = pl.BlockSpec((1, c), lambda bi, i: (0, 0))
    tile = pl.BlockSpec((None, TQ, c), lambda bi, i: (bi, i, 0))
    return pl.pallas_call(
        _conv_kernel,
        grid=(bsz, lq // TQ),
        in_specs=[pl.BlockSpec((None, HALO, c), lambda bi, i: (bi, jnp.maximum(i * per - 1, 0), 0)),
                  tile, tile, pl.BlockSpec((CONV_W, c), lambda bi, i: (0, 0)), vec, vec, vec],
        out_specs=tile,
        out_shape=jax.ShapeDtypeStruct((bsz, lq, c), MXU_DTYPE),
        scratch_shapes=[pltpu.VMEM((HALO + TQ, c), jnp.float32)],
        compiler_params=_cparams(2),
        name="conv",
    )(u, u, gc, w, row(b), row(lg), row(lb))


def _to_key(s):
    bits = lax.bitcast_convert_type(s, jnp.int32)
    return jnp.where(bits < 0, jnp.int32(INT_MIN) - bits, bits)


def _count(ref, nkc, pred):
    q = ref.shape[2]

    def body(c, accs):
        accs = list(accs)
        for n, r in enumerate(range(0, KC, ROWS)):
            accs[n % N_ACC] = accs[n % N_ACC] + jnp.where(pred(ref[c, r:r + ROWS, :]), 1.0, 0.0)
        return tuple(accs)

    accs = lax.fori_loop(0, nkc, body, (jnp.zeros((ROWS, q), jnp.float32),) * N_ACC)
    return jnp.sum(sum(accs[1:], accs[0]), axis=0, keepdims=True)


def _select(keys_ref, eidx_ref, madd_ref, nkc, kk, idx_bits):
    q = keys_ref.shape[2]
    kf = jnp.float32(kk)

    def value_bit(i, thr):
        cand = thr ^ lax.shift_left(jnp.int32(1), 31 - i)
        cnt = _count(keys_ref, nkc, lambda x: x >= cand)
        return jnp.where(cnt >= kf, cand, thr)

    thr = lax.fori_loop(0, 32, value_bit, jnp.full((1, q), INT_MIN, jnp.int32))

    def mark_ties(c, carry):
        n_gt, n_eq = carry
        row = lax.broadcasted_iota(jnp.int32, (ROWS, q), 0)
        for r in range(0, KC, ROWS):
            x = keys_ref[c, r:r + ROWS, :]
            eq = x == thr
            n_gt = n_gt + jnp.where(x > thr, 1.0, 0.0)
            n_eq = n_eq + jnp.where(eq, 1.0, 0.0)
            eidx_ref[c, r:r + ROWS, :] = jnp.where(eq, c * KC + r + row, IDX_BIG)
        return n_gt, n_eq

    zero = jnp.zeros((ROWS, q), jnp.float32)
    n_gt, n_eq = lax.fori_loop(0, nkc, mark_ties, (zero, zero))
    need = kf - jnp.sum(n_gt, axis=0, keepdims=True)
    admissible = thr != KEY_NEG_INF
    surplus = (jnp.sum(n_eq, axis=0, keepdims=True) > need) & admissible

    def index_bit(i, cut):
        cand = cut | lax.shift_left(jnp.int32(1), idx_bits - 1 - i)
        cnt = _count(eidx_ref, nkc, lambda x: x < cand)
        return jnp.where(cnt < need, cand, cut)

    n_steps = jnp.where(jnp.max(jnp.where(surplus, 1.0, 0.0)) > 0.0, idx_bits, 0)
    cut = lax.fori_loop(0, n_steps, index_bit, jnp.zeros((1, q), jnp.int32))
    cut = jnp.where(surplus, cut, IDX_BIG - 1)
    cut = jnp.where(admissible, cut, -1)

    def write(c, carry):
        for r in range(0, KC, ROWS):
            sel = (keys_ref[c, r:r + ROWS, :] > thr) | (eidx_ref[c, r:r + ROWS, :] <= cut)
            madd_ref[c, r:r + ROWS, :] = jnp.where(sel, 0.0, NEG)
        return carry

    lax.fori_loop(0, nkc, write, 0)


def _attend_kernel(iqt_ref, iwt_ref, qzt_ref, ga_ref, ik_ref, k_ref, vt_ref, toep_ref, o_ref,
                   keys_ref, eidx_ref, madd_ref, m_ref, l_ref, acc_ref, *, kk, idx_bits):
    qi = pl.program_id(1)
    nkc = qi // SUB + 1
    qpos = qi * TQ + lax.broadcasted_iota(jnp.int32, (TQ, TQ), 1)
    krow = lax.broadcasted_iota(jnp.int32, (TQ, TQ), 0)

    def score_chunk(c, carry):
        iqt = iqt_ref[...]
        w = iwt_ref[...]
        for j in range(SUB):
            st = jnp.dot(ik_ref[c, j * TQ:(j + 1) * TQ, :], iqt, preferred_element_type=jnp.float32)
            s = jnp.zeros((TQ, TQ), jnp.float32)
            for h in range(N_IDX_HEADS):
                s = s + w[h:h + 1, :] * jnp.maximum(st[:, h * TQ:(h + 1) * TQ], 0.0)
            kpos = c * KC + j * TQ + krow
            keys_ref[c, j * TQ:(j + 1) * TQ, :] = _to_key(jnp.where(kpos <= qpos, s, -jnp.inf))
        return carry

    lax.fori_loop(0, nkc, score_chunk, 0)
    _select(keys_ref, eidx_ref, madd_ref, nkc, kk, idx_bits)

    m_ref[...] = jnp.full(m_ref.shape, NEG, jnp.float32)
    l_ref[...] = jnp.zeros(l_ref.shape, jnp.float32)
    acc_ref[...] = jnp.zeros(acc_ref.shape, jnp.float32)

    def attend_chunk(c, carry):
        vt = vt_ref[c]
        madd = madd_ref[c]
        lg_all = jnp.dot(k_ref[c], qzt_ref[...], preferred_element_type=jnp.float32)
        delta = [jnp.clip(qi - (c * SUB + j), 0, 2) for j in range(SUB)]
        for h in range(N_HEADS):
            bias = jnp.concatenate([toep_ref[delta[j], h] for j in range(SUB)], axis=0)
            lg = lg_all[:, h * TQ:(h + 1) * TQ] + bias + madd
            m_old = m_ref[h]
            m_new = jnp.maximum(m_old, jnp.max(lg, axis=0, keepdims=True))
            alpha = jnp.exp(m_old - m_new)
            p = jnp.exp(lg - m_new)
            l_ref[h] = alpha * l_ref[h] + jnp.sum(p, axis=0, keepdims=True)
            acc_ref[h] = alpha * acc_ref[h] + jnp.dot(vt, p.astype(MXU_DTYPE), preferred_element_type=jnp.float32)
            m_ref[h] = m_new
        return carry

    lax.fori_loop(0, nkc, attend_chunk, 0)

    for j in range(GROUP):
        pair_t = jnp.where(krow < HEAD_DIM, acc_ref[j] / l_ref[j], acc_ref[j + GROUP] / l_ref[j + GROUP])
        sl = slice(j * LANE, (j + 1) * LANE)
        o_ref[:, sl] = (pair_t.T * _silu(ga_ref[:, sl])).astype(o_ref.dtype)


def _attend(iqt, iwt, qzt, ga, ik, k, vt, toep_t, kk):
    bsz, nq = qzt.shape[:2]
    nc = k.shape[1]
    idx_bits = max(1, int(nc * KC - 1).bit_length())
    tile_t = lambda a: pl.BlockSpec((None, None) + a.shape[2:], lambda b, i: (b, i, 0, 0))
    whole = lambda a: pl.BlockSpec((None,) + a.shape[1:], lambda b, i: (b,) + (0,) * (a.ndim - 1))
    rows = pl.BlockSpec((None, TQ, N_HEADS * HEAD_DIM), lambda b, i: (b, i, 0))
    return pl.pallas_call(
        functools.partial(_attend_kernel, kk=kk, idx_bits=idx_bits),
        grid=(bsz, nq),
        in_specs=[tile_t(iqt), tile_t(iwt), tile_t(qzt), rows, whole(ik), whole(k), whole(vt),
                  pl.BlockSpec(toep_t.shape, lambda b, i: (0, 0, 0, 0))],
        out_specs=rows,
        out_shape=jax.ShapeDtypeStruct((bsz, nq * TQ, N_HEADS * HEAD_DIM), MXU_DTYPE),
        scratch_shapes=[pltpu.VMEM((nc, KC, TQ), jnp.int32), pltpu.VMEM((nc, KC, TQ), jnp.int32),
                        pltpu.VMEM((nc, KC, TQ), jnp.float32), pltpu.VMEM((N_HEADS, 1, TQ), jnp.float32),
                        pltpu.VMEM((N_HEADS, 1, TQ), jnp.float32), pltpu.VMEM((N_HEADS, LANE, TQ), jnp.float32)],
        compiler_params=_cparams(2),
        name="attend",
    )(iqt, iwt, qzt, ga, ik, k, vt, toep_t)


def _out_kernel(x_ref, mc_ref, ma_ref, wc_ref, wa_ref, o_ref):
    o_ref[...] = (x_ref[...]
                  + jnp.dot(mc_ref[...], wc_ref[...], preferred_element_type=jnp.float32)
                  + jnp.dot(ma_ref[...], wa_ref[...], preferred_element_type=jnp.float32))


def _out_final_kernel(x_ref, mc_ref, ma_ref, wc_ref, wa_ref, g_ref, o_ref, y_ref):
    x = (x_ref[...]
         + jnp.dot(mc_ref[...], wc_ref[...], preferred_element_type=jnp.float32)
         + jnp.dot(ma_ref[...], wa_ref[...], preferred_element_type=jnp.float32))
    o_ref[...] = x
    ms = jnp.mean(x * x, axis=-1, keepdims=True)
    y_ref[...] = x * lax.rsqrt(ms + EPS) * g_ref[...]


def _out_proj(x, mc, ma, wc, wa, tm, final_g=None):
    r, d = x.shape
    c = mc.shape[1]
    rows = lambda w: pl.BlockSpec((tm, w), lambda i: (i, 0))
    full = lambda a: pl.BlockSpec(a.shape, lambda i: (0, 0))
    if final_g is None:
        return pl.pallas_call(
            _out_kernel, grid=(r // tm,),
            in_specs=[rows(d), rows(c), rows(c), full(wc), full(wa)],
            out_specs=rows(d), out_shape=jax.ShapeDtypeStruct((r, d), jnp.float32),
            compiler_params=_cparams(1), name="out",
        )(x, mc, ma, wc, wa)
    g = final_g.reshape(1, d)
    return pl.pallas_call(
        _out_final_kernel, grid=(r // tm,),
        in_specs=[rows(d), rows(c), rows(c), full(wc), full(wa), full(g)],
        out_specs=[rows(d), rows(d)], out_shape=[jax.ShapeDtypeStruct((r, d), jnp.float32)] * 2,
        compiler_params=_cparams(1), name="out_final",
    )(x, mc, ma, wc, wa, g)


def _sconv_kernel(st_ref, u_ref, gc_ref, w_ref, b_ref, lg_ref, lb_ref, o_ref):
    acc = jnp.zeros(u_ref.shape, jnp.float32)
    for j in range(CONV_W - 1):
        acc = acc + w_ref[j:j + 1, :] * st_ref[j]
    acc = acc + w_ref[CONV_W - 1:CONV_W, :] * u_ref[...]
    y = acc + b_ref[...]
    mu = jnp.mean(y, axis=-1, keepdims=True)
    dev = y - mu
    var = jnp.mean(dev * dev, axis=-1, keepdims=True)
    yn = dev * lax.rsqrt(var + LN_EPS) * lg_ref[...] + lb_ref[...]
    o_ref[...] = (_silu(yn) * _silu(gc_ref[...])).astype(o_ref.dtype)


def _sample_conv(state_t, u, gc, w, b, lg, lb):
    db, c = u.shape
    row = lambda a: a.reshape(1, c)
    full = lambda a: pl.BlockSpec(a.shape, lambda i: (0,) * a.ndim)
    args = (state_t, u, gc, w, row(b), row(lg), row(lb))
    return pl.pallas_call(
        _sconv_kernel, grid=(1,), in_specs=[full(a) for a in args],
        out_specs=pl.BlockSpec((db, c), lambda i: (0, 0)),
        out_shape=jax.ShapeDtypeStruct((db, c), MXU_DTYPE),
        compiler_params=_cparams(1), name="sconv",
    )(*args)


def _dec_score_kernel(pt_ref, iq_ref, iw_ref, ikn_ref, *refs, n_pages, width):
    del pt_ref
    pages, o_ref = refs[:n_pages], refs[n_pages]
    iq = iq_ref[...][:, :IDX_DIM]
    w = iw_ref[...]
    pieces = []
    for p in range(n_pages):
        s = lax.dot_general(iq, pages[p][...].astype(MXU_DTYPE), (((1,), (1,)), ((), ())),
                            preferred_element_type=jnp.float32)
        pieces.append(jnp.sum(w * jnp.maximum(s, 0.0), axis=0, keepdims=True))
    ikn = ikn_ref[...][:, :IDX_DIM].astype(MXU_DTYPE).astype(jnp.float32)
    sn = jnp.sum(iq.astype(jnp.float32) * ikn, axis=-1, keepdims=True)
    scn = jnp.sum(w * jnp.maximum(sn, 0.0), axis=0, keepdims=True)
    page = pieces[0].shape[1]
    lane = lax.broadcasted_iota(jnp.int32, (1, width - n_pages * page), 1)
    pieces.append(jnp.where(lane == 0, scn, -jnp.inf))
    o_ref[...] = jnp.concatenate(pieces, axis=1)


def _dec_scores(page_table, iq3, iw3, tail3, cache_ik, layer, width):
    db, n_pages = page_table.shape
    page, di = cache_ik.shape[2:]
    seq = lambda a: pl.BlockSpec((None,) + a.shape[1:], lambda b, pt: (b, 0, 0))
    page_spec = lambda p: pl.BlockSpec((None, None, page, di), lambda b, pt: (layer, pt[b, p], 0, 0))
    return pl.pallas_call(
        functools.partial(_dec_score_kernel, n_pages=n_pages, width=width),
        grid_spec=pltpu.PrefetchScalarGridSpec(
            num_scalar_prefetch=1, grid=(db,),
            in_specs=[seq(iq3), seq(iw3), seq(tail3)] + [page_spec(p) for p in range(n_pages)],
            out_specs=pl.BlockSpec((None, 1, width), lambda b, pt: (b, 0, 0))),
        out_shape=jax.ShapeDtypeStruct((db, 1, width), jnp.float32),
        compiler_params=_cparams(1), name="dec_score",
    )(page_table, iq3, iw3, tail3, *([cache_ik] * n_pages))


def _dec_select_kernel(s_ref, o_ref, keys_ref, eidx_ref, madd_ref, *, kk, idx_bits):
    nc = keys_ref.shape[0]
    for c in range(nc):
        keys_ref[c] = _to_key(s_ref[c * KC:(c + 1) * KC, :])
    _select(keys_ref, eidx_ref, madd_ref, nc, kk, idx_bits)
    for c in range(nc):
        o_ref[c * KC:(c + 1) * KC, :] = madd_ref[c]


def _dec_select(scores_t, kk):
    width, db = scores_t.shape
    nc = width // KC
    idx_bits = max(1, int(width - 1).bit_length())
    spec = pl.BlockSpec((width, db), lambda i: (0, 0))
    return pl.pallas_call(
        functools.partial(_dec_select_kernel, kk=kk, idx_bits=idx_bits),
        grid=(1,), in_specs=[spec], out_specs=spec,
        out_shape=jax.ShapeDtypeStruct((width, db), jnp.float32),
        scratch_shapes=[pltpu.VMEM((nc, KC, db), jnp.int32), pltpu.VMEM((nc, KC, db), jnp.int32),
                        pltpu.VMEM((nc, KC, db), jnp.float32)],
        compiler_params=_cparams(1), name="dec_select",
    )(scores_t)


def _dec_attend_kernel(pt_ref, qz_ref, kvn_ref, madd_ref, bias_ref, ga_ref, *refs, n_pages):
    del pt_ref
    kpages, vpages, o_ref = refs[:n_pages], refs[n_pages:2 * n_pages], refs[2 * n_pages]
    qz = qz_ref[...]
    page = kpages[0].shape[0]
    past = n_pages * page
    nt = (((1,), (1,)), ((), ()))
    logits = []
    for p in range(n_pages):
        sl = slice(p * page, (p + 1) * page)
        lg = lax.dot_general(qz, kpages[p][...].astype(MXU_DTYPE), nt, preferred_element_type=jnp.float32)
        logits.append(lg + bias_ref[:, sl] + madd_ref[:, sl])
    kvn = kvn_ref[...].astype(MXU_DTYPE).astype(jnp.float32)
    lgn = jnp.sum(qz.astype(jnp.float32) * kvn[:, :LANE], axis=-1, keepdims=True)
    lgn = lgn + bias_ref[:, past:past + 1] + madd_ref[:, past:past + 1]
    m = lgn
    for lg in logits:
        m = jnp.maximum(m, jnp.max(lg, axis=-1, keepdims=True))
    pn = jnp.exp(lgn - m)
    den = pn
    o = pn.astype(MXU_DTYPE).astype(jnp.float32) * kvn[:, LANE:]
    for p in range(n_pages):
        pp = jnp.exp(logits[p] - m)
        den = den + jnp.sum(pp, axis=-1, keepdims=True)
        o = o + jnp.dot(pp.astype(MXU_DTYPE), vpages[p][...].astype(MXU_DTYPE), preferred_element_type=jnp.float32)
    o = o / den
    lane = lax.broadcasted_iota(jnp.int32, (GROUP, LANE), 1)
    pair = jnp.where(lane < HEAD_DIM, o[:GROUP], o[GROUP:])
    o_ref[...] = (pair * _silu(ga_ref[...])).astype(o_ref.dtype)


def _dec_attend(page_table, qz3, kvn3, madd3, bias, ga3, cache_k, cache_v, layer):
    db, n_pages = page_table.shape
    page, feat = cache_k.shape[2:]
    seq = lambda a: pl.BlockSpec((None,) + a.shape[1:], lambda b, pt: (b, 0, 0))
    page_spec = lambda p: pl.BlockSpec((None, None, page, feat), lambda b, pt: (layer, pt[b, p], 0, 0))
    return pl.pallas_call(
        functools.partial(_dec_attend_kernel, n_pages=n_pages),
        grid_spec=pltpu.PrefetchScalarGridSpec(
            num_scalar_prefetch=1, grid=(db,),
            in_specs=[seq(qz3), seq(kvn3), seq(madd3), pl.BlockSpec(bias.shape, lambda b, pt: (0, 0)), seq(ga3)]
                     + [page_spec(p) for p in range(n_pages)] * 2,
            out_specs=pl.BlockSpec((None, GROUP, LANE), lambda b, pt: (b, 0, 0))),
        out_shape=jax.ShapeDtypeStruct((db, GROUP, LANE), MXU_DTYPE),
        compiler_params=_cparams(1), name="dec_attend",
    )(page_table, qz3, kvn3, madd3, bias, ga3, *([cache_k] * n_pages), *([cache_v] * n_pages))


def _round_up(x, m):
    return -(-x // m) * m


def _row_tile(rows):
    for f in (5, 4, 3, 2, 1):
        if rows % (f * TQ) == 0:
            return f * TQ
    return rows


def kernel(x_prompt, x_sample, cache_k, cache_v, cache_idx_k, state_conv, page_table, meta_tokens, rel_bias,
           norm_g, w_in, conv_w, conv_b, conv_ln_g, conv_ln_b, w_out, final_norm_g):
    bsz, seq, d = x_prompt.shape
    depth = w_in.shape[0]
    lp = seq + N_META
    lq = _round_up(lp, TQ)
    lk = _round_up(lp, KC)
    nq, nc = lq // TQ, lk // KC
    kk_p = min(TOPK_MAX, lp // 4)
    db = x_sample.shape[0]
    n_pool, page = cache_k.shape[1:3]
    n_pages = page_table.shape[1]
    past = n_pages * page
    kk_s = min(TOPK_MAX, (past + 1) // 4)
    width_s = _round_up(past + 1, KC)
    feat = N_KV_HEADS * HEAD_DIM
    tm = _row_tile(bsz * lq)

    xp = jnp.concatenate([jnp.broadcast_to(meta_tokens[None].astype(x_prompt.dtype), (bsz, N_META, d)), x_prompt,
                          jnp.zeros((bsz, lq - lp, d), x_prompt.dtype)], axis=1).reshape(bsz * lq, d)
    xs = x_sample.reshape(db, d)
    toep_t = _bias_tiles(rel_bias)
    bias_s = jnp.moveaxis(rel_bias[_t5_bucket(past - jnp.arange(width_s, dtype=jnp.int32))], -1, 0)
    cache_k4 = cache_k.reshape(depth, n_pool, page, feat)
    cache_v4 = cache_v.reshape(depth, n_pool, page, feat)

    def feature_major(a, heads):
        a = a.reshape(bsz, nq, TQ, heads, LANE)
        return jnp.transpose(a, (0, 1, 4, 3, 2)).reshape(bsz, nq, LANE, heads * TQ)

    def key_chunks(a):
        return jnp.pad(a, ((0, 0), (0, lk - lq), (0, 0))).astype(MXU_DTYPE).reshape(bsz, nc, KC, LANE)

    kp, vp, ikp, cp, ksm, vsm, iks, cs = ([] for _ in range(8))
    yp = ys = None
    for l in range(depth):
        w = _prep_w_in(w_in[l])
        wc = w_out[l, :C_CONV].astype(MXU_DTYPE)
        wa = w_out[l, C_CONV:][_PERM].astype(MXU_DTYPE)
        last = l == depth - 1

        u, gc, qz, kv, ga, iqz, tail = _project(xp, norm_g[l], w, tm)
        b3 = lambda a: a.reshape(bsz, lq, a.shape[-1])
        u3, kv3, tail3 = b3(u), b3(kv), b3(tail)
        mixc = _conv_branch(u3, b3(gc), conv_w[l], conv_b[l], conv_ln_g[l], conv_ln_b[l])
        lane = jnp.arange(LANE)
        ik = key_chunks(jnp.where(lane < IDX_DIM, tail3, 0.0))
        iwt = jnp.swapaxes(tail3[..., IDX_DIM:IDX_DIM + N_IDX_HEADS].reshape(bsz, nq, TQ, N_IDX_HEADS), 2, 3)
        vt = jnp.swapaxes(key_chunks(kv3[..., feat:]), 2, 3)
        mixa = _attend(feature_major(iqz, N_IDX_HEADS), iwt, feature_major(qz, N_HEADS), b3(ga),
                       ik, key_chunks(kv3[..., :feat]), vt, toep_t, kk_p)
        res = _out_proj(xp, mixc.reshape(bsz * lq, C_CONV), mixa.reshape(bsz * lq, -1), wc, wa, tm,
                        final_norm_g if last else None)
        xp, yp = res if last else (res, None)
        kp.append(kv3[:, :lp, :feat].reshape(bsz, lp, N_KV_HEADS, HEAD_DIM))
        vp.append(kv3[:, :lp, feat:].reshape(bsz, lp, N_KV_HEADS, HEAD_DIM))
        ikp.append(tail3[:, :lp, :IDX_DIM])
        cp.append(u3[:, lp - (CONV_W - 1):lp])

        u, gc, qz, kv, ga, iqz, tail = _project(xs, norm_g[l], w, db)
        mixc = _sample_conv(jnp.transpose(state_conv[l], (1, 0, 2)), u, gc,
                            conv_w[l], conv_b[l], conv_ln_g[l], conv_ln_b[l])
        iw3 = tail[:, IDX_DIM:IDX_DIM + N_IDX_HEADS].reshape(db, N_IDX_HEADS, 1)
        scores = _dec_scores(page_table, iqz.reshape(db, N_IDX_HEADS, LANE), iw3, tail.reshape(db, 1, LANE),
                             cache_idx_k, l, width_s)
        madd = _dec_select(scores.reshape(db, width_s).T, kk_s).T
        mixa = _dec_attend(page_table, qz.reshape(db, N_HEADS, LANE), kv.reshape(db, 1, 2 * feat),
                           madd.reshape(db, 1, width_s), bias_s, ga.reshape(db, GROUP, LANE),
                           cache_k4, cache_v4, l)
        res = _out_proj(xs, mixc, mixa.reshape(db, -1), wc, wa, db, final_norm_g if last else None)
        xs, ys = res if last else (res, None)
        ksm.append(kv[:, :feat].reshape(db, 1, N_KV_HEADS, HEAD_DIM))
        vsm.append(kv[:, feat:].reshape(db, 1, N_KV_HEADS, HEAD_DIM))
        iks.append(tail[:, None, :IDX_DIM])
        cs.append(jnp.concatenate([state_conv[l][:, 1:], u[:, None]], axis=1))

    y_prompt = yp.reshape(bsz, lq, d)[:, N_META:lp]
    y_sample = ys.reshape(db, 1, d)
    return (y_prompt, y_sample, jnp.stack(kp), jnp.stack(vp), jnp.stack(ikp), jnp.stack(cp),
            jnp.stack(ksm), jnp.stack(vsm), jnp.stack(iks), jnp.stack(cs))
```

```python
import functools
import math

import numpy as np
import jax
import jax.numpy as jnp
from jax import lax
from jax.experimental import pallas as pl
from jax.experimental.pallas import tpu as pltpu

N_HEADS = 8
N_KV_HEADS = 2
GROUP = N_HEADS // N_KV_HEADS
HEAD_DIM = 64
N_IDX_HEADS = 4
IDX_DIM = 64
C_CONV = 512
CONV_W = 31
TOPK_MAX = 256
N_META = 16
NUM_BUCKETS = 32
MAX_DISTANCE = 128
EPS = 1e-6
LN_EPS = 1e-5

LANE = 128
ROWS = 8
TQ = 128
KC = 512
SUB = KC // TQ
N_ACC = 4
BITS_PER_CHECK = 4
HALO = 32
CONV_SUB = 64
MXU_DTYPE = jnp.bfloat16
NEG = -1e30
LOG2E = 1.4426950408889634
INT_MIN = -2 ** 31
KEY_NEG_INF = -0x7F800000
IDX_BIG = 0x3FFFFFFF
VMEM_LIMIT = 56 * 1024 * 1024

_Q_OFF = 3 * C_CONV
_KV_OFF = _Q_OFF + N_HEADS * LANE
_GA_OFF = _KV_OFF + 2 * N_KV_HEADS * HEAD_DIM
_IQ_OFF = _GA_OFF + N_HEADS * HEAD_DIM
_TAIL_OFF = _IQ_OFF + N_IDX_HEADS * LANE
_W_COLS = _TAIL_OFF + LANE

_HEAD_ORDER = [h for j in range(GROUP) for h in (j, j + GROUP)]
_PERM = np.concatenate([np.arange(h * HEAD_DIM, (h + 1) * HEAD_DIM) for h in _HEAD_ORDER])

_PROJ_SIZES = (C_CONV, C_CONV, C_CONV, N_HEADS * HEAD_DIM, N_KV_HEADS * HEAD_DIM, N_KV_HEADS * HEAD_DIM,
               N_HEADS * HEAD_DIM, N_IDX_HEADS * IDX_DIM, IDX_DIM, N_IDX_HEADS)
_PROJ_SPLITS = tuple(int(s) for s in np.cumsum(_PROJ_SIZES)[:-1])


def _cparams(n_axes):
    return pltpu.CompilerParams(dimension_semantics=("arbitrary",) * n_axes, vmem_limit_bytes=VMEM_LIMIT)


def _silu(x):
    return x * jax.nn.sigmoid(x)


def _prep_w_in(w):
    ua, ub, gc, q, k, v, ga, iq, ik, iw = jnp.split(w, _PROJ_SPLITS, axis=1)
    d = w.shape[0]
    z = jnp.zeros((d, HEAD_DIM), w.dtype)
    qz = []
    for h in range(N_HEADS):
        qh = q[:, h * HEAD_DIM:(h + 1) * HEAD_DIM]
        qz += [qh, z] if h // GROUP == 0 else [z, qh]
    iqz = []
    for h in range(N_IDX_HEADS):
        iqz += [iq[:, h * IDX_DIM:(h + 1) * IDX_DIM], z]
    tail = jnp.concatenate([ik, iw, jnp.zeros((d, LANE - IDX_DIM - N_IDX_HEADS), w.dtype)], axis=1)
    out = jnp.concatenate([ua, ub, gc] + qz + [k, v, ga[:, _PERM]] + iqz + [tail], axis=1)
    assert out.shape[1] == _W_COLS
    return out.astype(MXU_DTYPE)


def _t5_bucket(rel):
    n = jnp.maximum(rel, 0)
    max_exact = NUM_BUCKETS // 2
    large = max_exact + (jnp.log(jnp.maximum(n, 1).astype(jnp.float32) / max_exact)
                         / math.log(MAX_DISTANCE / max_exact)
                         * (NUM_BUCKETS - max_exact)).astype(jnp.int32)
    large = jnp.minimum(large, NUM_BUCKETS - 1)
    return jnp.where(n < max_exact, n, large)


def _bias_tiles(rel_bias):
    far = rel_bias[NUM_BUCKETS - 1]
    i = jnp.arange(TQ, dtype=jnp.int32)
    tiles = []
    for delta in range(2):
        dist = delta * TQ + i[None, :] - i[:, None]
        tiles.append(jnp.moveaxis(rel_bias[_t5_bucket(dist)] - far, -1, 0) * LOG2E)
    tiles.append(jnp.zeros_like(tiles[0]))
    return jnp.stack(tiles).astype(jnp.float32)


def _proj_kernel(x_ref, g_ref, w_ref, u_ref, gc_ref, qz_ref, kv_ref, ga_ref, iqz_ref, tail_ref):
    x = x_ref[...]
    ms = jnp.mean(x * x, axis=-1, keepdims=True)
    xn = (x * lax.rsqrt(ms + EPS) * g_ref[...]).astype(MXU_DTYPE)

    def mm(lo, hi):
        return jnp.dot(xn, w_ref[:, lo:hi], preferred_element_type=jnp.float32)

    u_ref[...] = mm(0, C_CONV) * jax.nn.sigmoid(mm(C_CONV, 2 * C_CONV))
    gc_ref[...] = mm(2 * C_CONV, 3 * C_CONV)
    qz_ref[...] = (mm(_Q_OFF, _KV_OFF) * (HEAD_DIM ** -0.5 * LOG2E)).astype(qz_ref.dtype)
    kv_ref[...] = mm(_KV_OFF, _GA_OFF)
    ga_ref[...] = mm(_GA_OFF, _IQ_OFF)
    iqz_ref[...] = (mm(_IQ_OFF, _TAIL_OFF) * (IDX_DIM ** -0.5)).astype(iqz_ref.dtype)
    t = mm(_TAIL_OFF, _W_COLS)
    lane = lax.broadcasted_iota(jnp.int32, t.shape, 1)
    tail_ref[...] = t * jnp.where(lane >= IDX_DIM, N_IDX_HEADS ** -0.5, 1.0)


def _project(x, g, w, tm):
    r, d = x.shape
    assert r % tm == 0

    def rows(c):
        return pl.BlockSpec((tm, c), lambda i: (i, 0))

    widths = (C_CONV, C_CONV, N_HEADS * LANE, _GA_OFF - _KV_OFF, N_HEADS * HEAD_DIM, N_IDX_HEADS * LANE, LANE)
    dtypes = (jnp.float32, jnp.float32, MXU_DTYPE, jnp.float32, jnp.float32, MXU_DTYPE, jnp.float32)
    return pl.pallas_call(
        _proj_kernel,
        grid=(r // tm,),
        in_specs=[rows(d), pl.BlockSpec((1, d), lambda i: (0, 0)), pl.BlockSpec((d, _W_COLS), lambda i: (0, 0))],
        out_specs=[rows(c) for c in widths],
        out_shape=[jax.ShapeDtypeStruct((r, c), t) for c, t in zip(widths, dtypes)],
        compiler_params=_cparams(1),
        name="proj",
    )(x, g.reshape(1, d), w)


def _conv_kernel(prev_ref, cur_ref, gc_ref, w_ref, b_ref, lg_ref, lb_ref, o_ref, ext_ref):
    i = pl.program_id(1)
    ext_ref[0:HALO, :] = jnp.where(i > 0, prev_ref[...], 0.0)
    ext_ref[HALO:HALO + TQ, :] = cur_ref[...]
    off = HALO - (CONV_W - 1)
    for r0 in range(0, TQ, CONV_SUB):
        acc = jnp.zeros((CONV_SUB, C_CONV), jnp.float32)
        for j in range(CONV_W):
            acc = acc + w_ref[j:j + 1, :] * ext_ref[r0 + off + j:r0 + off + j + CONV_SUB, :]
        y = acc + b_ref[...]
        mu = jnp.mean(y, axis=-1, keepdims=True)
        dev = y - mu
        var = jnp.mean(dev * dev, axis=-1, keepdims=True)
        yn = dev * lax.rsqrt(var + LN_EPS) * lg_ref[...] + lb_ref[...]
        o_ref[r0:r0 + CONV_SUB, :] = (_silu(yn) * _silu(gc_ref[r0:r0 + CONV_SUB, :])).astype(o_ref.dtype)


def _conv_branch(u, gc, w, b, lg, lb):
    bsz, lq, c = u.shape
    per = TQ // HALO
    row = lambda a: a.reshape(1, c)
    vec = pl.BlockSpec((1, c), lambda bi, i: (0, 0))
    tile = pl.BlockSpec((None, TQ, c), lambda bi, i: (bi, i, 0))
    return pl.pallas_call(
        _conv_kernel,
        grid=(bsz, lq // TQ),
        in_specs=[pl.BlockSpec((None, HALO, c), lambda bi, i: (bi, jnp.maximum(i * per - 1, 0), 0)),
                  tile, tile, pl.BlockSpec((CONV_W, c), lambda bi, i: (0, 0)), vec, vec, vec],
        out_specs=tile,
        out_shape=jax.ShapeDtypeStruct((bsz, lq, c), MXU_DTYPE),
        scratch_shapes=[pltpu.VMEM((HALO + TQ, c), jnp.float32)],
        compiler_params=_cparams(2),
        name="conv",
    )(u, u, gc, w, row(b), row(lg), row(lb))


def _to_key(s):
    bits = lax.bitcast_convert_type(s, jnp.int32)
    return jnp.where(bits < 0, jnp.int32(INT_MIN) - bits, bits)


def _count(ref, nkc, pred):
    q = ref.shape[2]

    def body(c, accs):
        accs = list(accs)
        for n, r in enumerate(range(0, KC, ROWS)):
            accs[n % N_ACC] = accs[n % N_ACC] + jnp.where(pred(ref[c, r:r + ROWS, :]), 1.0, 0.0)
        return tuple(accs)

    accs = lax.fori_loop(0, nkc, body, (jnp.zeros((ROWS, q), jnp.float32),) * N_ACC)
    return jnp.sum(sum(accs[1:], accs[0]), axis=0, keepdims=True)


def _threshold(keys_ref, eidx_ref, nkc, kk, idx_bits):
    q = keys_ref.shape[2]
    kf = jnp.float32(kk)

    def value_bits(state):
        g, thr, cnt_thr, done = state
        for b in range(BITS_PER_CHECK):
            cand = thr ^ lax.shift_left(jnp.int32(1), 31 - (g * BITS_PER_CHECK + b))
            cnt = _count(keys_ref, nkc, lambda x: x >= cand)
            take = (cnt >= kf) & (done == 0.0)
            thr = jnp.where(take, cand, thr)
            cnt_thr = jnp.where(take, cnt, cnt_thr)
            done = jnp.where(take & (cnt == kf), 1.0, done)
        return g + 1, thr, cnt_thr, done

    def unfinished(state):
        g, _, _, done = state
        return (g < 32 // BITS_PER_CHECK) & (jnp.min(done) == 0.0)

    _, thr, cnt_thr, _ = lax.while_loop(
        unfinished, value_bits,
        (0, jnp.full((1, q), INT_MIN, jnp.int32), jnp.full((1, q), kf + 1.0, jnp.float32),
         jnp.zeros((1, q), jnp.float32)))
    admissible = thr != KEY_NEG_INF
    surplus = (cnt_thr > kf) & admissible

    @pl.when(jnp.max(jnp.where(surplus, 1.0, 0.0)) > 0.0)
    def _():
        def mark_ties(c, n_gt):
            row = lax.broadcasted_iota(jnp.int32, (ROWS, q), 0)
            for r in range(0, KC, ROWS):
                x = keys_ref[c, r:r + ROWS, :]
                n_gt = n_gt + jnp.where(x > thr, 1.0, 0.0)
                eidx_ref[c, r:r + ROWS, :] = jnp.where(x == thr, c * KC + r + row, IDX_BIG)
            return n_gt

        n_gt = lax.fori_loop(0, nkc, mark_ties, jnp.zeros((ROWS, q), jnp.float32))
        need = kf - jnp.sum(n_gt, axis=0, keepdims=True)

        def index_bit(i, cut):
            cand = cut | lax.shift_left(jnp.int32(1), idx_bits - 1 - i)
            cnt = _count(eidx_ref, nkc, lambda x: x < cand)
            return jnp.where(cnt < need, cand, cut)

        cut = lax.fori_loop(0, idx_bits, index_bit, jnp.zeros((1, q), jnp.int32))
        cut = jnp.where(surplus, cut, IDX_BIG - 1)

        def drop_losers(c, carry):
            for r in range(0, KC, ROWS):
                e = eidx_ref[c, r:r + ROWS, :]
                lost = (e != IDX_BIG) & (e > cut)
                keys_ref[c, r:r + ROWS, :] = jnp.where(lost, thr - 1, keys_ref[c, r:r + ROWS, :])
            return carry

        lax.fori_loop(0, nkc, drop_losers, 0)

    return jnp.where(admissible, thr, KEY_NEG_INF + 1)


def _attend_kernel(iqt_ref, iwt_ref, qzt_ref, ga_ref, ik_ref, k_ref, vt_ref, toep_ref, o_ref,
                   keys_ref, eidx_ref, m_ref, l_ref, acc_ref, lg0_ref, lg1_ref, *, kk, idx_bits):
    qi = pl.program_id(1)
    nkc = qi // SUB + 1
    qpos = qi * TQ + lax.broadcasted_iota(jnp.int32, (TQ, TQ), 1)
    krow = lax.broadcasted_iota(jnp.int32, (TQ, TQ), 0)

    def score_chunk(c, carry):
        iqt = iqt_ref[...]
        w = iwt_ref[...]
        for j in range(SUB):
            st = jnp.dot(ik_ref[c, j * TQ:(j + 1) * TQ, :], iqt, preferred_element_type=jnp.float32)
            s = jnp.zeros((TQ, TQ), jnp.float32)
            for h in range(N_IDX_HEADS):
                s = s + w[h:h + 1, :] * jnp.maximum(st[:, h * TQ:(h + 1) * TQ], 0.0)
            kpos = c * KC + j * TQ + krow
            keys_ref[c, j * TQ:(j + 1) * TQ, :] = _to_key(jnp.where(kpos <= qpos, s, -jnp.inf))
        return carry

    lax.fori_loop(0, nkc, score_chunk, 0)
    thr = _threshold(keys_ref, eidx_ref, nkc, kk, idx_bits)

    m_ref[...] = jnp.full(m_ref.shape, NEG, jnp.float32)
    l_ref[...] = jnp.zeros(l_ref.shape, jnp.float32)
    acc_ref[...] = jnp.zeros(acc_ref.shape, jnp.float32)

    def logits(c, lg_ref):
        lg_ref[...] = jnp.dot(k_ref[c], qzt_ref[...], preferred_element_type=jnp.float32)

    def softmax_pv(c, lg_ref, near):
        md = jnp.where(keys_ref[c] >= thr, 0.0, NEG)
        md2 = jnp.concatenate([md, md], axis=1)
        if near:
            delta = [jnp.clip(qi - (c * SUB + j), 0, 2) for j in range(SUB)]
        for pr in range(N_HEADS // 2):
            kv = pr // (GROUP // 2)
            lg = lg_ref[:, 2 * pr * TQ:(2 * pr + 2) * TQ] + md2
            if near:
                lg = lg + jnp.concatenate(
                    [jnp.concatenate([toep_ref[delta[j], 2 * pr], toep_ref[delta[j], 2 * pr + 1]], axis=1)
                     for j in range(SUB)], axis=0)
            m_old = m_ref[pr]
            m_new = jnp.maximum(m_old, jnp.max(lg, axis=0, keepdims=True))
            alpha = jnp.exp2(m_old - m_new)
            p = jnp.exp2(lg - m_new)
            l_ref[pr] = alpha * l_ref[pr] + jnp.sum(p, axis=0, keepdims=True)
            pv = jnp.dot(vt_ref[c, kv * HEAD_DIM:(kv + 1) * HEAD_DIM, :], p.astype(MXU_DTYPE),
                         preferred_element_type=jnp.float32)
            acc_ref[pr] = alpha * acc_ref[pr] + pv
            m_ref[pr] = m_new

    n_far2 = (jnp.maximum(qi - 1, 0) // SUB) // 2

    def far_pair(i, carry):
        logits(2 * i + 1, lg1_ref)
        softmax_pv(2 * i, lg0_ref, near=False)
        logits(2 * i + 2, lg0_ref)
        softmax_pv(2 * i + 1, lg1_ref, near=False)
        return carry

    logits(0, lg0_ref)
    lax.fori_loop(0, n_far2, far_pair, 0)

    def near_chunk(c, carry):
        logits(c, lg1_ref)
        softmax_pv(c, lg1_ref, near=True)
        return carry

    lax.fori_loop(2 * n_far2, nkc, near_chunk, 0)

    for j in range(GROUP):
        half = slice((j % 2) * TQ, (j % 2 + 1) * TQ)
        lo, hi = j // 2, j // 2 + GROUP // 2
        pair_t = jnp.concatenate([acc_ref[lo][:, half] / l_ref[lo][:, half],
                                  acc_ref[hi][:, half] / l_ref[hi][:, half]], axis=0)
        sl = slice(j * LANE, (j + 1) * LANE)
        o_ref[:, sl] = (pair_t.T * _silu(ga_ref[:, sl])).astype(o_ref.dtype)


def _attend(iqt, iwt, qzt, ga, ik, k, vt, toep_t, kk):
    bsz, nq = qzt.shape[:2]
    nc = k.shape[1]
    idx_bits = max(1, int(nc * KC - 1).bit_length())
    tile_t = lambda a: pl.BlockSpec((None, None) + a.shape[2:], lambda b, i: (b, i, 0, 0))
    whole = lambda a: pl.BlockSpec((None,) + a.shape[1:], lambda b, i: (b,) + (0,) * (a.ndim - 1))
    rows = pl.BlockSpec((None, TQ, N_HEADS * HEAD_DIM), lambda b, i: (b, i, 0))
    return pl.pallas_call(
        functools.partial(_attend_kernel, kk=kk, idx_bits=idx_bits),
        grid=(bsz, nq),
        in_specs=[tile_t(iqt), tile_t(iwt), tile_t(qzt), rows, whole(ik), whole(k), whole(vt),
                  pl.BlockSpec(toep_t.shape, lambda b, i: (0, 0, 0, 0))],
        out_specs=rows,
        out_shape=jax.ShapeDtypeStruct((bsz, nq * TQ, N_HEADS * HEAD_DIM), MXU_DTYPE),
        scratch_shapes=[pltpu.VMEM((nc, KC, TQ), jnp.int32), pltpu.VMEM((nc, KC, TQ), jnp.int32),
                        pltpu.VMEM((N_HEADS // 2, 1, 2 * TQ), jnp.float32),
                        pltpu.VMEM((N_HEADS // 2, 1, 2 * TQ), jnp.float32),
                        pltpu.VMEM((N_HEADS // 2, HEAD_DIM, 2 * TQ), jnp.float32),
                        pltpu.VMEM((KC, N_HEADS * TQ), jnp.float32), pltpu.VMEM((KC, N_HEADS * TQ), jnp.float32)],
        compiler_params=_cparams(2),
        name="attend",
    )(iqt, iwt, qzt, ga, ik, k, vt, toep_t)


def _out_kernel(x_ref, mc_ref, ma_ref, wc_ref, wa_ref, o_ref):
    o_ref[...] = (x_ref[...]
                  + jnp.dot(mc_ref[...], wc_ref[...], preferred_element_type=jnp.float32)
                  + jnp.dot(ma_ref[...], wa_ref[...], preferred_element_type=jnp.float32))


def _out_final_kernel(x_ref, mc_ref, ma_ref, wc_ref, wa_ref, g_ref, o_ref, y_ref):
    x = (x_ref[...]
         + jnp.dot(mc_ref[...], wc_ref[...], preferred_element_type=jnp.float32)
         + jnp.dot(ma_ref[...], wa_ref[...], preferred_element_type=jnp.float32))
    o_ref[...] = x
    ms = jnp.mean(x * x, axis=-1, keepdims=True)
    y_ref[...] = x * lax.rsqrt(ms + EPS) * g_ref[...]


def _out_proj(x, mc, ma, wc, wa, tm, final_g=None):
    r, d = x.shape
    c = mc.shape[1]
    rows = lambda w: pl.BlockSpec((tm, w), lambda i: (i, 0))
    full = lambda a: pl.BlockSpec(a.shape, lambda i: (0, 0))
    if final_g is None:
        return pl.pallas_call(
            _out_kernel, grid=(r // tm,),
            in_specs=[rows(d), rows(c), rows(c), full(wc), full(wa)],
            out_specs=rows(d), out_shape=jax.ShapeDtypeStruct((r, d), jnp.float32),
            compiler_params=_cparams(1), name="out",
        )(x, mc, ma, wc, wa)
    g = final_g.reshape(1, d)
    return pl.pallas_call(
        _out_final_kernel, grid=(r // tm,),
        in_specs=[rows(d), rows(c), rows(c), full(wc), full(wa), full(g)],
        out_specs=[rows(d), rows(d)], out_shape=[jax.ShapeDtypeStruct((r, d), jnp.float32)] * 2,
        compiler_params=_cparams(1), name="out_final",
    )(x, mc, ma, wc, wa, g)


def _sconv_kernel(st_ref, u_ref, gc_ref, w_ref, b_ref, lg_ref, lb_ref, o_ref):
    acc = jnp.zeros(u_ref.shape, jnp.float32)
    for j in range(CONV_W - 1):
        acc = acc + w_ref[j:j + 1, :] * st_ref[j]
    acc = acc + w_ref[CONV_W - 1:CONV_W, :] * u_ref[...]
    y = acc + b_ref[...]
    mu = jnp.mean(y, axis=-1, keepdims=True)
    dev = y - mu
    var = jnp.mean(dev * dev, axis=-1, keepdims=True)
    yn = dev * lax.rsqrt(var + LN_EPS) * lg_ref[...] + lb_ref[...]
    o_ref[...] = (_silu(yn) * _silu(gc_ref[...])).astype(o_ref.dtype)


def _sample_conv(state_t, u, gc, w, b, lg, lb):
    db, c = u.shape
    row = lambda a: a.reshape(1, c)
    full = lambda a: pl.BlockSpec(a.shape, lambda i: (0,) * a.ndim)
    args = (state_t, u, gc, w, row(b), row(lg), row(lb))
    return pl.pallas_call(
        _sconv_kernel, grid=(1,), in_specs=[full(a) for a in args],
        out_specs=pl.BlockSpec((db, c), lambda i: (0, 0)),
        out_shape=jax.ShapeDtypeStruct((db, c), MXU_DTYPE),
        compiler_params=_cparams(1), name="sconv",
    )(*args)


def _dec_score_kernel(pt_ref, iq_ref, iw_ref, ikn_ref, *refs, n_pages, width):
    del pt_ref
    pages, o_ref = refs[:n_pages], refs[n_pages]
    iq = iq_ref[...][:, :IDX_DIM]
    w = iw_ref[...]
    pieces = []
    for p in range(n_pages):
        s = jnp.dot(iq, pages[p][...].astype(MXU_DTYPE), preferred_element_type=jnp.float32)
        pieces.append(jnp.sum(w * jnp.maximum(s, 0.0), axis=0, keepdims=True))
    ikn = ikn_ref[...][:, :IDX_DIM].astype(MXU_DTYPE).astype(jnp.float32)
    sn = jnp.sum(iq.astype(jnp.float32) * ikn, axis=-1, keepdims=True)
    scn = jnp.sum(w * jnp.maximum(sn, 0.0), axis=0, keepdims=True)
    page = pieces[0].shape[1]
    lane = lax.broadcasted_iota(jnp.int32, (1, width - n_pages * page), 1)
    pieces.append(jnp.where(lane == 0, scn, -jnp.inf))
    o_ref[...] = jnp.concatenate(pieces, axis=1)


def _dec_scores(page_table, iq3, iw3, tail3, cache_ik, layer, width):
    db, n_pages = page_table.shape
    di, page = cache_ik.shape[2:]
    seq = lambda a: pl.BlockSpec((None,) + a.shape[1:], lambda b, pt: (b, 0, 0))
    page_spec = lambda p: pl.BlockSpec((None, None, di, page), lambda b, pt: (layer, pt[b, p], 0, 0))
    return pl.pallas_call(
        functools.partial(_dec_score_kernel, n_pages=n_pages, width=width),
        grid_spec=pltpu.PrefetchScalarGridSpec(
            num_scalar_prefetch=1, grid=(db,),
            in_specs=[seq(iq3), seq(iw3), seq(tail3)] + [page_spec(p) for p in range(n_pages)],
            out_specs=pl.BlockSpec((None, 1, width), lambda b, pt: (b, 0, 0))),
        out_shape=jax.ShapeDtypeStruct((db, 1, width), jnp.float32),
        compiler_params=_cparams(1), name="dec_score",
    )(page_table, iq3, iw3, tail3, *([cache_ik] * n_pages))


def _dec_select_kernel(s_ref, o_ref, keys_ref, eidx_ref, *, kk, idx_bits):
    nc = keys_ref.shape[0]
    for c in range(nc):
        keys_ref[c] = _to_key(s_ref[c * KC:(c + 1) * KC, :])
    thr = _threshold(keys_ref, eidx_ref, nc, kk, idx_bits)
    for c in range(nc):
        o_ref[c * KC:(c + 1) * KC, :] = jnp.where(keys_ref[c] >= thr, 0.0, NEG)


def _dec_select(scores_t, kk):
    width, db = scores_t.shape
    nc = width // KC
    idx_bits = max(1, int(width - 1).bit_length())
    spec = pl.BlockSpec((width, db), lambda i: (0, 0))
    return pl.pallas_call(
        functools.partial(_dec_select_kernel, kk=kk, idx_bits=idx_bits),
        grid=(1,), in_specs=[spec], out_specs=spec,
        out_shape=jax.ShapeDtypeStruct((width, db), jnp.float32),
        scratch_shapes=[pltpu.VMEM((nc, KC, db), jnp.int32), pltpu.VMEM((nc, KC, db), jnp.int32)],
        compiler_params=_cparams(1), name="dec_select",
    )(scores_t)


def _dec_attend_kernel(pt_ref, qz_ref, kvn_ref, madd_ref, bias_ref, ga_ref, *refs, n_pages):
    del pt_ref
    kpages, vpages, o_ref = refs[:n_pages], refs[n_pages:2 * n_pages], refs[2 * n_pages]
    qz = qz_ref[...]
    page = kpages[0].shape[1]
    past = n_pages * page
    nt = (((1,), (1,)), ((), ()))
    logits = []
    for p in range(n_pages):
        sl = slice(p * page, (p + 1) * page)
        lg = jnp.dot(qz, kpages[p][...].astype(MXU_DTYPE), preferred_element_type=jnp.float32)
        logits.append(lg + bias_ref[:, sl] + madd_ref[:, sl])
    kvn = kvn_ref[...].astype(MXU_DTYPE).astype(jnp.float32)
    lgn = jnp.sum(qz.astype(jnp.float32) * kvn[:, :LANE], axis=-1, keepdims=True)
    lgn = lgn + bias_ref[:, past:past + 1] + madd_ref[:, past:past + 1]
    m = lgn
    for lg in logits:
        m = jnp.maximum(m, jnp.max(lg, axis=-1, keepdims=True))
    pn = jnp.exp2(lgn - m)
    den = pn
    o = pn.astype(MXU_DTYPE).astype(jnp.float32) * kvn[:, LANE:]
    for p in range(n_pages):
        pp = jnp.exp2(logits[p] - m)
        den = den + jnp.sum(pp, axis=-1, keepdims=True)
        o = o + lax.dot_general(pp.astype(MXU_DTYPE), vpages[p][...].astype(MXU_DTYPE), nt,
                                preferred_element_type=jnp.float32)
    o = o / den
    lane = lax.broadcasted_iota(jnp.int32, (GROUP, LANE), 1)
    pair = jnp.where(lane < HEAD_DIM, o[:GROUP], o[GROUP:])
    o_ref[...] = (pair * _silu(ga_ref[...])).astype(o_ref.dtype)


def _dec_attend(page_table, qz3, kvn3, madd3, bias, ga3, cache_k, cache_v, layer):
    db, n_pages = page_table.shape
    feat, page = cache_k.shape[2:]
    seq = lambda a: pl.BlockSpec((None,) + a.shape[1:], lambda b, pt: (b, 0, 0))
    page_spec = lambda p: pl.BlockSpec((None, None, feat, page), lambda b, pt: (layer, pt[b, p], 0, 0))
    return pl.pallas_call(
        functools.partial(_dec_attend_kernel, n_pages=n_pages),
        grid_spec=pltpu.PrefetchScalarGridSpec(
            num_scalar_prefetch=1, grid=(db,),
            in_specs=[seq(qz3), seq(kvn3), seq(madd3), pl.BlockSpec(bias.shape, lambda b, pt: (0, 0)), seq(ga3)]
                     + [page_spec(p) for p in range(n_pages)] * 2,
            out_specs=pl.BlockSpec((None, GROUP, LANE), lambda b, pt: (b, 0, 0))),
        out_shape=jax.ShapeDtypeStruct((db, GROUP, LANE), MXU_DTYPE),
        compiler_params=_cparams(1), name="dec_attend",
    )(page_table, qz3, kvn3, madd3, bias, ga3, *([cache_k] * n_pages), *([cache_v] * n_pages))


def _round_up(x, m):
    return -(-x // m) * m


def _row_tile(rows):
    for f in (5, 4, 3, 2, 1):
        if rows % (f * TQ) == 0:
            return f * TQ
    return rows


def kernel(x_prompt, x_sample, cache_k, cache_v, cache_idx_k, state_conv, page_table, meta_tokens, rel_bias,
           norm_g, w_in, conv_w, conv_b, conv_ln_g, conv_ln_b, w_out, final_norm_g):
    bsz, seq, d = x_prompt.shape
    depth = w_in.shape[0]
    lp = seq + N_META
    lq = _round_up(lp, TQ)
    lk = _round_up(lp, KC)
    nq, nc = lq // TQ, lk // KC
    kk_p = min(TOPK_MAX, lp // 4)
    db = x_sample.shape[0]
    n_pool, page = cache_k.shape[1:3]
    n_pages = page_table.shape[1]
    past = n_pages * page
    kk_s = min(TOPK_MAX, (past + 1) // 4)
    width_s = _round_up(past + 1, KC)
    feat = N_KV_HEADS * HEAD_DIM
    tm = _row_tile(bsz * lq)

    xp = jnp.concatenate([jnp.broadcast_to(meta_tokens[None].astype(x_prompt.dtype), (bsz, N_META, d)), x_prompt,
                          jnp.zeros((bsz, lq - lp, d), x_prompt.dtype)], axis=1).reshape(bsz * lq, d)
    xs = x_sample.reshape(db, d)
    toep_t = _bias_tiles(rel_bias)
    bias_s = jnp.moveaxis(rel_bias[_t5_bucket(past - jnp.arange(width_s, dtype=jnp.int32))], -1, 0) * LOG2E
    cache_k4 = jnp.transpose(cache_k, (0, 1, 3, 4, 2)).reshape(depth, n_pool, feat, page)
    cache_v4 = jnp.transpose(cache_v, (0, 1, 3, 4, 2)).reshape(depth, n_pool, feat, page)
    cache_ik4 = jnp.swapaxes(cache_idx_k, 2, 3)

    def feature_major(a, heads):
        a = a.reshape(bsz, nq, TQ, heads, LANE)
        return jnp.transpose(a, (0, 1, 4, 3, 2)).reshape(bsz, nq, LANE, heads * TQ)

    def key_chunks(a):
        return jnp.pad(a, ((0, 0), (0, lk - lq), (0, 0))).astype(MXU_DTYPE).reshape(bsz, nc, KC, LANE)

    kp, vp, ikp, cp, ksm, vsm, iks, cs = ([] for _ in range(8))
    yp = ys = None
    for l in range(depth):
        w = _prep_w_in(w_in[l])
        wc = w_out[l, :C_CONV].astype(MXU_DTYPE)
        wa = w_out[l, C_CONV:][_PERM].astype(MXU_DTYPE)
        last = l == depth - 1

        u, gc, qz, kv, ga, iqz, tail = _project(xp, norm_g[l], w, tm)
        b3 = lambda a: a.reshape(bsz, lq, a.shape[-1])
        u3, kv3, tail3 = b3(u), b3(kv), b3(tail)
        mixc = _conv_branch(u3, b3(gc), conv_w[l], conv_b[l], conv_ln_g[l], conv_ln_b[l])
        lane = jnp.arange(LANE)
        ik = key_chunks(jnp.where(lane < IDX_DIM, tail3, 0.0))
        iwt = jnp.swapaxes(tail3[..., IDX_DIM:IDX_DIM + N_IDX_HEADS].reshape(bsz, nq, TQ, N_IDX_HEADS), 2, 3)
        vt = jnp.swapaxes(key_chunks(kv3[..., feat:]), 2, 3)
        mixa = _attend(feature_major(iqz, N_IDX_HEADS), iwt, feature_major(qz, N_HEADS), b3(ga),
                       ik, key_chunks(kv3[..., :feat]), vt, toep_t, kk_p)
        res = _out_proj(xp, mixc.reshape(bsz * lq, C_CONV), mixa.reshape(bsz * lq, -1), wc, wa, tm,
                        final_norm_g if last else None)
        xp, yp = res if last else (res, None)
        kp.append(kv3[:, :lp, :feat].reshape(bsz, lp, N_KV_HEADS, HEAD_DIM))
        vp.append(kv3[:, :lp, feat:].reshape(bsz, lp, N_KV_HEADS, HEAD_DIM))
        ikp.append(tail3[:, :lp, :IDX_DIM])
        cp.append(u3[:, lp - (CONV_W - 1):lp])

        u, gc, qz, kv, ga, iqz, tail = _project(xs, norm_g[l], w, db)
        mixc = _sample_conv(jnp.transpose(state_conv[l], (1, 0, 2)), u, gc,
                            conv_w[l], conv_b[l], conv_ln_g[l], conv_ln_b[l])
        iw3 = tail[:, IDX_DIM:IDX_DIM + N_IDX_HEADS].reshape(db, N_IDX_HEADS, 1)
        scores = _dec_scores(page_table, iqz.reshape(db, N_IDX_HEADS, LANE), iw3, tail.reshape(db, 1, LANE),
                             cache_ik4, l, width_s)
        madd = _dec_select(scores.reshape(db, width_s).T, kk_s).T
        mixa = _dec_attend(page_table, qz.reshape(db, N_HEADS, LANE), kv.reshape(db, 1, 2 * feat),
                           madd.reshape(db, 1, width_s), bias_s, ga.reshape(db, GROUP, LANE),
                           cache_k4, cache_v4, l)
        res = _out_proj(xs, mixc, mixa.reshape(db, -1), wc, wa, db, final_norm_g if last else None)
        xs, ys = res if last else (res, None)
        ksm.append(kv[:, :feat].reshape(db, 1, N_KV_HEADS, HEAD_DIM))
        vsm.append(kv[:, feat:].reshape(db, 1, N_KV_HEADS, HEAD_DIM))
        iks.append(tail[:, None, :IDX_DIM])
        cs.append(jnp.concatenate([state_conv[l][:, 1:], u[:, None]], axis=1))

    y_prompt = yp.reshape(bsz, lq, d)[:, N_META:lp]
    y_sample = ys.reshape(db, 1, d)
    return (y_prompt, y_sample, jnp.stack(kp), jnp.stack(vp), jnp.stack(ikp), jnp.stack(cp),
            jnp.stack(ksm), jnp.stack(vsm), jnp.stack(iks), jnp.stack(cs))
```

```python
import functools
import math

import numpy as np
import jax
import jax.numpy as jnp
from jax import lax
from jax.experimental import pallas as pl
from jax.experimental.pallas import tpu as pltpu

N_HEADS = 8
N_KV_HEADS = 2
GROUP = N_HEADS // N_KV_HEADS
HEAD_DIM = 64
N_IDX_HEADS = 4
IDX_DIM = 64
C_CONV = 512
CONV_W = 31
TOPK_MAX = 256
N_META = 16
NUM_BUCKETS = 32
MAX_DISTANCE = 128
EPS = 1e-6
LN_EPS = 1e-5

LANE = 128
ROWS = 8
TQ = 128
KC = 512
SUB = KC // TQ
N_ACC = 4
BITS_PER_CHECK = 4
HALO = 32
CONV_SUB = 64
MXU_DTYPE = jnp.bfloat16
NEG = -1e30
LOG2E = 1.4426950408889634
INT_MIN = -2 ** 31
KEY_NEG_INF = -0x7F800000
IDX_BIG = 0x3FFFFFFF
VMEM_LIMIT = 56 * 1024 * 1024

_Q_OFF = 3 * C_CONV
_KV_OFF = _Q_OFF + N_HEADS * LANE
_GA_OFF = _KV_OFF + 2 * N_KV_HEADS * HEAD_DIM
_IQ_OFF = _GA_OFF + N_HEADS * HEAD_DIM
_TAIL_OFF = _IQ_OFF + N_IDX_HEADS * LANE
_W_COLS = _TAIL_OFF + LANE

_HEAD_ORDER = [h for j in range(GROUP) for h in (j, j + GROUP)]
_PERM = np.concatenate([np.arange(h * HEAD_DIM, (h + 1) * HEAD_DIM) for h in _HEAD_ORDER])

_PROJ_SIZES = (C_CONV, C_CONV, C_CONV, N_HEADS * HEAD_DIM, N_KV_HEADS * HEAD_DIM, N_KV_HEADS * HEAD_DIM,
               N_HEADS * HEAD_DIM, N_IDX_HEADS * IDX_DIM, IDX_DIM, N_IDX_HEADS)
_PROJ_SPLITS = tuple(int(s) for s in np.cumsum(_PROJ_SIZES)[:-1])


def _cparams(n_axes):
    return pltpu.CompilerParams(dimension_semantics=("arbitrary",) * n_axes, vmem_limit_bytes=VMEM_LIMIT)


def _silu(x):
    return x * jax.nn.sigmoid(x)


def _prep_w_in(w):
    ua, ub, gc, q, k, v, ga, iq, ik, iw = jnp.split(w.T, _PROJ_SPLITS, axis=0)
    d = w.shape[0]
    z = jnp.zeros((HEAD_DIM, d), w.dtype)
    qz = []
    for h in range(N_HEADS):
        qh = q[h * HEAD_DIM:(h + 1) * HEAD_DIM]
        qz += [qh, z] if h // GROUP == 0 else [z, qh]
    iqz = []
    for h in range(N_IDX_HEADS):
        iqz += [iq[h * IDX_DIM:(h + 1) * IDX_DIM], z]
    ga_p = [ga[h * HEAD_DIM:(h + 1) * HEAD_DIM] for h in _HEAD_ORDER]
    tail = [ik, iw, jnp.zeros((LANE - IDX_DIM - N_IDX_HEADS, d), w.dtype)]
    out = jnp.concatenate([ua, ub, gc] + qz + [k, v] + ga_p + iqz + tail, axis=0)
    assert out.shape[0] == _W_COLS
    return out.astype(MXU_DTYPE)


def _t5_bucket(rel):
    n = jnp.maximum(rel, 0)
    max_exact = NUM_BUCKETS // 2
    large = max_exact + (jnp.log(jnp.maximum(n, 1).astype(jnp.float32) / max_exact)
                         / math.log(MAX_DISTANCE / max_exact)
                         * (NUM_BUCKETS - max_exact)).astype(jnp.int32)
    large = jnp.minimum(large, NUM_BUCKETS - 1)
    return jnp.where(n < max_exact, n, large)


def _bias_tiles(rel_bias):
    far = rel_bias[NUM_BUCKETS - 1]
    d = jnp.arange(2 * TQ, dtype=jnp.int32)
    dists = (jnp.where(d < TQ, d, 0),
             jnp.where(d < TQ, d + TQ, d - TQ))
    tiles = []
    for dist in dists:
        v = ((rel_bias[_t5_bucket(dist)] - far) * LOG2E).T
        rep = jnp.tile(v, (1, TQ))[:, :TQ * (2 * TQ - 1)].reshape(N_HEADS, TQ, 2 * TQ - 1)
        tiles.append(rep[:, :, :TQ])
    tiles.append(jnp.zeros_like(tiles[0]))
    return jnp.stack(tiles).astype(jnp.float32)


_NT = (((1,), (1,)), ((), ()))
_Q_SCALE = HEAD_DIM ** -0.5 * LOG2E
_IQ_SCALE = IDX_DIM ** -0.5
_IW_SCALE = N_IDX_HEADS ** -0.5
_IW_ROWS = 16


def _proj_kernel(x_ref, g_ref, w_ref, u_ref, gc_ref, kv_ref, ga_ref, tail_ref, *q_refs, feature_major):
    x = x_ref[...]
    ms = jnp.mean(x * x, axis=-1, keepdims=True)
    xn = (x * lax.rsqrt(ms + EPS) * g_ref[...]).astype(MXU_DTYPE)

    def mm(lo, hi):
        return lax.dot_general(xn, w_ref[lo:hi, :], _NT, preferred_element_type=jnp.float32)

    def mm_t(lo, hi):
        return lax.dot_general(w_ref[lo:hi, :], xn, _NT, preferred_element_type=jnp.float32)

    u_ref[...] = mm(0, C_CONV) * jax.nn.sigmoid(mm(C_CONV, 2 * C_CONV))
    gc_ref[...] = mm(2 * C_CONV, 3 * C_CONV)
    kv_ref[...] = mm(_KV_OFF, _GA_OFF)
    ga_ref[...] = mm(_GA_OFF, _IQ_OFF)
    t = mm(_TAIL_OFF, _W_COLS)
    lane = lax.broadcasted_iota(jnp.int32, t.shape, 1)
    tail_ref[...] = t * jnp.where(lane >= IDX_DIM, _IW_SCALE, 1.0)
    if feature_major:
        qzt_ref, iqzt_ref, vt_ref, iwt_ref = q_refs
        qzt_ref[...] = (mm_t(_Q_OFF, _KV_OFF) * _Q_SCALE).astype(qzt_ref.dtype)
        iqzt_ref[...] = (mm_t(_IQ_OFF, _TAIL_OFF) * _IQ_SCALE).astype(iqzt_ref.dtype)
        vt_ref[...] = mm_t(_KV_OFF + N_KV_HEADS * HEAD_DIM, _GA_OFF).astype(vt_ref.dtype)
        iwt_ref[...] = mm_t(_TAIL_OFF + IDX_DIM, _TAIL_OFF + IDX_DIM + _IW_ROWS) * _IW_SCALE
    else:
        qz_ref, iqz_ref = q_refs
        qz_ref[...] = (mm(_Q_OFF, _KV_OFF) * _Q_SCALE).astype(qz_ref.dtype)
        iqz_ref[...] = (mm(_IQ_OFF, _TAIL_OFF) * _IQ_SCALE).astype(iqz_ref.dtype)


def _project(x, g, w, tm, feature_major):
    r, d = x.shape
    assert r % tm == 0
    rows = lambda c: pl.BlockSpec((tm, c), lambda i: (i, 0))
    cols = lambda c: pl.BlockSpec((c, tm), lambda i: (0, i))
    feat = N_KV_HEADS * HEAD_DIM
    specs = [rows(C_CONV), rows(C_CONV), rows(2 * feat), rows(N_HEADS * HEAD_DIM), rows(LANE)]
    shapes = [jax.ShapeDtypeStruct((r, s.block_shape[1]), jnp.float32) for s in specs]
    if feature_major:
        extra = [(N_HEADS * LANE, MXU_DTYPE), (N_IDX_HEADS * LANE, MXU_DTYPE), (feat, MXU_DTYPE),
                 (_IW_ROWS, jnp.float32)]
        specs += [cols(c) for c, _ in extra]
        shapes += [jax.ShapeDtypeStruct((c, r), t) for c, t in extra]
    else:
        extra = [(N_HEADS * LANE, MXU_DTYPE), (N_IDX_HEADS * LANE, MXU_DTYPE)]
        specs += [rows(c) for c, _ in extra]
        shapes += [jax.ShapeDtypeStruct((r, c), t) for c, t in extra]
    return pl.pallas_call(
        functools.partial(_proj_kernel, feature_major=feature_major),
        grid=(r // tm,),
        in_specs=[rows(d), pl.BlockSpec((1, d), lambda i: (0, 0)), pl.BlockSpec((_W_COLS, d), lambda i: (0, 0))],
        out_specs=specs,
        out_shape=shapes,
        compiler_params=_cparams(1),
        name="proj",
    )(x, g.reshape(1, d), w)


def _conv_kernel(prev_ref, cur_ref, gc_ref, w_ref, b_ref, lg_ref, lb_ref, o_ref, ext_ref):
    i = pl.program_id(1)
    ext_ref[0, 0:HALO, :] = jnp.where(i > 0, prev_ref[...], 0.0)
    ext_ref[0, HALO:HALO + TQ, :] = cur_ref[...]
    n_ext = HALO + TQ
    for s in range(1, ROWS):
        ext_ref[s, 0:n_ext - ROWS, :] = ext_ref[0, s:s + n_ext - ROWS, :]
    off = HALO - (CONV_W - 1)
    for r0 in range(0, TQ, CONV_SUB):
        acc = jnp.zeros((CONV_SUB, C_CONV), jnp.float32)
        for j in range(CONV_W):
            s = (off + j) % ROWS
            lo = r0 + off + j - s
            acc = acc + w_ref[j:j + 1, :] * ext_ref[s, lo:lo + CONV_SUB, :]
        y = acc + b_ref[...]
        mu = jnp.mean(y, axis=-1, keepdims=True)
        dev = y - mu
        var = jnp.mean(dev * dev, axis=-1, keepdims=True)
        yn = dev * lax.rsqrt(var + LN_EPS) * lg_ref[...] + lb_ref[...]
        o_ref[r0:r0 + CONV_SUB, :] = (_silu(yn) * _silu(gc_ref[r0:r0 + CONV_SUB, :])).astype(o_ref.dtype)


def _conv_branch(u, gc, w, b, lg, lb):
    bsz, lq, c = u.shape
    per = TQ // HALO
    row = lambda a: a.reshape(1, c)
    vec = pl.BlockSpec((1, c), lambda bi, i: (0, 0))
    tile = pl.BlockSpec((None, TQ, c), lambda bi, i: (bi, i, 0))
    return pl.pallas_call(
        _conv_kernel,
        grid=(bsz, lq // TQ),
        in_specs=[pl.BlockSpec((None, HALO, c), lambda bi, i: (bi, jnp.maximum(i * per - 1, 0), 0)),
                  tile, tile, pl.BlockSpec((CONV_W, c), lambda bi, i: (0, 0)), vec, vec, vec],
        out_specs=tile,
        out_shape=jax.ShapeDtypeStruct((bsz, lq, c), MXU_DTYPE),
        scratch_shapes=[pltpu.VMEM((ROWS, HALO + TQ, c), jnp.float32)],
        compiler_params=_cparams(2),
        name="conv",
    )(u, u, gc, w, row(b), row(lg), row(lb))


def _to_key(s):
    bits = lax.bitcast_convert_type(s, jnp.int32)
    return jnp.where(bits < 0, jnp.int32(INT_MIN) - bits, bits)


def _count(ref, nkc, pred):
    q = ref.shape[2]

    def body(c, accs):
        accs = list(accs)
        for n, r in enumerate(range(0, KC, ROWS)):
            accs[n % N_ACC] = accs[n % N_ACC] + jnp.where(pred(ref[c, r:r + ROWS, :]), 1.0, 0.0)
        return tuple(accs)

    accs = lax.fori_loop(0, nkc, body, (jnp.zeros((ROWS, q), jnp.float32),) * N_ACC)
    return jnp.sum(sum(accs[1:], accs[0]), axis=0, keepdims=True)


def _threshold(keys_ref, eidx_ref, nkc, kk, idx_bits):
    q = keys_ref.shape[2]
    kf = jnp.float32(kk)

    def value_bits(state):
        g, thr, cnt_thr, done = state
        for b in range(BITS_PER_CHECK):
            cand = thr ^ lax.shift_left(jnp.int32(1), 31 - (g * BITS_PER_CHECK + b))
            cnt = _count(keys_ref, nkc, lambda x: x >= cand)
            take = (cnt >= kf) & (done == 0.0)
            thr = jnp.where(take, cand, thr)
            cnt_thr = jnp.where(take, cnt, cnt_thr)
            done = jnp.where(take & (cnt == kf), 1.0, done)
        return g + 1, thr, cnt_thr, done

    def unfinished(state):
        g, _, _, done = state
        return (g < 32 // BITS_PER_CHECK) & (jnp.min(done) == 0.0)

    _, thr, cnt_thr, _ = lax.while_loop(
        unfinished, value_bits,
        (0, jnp.full((1, q), INT_MIN, jnp.int32), jnp.full((1, q), kf + 1.0, jnp.float32),
         jnp.zeros((1, q), jnp.float32)))
    admissible = thr != KEY_NEG_INF
    surplus = (cnt_thr > kf) & admissible

    @pl.when(jnp.max(jnp.where(surplus, 1.0, 0.0)) > 0.0)
    def _():
        def mark_ties(c, n_gt):
            row = lax.broadcasted_iota(jnp.int32, (ROWS, q), 0)
            for r in range(0, KC, ROWS):
                x = keys_ref[c, r:r + ROWS, :]
                n_gt = n_gt + jnp.where(x > thr, 1.0, 0.0)
                eidx_ref[c, r:r + ROWS, :] = jnp.where(x == thr, c * KC + r + row, IDX_BIG)
            return n_gt

        n_gt = lax.fori_loop(0, nkc, mark_ties, jnp.zeros((ROWS, q), jnp.float32))
        need = kf - jnp.sum(n_gt, axis=0, keepdims=True)

        def index_bit(i, cut):
            cand = cut | lax.shift_left(jnp.int32(1), idx_bits - 1 - i)
            cnt = _count(eidx_ref, nkc, lambda x: x < cand)
            return jnp.where(cnt < need, cand, cut)

        cut = lax.fori_loop(0, idx_bits, index_bit, jnp.zeros((1, q), jnp.int32))
        cut = jnp.where(surplus, cut, IDX_BIG - 1)

        def drop_losers(c, carry):
            for r in range(0, KC, ROWS):
                e = eidx_ref[c, r:r + ROWS, :]
                lost = (e != IDX_BIG) & (e > cut)
                keys_ref[c, r:r + ROWS, :] = jnp.where(lost, thr - 1, keys_ref[c, r:r + ROWS, :])
            return carry

        lax.fori_loop(0, nkc, drop_losers, 0)

    return jnp.where(admissible, thr, KEY_NEG_INF + 1)


def _attend_kernel(iqt_ref, iwt_ref, qzt_ref, ga_ref, ik_ref, k_ref, vt_ref, toep_ref, o_ref,
                   keys_ref, eidx_ref, m_ref, l_ref, acc_ref, lg0_ref, lg1_ref, wq_ref, *, kk, idx_bits):
    qi = pl.program_id(1)
    nkc = qi // SUB + 1
    qpos = qi * TQ + lax.broadcasted_iota(jnp.int32, (TQ, TQ), 1)
    krow = lax.broadcasted_iota(jnp.int32, (TQ, TQ), 0)
    for h in range(N_HEADS):
        wq_ref[:, h * TQ:(h + 1) * TQ] = qzt_ref[h]

    def score_chunk(c, carry):
        iqt = jnp.concatenate([iqt_ref[h] for h in range(N_IDX_HEADS)], axis=1)
        w = iwt_ref[...]
        for j in range(SUB):
            st = jnp.dot(ik_ref[c, j * TQ:(j + 1) * TQ, :], iqt, preferred_element_type=jnp.float32)
            s = jnp.zeros((TQ, TQ), jnp.float32)
            for h in range(N_IDX_HEADS):
                s = s + w[h:h + 1, :] * jnp.maximum(st[:, h * TQ:(h + 1) * TQ], 0.0)
            kpos = c * KC + j * TQ + krow
            keys_ref[c, j * TQ:(j + 1) * TQ, :] = _to_key(jnp.where(kpos <= qpos, s, -jnp.inf))
        return carry

    lax.fori_loop(0, nkc, score_chunk, 0)
    thr = _threshold(keys_ref, eidx_ref, nkc, kk, idx_bits)

    m_ref[...] = jnp.full(m_ref.shape, NEG, jnp.float32)
    l_ref[...] = jnp.zeros(l_ref.shape, jnp.float32)
    acc_ref[...] = jnp.zeros(acc_ref.shape, jnp.float32)

    def logits(c, lg_ref):
        lg_ref[...] = jnp.dot(k_ref[c], wq_ref[...], preferred_element_type=jnp.float32)

    def softmax_pv(c, lg_ref, near):
        md = jnp.where(keys_ref[c] >= thr, 0.0, NEG)
        md2 = jnp.concatenate([md, md], axis=1)
        if near:
            delta = [jnp.clip(qi - (c * SUB + j), 0, 2) for j in range(SUB)]
        for pr in range(N_HEADS // 2):
            kv = pr // (GROUP // 2)
            lg = lg_ref[:, 2 * pr * TQ:(2 * pr + 2) * TQ] + md2
            if near:
                lg = lg + jnp.concatenate(
                    [jnp.concatenate([toep_ref[delta[j], 2 * pr], toep_ref[delta[j], 2 * pr + 1]], axis=1)
                     for j in range(SUB)], axis=0)
            m_old = m_ref[pr]
            m_new = jnp.maximum(m_old, jnp.max(lg, axis=0, keepdims=True))
            alpha = jnp.exp2(m_old - m_new)
            p = jnp.exp2(lg - m_new)
            l_ref[pr] = alpha * l_ref[pr] + jnp.sum(p, axis=0, keepdims=True)
            pv = jnp.dot(vt_ref[c, kv * HEAD_DIM:(kv + 1) * HEAD_DIM, :], p.astype(MXU_DTYPE),
                         preferred_element_type=jnp.float32)
            acc_ref[pr] = alpha * acc_ref[pr] + pv
            m_ref[pr] = m_new

    n_far2 = (jnp.maximum(qi - 1, 0) // SUB) // 2

    def far_pair(i, carry):
        logits(2 * i + 1, lg1_ref)
        softmax_pv(2 * i, lg0_ref, near=False)
        logits(2 * i + 2, lg0_ref)
        softmax_pv(2 * i + 1, lg1_ref, near=False)
        return carry

    logits(0, lg0_ref)
    lax.fori_loop(0, n_far2, far_pair, 0)

    def near_chunk(c, carry):
        logits(c, lg1_ref)
        softmax_pv(c, lg1_ref, near=True)
        return carry

    lax.fori_loop(2 * n_far2, nkc, near_chunk, 0)

    for j in range(GROUP):
        half = slice((j % 2) * TQ, (j % 2 + 1) * TQ)
        lo, hi = j // 2, j // 2 + GROUP // 2
        pair_t = jnp.concatenate([acc_ref[lo][:, half] / l_ref[lo][:, half],
                                  acc_ref[hi][:, half] / l_ref[hi][:, half]], axis=0)
        sl = slice(j * LANE, (j + 1) * LANE)
        o_ref[:, sl] = (pair_t.T * _silu(ga_ref[:, sl])).astype(o_ref.dtype)


def _attend(iqt, iwt, qzt, ga, ik, k, vt, toep_t, kk):
    bsz, lq = ga.shape[:2]
    nq = lq // TQ
    nc = k.shape[1]
    idx_bits = max(1, int(nc * KC - 1).bit_length())
    lanes = lambda a: pl.BlockSpec(a.shape[:-1] + (TQ,), lambda b, i: (0,) * (a.ndim - 1) + (b * nq + i,))
    whole = lambda a: pl.BlockSpec((None,) + a.shape[1:], lambda b, i: (b,) + (0,) * (a.ndim - 1))
    rows = pl.BlockSpec((None, TQ, N_HEADS * HEAD_DIM), lambda b, i: (b, i, 0))
    return pl.pallas_call(
        functools.partial(_attend_kernel, kk=kk, idx_bits=idx_bits),
        grid=(bsz, nq),
        in_specs=[lanes(iqt), lanes(iwt), lanes(qzt), rows, whole(ik), whole(k), whole(vt),
                  pl.BlockSpec(toep_t.shape, lambda b, i: (0, 0, 0, 0))],
        out_specs=rows,
        out_shape=jax.ShapeDtypeStruct((bsz, lq, N_HEADS * HEAD_DIM), MXU_DTYPE),
        scratch_shapes=[pltpu.VMEM((nc, KC, TQ), jnp.int32), pltpu.VMEM((nc, KC, TQ), jnp.int32),
                        pltpu.VMEM((N_HEADS // 2, 1, 2 * TQ), jnp.float32),
                        pltpu.VMEM((N_HEADS // 2, 1, 2 * TQ), jnp.float32),
                        pltpu.VMEM((N_HEADS // 2, HEAD_DIM, 2 * TQ), jnp.float32),
                        pltpu.VMEM((KC, N_HEADS * TQ), jnp.float32), pltpu.VMEM((KC, N_HEADS * TQ), jnp.float32),
                        pltpu.VMEM((LANE, N_HEADS * TQ), MXU_DTYPE)],
        compiler_params=_cparams(2),
        name="attend",
    )(iqt, iwt, qzt, ga, ik, k, vt, toep_t)


def _out_kernel(x_ref, mc_ref, ma_ref, wc_ref, wa_ref, o_ref):
    o_ref[...] = (x_ref[...]
                  + jnp.dot(mc_ref[...], wc_ref[...], preferred_element_type=jnp.float32)
                  + jnp.dot(ma_ref[...], wa_ref[...], preferred_element_type=jnp.float32))


def _out_final_kernel(x_ref, mc_ref, ma_ref, wc_ref, wa_ref, g_ref, o_ref, y_ref):
    x = (x_ref[...]
         + jnp.dot(mc_ref[...], wc_ref[...], preferred_element_type=jnp.float32)
         + jnp.dot(ma_ref[...], wa_ref[...], preferred_element_type=jnp.float32))
    o_ref[...] = x
    ms = jnp.mean(x * x, axis=-1, keepdims=True)
    y_ref[...] = x * lax.rsqrt(ms + EPS) * g_ref[...]


def _out_proj(x, mc, ma, wc, wa, tm, final_g=None):
    r, d = x.shape
    c = mc.shape[1]
    rows = lambda w: pl.BlockSpec((tm, w), lambda i: (i, 0))
    full = lambda a: pl.BlockSpec(a.shape, lambda i: (0, 0))
    if final_g is None:
        return pl.pallas_call(
            _out_kernel, grid=(r // tm,),
            in_specs=[rows(d), rows(c), rows(c), full(wc), full(wa)],
            out_specs=rows(d), out_shape=jax.ShapeDtypeStruct((r, d), jnp.float32),
            compiler_params=_cparams(1), name="out",
        )(x, mc, ma, wc, wa)
    g = final_g.reshape(1, d)
    return pl.pallas_call(
        _out_final_kernel, grid=(r // tm,),
        in_specs=[rows(d), rows(c), rows(c), full(wc), full(wa), full(g)],
        out_specs=[rows(d), rows(d)], out_shape=[jax.ShapeDtypeStruct((r, d), jnp.float32)] * 2,
        compiler_params=_cparams(1), name="out_final",
    )(x, mc, ma, wc, wa, g)


def _sconv_kernel(st_ref, u_ref, gc_ref, w_ref, b_ref, lg_ref, lb_ref, o_ref):
    acc = jnp.zeros(u_ref.shape, jnp.float32)
    for j in range(CONV_W - 1):
        acc = acc + w_ref[j:j + 1, :] * st_ref[j]
    acc = acc + w_ref[CONV_W - 1:CONV_W, :] * u_ref[...]
    y = acc + b_ref[...]
    mu = jnp.mean(y, axis=-1, keepdims=True)
    dev = y - mu
    var = jnp.mean(dev * dev, axis=-1, keepdims=True)
    yn = dev * lax.rsqrt(var + LN_EPS) * lg_ref[...] + lb_ref[...]
    o_ref[...] = (_silu(yn) * _silu(gc_ref[...])).astype(o_ref.dtype)


def _sample_conv(state_t, u, gc, w, b, lg, lb):
    db, c = u.shape
    row = lambda a: a.reshape(1, c)
    full = lambda a: pl.BlockSpec(a.shape, lambda i: (0,) * a.ndim)
    args = (state_t, u, gc, w, row(b), row(lg), row(lb))
    return pl.pallas_call(
        _sconv_kernel, grid=(1,), in_specs=[full(a) for a in args],
        out_specs=pl.BlockSpec((db, c), lambda i: (0, 0)),
        out_shape=jax.ShapeDtypeStruct((db, c), MXU_DTYPE),
        compiler_params=_cparams(1), name="sconv",
    )(*args)


def _dec_score_kernel(pt_ref, iq_ref, iw_ref, ikn_ref, *refs, n_pages, width):
    del pt_ref
    pages, o_ref = refs[:n_pages], refs[n_pages]
    iq = iq_ref[...][:, :IDX_DIM]
    w = iw_ref[...]
    pieces = []
    for p in range(n_pages):
        s = jnp.dot(iq, pages[p][...].astype(MXU_DTYPE), preferred_element_type=jnp.float32)
        pieces.append(jnp.sum(w * jnp.maximum(s, 0.0), axis=0, keepdims=True))
    ikn = ikn_ref[...][:, :IDX_DIM].astype(MXU_DTYPE).astype(jnp.float32)
    sn = jnp.sum(iq.astype(jnp.float32) * ikn, axis=-1, keepdims=True)
    scn = jnp.sum(w * jnp.maximum(sn, 0.0), axis=0, keepdims=True)
    page = pieces[0].shape[1]
    lane = lax.broadcasted_iota(jnp.int32, (1, width - n_pages * page), 1)
    pieces.append(jnp.where(lane == 0, scn, -jnp.inf))
    o_ref[...] = jnp.concatenate(pieces, axis=1)


def _dec_scores(page_table, iq3, iw3, tail3, cache_ik, layer, width):
    db, n_pages = page_table.shape
    di, page = cache_ik.shape[2:]
    seq = lambda a: pl.BlockSpec((None,) + a.shape[1:], lambda b, pt: (b, 0, 0))
    page_spec = lambda p: pl.BlockSpec((None, None, di, page), lambda b, pt: (layer, pt[b, p], 0, 0))
    return pl.pallas_call(
        functools.partial(_dec_score_kernel, n_pages=n_pages, width=width),
        grid_spec=pltpu.PrefetchScalarGridSpec(
            num_scalar_prefetch=1, grid=(db,),
            in_specs=[seq(iq3), seq(iw3), seq(tail3)] + [page_spec(p) for p in range(n_pages)],
            out_specs=pl.BlockSpec((None, 1, width), lambda b, pt: (b, 0, 0))),
        out_shape=jax.ShapeDtypeStruct((db, 1, width), jnp.float32),
        compiler_params=_cparams(1), name="dec_score",
    )(page_table, iq3, iw3, tail3, *([cache_ik] * n_pages))


def _dec_select_kernel(s_ref, o_ref, keys_ref, eidx_ref, *, kk, idx_bits):
    nc = keys_ref.shape[0]
    for c in range(nc):
        keys_ref[c] = _to_key(s_ref[c * KC:(c + 1) * KC, :])
    thr = _threshold(keys_ref, eidx_ref, nc, kk, idx_bits)
    for c in range(nc):
        o_ref[c * KC:(c + 1) * KC, :] = jnp.where(keys_ref[c] >= thr, 0.0, NEG)


def _dec_select(scores_t, kk):
    width, db = scores_t.shape
    nc = width // KC
    idx_bits = max(1, int(width - 1).bit_length())
    spec = pl.BlockSpec((width, db), lambda i: (0, 0))
    return pl.pallas_call(
        functools.partial(_dec_select_kernel, kk=kk, idx_bits=idx_bits),
        grid=(1,), in_specs=[spec], out_specs=spec,
        out_shape=jax.ShapeDtypeStruct((width, db), jnp.float32),
        scratch_shapes=[pltpu.VMEM((nc, KC, db), jnp.int32), pltpu.VMEM((nc, KC, db), jnp.int32)],
        compiler_params=_cparams(1), name="dec_select",
    )(scores_t)


def _dec_attend_kernel(pt_ref, qz_ref, kvn_ref, madd_ref, bias_ref, ga_ref, *refs, n_pages):
    del pt_ref
    kpages, vpages, o_ref = refs[:n_pages], refs[n_pages:2 * n_pages], refs[2 * n_pages]
    qz = qz_ref[...]
    page = kpages[0].shape[1]
    past = n_pages * page
    nt = (((1,), (1,)), ((), ()))
    logits = []
    for p in range(n_pages):
        sl = slice(p * page, (p + 1) * page)
        lg = jnp.dot(qz, kpages[p][...].astype(MXU_DTYPE), preferred_element_type=jnp.float32)
        logits.append(lg + bias_ref[:, sl] + madd_ref[:, sl])
    kvn = kvn_ref[...].astype(MXU_DTYPE).astype(jnp.float32)
    lgn = jnp.sum(qz.astype(jnp.float32) * kvn[:, :LANE], axis=-1, keepdims=True)
    lgn = lgn + bias_ref[:, past:past + 1] + madd_ref[:, past:past + 1]
    m = lgn
    for lg in logits:
        m = jnp.maximum(m, jnp.max(lg, axis=-1, keepdims=True))
    pn = jnp.exp2(lgn - m)
    den = pn
    o = pn.astype(MXU_DTYPE).astype(jnp.float32) * kvn[:, LANE:]
    for p in range(n_pages):
        pp = jnp.exp2(logits[p] - m)
        den = den + jnp.sum(pp, axis=-1, keepdims=True)
        o = o + lax.dot_general(pp.astype(MXU_DTYPE), vpages[p][...].astype(MXU_DTYPE), nt,
                                preferred_element_type=jnp.float32)
    o = o / den
    lane = lax.broadcasted_iota(jnp.int32, (GROUP, LANE), 1)
    pair = jnp.where(lane < HEAD_DIM, o[:GROUP], o[GROUP:])
    o_ref[...] = (pair * _silu(ga_ref[...])).astype(o_ref.dtype)


def _dec_attend(page_table, qz3, kvn3, madd3, bias, ga3, cache_k, cache_v, layer):
    db, n_pages = page_table.shape
    feat, page = cache_k.shape[2:]
    seq = lambda a: pl.BlockSpec((None,) + a.shape[1:], lambda b, pt: (b, 0, 0))
    page_spec = lambda p: pl.BlockSpec((None, None, feat, page), lambda b, pt: (layer, pt[b, p], 0, 0))
    return pl.pallas_call(
        functools.partial(_dec_attend_kernel, n_pages=n_pages),
        grid_spec=pltpu.PrefetchScalarGridSpec(
            num_scalar_prefetch=1, grid=(db,),
            in_specs=[seq(qz3), seq(kvn3), seq(madd3), pl.BlockSpec(bias.shape, lambda b, pt: (0, 0)), seq(ga3)]
                     + [page_spec(p) for p in range(n_pages)] * 2,
            out_specs=pl.BlockSpec((None, GROUP, LANE), lambda b, pt: (b, 0, 0))),
        out_shape=jax.ShapeDtypeStruct((db, GROUP, LANE), MXU_DTYPE),
        compiler_params=_cparams(1), name="dec_attend",
    )(page_table, qz3, kvn3, madd3, bias, ga3, *([cache_k] * n_pages), *([cache_v] * n_pages))


def _round_up(x, m):
    return -(-x // m) * m


def _row_tile(rows):
    for f in (5, 4, 3, 2, 1):
        if rows % (f * TQ) == 0:
            return f * TQ
    return rows


def kernel(x_prompt, x_sample, cache_k, cache_v, cache_idx_k, state_conv, page_table, meta_tokens, rel_bias,
           norm_g, w_in, conv_w, conv_b, conv_ln_g, conv_ln_b, w_out, final_norm_g):
    bsz, seq, d = x_prompt.shape
    depth = w_in.shape[0]
    lp = seq + N_META
    lq = _round_up(lp, TQ)
    lk = _round_up(lp, KC)
    nq, nc = lq // TQ, lk // KC
    kk_p = min(TOPK_MAX, lp // 4)
    db = x_sample.shape[0]
    n_pool, page = cache_k.shape[1:3]
    n_pages = page_table.shape[1]
    past = n_pages * page
    kk_s = min(TOPK_MAX, (past + 1) // 4)
    width_s = _round_up(past + 1, KC)
    feat = N_KV_HEADS * HEAD_DIM
    tm = _row_tile(bsz * lq)

    xp = jnp.concatenate([jnp.broadcast_to(meta_tokens[None].astype(x_prompt.dtype), (bsz, N_META, d)), x_prompt,
                          jnp.zeros((bsz, lq - lp, d), x_prompt.dtype)], axis=1).reshape(bsz * lq, d)
    xs = x_sample.reshape(db, d)
    toep_t = _bias_tiles(rel_bias)
    bias_s = jnp.moveaxis(rel_bias[_t5_bucket(past - jnp.arange(width_s, dtype=jnp.int32))], -1, 0) * LOG2E
    cache_k4 = jnp.transpose(cache_k, (0, 1, 3, 4, 2)).reshape(depth, n_pool, feat, page)
    cache_v4 = jnp.transpose(cache_v, (0, 1, 3, 4, 2)).reshape(depth, n_pool, feat, page)
    cache_ik4 = jnp.swapaxes(cache_idx_k, 2, 3)

    def key_chunks(a):
        return jnp.pad(a, ((0, 0), (0, lk - lq), (0, 0))).astype(MXU_DTYPE).reshape(bsz, nc, KC, LANE)

    def key_chunks_t(a):
        a = jnp.pad(a.reshape(LANE, bsz, lq), ((0, 0), (0, 0), (0, lk - lq)))
        return jnp.transpose(a.reshape(LANE, bsz, nc, KC), (1, 2, 0, 3))

    kp, vp, ikp, cp, ksm, vsm, iks, cs = ([] for _ in range(8))
    yp = ys = None
    for l in range(depth):
        w = _prep_w_in(w_in[l])
        wc = w_out[l, :C_CONV].astype(MXU_DTYPE)
        wa = w_out[l, C_CONV:][_PERM].astype(MXU_DTYPE)
        last = l == depth - 1

        u, gc, kv, ga, tail, qzt, iqzt, vt, iwt = _project(xp, norm_g[l], w, tm, feature_major=True)
        b3 = lambda a: a.reshape(bsz, lq, a.shape[-1])
        u3, kv3, tail3 = b3(u), b3(kv), b3(tail)
        mixc = _conv_branch(u3, b3(gc), conv_w[l], conv_b[l], conv_ln_g[l], conv_ln_b[l])
        lane = jnp.arange(LANE)
        ik = key_chunks(jnp.where(lane < IDX_DIM, tail3, 0.0))
        mixa = _attend(iqzt.reshape(N_IDX_HEADS, LANE, bsz * lq), iwt, qzt.reshape(N_HEADS, LANE, bsz * lq), b3(ga),
                       ik, key_chunks(kv3[..., :feat]), key_chunks_t(vt), toep_t, kk_p)
        res = _out_proj(xp, mixc.reshape(bsz * lq, C_CONV), mixa.reshape(bsz * lq, -1), wc, wa, tm,
                        final_norm_g if last else None)
        xp, yp = res if last else (res, None)
        kp.append(kv3[:, :lp, :feat].reshape(bsz, lp, N_KV_HEADS, HEAD_DIM))
        vp.append(kv3[:, :lp, feat:].reshape(bsz, lp, N_KV_HEADS, HEAD_DIM))
        ikp.append(tail3[:, :lp, :IDX_DIM])
        cp.append(u3[:, lp - (CONV_W - 1):lp])

        u, gc, kv, ga, tail, qz, iqz = _project(xs, norm_g[l], w, db, feature_major=False)
        mixc = _sample_conv(jnp.transpose(state_conv[l], (1, 0, 2)), u, gc,
                            conv_w[l], conv_b[l], conv_ln_g[l], conv_ln_b[l])
        iw3 = tail[:, IDX_DIM:IDX_DIM + N_IDX_HEADS].reshape(db, N_IDX_HEADS, 1)
        scores = _dec_scores(page_table, iqz.reshape(db, N_IDX_HEADS, LANE), iw3, tail.reshape(db, 1, LANE),
                             cache_ik4, l, width_s)
        madd = _dec_select(scores.reshape(db, width_s).T, kk_s).T
        mixa = _dec_attend(page_table, qz.reshape(db, N_HEADS, LANE), kv.reshape(db, 1, 2 * feat),
                           madd.reshape(db, 1, width_s), bias_s, ga.reshape(db, GROUP, LANE),
                           cache_k4, cache_v4, l)
        res = _out_proj(xs, mixc, mixa.reshape(db, -1), wc, wa, db, final_norm_g if last else None)
        xs, ys = res if last else (res, None)
        ksm.append(kv[:, :feat].reshape(db, 1, N_KV_HEADS, HEAD_DIM))
        vsm.append(kv[:, feat:].reshape(db, 1, N_KV_HEADS, HEAD_DIM))
        iks.append(tail[:, None, :IDX_DIM])
        cs.append(jnp.concatenate([state_conv[l][:, 1:], u[:, None]], axis=1))

    y_prompt = yp.reshape(bsz, lq, d)[:, N_META:lp]
    y_sample = ys.reshape(db, 1, d)
    return (y_prompt, y_sample, jnp.stack(kp), jnp.stack(vp), jnp.stack(ikp), jnp.stack(cp),
            jnp.stack(ksm), jnp.stack(vsm), jnp.stack(iks), jnp.stack(cs))
```

```python
import functools
import math

import numpy as np
import jax
import jax.numpy as jnp
from jax import lax
from jax.experimental import pallas as pl
from jax.experimental.pallas import tpu as pltpu

N_HEADS = 8
N_KV_HEADS = 2
GROUP = N_HEADS // N_KV_HEADS
HEAD_DIM = 64
N_IDX_HEADS = 4
IDX_DIM = 64
C_CONV = 512
CONV_W = 31
TOPK_MAX = 256
N_META = 16
NUM_BUCKETS = 32
MAX_DISTANCE = 128
EPS = 1e-6
LN_EPS = 1e-5

LANE = 128
ROWS = 8
TQ = 128
KC = 512
SUB = KC // TQ
N_ACC = 4
DIGIT_BITS = 7
N_LEVELS = 4
LOW_BITS = 32 - DIGIT_BITS * N_LEVELS
DIGIT_MASK = (1 << DIGIT_BITS) - 1
HALO = 32
CONV_SUB = 64
MXU_DTYPE = jnp.bfloat16
NEG = -1e30
LOG2E = 1.4426950408889634
INT_MIN = -2 ** 31
KEY_NEG_INF = -0x7F800000
IDX_BIG = 0x3FFFFFFF
VMEM_LIMIT = 56 * 1024 * 1024

_Q_OFF = 3 * C_CONV
_KV_OFF = _Q_OFF + N_HEADS * LANE
_GA_OFF = _KV_OFF + 2 * N_KV_HEADS * HEAD_DIM
_IQ_OFF = _GA_OFF + N_HEADS * HEAD_DIM
_TAIL_OFF = _IQ_OFF + N_IDX_HEADS * LANE
_W_COLS = _TAIL_OFF + LANE

_HEAD_ORDER = [h for j in range(GROUP) for h in (j, j + GROUP)]
_PERM = np.concatenate([np.arange(h * HEAD_DIM, (h + 1) * HEAD_DIM) for h in _HEAD_ORDER])

_PROJ_SIZES = (C_CONV, C_CONV, C_CONV, N_HEADS * HEAD_DIM, N_KV_HEADS * HEAD_DIM, N_KV_HEADS * HEAD_DIM,
               N_HEADS * HEAD_DIM, N_IDX_HEADS * IDX_DIM, IDX_DIM, N_IDX_HEADS)
_PROJ_SPLITS = tuple(int(s) for s in np.cumsum(_PROJ_SIZES)[:-1])


def _cparams(n_axes):
    return pltpu.CompilerParams(dimension_semantics=("arbitrary",) * n_axes, vmem_limit_bytes=VMEM_LIMIT)


def _silu(x):
    return x * jax.nn.sigmoid(x)


def _prep_w_in(w):
    ua, ub, gc, q, k, v, ga, iq, ik, iw = jnp.split(w.T, _PROJ_SPLITS, axis=0)
    d = w.shape[0]
    z = jnp.zeros((HEAD_DIM, d), w.dtype)
    qz = []
    for h in range(N_HEADS):
        qh = q[h * HEAD_DIM:(h + 1) * HEAD_DIM]
        qz += [qh, z] if h // GROUP == 0 else [z, qh]
    iqz = []
    for h in range(N_IDX_HEADS):
        iqz += [iq[h * IDX_DIM:(h + 1) * IDX_DIM], z]
    ga_p = [ga[h * HEAD_DIM:(h + 1) * HEAD_DIM] for h in _HEAD_ORDER]
    tail = [ik, iw, jnp.zeros((LANE - IDX_DIM - N_IDX_HEADS, d), w.dtype)]
    out = jnp.concatenate([ua, ub, gc] + qz + [k, v] + ga_p + iqz + tail, axis=0)
    assert out.shape[0] == _W_COLS
    return out.astype(MXU_DTYPE)


def _t5_bucket(rel):
    n = jnp.maximum(rel, 0)
    max_exact = NUM_BUCKETS // 2
    large = max_exact + (jnp.log(jnp.maximum(n, 1).astype(jnp.float32) / max_exact)
                         / math.log(MAX_DISTANCE / max_exact)
                         * (NUM_BUCKETS - max_exact)).astype(jnp.int32)
    large = jnp.minimum(large, NUM_BUCKETS - 1)
    return jnp.where(n < max_exact, n, large)


def _bias_tiles(rel_bias):
    far = rel_bias[NUM_BUCKETS - 1]
    d = jnp.arange(2 * TQ, dtype=jnp.int32)
    dists = (jnp.where(d < TQ, d, 0),
             jnp.where(d < TQ, d + TQ, d - TQ))
    tiles = []
    for dist in dists:
        v = ((rel_bias[_t5_bucket(dist)] - far) * LOG2E).T
        rep = jnp.tile(v, (1, TQ))[:, :TQ * (2 * TQ - 1)].reshape(N_HEADS, TQ, 2 * TQ - 1)
        tiles.append(rep[:, :, :TQ])
    tiles.append(jnp.zeros_like(tiles[0]))
    return jnp.stack(tiles).astype(jnp.float32)


_NT = (((1,), (1,)), ((), ()))
_Q_SCALE = HEAD_DIM ** -0.5 * LOG2E
_IQ_SCALE = IDX_DIM ** -0.5
_IW_SCALE = N_IDX_HEADS ** -0.5
_IW_ROWS = 16


def _proj_kernel(x_ref, g_ref, w_ref, u_ref, gc_ref, kv_ref, ga_ref, tail_ref, *q_refs, feature_major):
    x = x_ref[...]
    ms = jnp.mean(x * x, axis=-1, keepdims=True)
    xn = (x * lax.rsqrt(ms + EPS) * g_ref[...]).astype(MXU_DTYPE)

    def mm(lo, hi):
        return lax.dot_general(xn, w_ref[lo:hi, :], _NT, preferred_element_type=jnp.float32)

    def mm_t(lo, hi):
        return lax.dot_general(w_ref[lo:hi, :], xn, _NT, preferred_element_type=jnp.float32)

    u_ref[...] = mm(0, C_CONV) * jax.nn.sigmoid(mm(C_CONV, 2 * C_CONV))
    gc_ref[...] = mm(2 * C_CONV, 3 * C_CONV)
    kv_ref[...] = mm(_KV_OFF, _GA_OFF)
    ga_ref[...] = mm(_GA_OFF, _IQ_OFF)
    t = mm(_TAIL_OFF, _W_COLS)
    lane = lax.broadcasted_iota(jnp.int32, t.shape, 1)
    tail_ref[...] = t * jnp.where(lane >= IDX_DIM, _IW_SCALE, 1.0)
    if feature_major:
        qzt_ref, iqzt_ref, vt_ref, iwt_ref = q_refs
        qzt_ref[...] = (mm_t(_Q_OFF, _KV_OFF) * _Q_SCALE).astype(qzt_ref.dtype)
        iqzt_ref[...] = (mm_t(_IQ_OFF, _TAIL_OFF) * _IQ_SCALE).astype(iqzt_ref.dtype)
        vt_ref[...] = mm_t(_KV_OFF + N_KV_HEADS * HEAD_DIM, _GA_OFF).astype(vt_ref.dtype)
        iwt_ref[...] = mm_t(_TAIL_OFF + IDX_DIM, _TAIL_OFF + IDX_DIM + _IW_ROWS) * _IW_SCALE
    else:
        qz_ref, iqz_ref = q_refs
        qz_ref[...] = (mm(_Q_OFF, _KV_OFF) * _Q_SCALE).astype(qz_ref.dtype)
        iqz_ref[...] = (mm(_IQ_OFF, _TAIL_OFF) * _IQ_SCALE).astype(iqz_ref.dtype)


def _project(x, g, w, tm, feature_major):
    r, d = x.shape
    assert r % tm == 0
    rows = lambda c: pl.BlockSpec((tm, c), lambda i: (i, 0))
    cols = lambda c: pl.BlockSpec((c, tm), lambda i: (0, i))
    feat = N_KV_HEADS * HEAD_DIM
    specs = [rows(C_CONV), rows(C_CONV), rows(2 * feat), rows(N_HEADS * HEAD_DIM), rows(LANE)]
    shapes = [jax.ShapeDtypeStruct((r, s.block_shape[1]), jnp.float32) for s in specs]
    if feature_major:
        extra = [(N_HEADS * LANE, MXU_DTYPE), (N_IDX_HEADS * LANE, MXU_DTYPE), (feat, MXU_DTYPE),
                 (_IW_ROWS, jnp.float32)]
        specs += [cols(c) for c, _ in extra]
        shapes += [jax.ShapeDtypeStruct((c, r), t) for c, t in extra]
    else:
        extra = [(N_HEADS * LANE, MXU_DTYPE), (N_IDX_HEADS * LANE, MXU_DTYPE)]
        specs += [rows(c) for c, _ in extra]
        shapes += [jax.ShapeDtypeStruct((r, c), t) for c, t in extra]
    return pl.pallas_call(
        functools.partial(_proj_kernel, feature_major=feature_major),
        grid=(r // tm,),
        in_specs=[rows(d), pl.BlockSpec((1, d), lambda i: (0, 0)), pl.BlockSpec((_W_COLS, d), lambda i: (0, 0))],
        out_specs=specs,
        out_shape=shapes,
        compiler_params=_cparams(1),
        name="proj",
    )(x, g.reshape(1, d), w)


def _conv_kernel(prev_ref, cur_ref, gc_ref, w_ref, b_ref, lg_ref, lb_ref, o_ref, ext_ref):
    i = pl.program_id(1)
    ext_ref[0, 0:HALO, :] = jnp.where(i > 0, prev_ref[...], 0.0)
    ext_ref[0, HALO:HALO + TQ, :] = cur_ref[...]
    n_ext = HALO + TQ
    for s in range(1, ROWS):
        ext_ref[s, 0:n_ext - ROWS, :] = ext_ref[0, s:s + n_ext - ROWS, :]
    off = HALO - (CONV_W - 1)
    for r0 in range(0, TQ, CONV_SUB):
        acc = jnp.zeros((CONV_SUB, C_CONV), jnp.float32)
        for j in range(CONV_W):
            s = (off + j) % ROWS
            lo = r0 + off + j - s
            acc = acc + w_ref[j:j + 1, :] * ext_ref[s, lo:lo + CONV_SUB, :]
        y = acc + b_ref[...]
        mu = jnp.mean(y, axis=-1, keepdims=True)
        dev = y - mu
        var = jnp.mean(dev * dev, axis=-1, keepdims=True)
        yn = dev * lax.rsqrt(var + LN_EPS) * lg_ref[...] + lb_ref[...]
        o_ref[r0:r0 + CONV_SUB, :] = (_silu(yn) * _silu(gc_ref[r0:r0 + CONV_SUB, :])).astype(o_ref.dtype)


def _conv_branch(u, gc, w, b, lg, lb):
    bsz, lq, c = u.shape
    per = TQ // HALO
    row = lambda a: a.reshape(1, c)
    vec = pl.BlockSpec((1, c), lambda bi, i: (0, 0))
    tile = pl.BlockSpec((None, TQ, c), lambda bi, i: (bi, i, 0))
    return pl.pallas_call(
        _conv_kernel,
        grid=(bsz, lq // TQ),
        in_specs=[pl.BlockSpec((None, HALO, c), lambda bi, i: (bi, jnp.maximum(i * per - 1, 0), 0)),
                  tile, tile, pl.BlockSpec((CONV_W, c), lambda bi, i: (0, 0)), vec, vec, vec],
        out_specs=tile,
        out_shape=jax.ShapeDtypeStruct((bsz, lq, c), MXU_DTYPE),
        scratch_shapes=[pltpu.VMEM((ROWS, HALO + TQ, c), jnp.float32)],
        compiler_params=_cparams(2),
        name="conv",
    )(u, u, gc, w, row(b), row(lg), row(lb))


_INT_MIN = np.int32(INT_MIN)
_BYTE_ONES = np.int32(0x01010101)
_BYTE_LOW = np.int32(0x7F7F7F7F)
_GUARD = np.int32(0x80808080 - (1 << 32))
assert SUB == 4 and DIGIT_BITS == 7


def _to_key(s):
    bits = lax.bitcast_convert_type(s, jnp.int32)
    return jnp.where(bits < 0, _INT_MIN - bits, bits)


def _count(ref, nkc, pred):
    q = ref.shape[2]

    def body(c, accs):
        accs = list(accs)
        for n, r in enumerate(range(0, KC, ROWS)):
            accs[n % N_ACC] = accs[n % N_ACC] + jnp.where(pred(ref[c, r:r + ROWS, :]), 1.0, 0.0)
        return tuple(accs)

    accs = lax.fori_loop(0, nkc, body, (jnp.zeros((ROWS, q), jnp.float32),) * N_ACC)
    return jnp.sum(sum(accs[1:], accs[0]), axis=0, keepdims=True)


def _pack_digits(keys_ref, dig_ref, c):
    for lv in range(N_LEVELS):
        shift = 32 - DIGIT_BITS * (lv + 1)
        word = None
        for j in range(SUB):
            u = keys_ref[c, j * TQ:(j + 1) * TQ, :] ^ _INT_MIN
            field = lax.shift_right_logical(u, shift) & DIGIT_MASK
            field = field if j == 0 else lax.shift_left(field, 8 * j)
            word = field if word is None else word | field
        dig_ref[lv, c] = word


def _count_fields(work_ref, nkc, cand_bytes):
    q = work_ref.shape[2]

    def body(c, accs):
        accs = list(accs)
        for n, r in enumerate(range(0, TQ, ROWS)):
            diff = work_ref[c, r:r + ROWS, :] - cand_bytes
            accs[n % N_ACC] = accs[n % N_ACC] + (lax.shift_right_logical(diff, DIGIT_BITS) & _BYTE_ONES)
        return tuple(accs)

    accs = lax.fori_loop(0, nkc, body, (jnp.zeros((ROWS, q), jnp.int32),) * N_ACC)
    total = jnp.zeros((ROWS, q), jnp.int32)
    for a in accs:
        total = total + ((a & 0xFF) + (lax.shift_right_logical(a, 8) & 0xFF)
                         + (lax.shift_right_logical(a, 16) & 0xFF) + lax.shift_right_logical(a, 24))
    return jnp.sum(total.astype(jnp.float32), axis=0, keepdims=True)


def _threshold(keys_ref, eidx_ref, dig_ref, work_ref, alive_ref, nkc, kk, idx_bits):
    q = keys_ref.shape[2]
    kf = jnp.float32(kk)
    assert keys_ref.shape[0] * (TQ // ROWS) <= 255 * N_ACC
    zero = jnp.zeros((1, q), jnp.float32)
    prefix = jnp.zeros((1, q), jnp.int32)
    above, done, cnt_thr = zero, zero, jnp.full((1, q), kf + 1.0, jnp.float32)

    for lv in range(N_LEVELS):
        def load_fields(c, carry):
            word = dig_ref[lv, c]
            if lv > 0:
                word = word & alive_ref[c]
            work_ref[c] = word | _GUARD
            return carry

        lax.fori_loop(0, nkc, load_fields, 0)

        def digit_bit(i, state):
            digit, cnt_thr, cnt_rej, done = state
            cand = digit | lax.shift_left(jnp.int32(1), DIGIT_BITS - 1 - i)
            cnt_alive = _count_fields(work_ref, nkc, cand * _BYTE_ONES)
            cnt = above + cnt_alive
            take = (cnt >= kf) & (done == 0.0)
            digit = jnp.where(take, cand, digit)
            cnt_thr = jnp.where(take, cnt, cnt_thr)
            cnt_rej = jnp.where((cnt < kf) & (done == 0.0), cnt_alive, cnt_rej)
            done = jnp.where(take & (cnt == kf), 1.0, done)
            return digit, cnt_thr, cnt_rej, done

        digit, cnt_thr, cnt_rej, done = lax.fori_loop(
            0, DIGIT_BITS, digit_bit, (jnp.zeros((1, q), jnp.int32), cnt_thr, zero, done))
        above = above + cnt_rej
        prefix = prefix | lax.shift_left(digit, 32 - DIGIT_BITS * (lv + 1))

        if lv + 1 < N_LEVELS:
            digit_bytes = digit * _BYTE_ONES

            def narrow(c, carry):
                differs = lax.shift_right_logical(((dig_ref[lv, c] ^ digit_bytes) + _BYTE_LOW) & _GUARD, DIGIT_BITS)
                same = (_BYTE_ONES - differs) * DIGIT_MASK
                alive_ref[c] = same if lv == 0 else alive_ref[c] & same
                return carry

            lax.fori_loop(0, nkc, narrow, 0)

    def low_bit(i, state):
        thr, cnt_thr, done = state
        cand = thr | lax.shift_left(jnp.int32(1), LOW_BITS - 1 - i)
        cnt = _count(keys_ref, nkc, lambda x: x >= cand)
        take = (cnt >= kf) & (done == 0.0)
        return (jnp.where(take, cand, thr), jnp.where(take, cnt, cnt_thr),
                jnp.where(take & (cnt == kf), 1.0, done))

    n_low = jnp.where(jnp.min(done) == 0.0, LOW_BITS, 0)
    thr, cnt_thr, _ = lax.fori_loop(0, n_low, low_bit, (prefix ^ _INT_MIN, cnt_thr, done))
    admissible = thr != KEY_NEG_INF
    surplus = (cnt_thr > kf) & admissible

    @pl.when(jnp.max(jnp.where(surplus, 1.0, 0.0)) > 0.0)
    def _():
        def mark_ties(c, n_gt):
            row = lax.broadcasted_iota(jnp.int32, (ROWS, q), 0)
            for r in range(0, KC, ROWS):
                x = keys_ref[c, r:r + ROWS, :]
                n_gt = n_gt + jnp.where(x > thr, 1.0, 0.0)
                eidx_ref[c, r:r + ROWS, :] = jnp.where(x == thr, c * KC + r + row, IDX_BIG)
            return n_gt

        n_gt = lax.fori_loop(0, nkc, mark_ties, jnp.zeros((ROWS, q), jnp.float32))
        need = kf - jnp.sum(n_gt, axis=0, keepdims=True)

        def index_bit(i, cut):
            cand = cut | lax.shift_left(jnp.int32(1), idx_bits - 1 - i)
            cnt = _count(eidx_ref, nkc, lambda x: x < cand)
            return jnp.where(cnt < need, cand, cut)

        cut = lax.fori_loop(0, idx_bits, index_bit, jnp.zeros((1, q), jnp.int32))
        cut = jnp.where(surplus, cut, IDX_BIG - 1)

        def drop_losers(c, carry):
            for r in range(0, KC, ROWS):
                e = eidx_ref[c, r:r + ROWS, :]
                lost = (e != IDX_BIG) & (e > cut)
                keys_ref[c, r:r + ROWS, :] = jnp.where(lost, thr - 1, keys_ref[c, r:r + ROWS, :])
            return carry

        lax.fori_loop(0, nkc, drop_losers, 0)

    return jnp.where(admissible, thr, KEY_NEG_INF + 1)


def _attend_kernel(iqt_ref, iwt_ref, qzt_ref, ga_ref, ik_ref, k_ref, vt_ref, toep_ref, o_ref,
                   keys_ref, eidx_ref, dig_ref, work_ref, alive_ref, m_ref, l_ref, acc_ref, lg0_ref, lg1_ref,
                   wq_ref, *, kk, idx_bits):
    qi = pl.program_id(1)
    nkc = qi // SUB + 1
    qpos = qi * TQ + lax.broadcasted_iota(jnp.int32, (TQ, TQ), 1)
    krow = lax.broadcasted_iota(jnp.int32, (TQ, TQ), 0)
    for h in range(N_HEADS):
        wq_ref[:, h * TQ:(h + 1) * TQ] = qzt_ref[h]

    def score_chunk(c, carry):
        iqt = jnp.concatenate([iqt_ref[h] for h in range(N_IDX_HEADS)], axis=1)
        w = iwt_ref[...]
        for j in range(SUB):
            st = jnp.dot(ik_ref[c, j * TQ:(j + 1) * TQ, :], iqt, preferred_element_type=jnp.float32)
            s = jnp.zeros((TQ, TQ), jnp.float32)
            for h in range(N_IDX_HEADS):
                s = s + w[h:h + 1, :] * jnp.maximum(st[:, h * TQ:(h + 1) * TQ], 0.0)
            kpos = c * KC + j * TQ + krow
            keys_ref[c, j * TQ:(j + 1) * TQ, :] = _to_key(jnp.where(kpos <= qpos, s, -jnp.inf))
        _pack_digits(keys_ref, dig_ref, c)
        return carry

    lax.fori_loop(0, nkc, score_chunk, 0)
    thr = _threshold(keys_ref, eidx_ref, dig_ref, work_ref, alive_ref, nkc, kk, idx_bits)

    m_ref[...] = jnp.full(m_ref.shape, NEG, jnp.float32)
    l_ref[...] = jnp.zeros(l_ref.shape, jnp.float32)
    acc_ref[...] = jnp.zeros(acc_ref.shape, jnp.float32)

    def logits(c, lg_ref):
        lg_ref[...] = jnp.dot(k_ref[c], wq_ref[...], preferred_element_type=jnp.float32)

    def softmax_pv(c, lg_ref, near):
        md = jnp.where(keys_ref[c] >= thr, 0.0, NEG)
        md2 = jnp.concatenate([md, md], axis=1)
        if near:
            delta = [jnp.clip(qi - (c * SUB + j), 0, 2) for j in range(SUB)]
        for pr in range(N_HEADS // 2):
            kv = pr // (GROUP // 2)
            lg = lg_ref[:, 2 * pr * TQ:(2 * pr + 2) * TQ] + md2
            if near:
                lg = lg + jnp.concatenate(
                    [jnp.concatenate([toep_ref[delta[j], 2 * pr], toep_ref[delta[j], 2 * pr + 1]], axis=1)
                     for j in range(SUB)], axis=0)
            m_old = m_ref[pr]
            m_new = jnp.maximum(m_old, jnp.max(lg, axis=0, keepdims=True))
            alpha = jnp.exp2(m_old - m_new)
            p = jnp.exp2(lg - m_new)
            l_ref[pr] = alpha * l_ref[pr] + jnp.sum(p, axis=0, keepdims=True)
            pv = jnp.dot(vt_ref[c, kv * HEAD_DIM:(kv + 1) * HEAD_DIM, :], p.astype(MXU_DTYPE),
                         preferred_element_type=jnp.float32)
            acc_ref[pr] = alpha * acc_ref[pr] + pv
            m_ref[pr] = m_new

    n_far2 = (jnp.maximum(qi - 1, 0) // SUB) // 2

    def far_pair(i, carry):
        logits(2 * i + 1, lg1_ref)
        softmax_pv(2 * i, lg0_ref, near=False)
        logits(2 * i + 2, lg0_ref)
        softmax_pv(2 * i + 1, lg1_ref, near=False)
        return carry

    logits(0, lg0_ref)
    lax.fori_loop(0, n_far2, far_pair, 0)

    def near_chunk(c, carry):
        logits(c, lg1_ref)
        softmax_pv(c, lg1_ref, near=True)
        return carry

    lax.fori_loop(2 * n_far2, nkc, near_chunk, 0)

    for j in range(GROUP):
        half = slice((j % 2) * TQ, (j % 2 + 1) * TQ)
        lo, hi = j // 2, j // 2 + GROUP // 2
        pair_t = jnp.concatenate([acc_ref[lo][:, half] / l_ref[lo][:, half],
                                  acc_ref[hi][:, half] / l_ref[hi][:, half]], axis=0)
        sl = slice(j * LANE, (j + 1) * LANE)
        o_ref[:, sl] = (pair_t.T * _silu(ga_ref[:, sl])).astype(o_ref.dtype)


def _attend(iqt, iwt, qzt, ga, ik, k, vt, toep_t, kk):
    bsz, lq = ga.shape[:2]
    nq = lq // TQ
    nc = k.shape[1]
    idx_bits = max(1, int(nc * KC - 1).bit_length())
    lanes = lambda a: pl.BlockSpec(a.shape[:-1] + (TQ,), lambda b, i: (0,) * (a.ndim - 1) + (b * nq + i,))
    whole = lambda a: pl.BlockSpec((None,) + a.shape[1:], lambda b, i: (b,) + (0,) * (a.ndim - 1))
    rows = pl.BlockSpec((None, TQ, N_HEADS * HEAD_DIM), lambda b, i: (b, i, 0))
    return pl.pallas_call(
        functools.partial(_attend_kernel, kk=kk, idx_bits=idx_bits),
        grid=(bsz, nq),
        in_specs=[lanes(iqt), lanes(iwt), lanes(qzt), rows, whole(ik), whole(k), whole(vt),
                  pl.BlockSpec(toep_t.shape, lambda b, i: (0, 0, 0, 0))],
        out_specs=rows,
        out_shape=jax.ShapeDtypeStruct((bsz, lq, N_HEADS * HEAD_DIM), MXU_DTYPE),
        scratch_shapes=[pltpu.VMEM((nc, KC, TQ), jnp.int32), pltpu.VMEM((nc, KC, TQ), jnp.int32),
                        pltpu.VMEM((N_LEVELS, nc, TQ, TQ), jnp.int32), pltpu.VMEM((nc, TQ, TQ), jnp.int32),
                        pltpu.VMEM((nc, TQ, TQ), jnp.int32),
                        pltpu.VMEM((N_HEADS // 2, 1, 2 * TQ), jnp.float32),
                        pltpu.VMEM((N_HEADS // 2, 1, 2 * TQ), jnp.float32),
                        pltpu.VMEM((N_HEADS // 2, HEAD_DIM, 2 * TQ), jnp.float32),
                        pltpu.VMEM((KC, N_HEADS * TQ), jnp.float32), pltpu.VMEM((KC, N_HEADS * TQ), jnp.float32),
                        pltpu.VMEM((LANE, N_HEADS * TQ), MXU_DTYPE)],
        compiler_params=_cparams(2),
        name="attend",
    )(iqt, iwt, qzt, ga, ik, k, vt, toep_t)


def _out_kernel(x_ref, mc_ref, ma_ref, wc_ref, wa_ref, o_ref):
    o_ref[...] = (x_ref[...]
                  + jnp.dot(mc_ref[...], wc_ref[...], preferred_element_type=jnp.float32)
                  + jnp.dot(ma_ref[...], wa_ref[...], preferred_element_type=jnp.float32))


def _out_final_kernel(x_ref, mc_ref, ma_ref, wc_ref, wa_ref, g_ref, o_ref, y_ref):
    x = (x_ref[...]
         + jnp.dot(mc_ref[...], wc_ref[...], preferred_element_type=jnp.float32)
         + jnp.dot(ma_ref[...], wa_ref[...], preferred_element_type=jnp.float32))
    o_ref[...] = x
    ms = jnp.mean(x * x, axis=-1, keepdims=True)
    y_ref[...] = x * lax.rsqrt(ms + EPS) * g_ref[...]


def _out_proj(x, mc, ma, wc, wa, tm, final_g=None):
    r, d = x.shape
    c = mc.shape[1]
    rows = lambda w: pl.BlockSpec((tm, w), lambda i: (i, 0))
    full = lambda a: pl.BlockSpec(a.shape, lambda i: (0, 0))
    if final_g is None:
        return pl.pallas_call(
            _out_kernel, grid=(r // tm,),
            in_specs=[rows(d), rows(c), rows(c), full(wc), full(wa)],
            out_specs=rows(d), out_shape=jax.ShapeDtypeStruct((r, d), jnp.float32),
            compiler_params=_cparams(1), name="out",
        )(x, mc, ma, wc, wa)
    g = final_g.reshape(1, d)
    return pl.pallas_call(
        _out_final_kernel, grid=(r // tm,),
        in_specs=[rows(d), rows(c), rows(c), full(wc), full(wa), full(g)],
        out_specs=[rows(d), rows(d)], out_shape=[jax.ShapeDtypeStruct((r, d), jnp.float32)] * 2,
        compiler_params=_cparams(1), name="out_final",
    )(x, mc, ma, wc, wa, g)


def _sconv_kernel(st_ref, u_ref, gc_ref, w_ref, b_ref, lg_ref, lb_ref, o_ref):
    acc = jnp.zeros(u_ref.shape, jnp.float32)
    for j in range(CONV_W - 1):
        acc = acc + w_ref[j:j + 1, :] * st_ref[j]
    acc = acc + w_ref[CONV_W - 1:CONV_W, :] * u_ref[...]
    y = acc + b_ref[...]
    mu = jnp.mean(y, axis=-1, keepdims=True)
    dev = y - mu
    var = jnp.mean(dev * dev, axis=-1, keepdims=True)
    yn = dev * lax.rsqrt(var + LN_EPS) * lg_ref[...] + lb_ref[...]
    o_ref[...] = (_silu(yn) * _silu(gc_ref[...])).astype(o_ref.dtype)


def _sample_conv(state_t, u, gc, w, b, lg, lb):
    db, c = u.shape
    row = lambda a: a.reshape(1, c)
    full = lambda a: pl.BlockSpec(a.shape, lambda i: (0,) * a.ndim)
    args = (state_t, u, gc, w, row(b), row(lg), row(lb))
    return pl.pallas_call(
        _sconv_kernel, grid=(1,), in_specs=[full(a) for a in args],
        out_specs=pl.BlockSpec((db, c), lambda i: (0, 0)),
        out_shape=jax.ShapeDtypeStruct((db, c), MXU_DTYPE),
        compiler_params=_cparams(1), name="sconv",
    )(*args)


def _dec_score_kernel(pt_ref, iq_ref, iw_ref, ikn_ref, *refs, n_pages, width):
    del pt_ref
    pages, o_ref = refs[:n_pages], refs[n_pages]
    iq = iq_ref[...][:, :IDX_DIM]
    w = iw_ref[...]
    pieces = []
    for p in range(n_pages):
        s = jnp.dot(iq, pages[p][...].astype(MXU_DTYPE), preferred_element_type=jnp.float32)
        pieces.append(jnp.sum(w * jnp.maximum(s, 0.0), axis=0, keepdims=True))
    ikn = ikn_ref[...][:, :IDX_DIM].astype(MXU_DTYPE).astype(jnp.float32)
    sn = jnp.sum(iq.astype(jnp.float32) * ikn, axis=-1, keepdims=True)
    scn = jnp.sum(w * jnp.maximum(sn, 0.0), axis=0, keepdims=True)
    page = pieces[0].shape[1]
    lane = lax.broadcasted_iota(jnp.int32, (1, width - n_pages * page), 1)
    pieces.append(jnp.where(lane == 0, scn, -jnp.inf))
    o_ref[...] = jnp.concatenate(pieces, axis=1)


def _dec_scores(page_table, iq3, iw3, tail3, cache_ik, layer, width):
    db, n_pages = page_table.shape
    di, page = cache_ik.shape[2:]
    seq = lambda a: pl.BlockSpec((None,) + a.shape[1:], lambda b, pt: (b, 0, 0))
    page_spec = lambda p: pl.BlockSpec((None, None, di, page), lambda b, pt: (layer, pt[b, p], 0, 0))
    return pl.pallas_call(
        functools.partial(_dec_score_kernel, n_pages=n_pages, width=width),
        grid_spec=pltpu.PrefetchScalarGridSpec(
            num_scalar_prefetch=1, grid=(db,),
            in_specs=[seq(iq3), seq(iw3), seq(tail3)] + [page_spec(p) for p in range(n_pages)],
            out_specs=pl.BlockSpec((None, 1, width), lambda b, pt: (b, 0, 0))),
        out_shape=jax.ShapeDtypeStruct((db, 1, width), jnp.float32),
        compiler_params=_cparams(1), name="dec_score",
    )(page_table, iq3, iw3, tail3, *([cache_ik] * n_pages))


def _dec_select_kernel(s_ref, o_ref, keys_ref, eidx_ref, dig_ref, work_ref, alive_ref, *, kk, idx_bits):
    nc = keys_ref.shape[0]
    for c in range(nc):
        keys_ref[c] = _to_key(s_ref[c * KC:(c + 1) * KC, :])
        _pack_digits(keys_ref, dig_ref, c)
    thr = _threshold(keys_ref, eidx_ref, dig_ref, work_ref, alive_ref, nc, kk, idx_bits)
    for c in range(nc):
        o_ref[c * KC:(c + 1) * KC, :] = jnp.where(keys_ref[c] >= thr, 0.0, NEG)


def _dec_select(scores_t, kk):
    width, db = scores_t.shape
    nc = width // KC
    idx_bits = max(1, int(width - 1).bit_length())
    spec = pl.BlockSpec((width, db), lambda i: (0, 0))
    return pl.pallas_call(
        functools.partial(_dec_select_kernel, kk=kk, idx_bits=idx_bits),
        grid=(1,), in_specs=[spec], out_specs=spec,
        out_shape=jax.ShapeDtypeStruct((width, db), jnp.float32),
        scratch_shapes=[pltpu.VMEM((nc, KC, db), jnp.int32), pltpu.VMEM((nc, KC, db), jnp.int32),
                        pltpu.VMEM((N_LEVELS, nc, TQ, db), jnp.int32), pltpu.VMEM((nc, TQ, db), jnp.int32),
                        pltpu.VMEM((nc, TQ, db), jnp.int32)],
        compiler_params=_cparams(1), name="dec_select",
    )(scores_t)


def _dec_attend_kernel(pt_ref, qz_ref, kvn_ref, madd_ref, bias_ref, ga_ref, *refs, n_pages):
    del pt_ref
    kpages, vpages, o_ref = refs[:n_pages], refs[n_pages:2 * n_pages], refs[2 * n_pages]
    qz = qz_ref[...]
    page = kpages[0].shape[1]
    past = n_pages * page
    nt = (((1,), (1,)), ((), ()))
    logits = []
    for p in range(n_pages):
        sl = slice(p * page, (p + 1) * page)
        lg = jnp.dot(qz, kpages[p][...].astype(MXU_DTYPE), preferred_element_type=jnp.float32)
        logits.append(lg + bias_ref[:, sl] + madd_ref[:, sl])
    kvn = kvn_ref[...].astype(MXU_DTYPE).astype(jnp.float32)
    lgn = jnp.sum(qz.astype(jnp.float32) * kvn[:, :LANE], axis=-1, keepdims=True)
    lgn = lgn + bias_ref[:, past:past + 1] + madd_ref[:, past:past + 1]
    m = lgn
    for lg in logits:
        m = jnp.maximum(m, jnp.max(lg, axis=-1, keepdims=True))
    pn = jnp.exp2(lgn - m)
    den = pn
    o = pn.astype(MXU_DTYPE).astype(jnp.float32) * kvn[:, LANE:]
    for p in range(n_pages):
        pp = jnp.exp2(logits[p] - m)
        den = den + jnp.sum(pp, axis=-1, keepdims=True)
        o = o + lax.dot_general(pp.astype(MXU_DTYPE), vpages[p][...].astype(MXU_DTYPE), nt,
                                preferred_element_type=jnp.float32)
    o = o / den
    lane = lax.broadcasted_iota(jnp.int32, (GROUP, LANE), 1)
    pair = jnp.where(lane < HEAD_DIM, o[:GROUP], o[GROUP:])
    o_ref[...] = (pair * _silu(ga_ref[...])).astype(o_ref.dtype)


def _dec_attend(page_table, qz3, kvn3, madd3, bias, ga3, cache_k, cache_v, layer):
    db, n_pages = page_table.shape
    feat, page = cache_k.shape[2:]
    seq = lambda a: pl.BlockSpec((None,) + a.shape[1:], lambda b, pt: (b, 0, 0))
    page_spec = lambda p: pl.BlockSpec((None, None, feat, page), lambda b, pt: (layer, pt[b, p], 0, 0))
    return pl.pallas_call(
        functools.partial(_dec_attend_kernel, n_pages=n_pages),
        grid_spec=pltpu.PrefetchScalarGridSpec(
            num_scalar_prefetch=1, grid=(db,),
            in_specs=[seq(qz3), seq(kvn3), seq(madd3), pl.BlockSpec(bias.shape, lambda b, pt: (0, 0)), seq(ga3)]
                     + [page_spec(p) for p in range(n_pages)] * 2,
            out_specs=pl.BlockSpec((None, GROUP, LANE), lambda b, pt: (b, 0, 0))),
        out_shape=jax.ShapeDtypeStruct((db, GROUP, LANE), MXU_DTYPE),
        compiler_params=_cparams(1), name="dec_attend",
    )(page_table, qz3, kvn3, madd3, bias, ga3, *([cache_k] * n_pages), *([cache_v] * n_pages))


def _round_up(x, m):
    return -(-x // m) * m


def _row_tile(rows):
    for f in (5, 4, 3, 2, 1):
        if rows % (f * TQ) == 0:
            return f * TQ
    return rows


def kernel(x_prompt, x_sample, cache_k, cache_v, cache_idx_k, state_conv, page_table, meta_tokens, rel_bias,
           norm_g, w_in, conv_w, conv_b, conv_ln_g, conv_ln_b, w_out, final_norm_g):
    bsz, seq, d = x_prompt.shape
    depth = w_in.shape[0]
    lp = seq + N_META
    lq = _round_up(lp, TQ)
    lk = _round_up(lp, KC)
    nq, nc = lq // TQ, lk // KC
    kk_p = min(TOPK_MAX, lp // 4)
    db = x_sample.shape[0]
    n_pool, page = cache_k.shape[1:3]
    n_pages = page_table.shape[1]
    past = n_pages * page
    kk_s = min(TOPK_MAX, (past + 1) // 4)
    width_s = _round_up(past + 1, KC)
    feat = N_KV_HEADS * HEAD_DIM
    tm = _row_tile(bsz * lq)

    xp = jnp.concatenate([jnp.broadcast_to(meta_tokens[None].astype(x_prompt.dtype), (bsz, N_META, d)), x_prompt,
                          jnp.zeros((bsz, lq - lp, d), x_prompt.dtype)], axis=1).reshape(bsz * lq, d)
    xs = x_sample.reshape(db, d)
    toep_t = _bias_tiles(rel_bias)
    bias_s = jnp.moveaxis(rel_bias[_t5_bucket(past - jnp.arange(width_s, dtype=jnp.int32))], -1, 0) * LOG2E
    cache_k4 = jnp.transpose(cache_k, (0, 1, 3, 4, 2)).reshape(depth, n_pool, feat, page)
    cache_v4 = jnp.transpose(cache_v, (0, 1, 3, 4, 2)).reshape(depth, n_pool, feat, page)
    cache_ik4 = jnp.swapaxes(cache_idx_k, 2, 3)

    def key_chunks(a):
        return jnp.pad(a, ((0, 0), (0, lk - lq), (0, 0))).astype(MXU_DTYPE).reshape(bsz, nc, KC, LANE)

    def key_chunks_t(a):
        a = jnp.pad(a.reshape(LANE, bsz, lq), ((0, 0), (0, 0), (0, lk - lq)))
        return jnp.transpose(a.reshape(LANE, bsz, nc, KC), (1, 2, 0, 3))

    kp, vp, ikp, cp, ksm, vsm, iks, cs = ([] for _ in range(8))
    yp = ys = None
    for l in range(depth):
        w = _prep_w_in(w_in[l])
        wc = w_out[l, :C_CONV].astype(MXU_DTYPE)
        wa = w_out[l, C_CONV:][_PERM].astype(MXU_DTYPE)
        last = l == depth - 1

        u, gc, kv, ga, tail, qzt, iqzt, vt, iwt = _project(xp, norm_g[l], w, tm, feature_major=True)
        b3 = lambda a: a.reshape(bsz, lq, a.shape[-1])
        u3, kv3, tail3 = b3(u), b3(kv), b3(tail)
        mixc = _conv_branch(u3, b3(gc), conv_w[l], conv_b[l], conv_ln_g[l], conv_ln_b[l])
        lane = jnp.arange(LANE)
        ik = key_chunks(jnp.where(lane < IDX_DIM, tail3, 0.0))
        mixa = _attend(iqzt.reshape(N_IDX_HEADS, LANE, bsz * lq), iwt, qzt.reshape(N_HEADS, LANE, bsz * lq), b3(ga),
                       ik, key_chunks(kv3[..., :feat]), key_chunks_t(vt), toep_t, kk_p)
        res = _out_proj(xp, mixc.reshape(bsz * lq, C_CONV), mixa.reshape(bsz * lq, -1), wc, wa, tm,
                        final_norm_g if last else None)
        xp, yp = res if last else (res, None)
        kp.append(kv3[:, :lp, :feat].reshape(bsz, lp, N_KV_HEADS, HEAD_DIM))
        vp.append(kv3[:, :lp, feat:].reshape(bsz, lp, N_KV_HEADS, HEAD_DIM))
        ikp.append(tail3[:, :lp, :IDX_DIM])
        cp.append(u3[:, lp - (CONV_W - 1):lp])

        u, gc, kv, ga, tail, qz, iqz = _project(xs, norm_g[l], w, db, feature_major=False)
        mixc = _sample_conv(jnp.transpose(state_conv[l], (1, 0, 2)), u, gc,
                            conv_w[l], conv_b[l], conv_ln_g[l], conv_ln_b[l])
        iw3 = tail[:, IDX_DIM:IDX_DIM + N_IDX_HEADS].reshape(db, N_IDX_HEADS, 1)
        scores = _dec_scores(page_table, iqz.reshape(db, N_IDX_HEADS, LANE), iw3, tail.reshape(db, 1, LANE),
                             cache_ik4, l, width_s)
        madd = _dec_select(scores.reshape(db, width_s).T, kk_s).T
        mixa = _dec_attend(page_table, qz.reshape(db, N_HEADS, LANE), kv.reshape(db, 1, 2 * feat),
                           madd.reshape(db, 1, width_s), bias_s, ga.reshape(db, GROUP, LANE),
                           cache_k4, cache_v4, l)
        res = _out_proj(xs, mixc, mixa.reshape(db, -1), wc, wa, db, final_norm_g if last else None)
        xs, ys = res if last else (res, None)
        ksm.append(kv[:, :feat].reshape(db, 1, N_KV_HEADS, HEAD_DIM))
        vsm.append(kv[:, feat:].reshape(db, 1, N_KV_HEADS, HEAD_DIM))
        iks.append(tail[:, None, :IDX_DIM])
        cs.append(jnp.concatenate([state_conv[l][:, 1:], u[:, None]], axis=1))

    y_prompt = yp.reshape(bsz, lq, d)[:, N_META:lp]
    y_sample = ys.reshape(db, 1, d)
    return (y_prompt, y_sample, jnp.stack(kp), jnp.stack(vp), jnp.stack(ikp), jnp.stack(cp),
            jnp.stack(ksm), jnp.stack(vsm), jnp.stack(iks), jnp.stack(cs))
```

```python
import functools
import math

import numpy as np
import jax
import jax.numpy as jnp
from jax import lax
from jax.experimental import pallas as pl
from jax.experimental.pallas import tpu as pltpu

N_HEADS = 8
N_KV_HEADS = 2
GROUP = N_HEADS // N_KV_HEADS
HEAD_DIM = 64
N_IDX_HEADS = 4
IDX_DIM = 64
C_CONV = 512
CONV_W = 31
TOPK_MAX = 256
N_META = 16
NUM_BUCKETS = 32
MAX_DISTANCE = 128
EPS = 1e-6
LN_EPS = 1e-5

LANE = 128
ROWS = 8
TQ = 128
KC = 512
SUB = KC // TQ
N_ACC = 4
DIGIT_BITS = 7
N_LEVELS = 4
LOW_BITS = 32 - DIGIT_BITS * N_LEVELS
DIGIT_MASK = (1 << DIGIT_BITS) - 1
HALO = 32
CONV_SUB = 64
MXU_DTYPE = jnp.bfloat16
NEG = -1e30
LOG2E = 1.4426950408889634
INT_MIN = -2 ** 31
KEY_NEG_INF = -0x7F800000
IDX_BIG = 0x3FFFFFFF
VMEM_LIMIT = 56 * 1024 * 1024

_Q_OFF = 3 * C_CONV
_KV_OFF = _Q_OFF + N_HEADS * LANE
_GA_OFF = _KV_OFF + 2 * N_KV_HEADS * HEAD_DIM
_IQ_OFF = _GA_OFF + N_HEADS * HEAD_DIM
_TAIL_OFF = _IQ_OFF + N_IDX_HEADS * LANE
_W_COLS = _TAIL_OFF + LANE

_HEAD_ORDER = [h for j in range(GROUP) for h in (j, j + GROUP)]
_PERM = np.concatenate([np.arange(h * HEAD_DIM, (h + 1) * HEAD_DIM) for h in _HEAD_ORDER])

_PROJ_SIZES = (C_CONV, C_CONV, C_CONV, N_HEADS * HEAD_DIM, N_KV_HEADS * HEAD_DIM, N_KV_HEADS * HEAD_DIM,
               N_HEADS * HEAD_DIM, N_IDX_HEADS * IDX_DIM, IDX_DIM, N_IDX_HEADS)
_PROJ_SPLITS = tuple(int(s) for s in np.cumsum(_PROJ_SIZES)[:-1])


def _cparams(n_axes):
    return pltpu.CompilerParams(dimension_semantics=("arbitrary",) * n_axes, vmem_limit_bytes=VMEM_LIMIT)


def _silu(x):
    return x * jax.nn.sigmoid(x)


def _prep_w_in(w):
    ua, ub, gc, q, k, v, ga, iq, ik, iw = jnp.split(w.T, _PROJ_SPLITS, axis=0)
    d = w.shape[0]
    z = jnp.zeros((HEAD_DIM, d), w.dtype)
    qz = []
    for h in range(N_HEADS):
        qh = q[h * HEAD_DIM:(h + 1) * HEAD_DIM]
        qz += [qh, z] if h // GROUP == 0 else [z, qh]
    iqz = []
    for h in range(N_IDX_HEADS):
        iqz += [iq[h * IDX_DIM:(h + 1) * IDX_DIM], z]
    ga_p = [ga[h * HEAD_DIM:(h + 1) * HEAD_DIM] for h in _HEAD_ORDER]
    tail = [ik, iw, jnp.zeros((LANE - IDX_DIM - N_IDX_HEADS, d), w.dtype)]
    out = jnp.concatenate([ua, ub, gc] + qz + [k, v] + ga_p + iqz + tail, axis=0)
    assert out.shape[0] == _W_COLS
    return out.astype(MXU_DTYPE)


def _t5_bucket(rel):
    n = jnp.maximum(rel, 0)
    max_exact = NUM_BUCKETS // 2
    large = max_exact + (jnp.log(jnp.maximum(n, 1).astype(jnp.float32) / max_exact)
                         / math.log(MAX_DISTANCE / max_exact)
                         * (NUM_BUCKETS - max_exact)).astype(jnp.int32)
    large = jnp.minimum(large, NUM_BUCKETS - 1)
    return jnp.where(n < max_exact, n, large)


def _bias_tiles(rel_bias):
    far = rel_bias[NUM_BUCKETS - 1]
    d = jnp.arange(2 * TQ, dtype=jnp.int32)
    dists = (jnp.where(d < TQ, d, 0),
             jnp.where(d < TQ, d + TQ, d - TQ))
    tiles = []
    for dist in dists:
        v = ((rel_bias[_t5_bucket(dist)] - far) * LOG2E).T
        rep = jnp.tile(v, (1, TQ))[:, :TQ * (2 * TQ - 1)].reshape(N_HEADS, TQ, 2 * TQ - 1)
        tiles.append(rep[:, :, :TQ])
    tiles.append(jnp.zeros_like(tiles[0]))
    return jnp.stack(tiles).astype(jnp.float32)


_NT = (((1,), (1,)), ((), ()))
_Q_SCALE = HEAD_DIM ** -0.5 * LOG2E
_IQ_SCALE = IDX_DIM ** -0.5
_IW_SCALE = N_IDX_HEADS ** -0.5
_IW_ROWS = 16


def _proj_kernel(x_ref, g_ref, w_ref, u_ref, gc_ref, kv_ref, ga_ref, tail_ref, *q_refs, feature_major):
    x = x_ref[...]
    ms = jnp.mean(x * x, axis=-1, keepdims=True)
    xn = (x * lax.rsqrt(ms + EPS) * g_ref[...]).astype(MXU_DTYPE)

    def mm(lo, hi):
        return lax.dot_general(xn, w_ref[lo:hi, :], _NT, preferred_element_type=jnp.float32)

    def mm_t(lo, hi):
        return lax.dot_general(w_ref[lo:hi, :], xn, _NT, preferred_element_type=jnp.float32)

    u_ref[...] = mm(0, C_CONV) * jax.nn.sigmoid(mm(C_CONV, 2 * C_CONV))
    gc_ref[...] = mm(2 * C_CONV, 3 * C_CONV)
    kv_ref[...] = mm(_KV_OFF, _GA_OFF)
    ga_ref[...] = mm(_GA_OFF, _IQ_OFF)
    t = mm(_TAIL_OFF, _W_COLS)
    lane = lax.broadcasted_iota(jnp.int32, t.shape, 1)
    tail_ref[...] = t * jnp.where(lane >= IDX_DIM, _IW_SCALE, 1.0)
    if feature_major:
        qzt_ref, iqzt_ref, vt_ref, iwt_ref = q_refs
        qzt_ref[...] = (mm_t(_Q_OFF, _KV_OFF) * _Q_SCALE).astype(qzt_ref.dtype)
        iqzt_ref[...] = (mm_t(_IQ_OFF, _TAIL_OFF) * _IQ_SCALE).astype(iqzt_ref.dtype)
        vt_ref[...] = mm_t(_KV_OFF + N_KV_HEADS * HEAD_DIM, _GA_OFF).astype(vt_ref.dtype)
        iwt_ref[...] = mm_t(_TAIL_OFF + IDX_DIM, _TAIL_OFF + IDX_DIM + _IW_ROWS) * _IW_SCALE
    else:
        qz_ref, iqz_ref = q_refs
        qz_ref[...] = (mm(_Q_OFF, _KV_OFF) * _Q_SCALE).astype(qz_ref.dtype)
        iqz_ref[...] = (mm(_IQ_OFF, _TAIL_OFF) * _IQ_SCALE).astype(iqz_ref.dtype)


def _project(x, g, w, tm, feature_major):
    r, d = x.shape
    assert r % tm == 0
    rows = lambda c: pl.BlockSpec((tm, c), lambda i: (i, 0))
    cols = lambda c: pl.BlockSpec((c, tm), lambda i: (0, i))
    feat = N_KV_HEADS * HEAD_DIM
    specs = [rows(C_CONV), rows(C_CONV), rows(2 * feat), rows(N_HEADS * HEAD_DIM), rows(LANE)]
    shapes = [jax.ShapeDtypeStruct((r, s.block_shape[1]), jnp.float32) for s in specs]
    if feature_major:
        extra = [(N_HEADS * LANE, MXU_DTYPE), (N_IDX_HEADS * LANE, MXU_DTYPE), (feat, MXU_DTYPE),
                 (_IW_ROWS, jnp.float32)]
        specs += [cols(c) for c, _ in extra]
        shapes += [jax.ShapeDtypeStruct((c, r), t) for c, t in extra]
    else:
        extra = [(N_HEADS * LANE, MXU_DTYPE), (N_IDX_HEADS * LANE, MXU_DTYPE)]
        specs += [rows(c) for c, _ in extra]
        shapes += [jax.ShapeDtypeStruct((r, c), t) for c, t in extra]
    return pl.pallas_call(
        functools.partial(_proj_kernel, feature_major=feature_major),
        grid=(r // tm,),
        in_specs=[rows(d), pl.BlockSpec((1, d), lambda i: (0, 0)), pl.BlockSpec((_W_COLS, d), lambda i: (0, 0))],
        out_specs=specs,
        out_shape=shapes,
        compiler_params=_cparams(1),
        name="proj",
    )(x, g.reshape(1, d), w)


def _conv_kernel(prev_ref, cur_ref, gc_ref, w_ref, b_ref, lg_ref, lb_ref, o_ref, ext_ref):
    i = pl.program_id(1)
    ext_ref[0, 0:HALO, :] = jnp.where(i > 0, prev_ref[...], 0.0)
    ext_ref[0, HALO:HALO + TQ, :] = cur_ref[...]
    n_ext = HALO + TQ
    for s in range(1, ROWS):
        ext_ref[s, 0:n_ext - ROWS, :] = ext_ref[0, s:s + n_ext - ROWS, :]
    off = HALO - (CONV_W - 1)
    for r0 in range(0, TQ, CONV_SUB):
        acc = jnp.zeros((CONV_SUB, C_CONV), jnp.float32)
        for j in range(CONV_W):
            s = (off + j) % ROWS
            lo = r0 + off + j - s
            acc = acc + w_ref[j:j + 1, :] * ext_ref[s, lo:lo + CONV_SUB, :]
        y = acc + b_ref[...]
        mu = jnp.mean(y, axis=-1, keepdims=True)
        dev = y - mu
        var = jnp.mean(dev * dev, axis=-1, keepdims=True)
        yn = dev * lax.rsqrt(var + LN_EPS) * lg_ref[...] + lb_ref[...]
        o_ref[r0:r0 + CONV_SUB, :] = (_silu(yn) * _silu(gc_ref[r0:r0 + CONV_SUB, :])).astype(o_ref.dtype)


def _conv_branch(u, gc, w, b, lg, lb):
    bsz, lq, c = u.shape
    per = TQ // HALO
    row = lambda a: a.reshape(1, c)
    vec = pl.BlockSpec((1, c), lambda bi, i: (0, 0))
    tile = pl.BlockSpec((None, TQ, c), lambda bi, i: (bi, i, 0))
    return pl.pallas_call(
        _conv_kernel,
        grid=(bsz, lq // TQ),
        in_specs=[pl.BlockSpec((None, HALO, c), lambda bi, i: (bi, jnp.maximum(i * per - 1, 0), 0)),
                  tile, tile, pl.BlockSpec((CONV_W, c), lambda bi, i: (0, 0)), vec, vec, vec],
        out_specs=tile,
        out_shape=jax.ShapeDtypeStruct((bsz, lq, c), MXU_DTYPE),
        scratch_shapes=[pltpu.VMEM((ROWS, HALO + TQ, c), jnp.float32)],
        compiler_params=_cparams(2),
        name="conv",
    )(u, u, gc, w, row(b), row(lg), row(lb))


_INT_MIN = np.int32(INT_MIN)
_BYTE_ONES = np.int32(0x01010101)
_BYTE_LOW = np.int32(0x7F7F7F7F)
_GUARD = np.int32(0x80808080 - (1 << 32))
assert SUB == 4 and DIGIT_BITS == 7


def _to_key(s):
    bits = lax.bitcast_convert_type(s, jnp.int32)
    return jnp.where(bits < 0, _INT_MIN - bits, bits)


def _for_chunks(nkc, body, init, unroll):
    n_main = nkc // unroll

    def main(i, carry):
        for k in range(unroll):
            carry = body(i * unroll + k, carry)
        return carry

    carry = lax.fori_loop(0, n_main, main, init)
    return lax.fori_loop(n_main * unroll, nkc, body, carry)


def _count(ref, nkc, pred):
    q = ref.shape[2]

    def body(c, accs):
        accs = list(accs)
        for n, r in enumerate(range(0, KC, ROWS)):
            accs[n % N_ACC] = accs[n % N_ACC] + jnp.where(pred(ref[c, r:r + ROWS, :]), 1.0, 0.0)
        return tuple(accs)

    accs = _for_chunks(nkc, body, (jnp.zeros((ROWS, q), jnp.float32),) * N_ACC, unroll=2)
    return jnp.sum(sum(accs[1:], accs[0]), axis=0, keepdims=True)


def _pack_digits(keys_ref, dig_ref, c):
    for lv in range(N_LEVELS):
        shift = 32 - DIGIT_BITS * (lv + 1)
        word = None
        for j in range(SUB):
            u = keys_ref[c, j * TQ:(j + 1) * TQ, :] ^ _INT_MIN
            field = lax.shift_right_logical(u, shift) & DIGIT_MASK
            field = field if j == 0 else lax.shift_left(field, 8 * j)
            word = field if word is None else word | field
        dig_ref[lv, c] = word


def _count_fields(work_ref, nkc, cand_bytes):
    q = work_ref.shape[2]

    def body(c, accs):
        accs = list(accs)
        for n, r in enumerate(range(0, TQ, ROWS)):
            diff = work_ref[c, r:r + ROWS, :] - cand_bytes
            accs[n % N_ACC] = accs[n % N_ACC] + (lax.shift_right_logical(diff, DIGIT_BITS) & _BYTE_ONES)
        return tuple(accs)

    accs = _for_chunks(nkc, body, (jnp.zeros((ROWS, q), jnp.int32),) * N_ACC, unroll=4)
    total = jnp.zeros((ROWS, q), jnp.int32)
    for a in accs:
        total = total + ((a & 0xFF) + (lax.shift_right_logical(a, 8) & 0xFF)
                         + (lax.shift_right_logical(a, 16) & 0xFF) + lax.shift_right_logical(a, 24))
    return jnp.sum(total.astype(jnp.float32), axis=0, keepdims=True)


def _threshold(keys_ref, eidx_ref, dig_ref, work_ref, alive_ref, nkc, kk, idx_bits):
    q = keys_ref.shape[2]
    kf = jnp.float32(kk)
    assert keys_ref.shape[0] * (TQ // ROWS) <= 255 * N_ACC
    zero = jnp.zeros((1, q), jnp.float32)
    prefix = jnp.zeros((1, q), jnp.int32)
    above, done, cnt_thr = zero, zero, jnp.full((1, q), kf + 1.0, jnp.float32)

    for lv in range(N_LEVELS):
        def load_fields(c, carry):
            word = dig_ref[lv, c]
            if lv > 0:
                word = word & alive_ref[c]
            work_ref[c] = word | _GUARD
            return carry

        _for_chunks(nkc, load_fields, 0, unroll=4)

        def digit_bit(i, state):
            digit, cnt_thr, cnt_rej, done = state
            cand = digit | lax.shift_left(jnp.int32(1), DIGIT_BITS - 1 - i)
            cnt_alive = _count_fields(work_ref, nkc, cand * _BYTE_ONES)
            cnt = above + cnt_alive
            take = (cnt >= kf) & (done == 0.0)
            digit = jnp.where(take, cand, digit)
            cnt_thr = jnp.where(take, cnt, cnt_thr)
            cnt_rej = jnp.where((cnt < kf) & (done == 0.0), cnt_alive, cnt_rej)
            done = jnp.where(take & (cnt == kf), 1.0, done)
            return digit, cnt_thr, cnt_rej, done

        digit, cnt_thr, cnt_rej, done = lax.fori_loop(
            0, DIGIT_BITS, digit_bit, (jnp.zeros((1, q), jnp.int32), cnt_thr, zero, done))
        above = above + cnt_rej
        prefix = prefix | lax.shift_left(digit, 32 - DIGIT_BITS * (lv + 1))

        if lv + 1 < N_LEVELS:
            digit_bytes = digit * _BYTE_ONES

            def narrow(c, carry):
                differs = lax.shift_right_logical(((dig_ref[lv, c] ^ digit_bytes) + _BYTE_LOW) & _GUARD, DIGIT_BITS)
                same = (_BYTE_ONES - differs) * DIGIT_MASK
                alive_ref[c] = same if lv == 0 else alive_ref[c] & same
                return carry

            _for_chunks(nkc, narrow, 0, unroll=4)

    def low_bit(i, state):
        thr, cnt_thr, done = state
        cand = thr | lax.shift_left(jnp.int32(1), LOW_BITS - 1 - i)
        cnt = _count(keys_ref, nkc, lambda x: x >= cand)
        take = (cnt >= kf) & (done == 0.0)
        return (jnp.where(take, cand, thr), jnp.where(take, cnt, cnt_thr),
                jnp.where(take & (cnt == kf), 1.0, done))

    n_low = jnp.where(jnp.min(done) == 0.0, LOW_BITS, 0)
    thr, cnt_thr, _ = lax.fori_loop(0, n_low, low_bit, (prefix ^ _INT_MIN, cnt_thr, done))
    admissible = thr != KEY_NEG_INF
    surplus = (cnt_thr > kf) & admissible

    @pl.when(jnp.max(jnp.where(surplus, 1.0, 0.0)) > 0.0)
    def _():
        def mark_ties(c, n_gt):
            row = lax.broadcasted_iota(jnp.int32, (ROWS, q), 0)
            for r in range(0, KC, ROWS):
                x = keys_ref[c, r:r + ROWS, :]
                n_gt = n_gt + jnp.where(x > thr, 1.0, 0.0)
                eidx_ref[c, r:r + ROWS, :] = jnp.where(x == thr, c * KC + r + row, IDX_BIG)
            return n_gt

        n_gt = lax.fori_loop(0, nkc, mark_ties, jnp.zeros((ROWS, q), jnp.float32))
        need = kf - jnp.sum(n_gt, axis=0, keepdims=True)

        def index_bit(i, cut):
            cand = cut | lax.shift_left(jnp.int32(1), idx_bits - 1 - i)
            cnt = _count(eidx_ref, nkc, lambda x: x < cand)
            return jnp.where(cnt < need, cand, cut)

        cut = lax.fori_loop(0, idx_bits, index_bit, jnp.zeros((1, q), jnp.int32))
        cut = jnp.where(surplus, cut, IDX_BIG - 1)

        def drop_losers(c, carry):
            for r in range(0, KC, ROWS):
                e = eidx_ref[c, r:r + ROWS, :]
                lost = (e != IDX_BIG) & (e > cut)
                keys_ref[c, r:r + ROWS, :] = jnp.where(lost, thr - 1, keys_ref[c, r:r + ROWS, :])
            return carry

        lax.fori_loop(0, nkc, drop_losers, 0)

    return jnp.where(admissible, thr, KEY_NEG_INF + 1)


def _attend_kernel(iqt_ref, iwt_ref, qzt_ref, ga_ref, ik_ref, k_ref, vt_ref, toep_ref, o_ref,
                   keys_ref, eidx_ref, dig_ref, work_ref, alive_ref, m_ref, l_ref, acc_ref, lg0_ref, lg1_ref,
                   wq_ref, *, kk, idx_bits):
    qi = pl.program_id(1)
    nkc = qi // SUB + 1
    qpos = qi * TQ + lax.broadcasted_iota(jnp.int32, (TQ, TQ), 1)
    krow = lax.broadcasted_iota(jnp.int32, (TQ, TQ), 0)
    for h in range(N_HEADS):
        wq_ref[:, h * TQ:(h + 1) * TQ] = qzt_ref[h]

    def score_chunk(c, carry):
        iqt = jnp.concatenate([iqt_ref[h] for h in range(N_IDX_HEADS)], axis=1)
        w = iwt_ref[...]
        for j in range(SUB):
            st = jnp.dot(ik_ref[c, j * TQ:(j + 1) * TQ, :], iqt, preferred_element_type=jnp.float32)
            s = jnp.zeros((TQ, TQ), jnp.float32)
            for h in range(N_IDX_HEADS):
                s = s + w[h:h + 1, :] * jnp.maximum(st[:, h * TQ:(h + 1) * TQ], 0.0)
            kpos = c * KC + j * TQ + krow
            keys_ref[c, j * TQ:(j + 1) * TQ, :] = _to_key(jnp.where(kpos <= qpos, s, -jnp.inf))
        _pack_digits(keys_ref, dig_ref, c)
        return carry

    lax.fori_loop(0, nkc, score_chunk, 0)
    thr = _threshold(keys_ref, eidx_ref, dig_ref, work_ref, alive_ref, nkc, kk, idx_bits)

    m_ref[...] = jnp.full(m_ref.shape, NEG, jnp.float32)
    l_ref[...] = jnp.zeros(l_ref.shape, jnp.float32)
    acc_ref[...] = jnp.zeros(acc_ref.shape, jnp.float32)

    def logits(c, lg_ref):
        lg_ref[...] = jnp.dot(k_ref[c], wq_ref[...], preferred_element_type=jnp.float32)

    def softmax_pv(c, lg_ref, near):
        md = jnp.where(keys_ref[c] >= thr, 0.0, NEG)
        md2 = jnp.concatenate([md, md], axis=1)
        if near:
            delta = [jnp.clip(qi - (c * SUB + j), 0, 2) for j in range(SUB)]
        for pr in range(N_HEADS // 2):
            kv = pr // (GROUP // 2)
            lg = lg_ref[:, 2 * pr * TQ:(2 * pr + 2) * TQ] + md2
            if near:
                lg = lg + jnp.concatenate(
                    [jnp.concatenate([toep_ref[delta[j], 2 * pr], toep_ref[delta[j], 2 * pr + 1]], axis=1)
                     for j in range(SUB)], axis=0)
            m_old = m_ref[pr]
            m_new = jnp.maximum(m_old, jnp.max(lg, axis=0, keepdims=True))
            alpha = jnp.exp2(m_old - m_new)
            p = jnp.exp2(lg - m_new)
            l_ref[pr] = alpha * l_ref[pr] + jnp.sum(p, axis=0, keepdims=True)
            pv = jnp.dot(vt_ref[c, kv * HEAD_DIM:(kv + 1) * HEAD_DIM, :], p.astype(MXU_DTYPE),
                         preferred_element_type=jnp.float32)
            acc_ref[pr] = alpha * acc_ref[pr] + pv
            m_ref[pr] = m_new

    n_far2 = (jnp.maximum(qi - 1, 0) // SUB) // 2

    def far_pair(i, carry):
        logits(2 * i + 1, lg1_ref)
        softmax_pv(2 * i, lg0_ref, near=False)
        logits(2 * i + 2, lg0_ref)
        softmax_pv(2 * i + 1, lg1_ref, near=False)
        return carry

    logits(0, lg0_ref)
    lax.fori_loop(0, n_far2, far_pair, 0)

    c0 = 2 * n_far2
    c_last = k_ref.shape[0] - 1
    logits(jnp.minimum(c0 + 1, c_last), lg1_ref)
    softmax_pv(c0, lg0_ref, near=True)

    @pl.when(c0 + 1 < nkc)
    def _():
        logits(jnp.minimum(c0 + 2, c_last), lg0_ref)
        softmax_pv(c0 + 1, lg1_ref, near=True)

    @pl.when(c0 + 2 < nkc)
    def _():
        softmax_pv(c0 + 2, lg0_ref, near=True)

    for j in range(GROUP):
        half = slice((j % 2) * TQ, (j % 2 + 1) * TQ)
        lo, hi = j // 2, j // 2 + GROUP // 2
        pair_t = jnp.concatenate([acc_ref[lo][:, half] / l_ref[lo][:, half],
                                  acc_ref[hi][:, half] / l_ref[hi][:, half]], axis=0)
        sl = slice(j * LANE, (j + 1) * LANE)
        o_ref[:, sl] = (pair_t.T * _silu(ga_ref[:, sl])).astype(o_ref.dtype)


def _attend(iqt, iwt, qzt, ga, ik, k, vt, toep_t, kk):
    bsz, lq = ga.shape[:2]
    nq = lq // TQ
    nc = k.shape[1]
    idx_bits = max(1, int(nc * KC - 1).bit_length())
    lanes = lambda a: pl.BlockSpec(a.shape[:-1] + (TQ,), lambda b, i: (0,) * (a.ndim - 1) + (b * nq + i,))
    whole = lambda a: pl.BlockSpec((None,) + a.shape[1:], lambda b, i: (b,) + (0,) * (a.ndim - 1))
    rows = pl.BlockSpec((None, TQ, N_HEADS * HEAD_DIM), lambda b, i: (b, i, 0))
    return pl.pallas_call(
        functools.partial(_attend_kernel, kk=kk, idx_bits=idx_bits),
        grid=(bsz, nq),
        in_specs=[lanes(iqt), lanes(iwt), lanes(qzt), rows, whole(ik), whole(k), whole(vt),
                  pl.BlockSpec(toep_t.shape, lambda b, i: (0, 0, 0, 0))],
        out_specs=rows,
        out_shape=jax.ShapeDtypeStruct((bsz, lq, N_HEADS * HEAD_DIM), MXU_DTYPE),
        scratch_shapes=[pltpu.VMEM((nc, KC, TQ), jnp.int32), pltpu.VMEM((nc, KC, TQ), jnp.int32),
                        pltpu.VMEM((N_LEVELS, nc, TQ, TQ), jnp.int32), pltpu.VMEM((nc, TQ, TQ), jnp.int32),
                        pltpu.VMEM((nc, TQ, TQ), jnp.int32),
                        pltpu.VMEM((N_HEADS // 2, 1, 2 * TQ), jnp.float32),
                        pltpu.VMEM((N_HEADS // 2, 1, 2 * TQ), jnp.float32),
                        pltpu.VMEM((N_HEADS // 2, HEAD_DIM, 2 * TQ), jnp.float32),
                        pltpu.VMEM((KC, N_HEADS * TQ), jnp.float32), pltpu.VMEM((KC, N_HEADS * TQ), jnp.float32),
                        pltpu.VMEM((LANE, N_HEADS * TQ), MXU_DTYPE)],
        compiler_params=_cparams(2),
        name="attend",
    )(iqt, iwt, qzt, ga, ik, k, vt, toep_t)


def _out_kernel(x_ref, mc_ref, ma_ref, wc_ref, wa_ref, o_ref):
    o_ref[...] = (x_ref[...]
                  + jnp.dot(mc_ref[...], wc_ref[...], preferred_element_type=jnp.float32)
                  + jnp.dot(ma_ref[...], wa_ref[...], preferred_element_type=jnp.float32))


def _out_final_kernel(x_ref, mc_ref, ma_ref, wc_ref, wa_ref, g_ref, o_ref, y_ref):
    x = (x_ref[...]
         + jnp.dot(mc_ref[...], wc_ref[...], preferred_element_type=jnp.float32)
         + jnp.dot(ma_ref[...], wa_ref[...], preferred_element_type=jnp.float32))
    o_ref[...] = x
    ms = jnp.mean(x * x, axis=-1, keepdims=True)
    y_ref[...] = x * lax.rsqrt(ms + EPS) * g_ref[...]


def _out_proj(x, mc, ma, wc, wa, tm, final_g=None):
    r, d = x.shape
    c = mc.shape[1]
    rows = lambda w: pl.BlockSpec((tm, w), lambda i: (i, 0))
    full = lambda a: pl.BlockSpec(a.shape, lambda i: (0, 0))
    if final_g is None:
        return pl.pallas_call(
            _out_kernel, grid=(r // tm,),
            in_specs=[rows(d), rows(c), rows(c), full(wc), full(wa)],
            out_specs=rows(d), out_shape=jax.ShapeDtypeStruct((r, d), jnp.float32),
            compiler_params=_cparams(1), name="out",
        )(x, mc, ma, wc, wa)
    g = final_g.reshape(1, d)
    return pl.pallas_call(
        _out_final_kernel, grid=(r // tm,),
        in_specs=[rows(d), rows(c), rows(c), full(wc), full(wa), full(g)],
        out_specs=[rows(d), rows(d)], out_shape=[jax.ShapeDtypeStruct((r, d), jnp.float32)] * 2,
        compiler_params=_cparams(1), name="out_final",
    )(x, mc, ma, wc, wa, g)


def _sconv_kernel(st_ref, u_ref, gc_ref, w_ref, b_ref, lg_ref, lb_ref, o_ref):
    acc = jnp.zeros(u_ref.shape, jnp.float32)
    for j in range(CONV_W - 1):
        acc = acc + w_ref[j:j + 1, :] * st_ref[j]
    acc = acc + w_ref[CONV_W - 1:CONV_W, :] * u_ref[...]
    y = acc + b_ref[...]
    mu = jnp.mean(y, axis=-1, keepdims=True)
    dev = y - mu
    var = jnp.mean(dev * dev, axis=-1, keepdims=True)
    yn = dev * lax.rsqrt(var + LN_EPS) * lg_ref[...] + lb_ref[...]
    o_ref[...] = (_silu(yn) * _silu(gc_ref[...])).astype(o_ref.dtype)


def _sample_conv(state_t, u, gc, w, b, lg, lb):
    db, c = u.shape
    row = lambda a: a.reshape(1, c)
    full = lambda a: pl.BlockSpec(a.shape, lambda i: (0,) * a.ndim)
    args = (state_t, u, gc, w, row(b), row(lg), row(lb))
    return pl.pallas_call(
        _sconv_kernel, grid=(1,), in_specs=[full(a) for a in args],
        out_specs=pl.BlockSpec((db, c), lambda i: (0, 0)),
        out_shape=jax.ShapeDtypeStruct((db, c), MXU_DTYPE),
        compiler_params=_cparams(1), name="sconv",
    )(*args)


def _dec_score_kernel(pt_ref, iq_ref, iw_ref, ikn_ref, *refs, n_pages, width):
    del pt_ref
    pages, o_ref = refs[:n_pages], refs[n_pages]
    iq = iq_ref[...][:, :IDX_DIM]
    w = iw_ref[...]
    pieces = []
    for p in range(n_pages):
        s = jnp.dot(iq, pages[p][...].astype(MXU_DTYPE), preferred_element_type=jnp.float32)
        pieces.append(jnp.sum(w * jnp.maximum(s, 0.0), axis=0, keepdims=True))
    ikn = ikn_ref[...][:, :IDX_DIM].astype(MXU_DTYPE).astype(jnp.float32)
    sn = jnp.sum(iq.astype(jnp.float32) * ikn, axis=-1, keepdims=True)
    scn = jnp.sum(w * jnp.maximum(sn, 0.0), axis=0, keepdims=True)
    page = pieces[0].shape[1]
    lane = lax.broadcasted_iota(jnp.int32, (1, width - n_pages * page), 1)
    pieces.append(jnp.where(lane == 0, scn, -jnp.inf))
    o_ref[...] = jnp.concatenate(pieces, axis=1)


def _dec_scores(page_table, iq3, iw3, tail3, cache_ik, layer, width):
    db, n_pages = page_table.shape
    di, page = cache_ik.shape[2:]
    seq = lambda a: pl.BlockSpec((None,) + a.shape[1:], lambda b, pt: (b, 0, 0))
    page_spec = lambda p: pl.BlockSpec((None, None, di, page), lambda b, pt: (layer, pt[b, p], 0, 0))
    return pl.pallas_call(
        functools.partial(_dec_score_kernel, n_pages=n_pages, width=width),
        grid_spec=pltpu.PrefetchScalarGridSpec(
            num_scalar_prefetch=1, grid=(db,),
            in_specs=[seq(iq3), seq(iw3), seq(tail3)] + [page_spec(p) for p in range(n_pages)],
            out_specs=pl.BlockSpec((None, 1, width), lambda b, pt: (b, 0, 0))),
        out_shape=jax.ShapeDtypeStruct((db, 1, width), jnp.float32),
        compiler_params=_cparams(1), name="dec_score",
    )(page_table, iq3, iw3, tail3, *([cache_ik] * n_pages))


def _dec_select_kernel(s_ref, o_ref, keys_ref, eidx_ref, dig_ref, work_ref, alive_ref, *, kk, idx_bits):
    nc = keys_ref.shape[0]
    for c in range(nc):
        keys_ref[c] = _to_key(s_ref[c * KC:(c + 1) * KC, :])
        _pack_digits(keys_ref, dig_ref, c)
    thr = _threshold(keys_ref, eidx_ref, dig_ref, work_ref, alive_ref, nc, kk, idx_bits)
    for c in range(nc):
        o_ref[c * KC:(c + 1) * KC, :] = jnp.where(keys_ref[c] >= thr, 0.0, NEG)


def _dec_select(scores_t, kk):
    width, db = scores_t.shape
    nc = width // KC
    idx_bits = max(1, int(width - 1).bit_length())
    spec = pl.BlockSpec((width, db), lambda i: (0, 0))
    return pl.pallas_call(
        functools.partial(_dec_select_kernel, kk=kk, idx_bits=idx_bits),
        grid=(1,), in_specs=[spec], out_specs=spec,
        out_shape=jax.ShapeDtypeStruct((width, db), jnp.float32),
        scratch_shapes=[pltpu.VMEM((nc, KC, db), jnp.int32), pltpu.VMEM((nc, KC, db), jnp.int32),
                        pltpu.VMEM((N_LEVELS, nc, TQ, db), jnp.int32), pltpu.VMEM((nc, TQ, db), jnp.int32),
                        pltpu.VMEM((nc, TQ, db), jnp.int32)],
        compiler_params=_cparams(1), name="dec_select",
    )(scores_t)


def _dec_attend_kernel(pt_ref, qz_ref, kvn_ref, madd_ref, bias_ref, ga_ref, *refs, n_pages):
    del pt_ref
    kpages, vpages, o_ref = refs[:n_pages], refs[n_pages:2 * n_pages], refs[2 * n_pages]
    qz = qz_ref[...]
    page = kpages[0].shape[1]
    past = n_pages * page
    nt = (((1,), (1,)), ((), ()))
    logits = []
    for p in range(n_pages):
        sl = slice(p * page, (p + 1) * page)
        lg = jnp.dot(qz, kpages[p][...].astype(MXU_DTYPE), preferred_element_type=jnp.float32)
        logits.append(lg + bias_ref[:, sl] + madd_ref[:, sl])
    kvn = kvn_ref[...].astype(MXU_DTYPE).astype(jnp.float32)
    lgn = jnp.sum(qz.astype(jnp.float32) * kvn[:, :LANE], axis=-1, keepdims=True)
    lgn = lgn + bias_ref[:, past:past + 1] + madd_ref[:, past:past + 1]
    m = lgn
    for lg in logits:
        m = jnp.maximum(m, jnp.max(lg, axis=-1, keepdims=True))
    pn = jnp.exp2(lgn - m)
    den = pn
    o = pn.astype(MXU_DTYPE).astype(jnp.float32) * kvn[:, LANE:]
    for p in range(n_pages):
        pp = jnp.exp2(logits[p] - m)
        den = den + jnp.sum(pp, axis=-1, keepdims=True)
        o = o + lax.dot_general(pp.astype(MXU_DTYPE), vpages[p][...].astype(MXU_DTYPE), nt,
                                preferred_element_type=jnp.float32)
    o = o / den
    lane = lax.broadcasted_iota(jnp.int32, (GROUP, LANE), 1)
    pair = jnp.where(lane < HEAD_DIM, o[:GROUP], o[GROUP:])
    o_ref[...] = (pair * _silu(ga_ref[...])).astype(o_ref.dtype)


def _dec_attend(page_table, qz3, kvn3, madd3, bias, ga3, cache_k, cache_v, layer):
    db, n_pages = page_table.shape
    feat, page = cache_k.shape[2:]
    seq = lambda a: pl.BlockSpec((None,) + a.shape[1:], lambda b, pt: (b, 0, 0))
    page_spec = lambda p: pl.BlockSpec((None, None, feat, page), lambda b, pt: (layer, pt[b, p], 0, 0))
    return pl.pallas_call(
        functools.partial(_dec_attend_kernel, n_pages=n_pages),
        grid_spec=pltpu.PrefetchScalarGridSpec(
            num_scalar_prefetch=1, grid=(db,),
            in_specs=[seq(qz3), seq(kvn3), seq(madd3), pl.BlockSpec(bias.shape, lambda b, pt: (0, 0)), seq(ga3)]
                     + [page_spec(p) for p in range(n_pages)] * 2,
            out_specs=pl.BlockSpec((None, GROUP, LANE), lambda b, pt: (b, 0, 0))),
        out_shape=jax.ShapeDtypeStruct((db, GROUP, LANE), MXU_DTYPE),
        compiler_params=_cparams(1), name="dec_attend",
    )(page_table, qz3, kvn3, madd3, bias, ga3, *([cache_k] * n_pages), *([cache_v] * n_pages))


def _round_up(x, m):
    return -(-x // m) * m


def _row_tile(rows):
    for f in (5, 4, 3, 2, 1):
        if rows % (f * TQ) == 0:
            return f * TQ
    return rows


def kernel(x_prompt, x_sample, cache_k, cache_v, cache_idx_k, state_conv, page_table, meta_tokens, rel_bias,
           norm_g, w_in, conv_w, conv_b, conv_ln_g, conv_ln_b, w_out, final_norm_g):
    bsz, seq, d = x_prompt.shape
    depth = w_in.shape[0]
    lp = seq + N_META
    lq = _round_up(lp, TQ)
    lk = _round_up(lp, KC)
    nq, nc = lq // TQ, lk // KC
    kk_p = min(TOPK_MAX, lp // 4)
    db = x_sample.shape[0]
    n_pool, page = cache_k.shape[1:3]
    n_pages = page_table.shape[1]
    past = n_pages * page
    kk_s = min(TOPK_MAX, (past + 1) // 4)
    width_s = _round_up(past + 1, KC)
    feat = N_KV_HEADS * HEAD_DIM
    tm = _row_tile(bsz * lq)

    xp = jnp.concatenate([jnp.broadcast_to(meta_tokens[None].astype(x_prompt.dtype), (bsz, N_META, d)), x_prompt,
                          jnp.zeros((bsz, lq - lp, d), x_prompt.dtype)], axis=1).reshape(bsz * lq, d)
    xs = x_sample.reshape(db, d)
    toep_t = _bias_tiles(rel_bias)
    bias_s = jnp.moveaxis(rel_bias[_t5_bucket(past - jnp.arange(width_s, dtype=jnp.int32))], -1, 0) * LOG2E
    cache_k4 = jnp.transpose(cache_k, (0, 1, 3, 4, 2)).reshape(depth, n_pool, feat, page)
    cache_v4 = jnp.transpose(cache_v, (0, 1, 3, 4, 2)).reshape(depth, n_pool, feat, page)
    cache_ik4 = jnp.swapaxes(cache_idx_k, 2, 3)

    def key_chunks(a):
        return jnp.pad(a, ((0, 0), (0, lk - lq), (0, 0))).astype(MXU_DTYPE).reshape(bsz, nc, KC, LANE)

    def key_chunks_t(a):
        a = jnp.pad(a.reshape(LANE, bsz, lq), ((0, 0), (0, 0), (0, lk - lq)))
        return jnp.transpose(a.reshape(LANE, bsz, nc, KC), (1, 2, 0, 3))

    kp, vp, ikp, cp, ksm, vsm, iks, cs = ([] for _ in range(8))
    yp = ys = None
    for l in range(depth):
        w = _prep_w_in(w_in[l])
        wc = w_out[l, :C_CONV].astype(MXU_DTYPE)
        wa = w_out[l, C_CONV:][_PERM].astype(MXU_DTYPE)
        last = l == depth - 1

        u, gc, kv, ga, tail, qzt, iqzt, vt, iwt = _project(xp, norm_g[l], w, tm, feature_major=True)
        b3 = lambda a: a.reshape(bsz, lq, a.shape[-1])
        u3, kv3, tail3 = b3(u), b3(kv), b3(tail)
        mixc = _conv_branch(u3, b3(gc), conv_w[l], conv_b[l], conv_ln_g[l], conv_ln_b[l])
        lane = jnp.arange(LANE)
        ik = key_chunks(jnp.where(lane < IDX_DIM, tail3, 0.0))
        mixa = _attend(iqzt.reshape(N_IDX_HEADS, LANE, bsz * lq), iwt, qzt.reshape(N_HEADS, LANE, bsz * lq), b3(ga),
                       ik, key_chunks(kv3[..., :feat]), key_chunks_t(vt), toep_t, kk_p)
        res = _out_proj(xp, mixc.reshape(bsz * lq, C_CONV), mixa.reshape(bsz * lq, -1), wc, wa, tm,
                        final_norm_g if last else None)
        xp, yp = res if last else (res, None)
        kp.append(kv3[:, :lp, :feat].reshape(bsz, lp, N_KV_HEADS, HEAD_DIM))
        vp.append(kv3[:, :lp, feat:].reshape(bsz, lp, N_KV_HEADS, HEAD_DIM))
        ikp.append(tail3[:, :lp, :IDX_DIM])
        cp.append(u3[:, lp - (CONV_W - 1):lp])

        u, gc, kv, ga, tail, qz, iqz = _project(xs, norm_g[l], w, db, feature_major=False)
        mixc = _sample_conv(jnp.transpose(state_conv[l], (1, 0, 2)), u, gc,
                            conv_w[l], conv_b[l], conv_ln_g[l], conv_ln_b[l])
        iw3 = tail[:, IDX_DIM:IDX_DIM + N_IDX_HEADS].reshape(db, N_IDX_HEADS, 1)
        scores = _dec_scores(page_table, iqz.reshape(db, N_IDX_HEADS, LANE), iw3, tail.reshape(db, 1, LANE),
                             cache_ik4, l, width_s)
        madd = _dec_select(scores.reshape(db, width_s).T, kk_s).T
        mixa = _dec_attend(page_table, qz.reshape(db, N_HEADS, LANE), kv.reshape(db, 1, 2 * feat),
                           madd.reshape(db, 1, width_s), bias_s, ga.reshape(db, GROUP, LANE),
                           cache_k4, cache_v4, l)
        res = _out_proj(xs, mixc, mixa.reshape(db, -1), wc, wa, db, final_norm_g if last else None)
        xs, ys = res if last else (res, None)
        ksm.append(kv[:, :feat].reshape(db, 1, N_KV_HEADS, HEAD_DIM))
        vsm.append(kv[:, feat:].reshape(db, 1, N_KV_HEADS, HEAD_DIM))
        iks.append(tail[:, None, :IDX_DIM])
        cs.append(jnp.concatenate([state_conv[l][:, 1:], u[:, None]], axis=1))

    y_prompt = yp.reshape(bsz, lq, d)[:, N_META:lp]
    y_sample = ys.reshape(db, 1, d)
    return (y_prompt, y_sample, jnp.stack(kp), jnp.stack(vp), jnp.stack(ikp), jnp.stack(cp),
            jnp.stack(ksm), jnp.stack(vsm), jnp.stack(iks), jnp.stack(cs))
```

```python
import functools
import math

import numpy as np
import jax
import jax.numpy as jnp
from jax import lax
from jax.experimental import pallas as pl
from jax.experimental.pallas import tpu as pltpu

N_HEADS = 8
N_KV_HEADS = 2
GROUP = N_HEADS // N_KV_HEADS
HEAD_DIM = 64
N_IDX_HEADS = 4
IDX_DIM = 64
C_CONV = 512
CONV_W = 31
TOPK_MAX = 256
N_META = 16
NUM_BUCKETS = 32
MAX_DISTANCE = 128
EPS = 1e-6
LN_EPS = 1e-5

LANE = 128
ROWS = 8
TQ = 128
KC = 512
SUB = KC // TQ
N_ACC = 4
DIGIT_BITS = 7
N_LEVELS = 4
LOW_BITS = 32 - DIGIT_BITS * N_LEVELS
DIGIT_MASK = (1 << DIGIT_BITS) - 1
HALO = 32
CONV_SUB = 64
MXU_DTYPE = jnp.bfloat16
NEG = -1e30
LOG2E = 1.4426950408889634
INT_MIN = -2 ** 31
KEY_NEG_INF = -0x7F800000
VMEM_LIMIT = 56 * 1024 * 1024

_Q_OFF = 3 * C_CONV
_KV_OFF = _Q_OFF + N_HEADS * LANE
_GA_OFF = _KV_OFF + 2 * N_KV_HEADS * HEAD_DIM
_IQ_OFF = _GA_OFF + N_HEADS * HEAD_DIM
_TAIL_OFF = _IQ_OFF + N_IDX_HEADS * LANE
_W_COLS = _TAIL_OFF + LANE

_HEAD_ORDER = [h for j in range(GROUP) for h in (j, j + GROUP)]
_PERM = np.concatenate([np.arange(h * HEAD_DIM, (h + 1) * HEAD_DIM) for h in _HEAD_ORDER])

_PROJ_SIZES = (C_CONV, C_CONV, C_CONV, N_HEADS * HEAD_DIM, N_KV_HEADS * HEAD_DIM, N_KV_HEADS * HEAD_DIM,
               N_HEADS * HEAD_DIM, N_IDX_HEADS * IDX_DIM, IDX_DIM, N_IDX_HEADS)
_PROJ_SPLITS = tuple(int(s) for s in np.cumsum(_PROJ_SIZES)[:-1])


def _cparams(n_axes):
    return pltpu.CompilerParams(dimension_semantics=("arbitrary",) * n_axes, vmem_limit_bytes=VMEM_LIMIT)


def _silu(x):
    return x * jax.nn.sigmoid(x)


def _prep_w_in(w):
    ua, ub, gc, q, k, v, ga, iq, ik, iw = jnp.split(w.T, _PROJ_SPLITS, axis=0)
    d = w.shape[0]
    z = jnp.zeros((HEAD_DIM, d), w.dtype)
    qz = []
    for h in range(N_HEADS):
        qh = q[h * HEAD_DIM:(h + 1) * HEAD_DIM]
        qz += [qh, z] if h // GROUP == 0 else [z, qh]
    iqz = []
    for h in range(N_IDX_HEADS):
        iqz += [iq[h * IDX_DIM:(h + 1) * IDX_DIM], z]
    ga_p = [ga[h * HEAD_DIM:(h + 1) * HEAD_DIM] for h in _HEAD_ORDER]
    tail = [ik, iw, jnp.zeros((LANE - IDX_DIM - N_IDX_HEADS, d), w.dtype)]
    out = jnp.concatenate([ua, ub, gc] + qz + [k, v] + ga_p + iqz + tail, axis=0)
    assert out.shape[0] == _W_COLS
    return out.astype(MXU_DTYPE)


def _t5_bucket(rel):
    n = jnp.maximum(rel, 0)
    max_exact = NUM_BUCKETS // 2
    large = max_exact + (jnp.log(jnp.maximum(n, 1).astype(jnp.float32) / max_exact)
                         / math.log(MAX_DISTANCE / max_exact)
                         * (NUM_BUCKETS - max_exact)).astype(jnp.int32)
    large = jnp.minimum(large, NUM_BUCKETS - 1)
    return jnp.where(n < max_exact, n, large)


def _bias_tiles(rel_bias):
    far = rel_bias[NUM_BUCKETS - 1]
    d = jnp.arange(2 * TQ, dtype=jnp.int32)
    dists = (jnp.where(d < TQ, d, 0),
             jnp.where(d < TQ, d + TQ, d - TQ))
    tiles = []
    for dist in dists:
        v = ((rel_bias[_t5_bucket(dist)] - far) * LOG2E).T
        rep = jnp.tile(v, (1, TQ))[:, :TQ * (2 * TQ - 1)].reshape(N_HEADS, TQ, 2 * TQ - 1)
        tiles.append(rep[:, :, :TQ])
    tiles.append(jnp.zeros_like(tiles[0]))
    return jnp.stack(tiles).astype(jnp.float32)


_NT = (((1,), (1,)), ((), ()))
_Q_SCALE = HEAD_DIM ** -0.5 * LOG2E
_IQ_SCALE = IDX_DIM ** -0.5
_IW_SCALE = N_IDX_HEADS ** -0.5
_IW_ROWS = 16


def _proj_kernel(x_ref, g_ref, w_ref, u_ref, gc_ref, kv_ref, ga_ref, tail_ref, *q_refs, feature_major):
    x = x_ref[...]
    ms = jnp.mean(x * x, axis=-1, keepdims=True)
    xn = (x * lax.rsqrt(ms + EPS) * g_ref[...]).astype(MXU_DTYPE)

    def mm(lo, hi):
        return lax.dot_general(xn, w_ref[lo:hi, :], _NT, preferred_element_type=jnp.float32)

    def mm_t(lo, hi):
        return lax.dot_general(w_ref[lo:hi, :], xn, _NT, preferred_element_type=jnp.float32)

    u_ref[...] = mm(0, C_CONV) * jax.nn.sigmoid(mm(C_CONV, 2 * C_CONV))
    gc_ref[...] = mm(2 * C_CONV, 3 * C_CONV)
    kv_ref[...] = mm(_KV_OFF, _GA_OFF)
    ga_ref[...] = mm(_GA_OFF, _IQ_OFF)
    t = mm(_TAIL_OFF, _W_COLS)
    lane = lax.broadcasted_iota(jnp.int32, t.shape, 1)
    tail_ref[...] = t * jnp.where(lane >= IDX_DIM, _IW_SCALE, 1.0)
    if feature_major:
        qzt_ref, iqzt_ref, vt_ref, iwt_ref = q_refs
        qzt_ref[...] = (mm_t(_Q_OFF, _KV_OFF) * _Q_SCALE).astype(qzt_ref.dtype)
        iqzt_ref[...] = (mm_t(_IQ_OFF, _TAIL_OFF) * _IQ_SCALE).astype(iqzt_ref.dtype)
        vt_ref[...] = mm_t(_KV_OFF + N_KV_HEADS * HEAD_DIM, _GA_OFF).astype(vt_ref.dtype)
        iwt_ref[...] = mm_t(_TAIL_OFF + IDX_DIM, _TAIL_OFF + IDX_DIM + _IW_ROWS) * _IW_SCALE
    else:
        qz_ref, iqz_ref = q_refs
        qz_ref[...] = (mm(_Q_OFF, _KV_OFF) * _Q_SCALE).astype(qz_ref.dtype)
        iqz_ref[...] = (mm(_IQ_OFF, _TAIL_OFF) * _IQ_SCALE).astype(iqz_ref.dtype)


def _project(x, g, w, tm, feature_major):
    r, d = x.shape
    assert r % tm == 0
    rows = lambda c: pl.BlockSpec((tm, c), lambda i: (i, 0))
    cols = lambda c: pl.BlockSpec((c, tm), lambda i: (0, i))
    feat = N_KV_HEADS * HEAD_DIM
    specs = [rows(C_CONV), rows(C_CONV), rows(2 * feat), rows(N_HEADS * HEAD_DIM), rows(LANE)]
    shapes = [jax.ShapeDtypeStruct((r, s.block_shape[1]), jnp.float32) for s in specs]
    if feature_major:
        extra = [(N_HEADS * LANE, MXU_DTYPE), (N_IDX_HEADS * LANE, MXU_DTYPE), (feat, MXU_DTYPE),
                 (_IW_ROWS, jnp.float32)]
        specs += [cols(c) for c, _ in extra]
        shapes += [jax.ShapeDtypeStruct((c, r), t) for c, t in extra]
    else:
        extra = [(N_HEADS * LANE, MXU_DTYPE), (N_IDX_HEADS * LANE, MXU_DTYPE)]
        specs += [rows(c) for c, _ in extra]
        shapes += [jax.ShapeDtypeStruct((r, c), t) for c, t in extra]
    return pl.pallas_call(
        functools.partial(_proj_kernel, feature_major=feature_major),
        grid=(r // tm,),
        in_specs=[rows(d), pl.BlockSpec((1, d), lambda i: (0, 0)), pl.BlockSpec((_W_COLS, d), lambda i: (0, 0))],
        out_specs=specs,
        out_shape=shapes,
        compiler_params=_cparams(1),
        name="proj",
    )(x, g.reshape(1, d), w)


def _conv_kernel(prev_ref, cur_ref, gc_ref, w_ref, b_ref, lg_ref, lb_ref, o_ref, ext_ref):
    i = pl.program_id(1)
    ext_ref[0, 0:HALO, :] = jnp.where(i > 0, prev_ref[...], 0.0)
    ext_ref[0, HALO:HALO + TQ, :] = cur_ref[...]
    n_ext = HALO + TQ
    for s in range(1, ROWS):
        ext_ref[s, 0:n_ext - ROWS, :] = ext_ref[0, s:s + n_ext - ROWS, :]
    off = HALO - (CONV_W - 1)
    for r0 in range(0, TQ, CONV_SUB):
        acc = jnp.zeros((CONV_SUB, C_CONV), jnp.float32)
        for j in range(CONV_W):
            s = (off + j) % ROWS
            lo = r0 + off + j - s
            acc = acc + w_ref[j:j + 1, :] * ext_ref[s, lo:lo + CONV_SUB, :]
        y = acc + b_ref[...]
        mu = jnp.mean(y, axis=-1, keepdims=True)
        dev = y - mu
        var = jnp.mean(dev * dev, axis=-1, keepdims=True)
        yn = dev * lax.rsqrt(var + LN_EPS) * lg_ref[...] + lb_ref[...]
        o_ref[r0:r0 + CONV_SUB, :] = (_silu(yn) * _silu(gc_ref[r0:r0 + CONV_SUB, :])).astype(o_ref.dtype)


def _conv_branch(u, gc, w, b, lg, lb):
    bsz, lq, c = u.shape
    per = TQ // HALO
    row = lambda a: a.reshape(1, c)
    vec = pl.BlockSpec((1, c), lambda bi, i: (0, 0))
    tile = pl.BlockSpec((None, TQ, c), lambda bi, i: (bi, i, 0))
    return pl.pallas_call(
        _conv_kernel,
        grid=(bsz, lq // TQ),
        in_specs=[pl.BlockSpec((None, HALO, c), lambda bi, i: (bi, jnp.maximum(i * per - 1, 0), 0)),
                  tile, tile, pl.BlockSpec((CONV_W, c), lambda bi, i: (0, 0)), vec, vec, vec],
        out_specs=tile,
        out_shape=jax.ShapeDtypeStruct((bsz, lq, c), MXU_DTYPE),
        scratch_shapes=[pltpu.VMEM((ROWS, HALO + TQ, c), jnp.float32)],
        compiler_params=_cparams(2),
        name="conv",
    )(u, u, gc, w, row(b), row(lg), row(lb))


_INT_MIN = np.int32(INT_MIN)
_BYTE_ONES = np.int32(0x01010101)
_BYTE_LOW = np.int32(0x7F7F7F7F)
_GUARD = np.int32(0x80808080 - (1 << 32))
assert SUB == 4 and DIGIT_BITS == 7


def _to_key(s):
    bits = lax.bitcast_convert_type(s, jnp.int32)
    return jnp.where(bits < 0, _INT_MIN - bits, bits)


def _for_chunks(nkc, body, init, unroll):
    n_main = nkc // unroll

    def main(i, carry):
        for k in range(unroll):
            carry = body(i * unroll + k, carry)
        return carry

    carry = lax.fori_loop(0, n_main, main, init)
    return lax.fori_loop(n_main * unroll, nkc, body, carry)


def _count(ref, nkc, pred):
    q = ref.shape[2]

    def body(c, accs):
        accs = list(accs)
        for n, r in enumerate(range(0, KC, ROWS)):
            accs[n % N_ACC] = accs[n % N_ACC] + jnp.where(pred(ref[c, r:r + ROWS, :]), 1.0, 0.0)
        return tuple(accs)

    accs = _for_chunks(nkc, body, (jnp.zeros((ROWS, q), jnp.float32),) * N_ACC, unroll=2)
    return jnp.sum(sum(accs[1:], accs[0]), axis=0, keepdims=True)


def _pack_digits(keys_ref, dig_ref, c):
    for lv in range(N_LEVELS):
        shift = 32 - DIGIT_BITS * (lv + 1)
        word = None
        for j in range(SUB):
            u = keys_ref[c, j * TQ:(j + 1) * TQ, :] ^ _INT_MIN
            field = lax.shift_right_logical(u, shift) & DIGIT_MASK
            field = field if j == 0 else lax.shift_left(field, 8 * j)
            word = field if word is None else word | field
        dig_ref[lv, c] = word


def _count_fields(work_ref, nkc, cand_bytes):
    q = work_ref.shape[2]

    def body(c, accs):
        accs = list(accs)
        for n, r in enumerate(range(0, TQ, ROWS)):
            diff = work_ref[c, r:r + ROWS, :] - cand_bytes
            accs[n % N_ACC] = accs[n % N_ACC] + (lax.shift_right_logical(diff, DIGIT_BITS) & _BYTE_ONES)
        return tuple(accs)

    accs = _for_chunks(nkc, body, (jnp.zeros((ROWS, q), jnp.int32),) * N_ACC, unroll=4)
    total = jnp.zeros((ROWS, q), jnp.int32)
    for a in accs:
        total = total + ((a & 0xFF) + (lax.shift_right_logical(a, 8) & 0xFF)
                         + (lax.shift_right_logical(a, 16) & 0xFF) + lax.shift_right_logical(a, 24))
    return jnp.sum(total.astype(jnp.float32), axis=0, keepdims=True)


def _digit_search(dig_ref, work_ref, alive_ref, nkc, shifts, target):
    q = dig_ref.shape[3]
    assert dig_ref.shape[1] * (TQ // ROWS) <= 255 * N_ACC
    zero = jnp.zeros((1, q), jnp.float32)
    value = jnp.zeros((1, q), jnp.int32)
    above, done, cnt_ge, cnt_gt = zero, zero, target + 1.0, zero

    for lv, shift in enumerate(shifts):
        def load_fields(c, carry):
            work_ref[c] = (dig_ref[lv, c] & alive_ref[c]) | _GUARD
            return carry

        _for_chunks(nkc, load_fields, 0, unroll=4)

        def digit_bit(i, state):
            digit, cnt_ge, cnt_gt, cnt_rej, done = state
            cand = digit | lax.shift_left(jnp.int32(1), DIGIT_BITS - 1 - i)
            cnt_alive = _count_fields(work_ref, nkc, cand * _BYTE_ONES)
            cnt = above + cnt_alive
            take = (cnt >= target) & (done == 0.0)
            drop = (cnt < target) & (done == 0.0)
            return (jnp.where(take, cand, digit), jnp.where(take, cnt, cnt_ge), jnp.where(drop, cnt, cnt_gt),
                    jnp.where(drop, cnt_alive, cnt_rej),
                    jnp.where(take & (cnt == target), 1.0, done))

        digit, cnt_ge, cnt_gt, cnt_rej, done = lax.fori_loop(
            0, DIGIT_BITS, digit_bit, (jnp.zeros((1, q), jnp.int32), cnt_ge, cnt_gt, zero, done))
        above = above + cnt_rej
        value = value | lax.shift_left(digit, shift)

        if lv + 1 < len(shifts):
            digit_bytes = digit * _BYTE_ONES

            def narrow(c, carry):
                differs = lax.shift_right_logical(((dig_ref[lv, c] ^ digit_bytes) + _BYTE_LOW) & _GUARD, DIGIT_BITS)
                alive_ref[c] = alive_ref[c] & ((_BYTE_ONES - differs) * DIGIT_MASK)
                return carry

            _for_chunks(nkc, narrow, 0, unroll=4)

    return value, cnt_ge, cnt_gt, done


def _fill(ref, nkc, word):
    def body(c, carry):
        ref[c] = jnp.full(ref.shape[1:], word, ref.dtype)
        return carry

    _for_chunks(nkc, body, 0, unroll=4)


def _threshold(keys_ref, dig_ref, work_ref, alive_ref, nkc, kk):
    q = keys_ref.shape[2]
    kf = jnp.full((1, q), kk, jnp.float32)
    _fill(alive_ref, nkc, _BYTE_LOW)
    key_shifts = [32 - DIGIT_BITS * (lv + 1) for lv in range(N_LEVELS)]
    prefix, cnt_thr, cnt_gt, done = _digit_search(dig_ref, work_ref, alive_ref, nkc, key_shifts, kf)

    def low_bit(i, state):
        thr, cnt_thr, cnt_gt, done = state
        cand = thr | lax.shift_left(jnp.int32(1), LOW_BITS - 1 - i)
        cnt = _count(keys_ref, nkc, lambda x: x >= cand)
        take = (cnt >= kf) & (done == 0.0)
        drop = (cnt < kf) & (done == 0.0)
        return (jnp.where(take, cand, thr), jnp.where(take, cnt, cnt_thr), jnp.where(drop, cnt, cnt_gt),
                jnp.where(take & (cnt == kf), 1.0, done))

    n_low = jnp.where(jnp.min(done) == 0.0, LOW_BITS, 0)
    thr, cnt_thr, cnt_gt, _ = lax.fori_loop(0, n_low, low_bit, (prefix ^ _INT_MIN, cnt_thr, cnt_gt, done))
    admissible = thr != KEY_NEG_INF
    surplus = (cnt_thr > kf) & admissible

    @pl.when(jnp.max(jnp.where(surplus, 1.0, 0.0)) > 0.0)
    def _():
        assert keys_ref.shape[0] * KC <= 1 << (2 * DIGIT_BITS)
        top = (1 << (2 * DIGIT_BITS)) - 1
        row = lax.broadcasted_iota(jnp.int32, (TQ, q), 0)

        def tie_fields(c, carry):
            hi = lo = tie = None
            for j in range(SUB):
                is_tie = keys_ref[c, j * TQ:(j + 1) * TQ, :] == thr
                rev = top - (c * KC + j * TQ + row)
                fields = [jnp.where(is_tie, f, 0) for f in
                          (lax.shift_right_logical(rev, DIGIT_BITS), rev & DIGIT_MASK, DIGIT_MASK)]
                fields = [f if j == 0 else lax.shift_left(f, 8 * j) for f in fields]
                hi, lo, tie = fields if j == 0 else (hi | fields[0], lo | fields[1], tie | fields[2])
            dig_ref[0, c], dig_ref[1, c], alive_ref[c] = hi, lo, tie
            return carry

        lax.fori_loop(0, nkc, tie_fields, 0)
        cut, _, _, _ = _digit_search(dig_ref, work_ref, alive_ref, nkc, [DIGIT_BITS, 0], kf - cnt_gt)

        def drop_losers(c, carry):
            for j in range(SUB):
                rows = slice(j * TQ, (j + 1) * TQ)
                x = keys_ref[c, rows, :]
                lost = (x == thr) & (top - (c * KC + j * TQ + row) < cut)
                keys_ref[c, rows, :] = jnp.where(lost, thr - 1, x)
            return carry

        lax.fori_loop(0, nkc, drop_losers, 0)

    return jnp.where(admissible, thr, KEY_NEG_INF + 1)


def _attend_kernel(iqt_ref, iwt_ref, qzt_ref, ga_ref, ik_ref, k_ref, vt_ref, toep_ref, o_ref,
                   keys_ref, dig_ref, work_ref, alive_ref, m_ref, l_ref, acc_ref, lg0_ref, lg1_ref,
                   wq_ref, *, kk):
    qi = pl.program_id(1)
    nkc = qi // SUB + 1
    qpos = qi * TQ + lax.broadcasted_iota(jnp.int32, (TQ, TQ), 1)
    krow = lax.broadcasted_iota(jnp.int32, (TQ, TQ), 0)
    for h in range(N_HEADS):
        wq_ref[:, h * TQ:(h + 1) * TQ] = qzt_ref[h]

    def score_chunk(c, carry):
        iqt = jnp.concatenate([iqt_ref[h] for h in range(N_IDX_HEADS)], axis=1)
        w = iwt_ref[...]
        for j in range(SUB):
            st = jnp.dot(ik_ref[c, j * TQ:(j + 1) * TQ, :], iqt, preferred_element_type=jnp.float32)
            s = jnp.zeros((TQ, TQ), jnp.float32)
            for h in range(N_IDX_HEADS):
                s = s + w[h:h + 1, :] * jnp.maximum(st[:, h * TQ:(h + 1) * TQ], 0.0)
            kpos = c * KC + j * TQ + krow
            keys_ref[c, j * TQ:(j + 1) * TQ, :] = _to_key(jnp.where(kpos <= qpos, s, -jnp.inf))
        _pack_digits(keys_ref, dig_ref, c)
        return carry

    lax.fori_loop(0, nkc, score_chunk, 0)
    thr = _threshold(keys_ref, dig_ref, work_ref, alive_ref, nkc, kk)

    m_ref[...] = jnp.full(m_ref.shape, NEG, jnp.float32)
    l_ref[...] = jnp.zeros(l_ref.shape, jnp.float32)
    acc_ref[...] = jnp.zeros(acc_ref.shape, jnp.float32)

    def logits(c, lg_ref):
        lg_ref[...] = jnp.dot(k_ref[c], wq_ref[...], preferred_element_type=jnp.float32)

    def softmax_pv(c, lg_ref, near):
        md = jnp.where(keys_ref[c] >= thr, 0.0, NEG)
        md2 = jnp.concatenate([md, md], axis=1)
        if near:
            delta = [jnp.clip(qi - (c * SUB + j), 0, 2) for j in range(SUB)]
        for pr in range(N_HEADS // 2):
            kv = pr // (GROUP // 2)
            lg = lg_ref[:, 2 * pr * TQ:(2 * pr + 2) * TQ] + md2
            if near:
                lg = lg + jnp.concatenate(
                    [jnp.concatenate([toep_ref[delta[j], 2 * pr], toep_ref[delta[j], 2 * pr + 1]], axis=1)
                     for j in range(SUB)], axis=0)
            m_old = m_ref[pr]
            m_new = jnp.maximum(m_old, jnp.max(lg, axis=0, keepdims=True))
            alpha = jnp.exp2(m_old - m_new)
            p = jnp.exp2(lg - m_new)
            l_ref[pr] = alpha * l_ref[pr] + jnp.sum(p, axis=0, keepdims=True)
            pv = jnp.dot(vt_ref[c, kv * HEAD_DIM:(kv + 1) * HEAD_DIM, :], p.astype(MXU_DTYPE),
                         preferred_element_type=jnp.float32)
            acc_ref[pr] = alpha * acc_ref[pr] + pv
            m_ref[pr] = m_new

    n_far2 = (jnp.maximum(qi - 1, 0) // SUB) // 2

    def far_pair(i, carry):
        logits(2 * i + 1, lg1_ref)
        softmax_pv(2 * i, lg0_ref, near=False)
        logits(2 * i + 2, lg0_ref)
        softmax_pv(2 * i + 1, lg1_ref, near=False)
        return carry

    logits(0, lg0_ref)
    lax.fori_loop(0, n_far2, far_pair, 0)

    c0 = 2 * n_far2
    c_last = k_ref.shape[0] - 1
    logits(jnp.minimum(c0 + 1, c_last), lg1_ref)
    softmax_pv(c0, lg0_ref, near=True)

    @pl.when(c0 + 1 < nkc)
    def _():
        logits(jnp.minimum(c0 + 2, c_last), lg0_ref)
        softmax_pv(c0 + 1, lg1_ref, near=True)

    @pl.when(c0 + 2 < nkc)
    def _():
        softmax_pv(c0 + 2, lg0_ref, near=True)

    for j in range(GROUP):
        half = slice((j % 2) * TQ, (j % 2 + 1) * TQ)
        lo, hi = j // 2, j // 2 + GROUP // 2
        pair_t = jnp.concatenate([acc_ref[lo][:, half] / l_ref[lo][:, half],
                                  acc_ref[hi][:, half] / l_ref[hi][:, half]], axis=0)
        sl = slice(j * LANE, (j + 1) * LANE)
        o_ref[:, sl] = (pair_t.T * _silu(ga_ref[:, sl])).astype(o_ref.dtype)


def _attend(iqt, iwt, qzt, ga, ik, k, vt, toep_t, kk):
    bsz, lq = ga.shape[:2]
    nq = lq // TQ
    nc = k.shape[1]
    lanes = lambda a: pl.BlockSpec(a.shape[:-1] + (TQ,), lambda b, i: (0,) * (a.ndim - 1) + (b * nq + i,))
    whole = lambda a: pl.BlockSpec((None,) + a.shape[1:], lambda b, i: (b,) + (0,) * (a.ndim - 1))
    rows = pl.BlockSpec((None, TQ, N_HEADS * HEAD_DIM), lambda b, i: (b, i, 0))
    return pl.pallas_call(
        functools.partial(_attend_kernel, kk=kk),
        grid=(bsz, nq),
        in_specs=[lanes(iqt), lanes(iwt), lanes(qzt), rows, whole(ik), whole(k), whole(vt),
                  pl.BlockSpec(toep_t.shape, lambda b, i: (0, 0, 0, 0))],
        out_specs=rows,
        out_shape=jax.ShapeDtypeStruct((bsz, lq, N_HEADS * HEAD_DIM), MXU_DTYPE),
        scratch_shapes=[pltpu.VMEM((nc, KC, TQ), jnp.int32),
                        pltpu.VMEM((N_LEVELS, nc, TQ, TQ), jnp.int32), pltpu.VMEM((nc, TQ, TQ), jnp.int32),
                        pltpu.VMEM((nc, TQ, TQ), jnp.int32),
                        pltpu.VMEM((N_HEADS // 2, 1, 2 * TQ), jnp.float32),
                        pltpu.VMEM((N_HEADS // 2, 1, 2 * TQ), jnp.float32),
                        pltpu.VMEM((N_HEADS // 2, HEAD_DIM, 2 * TQ), jnp.float32),
                        pltpu.VMEM((KC, N_HEADS * TQ), jnp.float32), pltpu.VMEM((KC, N_HEADS * TQ), jnp.float32),
                        pltpu.VMEM((LANE, N_HEADS * TQ), MXU_DTYPE)],
        compiler_params=_cparams(2),
        name="attend",
    )(iqt, iwt, qzt, ga, ik, k, vt, toep_t)


def _out_kernel(x_ref, mc_ref, ma_ref, wc_ref, wa_ref, o_ref):
    o_ref[...] = (x_ref[...]
                  + jnp.dot(mc_ref[...], wc_ref[...], preferred_element_type=jnp.float32)
                  + jnp.dot(ma_ref[...], wa_ref[...], preferred_element_type=jnp.float32))


def _out_final_kernel(x_ref, mc_ref, ma_ref, wc_ref, wa_ref, g_ref, o_ref, y_ref):
    x = (x_ref[...]
         + jnp.dot(mc_ref[...], wc_ref[...], preferred_element_type=jnp.float32)
         + jnp.dot(ma_ref[...], wa_ref[...], preferred_element_type=jnp.float32))
    o_ref[...] = x
    ms = jnp.mean(x * x, axis=-1, keepdims=True)
    y_ref[...] = x * lax.rsqrt(ms + EPS) * g_ref[...]


def _out_proj(x, mc, ma, wc, wa, tm, final_g=None):
    r, d = x.shape
    c = mc.shape[1]
    rows = lambda w: pl.BlockSpec((tm, w), lambda i: (i, 0))
    full = lambda a: pl.BlockSpec(a.shape, lambda i: (0, 0))
    if final_g is None:
        return pl.pallas_call(
            _out_kernel, grid=(r // tm,),
            in_specs=[rows(d), rows(c), rows(c), full(wc), full(wa)],
            out_specs=rows(d), out_shape=jax.ShapeDtypeStruct((r, d), jnp.float32),
            compiler_params=_cparams(1), name="out",
        )(x, mc, ma, wc, wa)
    g = final_g.reshape(1, d)
    return pl.pallas_call(
        _out_final_kernel, grid=(r // tm,),
        in_specs=[rows(d), rows(c), rows(c), full(wc), full(wa), full(g)],
        out_specs=[rows(d), rows(d)], out_shape=[jax.ShapeDtypeStruct((r, d), jnp.float32)] * 2,
        compiler_params=_cparams(1), name="out_final",
    )(x, mc, ma, wc, wa, g)


def _sconv_kernel(st_ref, u_ref, gc_ref, w_ref, b_ref, lg_ref, lb_ref, o_ref):
    acc = jnp.zeros(u_ref.shape, jnp.float32)
    for j in range(CONV_W - 1):
        acc = acc + w_ref[j:j + 1, :] * st_ref[j]
    acc = acc + w_ref[CONV_W - 1:CONV_W, :] * u_ref[...]
    y = acc + b_ref[...]
    mu = jnp.mean(y, axis=-1, keepdims=True)
    dev = y - mu
    var = jnp.mean(dev * dev, axis=-1, keepdims=True)
    yn = dev * lax.rsqrt(var + LN_EPS) * lg_ref[...] + lb_ref[...]
    o_ref[...] = (_silu(yn) * _silu(gc_ref[...])).astype(o_ref.dtype)


def _sample_conv(state_t, u, gc, w, b, lg, lb):
    db, c = u.shape
    row = lambda a: a.reshape(1, c)
    full = lambda a: pl.BlockSpec(a.shape, lambda i: (0,) * a.ndim)
    args = (state_t, u, gc, w, row(b), row(lg), row(lb))
    return pl.pallas_call(
        _sconv_kernel, grid=(1,), in_specs=[full(a) for a in args],
        out_specs=pl.BlockSpec((db, c), lambda i: (0, 0)),
        out_shape=jax.ShapeDtypeStruct((db, c), MXU_DTYPE),
        compiler_params=_cparams(1), name="sconv",
    )(*args)


def _dec_score_kernel(pt_ref, iq_ref, iw_ref, ikn_ref, *refs, n_pages, width):
    del pt_ref
    pages, o_ref = refs[:n_pages], refs[n_pages]
    iq = iq_ref[...][:, :IDX_DIM]
    w = iw_ref[...]
    pieces = []
    for p in range(n_pages):
        s = jnp.dot(iq, pages[p][...].astype(MXU_DTYPE), preferred_element_type=jnp.float32)
        pieces.append(jnp.sum(w * jnp.maximum(s, 0.0), axis=0, keepdims=True))
    ikn = ikn_ref[...][:, :IDX_DIM].astype(MXU_DTYPE).astype(jnp.float32)
    sn = jnp.sum(iq.astype(jnp.float32) * ikn, axis=-1, keepdims=True)
    scn = jnp.sum(w * jnp.maximum(sn, 0.0), axis=0, keepdims=True)
    page = pieces[0].shape[1]
    lane = lax.broadcasted_iota(jnp.int32, (1, width - n_pages * page), 1)
    pieces.append(jnp.where(lane == 0, scn, -jnp.inf))
    o_ref[...] = jnp.concatenate(pieces, axis=1)


def _dec_scores(page_table, iq3, iw3, tail3, cache_ik, layer, width):
    db, n_pages = page_table.shape
    di, page = cache_ik.shape[2:]
    seq = lambda a: pl.BlockSpec((None,) + a.shape[1:], lambda b, pt: (b, 0, 0))
    page_spec = lambda p: pl.BlockSpec((None, None, di, page), lambda b, pt: (layer, pt[b, p], 0, 0))
    return pl.pallas_call(
        functools.partial(_dec_score_kernel, n_pages=n_pages, width=width),
        grid_spec=pltpu.PrefetchScalarGridSpec(
            num_scalar_prefetch=1, grid=(db,),
            in_specs=[seq(iq3), seq(iw3), seq(tail3)] + [page_spec(p) for p in range(n_pages)],
            out_specs=pl.BlockSpec((None, 1, width), lambda b, pt: (b, 0, 0))),
        out_shape=jax.ShapeDtypeStruct((db, 1, width), jnp.float32),
        compiler_params=_cparams(1), name="dec_score",
    )(page_table, iq3, iw3, tail3, *([cache_ik] * n_pages))


def _dec_select_kernel(s_ref, o_ref, keys_ref, dig_ref, work_ref, alive_ref, *, kk):
    nc = keys_ref.shape[0]
    for c in range(nc):
        keys_ref[c] = _to_key(s_ref[c * KC:(c + 1) * KC, :])
        _pack_digits(keys_ref, dig_ref, c)
    thr = _threshold(keys_ref, dig_ref, work_ref, alive_ref, nc, kk)
    for c in range(nc):
        o_ref[c * KC:(c + 1) * KC, :] = jnp.where(keys_ref[c] >= thr, 0.0, NEG)


def _dec_select(scores_t, kk):
    width, db = scores_t.shape
    nc = width // KC
    spec = pl.BlockSpec((width, db), lambda i: (0, 0))
    return pl.pallas_call(
        functools.partial(_dec_select_kernel, kk=kk),
        grid=(1,), in_specs=[spec], out_specs=spec,
        out_shape=jax.ShapeDtypeStruct((width, db), jnp.float32),
        scratch_shapes=[pltpu.VMEM((nc, KC, db), jnp.int32),
                        pltpu.VMEM((N_LEVELS, nc, TQ, db), jnp.int32), pltpu.VMEM((nc, TQ, db), jnp.int32),
                        pltpu.VMEM((nc, TQ, db), jnp.int32)],
        compiler_params=_cparams(1), name="dec_select",
    )(scores_t)


def _dec_attend_kernel(pt_ref, qz_ref, kvn_ref, madd_ref, bias_ref, ga_ref, *refs, n_pages):
    del pt_ref
    kpages, vpages, o_ref = refs[:n_pages], refs[n_pages:2 * n_pages], refs[2 * n_pages]
    qz = qz_ref[...]
    page = kpages[0].shape[1]
    past = n_pages * page
    nt = (((1,), (1,)), ((), ()))
    logits = []
    for p in range(n_pages):
        sl = slice(p * page, (p + 1) * page)
        lg = jnp.dot(qz, kpages[p][...].astype(MXU_DTYPE), preferred_element_type=jnp.float32)
        logits.append(lg + bias_ref[:, sl] + madd_ref[:, sl])
    kvn = kvn_ref[...].astype(MXU_DTYPE).astype(jnp.float32)
    lgn = jnp.sum(qz.astype(jnp.float32) * kvn[:, :LANE], axis=-1, keepdims=True)
    lgn = lgn + bias_ref[:, past:past + 1] + madd_ref[:, past:past + 1]
    m = lgn
    for lg in logits:
        m = jnp.maximum(m, jnp.max(lg, axis=-1, keepdims=True))
    pn = jnp.exp2(lgn - m)
    den = pn
    o = pn.astype(MXU_DTYPE).astype(jnp.float32) * kvn[:, LANE:]
    for p in range(n_pages):
        pp = jnp.exp2(logits[p] - m)
        den = den + jnp.sum(pp, axis=-1, keepdims=True)
        o = o + lax.dot_general(pp.astype(MXU_DTYPE), vpages[p][...].astype(MXU_DTYPE), nt,
                                preferred_element_type=jnp.float32)
    o = o / den
    lane = lax.broadcasted_iota(jnp.int32, (GROUP, LANE), 1)
    pair = jnp.where(lane < HEAD_DIM, o[:GROUP], o[GROUP:])
    o_ref[...] = (pair * _silu(ga_ref[...])).astype(o_ref.dtype)


def _dec_attend(page_table, qz3, kvn3, madd3, bias, ga3, cache_k, cache_v, layer):
    db, n_pages = page_table.shape
    feat, page = cache_k.shape[2:]
    seq = lambda a: pl.BlockSpec((None,) + a.shape[1:], lambda b, pt: (b, 0, 0))
    page_spec = lambda p: pl.BlockSpec((None, None, feat, page), lambda b, pt: (layer, pt[b, p], 0, 0))
    return pl.pallas_call(
        functools.partial(_dec_attend_kernel, n_pages=n_pages),
        grid_spec=pltpu.PrefetchScalarGridSpec(
            num_scalar_prefetch=1, grid=(db,),
            in_specs=[seq(qz3), seq(kvn3), seq(madd3), pl.BlockSpec(bias.shape, lambda b, pt: (0, 0)), seq(ga3)]
                     + [page_spec(p) for p in range(n_pages)] * 2,
            out_specs=pl.BlockSpec((None, GROUP, LANE), lambda b, pt: (b, 0, 0))),
        out_shape=jax.ShapeDtypeStruct((db, GROUP, LANE), MXU_DTYPE),
        compiler_params=_cparams(1), name="dec_attend",
    )(page_table, qz3, kvn3, madd3, bias, ga3, *([cache_k] * n_pages), *([cache_v] * n_pages))


def _round_up(x, m):
    return -(-x // m) * m


def _row_tile(rows):
    for f in (5, 4, 3, 2, 1):
        if rows % (f * TQ) == 0:
            return f * TQ
    return rows


def kernel(x_prompt, x_sample, cache_k, cache_v, cache_idx_k, state_conv, page_table, meta_tokens, rel_bias,
           norm_g, w_in, conv_w, conv_b, conv_ln_g, conv_ln_b, w_out, final_norm_g):
    bsz, seq, d = x_prompt.shape
    depth = w_in.shape[0]
    lp = seq + N_META
    lq = _round_up(lp, TQ)
    lk = _round_up(lp, KC)
    nq, nc = lq // TQ, lk // KC
    kk_p = min(TOPK_MAX, lp // 4)
    db = x_sample.shape[0]
    n_pool, page = cache_k.shape[1:3]
    n_pages = page_table.shape[1]
    past = n_pages * page
    kk_s = min(TOPK_MAX, (past + 1) // 4)
    width_s = _round_up(past + 1, KC)
    feat = N_KV_HEADS * HEAD_DIM
    tm = _row_tile(bsz * lq)

    xp = jnp.concatenate([jnp.broadcast_to(meta_tokens[None].astype(x_prompt.dtype), (bsz, N_META, d)), x_prompt,
                          jnp.zeros((bsz, lq - lp, d), x_prompt.dtype)], axis=1).reshape(bsz * lq, d)
    xs = x_sample.reshape(db, d)
    toep_t = _bias_tiles(rel_bias)
    bias_s = jnp.moveaxis(rel_bias[_t5_bucket(past - jnp.arange(width_s, dtype=jnp.int32))], -1, 0) * LOG2E
    cache_k4 = jnp.transpose(cache_k, (0, 1, 3, 4, 2)).reshape(depth, n_pool, feat, page)
    cache_v4 = jnp.transpose(cache_v, (0, 1, 3, 4, 2)).reshape(depth, n_pool, feat, page)
    cache_ik4 = jnp.swapaxes(cache_idx_k, 2, 3)

    def key_chunks(a):
        return jnp.pad(a, ((0, 0), (0, lk - lq), (0, 0))).astype(MXU_DTYPE).reshape(bsz, nc, KC, LANE)

    def key_chunks_t(a):
        a = jnp.pad(a.reshape(LANE, bsz, lq), ((0, 0), (0, 0), (0, lk - lq)))
        return jnp.transpose(a.reshape(LANE, bsz, nc, KC), (1, 2, 0, 3))

    kp, vp, ikp, cp, ksm, vsm, iks, cs = ([] for _ in range(8))
    yp = ys = None
    for l in range(depth):
        w = _prep_w_in(w_in[l])
        wc = w_out[l, :C_CONV].astype(MXU_DTYPE)
        wa = w_out[l, C_CONV:][_PERM].astype(MXU_DTYPE)
        last = l == depth - 1

        u, gc, kv, ga, tail, qzt, iqzt, vt, iwt = _project(xp, norm_g[l], w, tm, feature_major=True)
        b3 = lambda a: a.reshape(bsz, lq, a.shape[-1])
        u3, kv3, tail3 = b3(u), b3(kv), b3(tail)
        mixc = _conv_branch(u3, b3(gc), conv_w[l], conv_b[l], conv_ln_g[l], conv_ln_b[l])
        lane = jnp.arange(LANE)
        ik = key_chunks(jnp.where(lane < IDX_DIM, tail3, 0.0))
        mixa = _attend(iqzt.reshape(N_IDX_HEADS, LANE, bsz * lq), iwt, qzt.reshape(N_HEADS, LANE, bsz * lq), b3(ga),
                       ik, key_chunks(kv3[..., :feat]), key_chunks_t(vt), toep_t, kk_p)
        res = _out_proj(xp, mixc.reshape(bsz * lq, C_CONV), mixa.reshape(bsz * lq, -1), wc, wa, tm,
                        final_norm_g if last else None)
        xp, yp = res if last else (res, None)
        kp.append(kv3[:, :lp, :feat].reshape(bsz, lp, N_KV_HEADS, HEAD_DIM))
        vp.append(kv3[:, :lp, feat:].reshape(bsz, lp, N_KV_HEADS, HEAD_DIM))
        ikp.append(tail3[:, :lp, :IDX_DIM])
        cp.append(u3[:, lp - (CONV_W - 1):lp])

        u, gc, kv, ga, tail, qz, iqz = _project(xs, norm_g[l], w, db, feature_major=False)
        mixc = _sample_conv(jnp.transpose(state_conv[l], (1, 0, 2)), u, gc,
                            conv_w[l], conv_b[l], conv_ln_g[l], conv_ln_b[l])
        iw3 = tail[:, IDX_DIM:IDX_DIM + N_IDX_HEADS].reshape(db, N_IDX_HEADS, 1)
        scores = _dec_scores(page_table, iqz.reshape(db, N_IDX_HEADS, LANE), iw3, tail.reshape(db, 1, LANE),
                             cache_ik4, l, width_s)
        madd = _dec_select(scores.reshape(db, width_s).T, kk_s).T
        mixa = _dec_attend(page_table, qz.reshape(db, N_HEADS, LANE), kv.reshape(db, 1, 2 * feat),
                           madd.reshape(db, 1, width_s), bias_s, ga.reshape(db, GROUP, LANE),
                           cache_k4, cache_v4, l)
        res = _out_proj(xs, mixc, mixa.reshape(db, -1), wc, wa, db, final_norm_g if last else None)
        xs, ys = res if last else (res, None)
        ksm.append(kv[:, :feat].reshape(db, 1, N_KV_HEADS, HEAD_DIM))
        vsm.append(kv[:, feat:].reshape(db, 1, N_KV_HEADS, HEAD_DIM))
        iks.append(tail[:, None, :IDX_DIM])
        cs.append(jnp.concatenate([state_conv[l][:, 1:], u[:, None]], axis=1))

    y_prompt = yp.reshape(bsz, lq, d)[:, N_META:lp]
    y_sample = ys.reshape(db, 1, d)
    return (y_prompt, y_sample, jnp.stack(kp), jnp.stack(vp), jnp.stack(ikp), jnp.stack(cp),
            jnp.stack(ksm), jnp.stack(vsm), jnp.stack(iks), jnp.stack(cs))
```

```python
import functools
import math

import numpy as np
import jax
import jax.numpy as jnp
from jax import lax
from jax.experimental import pallas as pl
from jax.experimental.pallas import tpu as pltpu

N_HEADS = 8
N_KV_HEADS = 2
GROUP = N_HEADS // N_KV_HEADS
HEAD_DIM = 64
N_IDX_HEADS = 4
IDX_DIM = 64
C_CONV = 512
CONV_W = 31
TOPK_MAX = 256
N_META = 16
NUM_BUCKETS = 32
MAX_DISTANCE = 128
EPS = 1e-6
LN_EPS = 1e-5

LANE = 128
ROWS = 8
TQ = 128
KC = 512
SUB = KC // TQ
N_ACC = 4
DIGIT_BITS = 7
DIGIT_MASK = (1 << DIGIT_BITS) - 1
KEY_DIGITS = (7, 7, 7, 7, 4)
KEY_SHIFTS = tuple(32 - sum(KEY_DIGITS[:i + 1]) for i in range(len(KEY_DIGITS)))
N_LEVELS = len(KEY_DIGITS)
HALO = 32
CONV_SUB = 64
MXU_DTYPE = jnp.bfloat16
NEG = -1e30
LOG2E = 1.4426950408889634
INT_MIN = -2 ** 31
TIE_SPAN = 1 << 14
KEY_NEG_INF = -0x7F800000 - TIE_SPAN
VMEM_LIMIT = 56 * 1024 * 1024

_Q_OFF = 3 * C_CONV
_KV_OFF = _Q_OFF + N_HEADS * LANE
_GA_OFF = _KV_OFF + 2 * N_KV_HEADS * HEAD_DIM
_IQ_OFF = _GA_OFF + N_HEADS * HEAD_DIM
_TAIL_OFF = _IQ_OFF + N_IDX_HEADS * LANE
_W_COLS = _TAIL_OFF + LANE

_HEAD_ORDER = [h for j in range(GROUP) for h in (j, j + GROUP)]
_PERM = np.concatenate([np.arange(h * HEAD_DIM, (h + 1) * HEAD_DIM) for h in _HEAD_ORDER])

_PROJ_SIZES = (C_CONV, C_CONV, C_CONV, N_HEADS * HEAD_DIM, N_KV_HEADS * HEAD_DIM, N_KV_HEADS * HEAD_DIM,
               N_HEADS * HEAD_DIM, N_IDX_HEADS * IDX_DIM, IDX_DIM, N_IDX_HEADS)
_PROJ_SPLITS = tuple(int(s) for s in np.cumsum(_PROJ_SIZES)[:-1])


def _cparams(n_axes):
    return pltpu.CompilerParams(dimension_semantics=("arbitrary",) * n_axes, vmem_limit_bytes=VMEM_LIMIT)


def _silu(x):
    return x * jax.nn.sigmoid(x)


def _prep_w_in(w):
    ua, ub, gc, q, k, v, ga, iq, ik, iw = jnp.split(w.T, _PROJ_SPLITS, axis=0)
    d = w.shape[0]
    z = jnp.zeros((HEAD_DIM, d), w.dtype)
    qz = []
    for h in range(N_HEADS):
        qh = q[h * HEAD_DIM:(h + 1) * HEAD_DIM]
        qz += [qh, z] if h // GROUP == 0 else [z, qh]
    iqz = []
    for h in range(N_IDX_HEADS):
        iqz += [iq[h * IDX_DIM:(h + 1) * IDX_DIM], z]
    ga_p = [ga[h * HEAD_DIM:(h + 1) * HEAD_DIM] for h in _HEAD_ORDER]
    tail = [ik, iw, jnp.zeros((LANE - IDX_DIM - N_IDX_HEADS, d), w.dtype)]
    out = jnp.concatenate([ua, ub, gc] + qz + [k, v] + ga_p + iqz + tail, axis=0)
    assert out.shape[0] == _W_COLS
    return out.astype(MXU_DTYPE)


def _t5_bucket(rel):
    n = jnp.maximum(rel, 0)
    max_exact = NUM_BUCKETS // 2
    large = max_exact + (jnp.log(jnp.maximum(n, 1).astype(jnp.float32) / max_exact)
                         / math.log(MAX_DISTANCE / max_exact)
                         * (NUM_BUCKETS - max_exact)).astype(jnp.int32)
    large = jnp.minimum(large, NUM_BUCKETS - 1)
    return jnp.where(n < max_exact, n, large)


def _bias_tiles(rel_bias):
    far = rel_bias[NUM_BUCKETS - 1]
    d = jnp.arange(2 * TQ, dtype=jnp.int32)
    dists = (jnp.where(d < TQ, d, 0),
             jnp.where(d < TQ, d + TQ, d - TQ))
    tiles = []
    for dist in dists:
        v = ((rel_bias[_t5_bucket(dist)] - far) * LOG2E).T
        rep = jnp.tile(v, (1, TQ))[:, :TQ * (2 * TQ - 1)].reshape(N_HEADS, TQ, 2 * TQ - 1)
        tiles.append(rep[:, :, :TQ])
    tiles.append(jnp.zeros_like(tiles[0]))
    return jnp.stack(tiles).astype(jnp.float32)


_NT = (((1,), (1,)), ((), ()))
_Q_SCALE = HEAD_DIM ** -0.5 * LOG2E
_IQ_SCALE = IDX_DIM ** -0.5
_IW_SCALE = N_IDX_HEADS ** -0.5
_IW_ROWS = 16


def _proj_kernel(x_ref, g_ref, w_ref, u_ref, gc_ref, kv_ref, ga_ref, tail_ref, *q_refs, feature_major):
    x = x_ref[...]
    ms = jnp.mean(x * x, axis=-1, keepdims=True)
    xn = (x * lax.rsqrt(ms + EPS) * g_ref[...]).astype(MXU_DTYPE)

    def mm(lo, hi):
        return lax.dot_general(xn, w_ref[lo:hi, :], _NT, preferred_element_type=jnp.float32)

    def mm_t(lo, hi):
        return lax.dot_general(w_ref[lo:hi, :], xn, _NT, preferred_element_type=jnp.float32)

    u_ref[...] = mm(0, C_CONV) * jax.nn.sigmoid(mm(C_CONV, 2 * C_CONV))
    gc_ref[...] = mm(2 * C_CONV, 3 * C_CONV)
    kv_ref[...] = mm(_KV_OFF, _GA_OFF)
    ga_ref[...] = mm(_GA_OFF, _IQ_OFF)
    t = mm(_TAIL_OFF, _W_COLS)
    lane = lax.broadcasted_iota(jnp.int32, t.shape, 1)
    tail_ref[...] = t * jnp.where(lane >= IDX_DIM, _IW_SCALE, 1.0)
    if feature_major:
        qzt_ref, iqzt_ref, vt_ref, iwt_ref = q_refs
        qzt_ref[...] = (mm_t(_Q_OFF, _KV_OFF) * _Q_SCALE).astype(qzt_ref.dtype)
        iqzt_ref[...] = (mm_t(_IQ_OFF, _TAIL_OFF) * _IQ_SCALE).astype(iqzt_ref.dtype)
        vt_ref[...] = mm_t(_KV_OFF + N_KV_HEADS * HEAD_DIM, _GA_OFF).astype(vt_ref.dtype)
        iwt_ref[...] = mm_t(_TAIL_OFF + IDX_DIM, _TAIL_OFF + IDX_DIM + _IW_ROWS) * _IW_SCALE
    else:
        qz_ref, iqz_ref = q_refs
        qz_ref[...] = (mm(_Q_OFF, _KV_OFF) * _Q_SCALE).astype(qz_ref.dtype)
        iqz_ref[...] = (mm(_IQ_OFF, _TAIL_OFF) * _IQ_SCALE).astype(iqz_ref.dtype)


def _project(x, g, w, tm, feature_major):
    r, d = x.shape
    assert r % tm == 0
    rows = lambda c: pl.BlockSpec((tm, c), lambda i: (i, 0))
    cols = lambda c: pl.BlockSpec((c, tm), lambda i: (0, i))
    feat = N_KV_HEADS * HEAD_DIM
    specs = [rows(C_CONV), rows(C_CONV), rows(2 * feat), rows(N_HEADS * HEAD_DIM), rows(LANE)]
    shapes = [jax.ShapeDtypeStruct((r, s.block_shape[1]), jnp.float32) for s in specs]
    if feature_major:
        extra = [(N_HEADS * LANE, MXU_DTYPE), (N_IDX_HEADS * LANE, MXU_DTYPE), (feat, MXU_DTYPE),
                 (_IW_ROWS, jnp.float32)]
        specs += [cols(c) for c, _ in extra]
        shapes += [jax.ShapeDtypeStruct((c, r), t) for c, t in extra]
    else:
        extra = [(N_HEADS * LANE, MXU_DTYPE), (N_IDX_HEADS * LANE, MXU_DTYPE)]
        specs += [rows(c) for c, _ in extra]
        shapes += [jax.ShapeDtypeStruct((r, c), t) for c, t in extra]
    return pl.pallas_call(
        functools.partial(_proj_kernel, feature_major=feature_major),
        grid=(r // tm,),
        in_specs=[rows(d), pl.BlockSpec((1, d), lambda i: (0, 0)), pl.BlockSpec((_W_COLS, d), lambda i: (0, 0))],
        out_specs=specs,
        out_shape=shapes,
        compiler_params=_cparams(1),
        name="proj",
    )(x, g.reshape(1, d), w)


def _conv_kernel(prev_ref, cur_ref, gc_ref, w_ref, b_ref, lg_ref, lb_ref, o_ref, ext_ref):
    i = pl.program_id(1)
    ext_ref[0, 0:HALO, :] = jnp.where(i > 0, prev_ref[...], 0.0)
    ext_ref[0, HALO:HALO + TQ, :] = cur_ref[...]
    n_ext = HALO + TQ
    for s in range(1, ROWS):
        ext_ref[s, 0:n_ext - ROWS, :] = ext_ref[0, s:s + n_ext - ROWS, :]
    off = HALO - (CONV_W - 1)
    for r0 in range(0, TQ, CONV_SUB):
        acc = jnp.zeros((CONV_SUB, C_CONV), jnp.float32)
        for j in range(CONV_W):
            s = (off + j) % ROWS
            lo = r0 + off + j - s
            acc = acc + w_ref[j:j + 1, :] * ext_ref[s, lo:lo + CONV_SUB, :]
        y = acc + b_ref[...]
        mu = jnp.mean(y, axis=-1, keepdims=True)
        dev = y - mu
        var = jnp.mean(dev * dev, axis=-1, keepdims=True)
        yn = dev * lax.rsqrt(var + LN_EPS) * lg_ref[...] + lb_ref[...]
        o_ref[r0:r0 + CONV_SUB, :] = (_silu(yn) * _silu(gc_ref[r0:r0 + CONV_SUB, :])).astype(o_ref.dtype)


def _conv_branch(u, gc, w, b, lg, lb):
    bsz, lq, c = u.shape
    per = TQ // HALO
    row = lambda a: a.reshape(1, c)
    vec = pl.BlockSpec((1, c), lambda bi, i: (0, 0))
    tile = pl.BlockSpec((None, TQ, c), lambda bi, i: (bi, i, 0))
    return pl.pallas_call(
        _conv_kernel,
        grid=(bsz, lq // TQ),
        in_specs=[pl.BlockSpec((None, HALO, c), lambda bi, i: (bi, jnp.maximum(i * per - 1, 0), 0)),
                  tile, tile, pl.BlockSpec((CONV_W, c), lambda bi, i: (0, 0)), vec, vec, vec],
        out_specs=tile,
        out_shape=jax.ShapeDtypeStruct((bsz, lq, c), MXU_DTYPE),
        scratch_shapes=[pltpu.VMEM((ROWS, HALO + TQ, c), jnp.float32)],
        compiler_params=_cparams(2),
        name="conv",
    )(u, u, gc, w, row(b), row(lg), row(lb))


_INT_MIN = np.int32(INT_MIN)
_BYTE_ONES = np.int32(0x01010101)
_BYTE_LOW = np.int32(0x7F7F7F7F)
_GUARD = np.int32(0x80808080 - (1 << 32))
assert SUB == 4 and DIGIT_BITS == 7


def _to_key(s, idx):
    bits = lax.bitcast_convert_type(s, jnp.int32)
    key = jnp.where(bits < 0, (_INT_MIN - bits) - TIE_SPAN, bits)
    return jnp.where((bits == 0) | (bits == _INT_MIN), -1 - idx, key)


def _for_chunks(nkc, body, init, unroll):
    n_main = nkc // unroll

    def main(i, carry):
        for k in range(unroll):
            carry = body(i * unroll + k, carry)
        return carry

    carry = lax.fori_loop(0, n_main, main, init)
    return lax.fori_loop(n_main * unroll, nkc, body, carry)


def _pack_digits(keys_ref, dig_ref, c):
    for lv in range(N_LEVELS):
        shift = KEY_SHIFTS[lv]
        word = None
        for j in range(SUB):
            u = keys_ref[c, j * TQ:(j + 1) * TQ, :] ^ _INT_MIN
            field = lax.shift_right_logical(u, shift) & ((1 << KEY_DIGITS[lv]) - 1)
            field = field if j == 0 else lax.shift_left(field, 8 * j)
            word = field if word is None else word | field
        dig_ref[lv, c] = word


def _count_fields(work_ref, nkc, cand_bytes):
    q = work_ref.shape[2]

    def body(c, accs):
        accs = list(accs)
        for n, r in enumerate(range(0, TQ, ROWS)):
            diff = work_ref[c, r:r + ROWS, :] - cand_bytes
            accs[n % N_ACC] = accs[n % N_ACC] + (lax.shift_right_logical(diff, DIGIT_BITS) & _BYTE_ONES)
        return tuple(accs)

    accs = _for_chunks(nkc, body, (jnp.zeros((ROWS, q), jnp.int32),) * N_ACC, unroll=4)
    total = jnp.zeros((ROWS, q), jnp.int32)
    for a in accs:
        total = total + ((a & 0xFF) + (lax.shift_right_logical(a, 8) & 0xFF)
                         + (lax.shift_right_logical(a, 16) & 0xFF) + lax.shift_right_logical(a, 24))
    return jnp.sum(total.astype(jnp.float32), axis=0, keepdims=True)


def _digit_search(dig_ref, work_ref, alive_ref, nkc, shifts, digits, target):
    q = dig_ref.shape[3]
    assert dig_ref.shape[1] * (TQ // ROWS) <= 255 * N_ACC
    zero = jnp.zeros((1, q), jnp.float32)
    value = jnp.zeros((1, q), jnp.int32)
    above, done, cnt_ge, cnt_gt = zero, zero, target + 1.0, zero

    for lv, (shift, n_bits) in enumerate(zip(shifts, digits)):
        def load_fields(c, carry):
            work_ref[c] = (dig_ref[lv, c] & alive_ref[c]) | _GUARD
            return carry

        _for_chunks(nkc, load_fields, 0, unroll=4)

        def digit_bit(i, state):
            digit, cnt_ge, cnt_gt, cnt_rej, done = state
            cand = digit | lax.shift_left(jnp.int32(1), n_bits - 1 - i)
            cnt_alive = _count_fields(work_ref, nkc, cand * _BYTE_ONES)
            cnt = above + cnt_alive
            take = (cnt >= target) & (done == 0.0)
            drop = (cnt < target) & (done == 0.0)
            return (jnp.where(take, cand, digit), jnp.where(take, cnt, cnt_ge), jnp.where(drop, cnt, cnt_gt),
                    jnp.where(drop, cnt_alive, cnt_rej),
                    jnp.where(take & (cnt == target), 1.0, done))

        digit, cnt_ge, cnt_gt, cnt_rej, done = lax.fori_loop(
            0, n_bits, digit_bit, (jnp.zeros((1, q), jnp.int32), cnt_ge, cnt_gt, zero, done))
        above = above + cnt_rej
        value = value | lax.shift_left(digit, shift)

        if lv + 1 < len(shifts):
            digit_bytes = digit * _BYTE_ONES

            def narrow(c, carry):
                differs = lax.shift_right_logical(((dig_ref[lv, c] ^ digit_bytes) + _BYTE_LOW) & _GUARD, DIGIT_BITS)
                alive_ref[c] = alive_ref[c] & ((_BYTE_ONES - differs) * DIGIT_MASK)
                return carry

            _for_chunks(nkc, narrow, 0, unroll=4)

    return value, cnt_ge, cnt_gt, done


def _fill(ref, nkc, word):
    def body(c, carry):
        ref[c] = jnp.full(ref.shape[1:], word, ref.dtype)
        return carry

    _for_chunks(nkc, body, 0, unroll=4)


def _threshold(keys_ref, dig_ref, work_ref, alive_ref, nkc, kk):
    q = keys_ref.shape[2]
    kf = jnp.full((1, q), kk, jnp.float32)
    _fill(alive_ref, nkc, _BYTE_LOW)
    prefix, cnt_thr, cnt_gt, _ = _digit_search(dig_ref, work_ref, alive_ref, nkc, KEY_SHIFTS, KEY_DIGITS, kf)
    thr = prefix ^ _INT_MIN
    admissible = thr != KEY_NEG_INF
    surplus = (cnt_thr > kf) & admissible

    @pl.when(jnp.max(jnp.where(surplus, 1.0, 0.0)) > 0.0)
    def _():
        assert keys_ref.shape[0] * KC <= 1 << (2 * DIGIT_BITS)
        top = (1 << (2 * DIGIT_BITS)) - 1
        row = lax.broadcasted_iota(jnp.int32, (TQ, q), 0)

        def tie_fields(c, carry):
            hi = lo = tie = None
            for j in range(SUB):
                is_tie = keys_ref[c, j * TQ:(j + 1) * TQ, :] == thr
                rev = top - (c * KC + j * TQ + row)
                fields = [jnp.where(is_tie, f, 0) for f in
                          (lax.shift_right_logical(rev, DIGIT_BITS), rev & DIGIT_MASK, DIGIT_MASK)]
                fields = [f if j == 0 else lax.shift_left(f, 8 * j) for f in fields]
                hi, lo, tie = fields if j == 0 else (hi | fields[0], lo | fields[1], tie | fields[2])
            dig_ref[0, c], dig_ref[1, c], alive_ref[c] = hi, lo, tie
            return carry

        lax.fori_loop(0, nkc, tie_fields, 0)
        cut, _, _, _ = _digit_search(dig_ref, work_ref, alive_ref, nkc, (DIGIT_BITS, 0), (DIGIT_BITS, DIGIT_BITS),
                                     kf - cnt_gt)

        def drop_losers(c, carry):
            for j in range(SUB):
                rows = slice(j * TQ, (j + 1) * TQ)
                x = keys_ref[c, rows, :]
                lost = (x == thr) & (top - (c * KC + j * TQ + row) < cut)
                keys_ref[c, rows, :] = jnp.where(lost, thr - 1, x)
            return carry

        lax.fori_loop(0, nkc, drop_losers, 0)

    return jnp.where(admissible, thr, KEY_NEG_INF + 1)


def _attend_kernel(iqt_ref, iwt_ref, qzt_ref, ga_ref, ik_ref, k_ref, vt_ref, toep_ref, o_ref,
                   keys_ref, dig_ref, work_ref, alive_ref, m_ref, l_ref, acc_ref, lg0_ref, lg1_ref,
                   wq_ref, *, kk):
    qi = pl.program_id(1)
    nkc = qi // SUB + 1
    qpos = qi * TQ + lax.broadcasted_iota(jnp.int32, (TQ, TQ), 1)
    krow = lax.broadcasted_iota(jnp.int32, (TQ, TQ), 0)
    for h in range(N_HEADS):
        wq_ref[:, h * TQ:(h + 1) * TQ] = qzt_ref[h]

    def score_chunk(c, carry):
        iqt = jnp.concatenate([iqt_ref[h] for h in range(N_IDX_HEADS)], axis=1)
        w = iwt_ref[...]
        for j in range(SUB):
            st = jnp.dot(ik_ref[c, j * TQ:(j + 1) * TQ, :], iqt, preferred_element_type=jnp.float32)
            s = jnp.zeros((TQ, TQ), jnp.float32)
            for h in range(N_IDX_HEADS):
                s = s + w[h:h + 1, :] * jnp.maximum(st[:, h * TQ:(h + 1) * TQ], 0.0)
            kpos = c * KC + j * TQ + krow
            keys_ref[c, j * TQ:(j + 1) * TQ, :] = _to_key(jnp.where(kpos <= qpos, s, -jnp.inf), kpos)
        _pack_digits(keys_ref, dig_ref, c)
        return carry

    lax.fori_loop(0, nkc, score_chunk, 0)
    thr = _threshold(keys_ref, dig_ref, work_ref, alive_ref, nkc, kk)

    m_ref[...] = jnp.full(m_ref.shape, NEG, jnp.float32)
    l_ref[...] = jnp.zeros(l_ref.shape, jnp.float32)
    acc_ref[...] = jnp.zeros(acc_ref.shape, jnp.float32)

    def logits(c, lg_ref):
        lg_ref[...] = jnp.dot(k_ref[c], wq_ref[...], preferred_element_type=jnp.float32)

    def softmax_pv(c, lg_ref, near):
        md = jnp.where(keys_ref[c] >= thr, 0.0, NEG)
        md2 = jnp.concatenate([md, md], axis=1)
        if near:
            delta = [jnp.clip(qi - (c * SUB + j), 0, 2) for j in range(SUB)]
        for pr in range(N_HEADS // 2):
            kv = pr // (GROUP // 2)
            lg = lg_ref[:, 2 * pr * TQ:(2 * pr + 2) * TQ] + md2
            if near:
                lg = lg + jnp.concatenate(
                    [jnp.concatenate([toep_ref[delta[j], 2 * pr], toep_ref[delta[j], 2 * pr + 1]], axis=1)
                     for j in range(SUB)], axis=0)
            m_old = m_ref[pr]
            m_new = jnp.maximum(m_old, jnp.max(lg, axis=0, keepdims=True))
            alpha = jnp.exp2(m_old - m_new)
            p = jnp.exp2(lg - m_new)
            l_ref[pr] = alpha * l_ref[pr] + jnp.sum(p, axis=0, keepdims=True)
            pv = jnp.dot(vt_ref[c, kv * HEAD_DIM:(kv + 1) * HEAD_DIM, :], p.astype(MXU_DTYPE),
                         preferred_element_type=jnp.float32)
            acc_ref[pr] = alpha * acc_ref[pr] + pv
            m_ref[pr] = m_new

    n_far2 = (jnp.maximum(qi - 1, 0) // SUB) // 2

    def far_pair(i, carry):
        logits(2 * i + 1, lg1_ref)
        softmax_pv(2 * i, lg0_ref, near=False)
        logits(2 * i + 2, lg0_ref)
        softmax_pv(2 * i + 1, lg1_ref, near=False)
        return carry

    logits(0, lg0_ref)
    lax.fori_loop(0, n_far2, far_pair, 0)

    c0 = 2 * n_far2
    c_last = k_ref.shape[0] - 1
    logits(jnp.minimum(c0 + 1, c_last), lg1_ref)
    softmax_pv(c0, lg0_ref, near=True)

    @pl.when(c0 + 1 < nkc)
    def _():
        logits(jnp.minimum(c0 + 2, c_last), lg0_ref)
        softmax_pv(c0 + 1, lg1_ref, near=True)

    @pl.when(c0 + 2 < nkc)
    def _():
        softmax_pv(c0 + 2, lg0_ref, near=True)

    for j in range(GROUP):
        half = slice((j % 2) * TQ, (j % 2 + 1) * TQ)
        lo, hi = j // 2, j // 2 + GROUP // 2
        pair_t = jnp.concatenate([acc_ref[lo][:, half] / l_ref[lo][:, half],
                                  acc_ref[hi][:, half] / l_ref[hi][:, half]], axis=0)
        sl = slice(j * LANE, (j + 1) * LANE)
        o_ref[:, sl] = (pair_t.T * _silu(ga_ref[:, sl])).astype(o_ref.dtype)


def _attend(iqt, iwt, qzt, ga, ik, k, vt, toep_t, kk):
    bsz, lq = ga.shape[:2]
    nq = lq // TQ
    nc = k.shape[1]
    lanes = lambda a: pl.BlockSpec(a.shape[:-1] + (TQ,), lambda b, i: (0,) * (a.ndim - 1) + (b * nq + i,))
    whole = lambda a: pl.BlockSpec((None,) + a.shape[1:], lambda b, i: (b,) + (0,) * (a.ndim - 1))
    rows = pl.BlockSpec((None, TQ, N_HEADS * HEAD_DIM), lambda b, i: (b, i, 0))
    return pl.pallas_call(
        functools.partial(_attend_kernel, kk=kk),
        grid=(bsz, nq),
        in_specs=[lanes(iqt), lanes(iwt), lanes(qzt), rows, whole(ik), whole(k), whole(vt),
                  pl.BlockSpec(toep_t.shape, lambda b, i: (0, 0, 0, 0))],
        out_specs=rows,
        out_shape=jax.ShapeDtypeStruct((bsz, lq, N_HEADS * HEAD_DIM), MXU_DTYPE),
        scratch_shapes=[pltpu.VMEM((nc, KC, TQ), jnp.int32),
                        pltpu.VMEM((N_LEVELS, nc, TQ, TQ), jnp.int32), pltpu.VMEM((nc, TQ, TQ), jnp.int32),
                        pltpu.VMEM((nc, TQ, TQ), jnp.int32),
                        pltpu.VMEM((N_HEADS // 2, 1, 2 * TQ), jnp.float32),
                        pltpu.VMEM((N_HEADS // 2, 1, 2 * TQ), jnp.float32),
                        pltpu.VMEM((N_HEADS // 2, HEAD_DIM, 2 * TQ), jnp.float32),
                        pltpu.VMEM((KC, N_HEADS * TQ), jnp.float32), pltpu.VMEM((KC, N_HEADS * TQ), jnp.float32),
                        pltpu.VMEM((LANE, N_HEADS * TQ), MXU_DTYPE)],
        compiler_params=_cparams(2),
        name="attend",
    )(iqt, iwt, qzt, ga, ik, k, vt, toep_t)


def _out_kernel(x_ref, mc_ref, ma_ref, wc_ref, wa_ref, o_ref):
    o_ref[...] = (x_ref[...]
                  + jnp.dot(mc_ref[...], wc_ref[...], preferred_element_type=jnp.float32)
                  + jnp.dot(ma_ref[...], wa_ref[...], preferred_element_type=jnp.float32))


def _out_final_kernel(x_ref, mc_ref, ma_ref, wc_ref, wa_ref, g_ref, o_ref, y_ref):
    x = (x_ref[...]
         + jnp.dot(mc_ref[...], wc_ref[...], preferred_element_type=jnp.float32)
         + jnp.dot(ma_ref[...], wa_ref[...], preferred_element_type=jnp.float32))
    o_ref[...] = x
    ms = jnp.mean(x * x, axis=-1, keepdims=True)
    y_ref[...] = x * lax.rsqrt(ms + EPS) * g_ref[...]


def _out_proj(x, mc, ma, wc, wa, tm, final_g=None):
    r, d = x.shape
    c = mc.shape[1]
    rows = lambda w: pl.BlockSpec((tm, w), lambda i: (i, 0))
    full = lambda a: pl.BlockSpec(a.shape, lambda i: (0, 0))
    if final_g is None:
        return pl.pallas_call(
            _out_kernel, grid=(r // tm,),
            in_specs=[rows(d), rows(c), rows(c), full(wc), full(wa)],
            out_specs=rows(d), out_shape=jax.ShapeDtypeStruct((r, d), jnp.float32),
            compiler_params=_cparams(1), name="out",
        )(x, mc, ma, wc, wa)
    g = final_g.reshape(1, d)
    return pl.pallas_call(
        _out_final_kernel, grid=(r // tm,),
        in_specs=[rows(d), rows(c), rows(c), full(wc), full(wa), full(g)],
        out_specs=[rows(d), rows(d)], out_shape=[jax.ShapeDtypeStruct((r, d), jnp.float32)] * 2,
        compiler_params=_cparams(1), name="out_final",
    )(x, mc, ma, wc, wa, g)


def _sconv_kernel(st_ref, u_ref, gc_ref, w_ref, b_ref, lg_ref, lb_ref, o_ref):
    acc = jnp.zeros(u_ref.shape, jnp.float32)
    for j in range(CONV_W - 1):
        acc = acc + w_ref[j:j + 1, :] * st_ref[j]
    acc = acc + w_ref[CONV_W - 1:CONV_W, :] * u_ref[...]
    y = acc + b_ref[...]
    mu = jnp.mean(y, axis=-1, keepdims=True)
    dev = y - mu
    var = jnp.mean(dev * dev, axis=-1, keepdims=True)
    yn = dev * lax.rsqrt(var + LN_EPS) * lg_ref[...] + lb_ref[...]
    o_ref[...] = (_silu(yn) * _silu(gc_ref[...])).astype(o_ref.dtype)


def _sample_conv(state_t, u, gc, w, b, lg, lb):
    db, c = u.shape
    row = lambda a: a.reshape(1, c)
    full = lambda a: pl.BlockSpec(a.shape, lambda i: (0,) * a.ndim)
    args = (state_t, u, gc, w, row(b), row(lg), row(lb))
    return pl.pallas_call(
        _sconv_kernel, grid=(1,), in_specs=[full(a) for a in args],
        out_specs=pl.BlockSpec((db, c), lambda i: (0, 0)),
        out_shape=jax.ShapeDtypeStruct((db, c), MXU_DTYPE),
        compiler_params=_cparams(1), name="sconv",
    )(*args)


def _dec_score_kernel(pt_ref, iq_ref, iw_ref, ikn_ref, *refs, n_pages, width):
    del pt_ref
    pages, o_ref = refs[:n_pages], refs[n_pages]
    iq = iq_ref[...][:, :IDX_DIM]
    w = iw_ref[...]
    pieces = []
    for p in range(n_pages):
        s = jnp.dot(iq, pages[p][...].astype(MXU_DTYPE), preferred_element_type=jnp.float32)
        pieces.append(jnp.sum(w * jnp.maximum(s, 0.0), axis=0, keepdims=True))
    ikn = ikn_ref[...][:, :IDX_DIM].astype(MXU_DTYPE).astype(jnp.float32)
    sn = jnp.sum(iq.astype(jnp.float32) * ikn, axis=-1, keepdims=True)
    scn = jnp.sum(w * jnp.maximum(sn, 0.0), axis=0, keepdims=True)
    page = pieces[0].shape[1]
    lane = lax.broadcasted_iota(jnp.int32, (1, width - n_pages * page), 1)
    pieces.append(jnp.where(lane == 0, scn, -jnp.inf))
    o_ref[...] = jnp.concatenate(pieces, axis=1)


def _dec_scores(page_table, iq3, iw3, tail3, cache_ik, layer, width):
    db, n_pages = page_table.shape
    di, page = cache_ik.shape[2:]
    seq = lambda a: pl.BlockSpec((None,) + a.shape[1:], lambda b, pt: (b, 0, 0))
    page_spec = lambda p: pl.BlockSpec((None, None, di, page), lambda b, pt: (layer, pt[b, p], 0, 0))
    return pl.pallas_call(
        functools.partial(_dec_score_kernel, n_pages=n_pages, width=width),
        grid_spec=pltpu.PrefetchScalarGridSpec(
            num_scalar_prefetch=1, grid=(db,),
            in_specs=[seq(iq3), seq(iw3), seq(tail3)] + [page_spec(p) for p in range(n_pages)],
            out_specs=pl.BlockSpec((None, 1, width), lambda b, pt: (b, 0, 0))),
        out_shape=jax.ShapeDtypeStruct((db, 1, width), jnp.float32),
        compiler_params=_cparams(1), name="dec_score",
    )(page_table, iq3, iw3, tail3, *([cache_ik] * n_pages))


def _dec_select_kernel(s_ref, o_ref, keys_ref, dig_ref, work_ref, alive_ref, *, kk):
    nc = keys_ref.shape[0]
    row = lax.broadcasted_iota(jnp.int32, (KC, keys_ref.shape[2]), 0)
    for c in range(nc):
        keys_ref[c] = _to_key(s_ref[c * KC:(c + 1) * KC, :], c * KC + row)
        _pack_digits(keys_ref, dig_ref, c)
    thr = _threshold(keys_ref, dig_ref, work_ref, alive_ref, nc, kk)
    for c in range(nc):
        o_ref[c * KC:(c + 1) * KC, :] = jnp.where(keys_ref[c] >= thr, 0.0, NEG)


def _dec_select(scores_t, kk):
    width, db = scores_t.shape
    nc = width // KC
    spec = pl.BlockSpec((width, db), lambda i: (0, 0))
    return pl.pallas_call(
        functools.partial(_dec_select_kernel, kk=kk),
        grid=(1,), in_specs=[spec], out_specs=spec,
        out_shape=jax.ShapeDtypeStruct((width, db), jnp.float32),
        scratch_shapes=[pltpu.VMEM((nc, KC, db), jnp.int32),
                        pltpu.VMEM((N_LEVELS, nc, TQ, db), jnp.int32), pltpu.VMEM((nc, TQ, db), jnp.int32),
                        pltpu.VMEM((nc, TQ, db), jnp.int32)],
        compiler_params=_cparams(1), name="dec_select",
    )(scores_t)


def _dec_attend_kernel(pt_ref, qz_ref, kvn_ref, madd_ref, bias_ref, ga_ref, *refs, n_pages):
    del pt_ref
    kpages, vpages, o_ref = refs[:n_pages], refs[n_pages:2 * n_pages], refs[2 * n_pages]
    qz = qz_ref[...]
    page = kpages[0].shape[1]
    past = n_pages * page
    nt = (((1,), (1,)), ((), ()))
    logits = []
    for p in range(n_pages):
        sl = slice(p * page, (p + 1) * page)
        lg = jnp.dot(qz, kpages[p][...].astype(MXU_DTYPE), preferred_element_type=jnp.float32)
        logits.append(lg + bias_ref[:, sl] + madd_ref[:, sl])
    kvn = kvn_ref[...].astype(MXU_DTYPE).astype(jnp.float32)
    lgn = jnp.sum(qz.astype(jnp.float32) * kvn[:, :LANE], axis=-1, keepdims=True)
    lgn = lgn + bias_ref[:, past:past + 1] + madd_ref[:, past:past + 1]
    m = lgn
    for lg in logits:
        m = jnp.maximum(m, jnp.max(lg, axis=-1, keepdims=True))
    pn = jnp.exp2(lgn - m)
    den = pn
    o = pn.astype(MXU_DTYPE).astype(jnp.float32) * kvn[:, LANE:]
    for p in range(n_pages):
        pp = jnp.exp2(logits[p] - m)
        den = den + jnp.sum(pp, axis=-1, keepdims=True)
        o = o + lax.dot_general(pp.astype(MXU_DTYPE), vpages[p][...].astype(MXU_DTYPE), nt,
                                preferred_element_type=jnp.float32)
    o = o / den
    lane = lax.broadcasted_iota(jnp.int32, (GROUP, LANE), 1)
    pair = jnp.where(lane < HEAD_DIM, o[:GROUP], o[GROUP:])
    o_ref[...] = (pair * _silu(ga_ref[...])).astype(o_ref.dtype)


def _dec_attend(page_table, qz3, kvn3, madd3, bias, ga3, cache_k, cache_v, layer):
    db, n_pages = page_table.shape
    feat, page = cache_k.shape[2:]
    seq = lambda a: pl.BlockSpec((None,) + a.shape[1:], lambda b, pt: (b, 0, 0))
    page_spec = lambda p: pl.BlockSpec((None, None, feat, page), lambda b, pt: (layer, pt[b, p], 0, 0))
    return pl.pallas_call(
        functools.partial(_dec_attend_kernel, n_pages=n_pages),
        grid_spec=pltpu.PrefetchScalarGridSpec(
            num_scalar_prefetch=1, grid=(db,),
            in_specs=[seq(qz3), seq(kvn3), seq(madd3), pl.BlockSpec(bias.shape, lambda b, pt: (0, 0)), seq(ga3)]
                     + [page_spec(p) for p in range(n_pages)] * 2,
            out_specs=pl.BlockSpec((None, GROUP, LANE), lambda b, pt: (b, 0, 0))),
        out_shape=jax.ShapeDtypeStruct((db, GROUP, LANE), MXU_DTYPE),
        compiler_params=_cparams(1), name="dec_attend",
    )(page_table, qz3, kvn3, madd3, bias, ga3, *([cache_k] * n_pages), *([cache_v] * n_pages))


def _round_up(x, m):
    return -(-x // m) * m


def _row_tile(rows):
    for f in (5, 4, 3, 2, 1):
        if rows % (f * TQ) == 0:
            return f * TQ
    return rows


def kernel(x_prompt, x_sample, cache_k, cache_v, cache_idx_k, state_conv, page_table, meta_tokens, rel_bias,
           norm_g, w_in, conv_w, conv_b, conv_ln_g, conv_ln_b, w_out, final_norm_g):
    bsz, seq, d = x_prompt.shape
    depth = w_in.shape[0]
    lp = seq + N_META
    lq = _round_up(lp, TQ)
    lk = _round_up(lp, KC)
    nq, nc = lq // TQ, lk // KC
    kk_p = min(TOPK_MAX, lp // 4)
    db = x_sample.shape[0]
    n_pool, page = cache_k.shape[1:3]
    n_pages = page_table.shape[1]
    past = n_pages * page
    kk_s = min(TOPK_MAX, (past + 1) // 4)
    width_s = _round_up(past + 1, KC)
    feat = N_KV_HEADS * HEAD_DIM
    tm = _row_tile(bsz * lq)

    xp = jnp.concatenate([jnp.broadcast_to(meta_tokens[None].astype(x_prompt.dtype), (bsz, N_META, d)), x_prompt,
                          jnp.zeros((bsz, lq - lp, d), x_prompt.dtype)], axis=1).reshape(bsz * lq, d)
    xs = x_sample.reshape(db, d)
    toep_t = _bias_tiles(rel_bias)
    bias_s = jnp.moveaxis(rel_bias[_t5_bucket(past - jnp.arange(width_s, dtype=jnp.int32))], -1, 0) * LOG2E
    cache_k4 = jnp.transpose(cache_k, (0, 1, 3, 4, 2)).reshape(depth, n_pool, feat, page)
    cache_v4 = jnp.transpose(cache_v, (0, 1, 3, 4, 2)).reshape(depth, n_pool, feat, page)
    cache_ik4 = jnp.swapaxes(cache_idx_k, 2, 3)

    def key_chunks(a):
        return jnp.pad(a, ((0, 0), (0, lk - lq), (0, 0))).astype(MXU_DTYPE).reshape(bsz, nc, KC, LANE)

    def key_chunks_t(a):
        a = jnp.pad(a.reshape(LANE, bsz, lq), ((0, 0), (0, 0), (0, lk - lq)))
        return jnp.transpose(a.reshape(LANE, bsz, nc, KC), (1, 2, 0, 3))

    kp, vp, ikp, cp, ksm, vsm, iks, cs = ([] for _ in range(8))
    yp = ys = None
    for l in range(depth):
        w = _prep_w_in(w_in[l])
        wc = w_out[l, :C_CONV].astype(MXU_DTYPE)
        wa = w_out[l, C_CONV:][_PERM].astype(MXU_DTYPE)
        last = l == depth - 1

        u, gc, kv, ga, tail, qzt, iqzt, vt, iwt = _project(xp, norm_g[l], w, tm, feature_major=True)
        b3 = lambda a: a.reshape(bsz, lq, a.shape[-1])
        u3, kv3, tail3 = b3(u), b3(kv), b3(tail)
        mixc = _conv_branch(u3, b3(gc), conv_w[l], conv_b[l], conv_ln_g[l], conv_ln_b[l])
        lane = jnp.arange(LANE)
        ik = key_chunks(jnp.where(lane < IDX_DIM, tail3, 0.0))
        mixa = _attend(iqzt.reshape(N_IDX_HEADS, LANE, bsz * lq), iwt, qzt.reshape(N_HEADS, LANE, bsz * lq), b3(ga),
                       ik, key_chunks(kv3[..., :feat]), key_chunks_t(vt), toep_t, kk_p)
        res = _out_proj(xp, mixc.reshape(bsz * lq, C_CONV), mixa.reshape(bsz * lq, -1), wc, wa, tm,
                        final_norm_g if last else None)
        xp, yp = res if last else (res, None)
        kp.append(kv3[:, :lp, :feat].reshape(bsz, lp, N_KV_HEADS, HEAD_DIM))
        vp.append(kv3[:, :lp, feat:].reshape(bsz, lp, N_KV_HEADS, HEAD_DIM))
        ikp.append(tail3[:, :lp, :IDX_DIM])
        cp.append(u3[:, lp - (CONV_W - 1):lp])

        u, gc, kv, ga, tail, qz, iqz = _project(xs, norm_g[l], w, db, feature_major=False)
        mixc = _sample_conv(jnp.transpose(state_conv[l], (1, 0, 2)), u, gc,
                            conv_w[l], conv_b[l], conv_ln_g[l], conv_ln_b[l])
        iw3 = tail[:, IDX_DIM:IDX_DIM + N_IDX_HEADS].reshape(db, N_IDX_HEADS, 1)
        scores = _dec_scores(page_table, iqz.reshape(db, N_IDX_HEADS, LANE), iw3, tail.reshape(db, 1, LANE),
                             cache_ik4, l, width_s)
        madd = _dec_select(scores.reshape(db, width_s).T, kk_s).T
        mixa = _dec_attend(page_table, qz.reshape(db, N_HEADS, LANE), kv.reshape(db, 1, 2 * feat),
                           madd.reshape(db, 1, width_s), bias_s, ga.reshape(db, GROUP, LANE),
                           cache_k4, cache_v4, l)
        res = _out_proj(xs, mixc, mixa.reshape(db, -1), wc, wa, db, final_norm_g if last else None)
        xs, ys = res if last else (res, None)
        ksm.append(kv[:, :feat].reshape(db, 1, N_KV_HEADS, HEAD_DIM))
        vsm.append(kv[:, feat:].reshape(db, 1, N_KV_HEADS, HEAD_DIM))
        iks.append(tail[:, None, :IDX_DIM])
        cs.append(jnp.concatenate([state_conv[l][:, 1:], u[:, None]], axis=1))

    y_prompt = yp.reshape(bsz, lq, d)[:, N_META:lp]
    y_sample = ys.reshape(db, 1, d)
    return (y_prompt, y_sample, jnp.stack(kp), jnp.stack(vp), jnp.stack(ikp), jnp.stack(cp),
            jnp.stack(ksm), jnp.stack(vsm), jnp.stack(iks), jnp.stack(cs))
```

```python
import functools
import math

import numpy as np
import jax
import jax.numpy as jnp
from jax import lax
from jax.experimental import pallas as pl
from jax.experimental.pallas import tpu as pltpu

N_HEADS = 8
N_KV_HEADS = 2
GROUP = N_HEADS // N_KV_HEADS
HEAD_DIM = 64
N_IDX_HEADS = 4
IDX_DIM = 64
C_CONV = 512
CONV_W = 31
TOPK_MAX = 256
N_META = 16
NUM_BUCKETS = 32
MAX_DISTANCE = 128
EPS = 1e-6
LN_EPS = 1e-5

LANE = 128
ROWS = 8
TQ = 128
KC = 512
SUB = KC // TQ
N_ACC = 4
DIGIT_BITS = 7
DIGIT_MASK = (1 << DIGIT_BITS) - 1
KEY_DIGITS = (7, 7, 7, 7, 4)
KEY_SHIFTS = tuple(32 - sum(KEY_DIGITS[:i + 1]) for i in range(len(KEY_DIGITS)))
N_LEVELS = len(KEY_DIGITS)
HALO = 32
CONV_SUB = 64
MXU_DTYPE = jnp.bfloat16
NEG = -1e30
LOG2E = 1.4426950408889634
INT_MIN = -2 ** 31
TIE_SPAN = 1 << 14
KEY_NEG_INF = -0x7F800000 - TIE_SPAN
VMEM_LIMIT = 56 * 1024 * 1024

_PROJ_SIZES = (C_CONV, C_CONV, C_CONV, N_HEADS * HEAD_DIM, N_KV_HEADS * HEAD_DIM, N_KV_HEADS * HEAD_DIM,
               N_HEADS * HEAD_DIM, N_IDX_HEADS * IDX_DIM, IDX_DIM, N_IDX_HEADS)
_UB_OFF, _GC_OFF, _Q_OFF, _K_OFF, _V_OFF, _GA_OFF, _IQ_OFF, _IK_OFF, _IW_OFF, _D_IN = (
    int(s) for s in np.cumsum(_PROJ_SIZES))
_W_ROWS = _IK_OFF + LANE


def _cparams(n_axes):
    return pltpu.CompilerParams(dimension_semantics=("arbitrary",) * n_axes, vmem_limit_bytes=VMEM_LIMIT)


def _silu(x):
    return x * jax.nn.sigmoid(x)


def _prep_w_in(w):
    assert w.shape[1] == _D_IN
    return jnp.pad(w.T, ((0, _W_ROWS - _D_IN), (0, 0))).astype(MXU_DTYPE)


def _t5_bucket(rel):
    n = jnp.maximum(rel, 0)
    max_exact = NUM_BUCKETS // 2
    large = max_exact + (jnp.log(jnp.maximum(n, 1).astype(jnp.float32) / max_exact)
                         / math.log(MAX_DISTANCE / max_exact)
                         * (NUM_BUCKETS - max_exact)).astype(jnp.int32)
    large = jnp.minimum(large, NUM_BUCKETS - 1)
    return jnp.where(n < max_exact, n, large)


def _bias_tiles(rel_bias):
    far = rel_bias[NUM_BUCKETS - 1]
    d = jnp.arange(2 * TQ, dtype=jnp.int32)
    dists = (jnp.where(d < TQ, d, 0),
             jnp.where(d < TQ, d + TQ, d - TQ))
    tiles = []
    for dist in dists:
        v = ((rel_bias[_t5_bucket(dist)] - far) * LOG2E).T
        rep = jnp.tile(v, (1, TQ))[:, :TQ * (2 * TQ - 1)].reshape(N_HEADS, TQ, 2 * TQ - 1)
        tiles.append(rep[:, :, :TQ])
    tiles.append(jnp.zeros_like(tiles[0]))
    return jnp.stack(tiles).astype(jnp.float32)


_NT = (((1,), (1,)), ((), ()))
_Q_SCALE = HEAD_DIM ** -0.5 * LOG2E
_IQ_SCALE = IDX_DIM ** -0.5
_IW_SCALE = N_IDX_HEADS ** -0.5
_IW_ROWS = 16


def _proj_kernel(x_ref, g_ref, w_ref, u_ref, gc_ref, kv_ref, ga_ref, tail_ref, *q_refs, feature_major):
    x = x_ref[...]
    ms = jnp.mean(x * x, axis=-1, keepdims=True)
    xn = (x * lax.rsqrt(ms + EPS) * g_ref[...]).astype(MXU_DTYPE)

    def mm(lo, hi):
        return lax.dot_general(xn, w_ref[lo:hi, :], _NT, preferred_element_type=jnp.float32)

    def mm_t(lo, hi):
        return lax.dot_general(w_ref[lo:hi, :], xn, _NT, preferred_element_type=jnp.float32)

    u_ref[...] = mm(0, _UB_OFF) * jax.nn.sigmoid(mm(_UB_OFF, _GC_OFF))
    gc_ref[...] = mm(_GC_OFF, _Q_OFF)
    kv_ref[...] = mm(_K_OFF, _GA_OFF)
    ga_ref[...] = mm(_GA_OFF, _IQ_OFF)
    t = mm(_IK_OFF, _W_ROWS)
    lane = lax.broadcasted_iota(jnp.int32, t.shape, 1)
    tail_ref[...] = t * jnp.where(lane >= IDX_DIM, _IW_SCALE, 1.0)
    if feature_major:
        qzt_ref, iqzt_ref, vt_ref, iwt_ref = q_refs
        qt = (mm_t(_Q_OFF, _K_OFF) * _Q_SCALE).astype(qzt_ref.dtype)
        qzt_ref[...] = jnp.zeros(qzt_ref.shape, qzt_ref.dtype)
        for h in range(N_HEADS):
            lo = h * LANE + (h // GROUP) * HEAD_DIM
            qzt_ref[lo:lo + HEAD_DIM, :] = qt[h * HEAD_DIM:(h + 1) * HEAD_DIM]
        iqt = (mm_t(_IQ_OFF, _IK_OFF) * _IQ_SCALE).astype(iqzt_ref.dtype)
        iqzt_ref[...] = jnp.zeros(iqzt_ref.shape, iqzt_ref.dtype)
        for h in range(N_IDX_HEADS):
            iqzt_ref[h * LANE:h * LANE + IDX_DIM, :] = iqt[h * IDX_DIM:(h + 1) * IDX_DIM]
        vt_ref[...] = mm_t(_V_OFF, _GA_OFF).astype(vt_ref.dtype)
        iwt_ref[...] = mm_t(_IW_OFF, _IW_OFF + _IW_ROWS) * _IW_SCALE
    else:
        q_ref, iq_ref = q_refs
        q_ref[...] = (mm(_Q_OFF, _K_OFF) * _Q_SCALE).astype(q_ref.dtype)
        iq_ref[...] = (mm(_IQ_OFF, _IK_OFF) * _IQ_SCALE).astype(iq_ref.dtype)


def _project(x, g, w, tm, feature_major):
    r, d = x.shape
    assert r % tm == 0
    rows = lambda c: pl.BlockSpec((tm, c), lambda i: (i, 0))
    cols = lambda c: pl.BlockSpec((c, tm), lambda i: (0, i))
    feat = N_KV_HEADS * HEAD_DIM
    specs = [rows(C_CONV), rows(C_CONV), rows(2 * feat), rows(N_HEADS * HEAD_DIM), rows(LANE)]
    shapes = [jax.ShapeDtypeStruct((r, s.block_shape[1]), jnp.float32) for s in specs]
    if feature_major:
        extra = [(N_HEADS * LANE, MXU_DTYPE), (N_IDX_HEADS * LANE, MXU_DTYPE), (feat, MXU_DTYPE),
                 (_IW_ROWS, jnp.float32)]
        specs += [cols(c) for c, _ in extra]
        shapes += [jax.ShapeDtypeStruct((c, r), t) for c, t in extra]
    else:
        extra = [(N_HEADS * HEAD_DIM, MXU_DTYPE), (N_IDX_HEADS * IDX_DIM, MXU_DTYPE)]
        specs += [rows(c) for c, _ in extra]
        shapes += [jax.ShapeDtypeStruct((r, c), t) for c, t in extra]
    return pl.pallas_call(
        functools.partial(_proj_kernel, feature_major=feature_major),
        grid=(r // tm,),
        in_specs=[rows(d), pl.BlockSpec((1, d), lambda i: (0, 0)), pl.BlockSpec((_W_ROWS, d), lambda i: (0, 0))],
        out_specs=specs,
        out_shape=shapes,
        compiler_params=_cparams(1),
        name="proj",
    )(x, g.reshape(1, d), w)


def _conv_kernel(prev_ref, cur_ref, gc_ref, w_ref, b_ref, lg_ref, lb_ref, o_ref, ext_ref):
    i = pl.program_id(1)
    ext_ref[0, 0:HALO, :] = jnp.where(i > 0, prev_ref[...], 0.0)
    ext_ref[0, HALO:HALO + TQ, :] = cur_ref[...]
    n_ext = HALO + TQ
    for s in range(1, ROWS):
        ext_ref[s, 0:n_ext - ROWS, :] = ext_ref[0, s:s + n_ext - ROWS, :]
    off = HALO - (CONV_W - 1)
    for r0 in range(0, TQ, CONV_SUB):
        acc = jnp.zeros((CONV_SUB, C_CONV), jnp.float32)
        for j in range(CONV_W):
            s = (off + j) % ROWS
            lo = r0 + off + j - s
            acc = acc + w_ref[j:j + 1, :] * ext_ref[s, lo:lo + CONV_SUB, :]
        y = acc + b_ref[...]
        mu = jnp.mean(y, axis=-1, keepdims=True)
        dev = y - mu
        var = jnp.mean(dev * dev, axis=-1, keepdims=True)
        yn = dev * lax.rsqrt(var + LN_EPS) * lg_ref[...] + lb_ref[...]
        o_ref[r0:r0 + CONV_SUB, :] = (_silu(yn) * _silu(gc_ref[r0:r0 + CONV_SUB, :])).astype(o_ref.dtype)


def _conv_branch(u, gc, w, b, lg, lb):
    bsz, lq, c = u.shape
    per = TQ // HALO
    row = lambda a: a.reshape(1, c)
    vec = pl.BlockSpec((1, c), lambda bi, i: (0, 0))
    tile = pl.BlockSpec((None, TQ, c), lambda bi, i: (bi, i, 0))
    return pl.pallas_call(
        _conv_kernel,
        grid=(bsz, lq // TQ),
        in_specs=[pl.BlockSpec((None, HALO, c), lambda bi, i: (bi, jnp.maximum(i * per - 1, 0), 0)),
                  tile, tile, pl.BlockSpec((CONV_W, c), lambda bi, i: (0, 0)), vec, vec, vec],
        out_specs=tile,
        out_shape=jax.ShapeDtypeStruct((bsz, lq, c), MXU_DTYPE),
        scratch_shapes=[pltpu.VMEM((ROWS, HALO + TQ, c), jnp.float32)],
        compiler_params=_cparams(2),
        name="conv",
    )(u, u, gc, w, row(b), row(lg), row(lb))


_INT_MIN = np.int32(INT_MIN)
_BYTE_ONES = np.int32(0x01010101)
_BYTE_LOW = np.int32(0x7F7F7F7F)
_GUARD = np.int32(0x80808080 - (1 << 32))
assert SUB == 4 and DIGIT_BITS == 7


def _to_key(s, idx):
    bits = lax.bitcast_convert_type(s, jnp.int32)
    key = jnp.where(bits < 0, (_INT_MIN - bits) - TIE_SPAN, bits)
    return jnp.where((bits == 0) | (bits == _INT_MIN), -1 - idx, key)


def _for_chunks(nkc, body, init, unroll):
    n_main = nkc // unroll

    def main(i, carry):
        for k in range(unroll):
            carry = body(i * unroll + k, carry)
        return carry

    carry = lax.fori_loop(0, n_main, main, init)
    return lax.fori_loop(n_main * unroll, nkc, body, carry)


def _pack_digits(keys_ref, dig_ref, c):
    for lv in range(N_LEVELS):
        word = None
        for j in range(SUB):
            u = keys_ref[c, j * TQ:(j + 1) * TQ, :]
            if lv == 0:
                u = u ^ _INT_MIN
            move = KEY_SHIFTS[lv] - 8 * j
            u = lax.shift_right_logical(u, move) if move >= 0 else lax.shift_left(u, -move)
            field = u & np.int32(((1 << KEY_DIGITS[lv]) - 1) << (8 * j))
            word = field if word is None else word | field
        dig_ref[lv, c] = word


def _count_fields(work_ref, nkc, cand_bytes):
    q = work_ref.shape[2]

    def body(c, accs):
        accs = list(accs)
        for n, r in enumerate(range(0, TQ, ROWS)):
            diff = work_ref[c, r:r + ROWS, :] - cand_bytes
            accs[n % N_ACC] = accs[n % N_ACC] + (lax.shift_right_logical(diff, DIGIT_BITS) & _BYTE_ONES)
        return tuple(accs)

    accs = _for_chunks(nkc, body, (jnp.zeros((ROWS, q), jnp.int32),) * N_ACC, unroll=4)
    total = jnp.zeros((ROWS, q), jnp.int32)
    for a in accs:
        total = total + ((a & 0xFF) + (lax.shift_right_logical(a, 8) & 0xFF)
                         + (lax.shift_right_logical(a, 16) & 0xFF) + lax.shift_right_logical(a, 24))
    return jnp.sum(total.astype(jnp.float32), axis=0, keepdims=True)


def _digit_search(dig_ref, work_ref, alive_ref, nkc, shifts, digits, target):
    q = dig_ref.shape[3]
    assert dig_ref.shape[1] * (TQ // ROWS) <= 255 * N_ACC
    zero = jnp.zeros((1, q), jnp.float32)
    value = jnp.zeros((1, q), jnp.int32)
    above, done, cnt_ge, cnt_gt = zero, zero, target + 1.0, zero

    for lv, (shift, n_bits) in enumerate(zip(shifts, digits)):
        def load_fields(c, carry):
            work_ref[c] = (dig_ref[lv, c] & alive_ref[c]) | _GUARD
            return carry

        _for_chunks(nkc, load_fields, 0, unroll=4)

        def digit_bit(i, state):
            digit, cnt_ge, cnt_gt, cnt_rej, done = state
            cand = digit | lax.shift_left(jnp.int32(1), n_bits - 1 - i)
            cnt_alive = _count_fields(work_ref, nkc, cand * _BYTE_ONES)
            cnt = above + cnt_alive
            take = (cnt >= target) & (done == 0.0)
            drop = (cnt < target) & (done == 0.0)
            return (jnp.where(take, cand, digit), jnp.where(take, cnt, cnt_ge), jnp.where(drop, cnt, cnt_gt),
                    jnp.where(drop, cnt_alive, cnt_rej),
                    jnp.where(take & (cnt == target), 1.0, done))

        digit, cnt_ge, cnt_gt, cnt_rej, done = lax.fori_loop(
            0, n_bits, digit_bit, (jnp.zeros((1, q), jnp.int32), cnt_ge, cnt_gt, zero, done))
        above = above + cnt_rej
        value = value | lax.shift_left(digit, shift)

        if lv + 1 < len(shifts):
            digit_bytes = digit * _BYTE_ONES

            def narrow(c, carry):
                differs = lax.shift_right_logical(((dig_ref[lv, c] ^ digit_bytes) + _BYTE_LOW) & _GUARD, DIGIT_BITS)
                alive_ref[c] = alive_ref[c] & ((_BYTE_ONES - differs) * DIGIT_MASK)
                return carry

            _for_chunks(nkc, narrow, 0, unroll=4)

    return value, cnt_ge, cnt_gt, done


def _fill(ref, nkc, word):
    def body(c, carry):
        ref[c] = jnp.full(ref.shape[1:], word, ref.dtype)
        return carry

    _for_chunks(nkc, body, 0, unroll=4)


def _threshold(keys_ref, dig_ref, work_ref, alive_ref, nkc, kk):
    q = keys_ref.shape[2]
    kf = jnp.full((1, q), kk, jnp.float32)
    _fill(alive_ref, nkc, _BYTE_LOW)
    prefix, cnt_thr, cnt_gt, _ = _digit_search(dig_ref, work_ref, alive_ref, nkc, KEY_SHIFTS, KEY_DIGITS, kf)
    thr = prefix ^ _INT_MIN
    admissible = thr != KEY_NEG_INF
    surplus = (cnt_thr > kf) & admissible

    @pl.when(jnp.max(jnp.where(surplus, 1.0, 0.0)) > 0.0)
    def _():
        assert keys_ref.shape[0] * KC <= 1 << (2 * DIGIT_BITS)
        top = (1 << (2 * DIGIT_BITS)) - 1
        row = lax.broadcasted_iota(jnp.int32, (TQ, q), 0)

        def tie_fields(c, carry):
            hi = lo = tie = None
            for j in range(SUB):
                is_tie = keys_ref[c, j * TQ:(j + 1) * TQ, :] == thr
                rev = top - (c * KC + j * TQ + row)
                fields = [jnp.where(is_tie, f, 0) for f in
                          (lax.shift_right_logical(rev, DIGIT_BITS), rev & DIGIT_MASK, DIGIT_MASK)]
                fields = [f if j == 0 else lax.shift_left(f, 8 * j) for f in fields]
                hi, lo, tie = fields if j == 0 else (hi | fields[0], lo | fields[1], tie | fields[2])
            dig_ref[0, c], dig_ref[1, c], alive_ref[c] = hi, lo, tie
            return carry

        lax.fori_loop(0, nkc, tie_fields, 0)
        cut, _, _, _ = _digit_search(dig_ref, work_ref, alive_ref, nkc, (DIGIT_BITS, 0), (DIGIT_BITS, DIGIT_BITS),
                                     kf - cnt_gt)

        def drop_losers(c, carry):
            for j in range(SUB):
                rows = slice(j * TQ, (j + 1) * TQ)
                x = keys_ref[c, rows, :]
                lost = (x == thr) & (top - (c * KC + j * TQ + row) < cut)
                keys_ref[c, rows, :] = jnp.where(lost, thr - 1, x)
            return carry

        lax.fori_loop(0, nkc, drop_losers, 0)

    return jnp.where(admissible, thr, KEY_NEG_INF + 1)


def _attend_kernel(iqt_ref, iwt_ref, qzt_ref, ga_ref, ik_ref, k_ref, vt_ref, toep_ref, o_ref,
                   keys_ref, dig_ref, work_ref, alive_ref, m_ref, l_ref, acc_ref, lg0_ref, lg1_ref,
                   wq_ref, *, kk):
    qi = pl.program_id(1)
    nkc = qi // SUB + 1
    qpos = qi * TQ + lax.broadcasted_iota(jnp.int32, (TQ, TQ), 1)
    krow = lax.broadcasted_iota(jnp.int32, (TQ, TQ), 0)
    for h in range(N_HEADS):
        wq_ref[:, h * TQ:(h + 1) * TQ] = qzt_ref[h]

    def score_chunk(c, carry):
        iqt = jnp.concatenate([iqt_ref[h] for h in range(N_IDX_HEADS)], axis=1)
        w = iwt_ref[...]
        for j in range(SUB):
            st = jnp.dot(ik_ref[c, j * TQ:(j + 1) * TQ, :], iqt, preferred_element_type=jnp.float32)
            s = jnp.zeros((TQ, TQ), jnp.float32)
            for h in range(N_IDX_HEADS):
                s = s + w[h:h + 1, :] * jnp.maximum(st[:, h * TQ:(h + 1) * TQ], 0.0)
            kpos = c * KC + j * TQ + krow
            keys_ref[c, j * TQ:(j + 1) * TQ, :] = _to_key(jnp.where(kpos <= qpos, s, -jnp.inf), kpos)
        _pack_digits(keys_ref, dig_ref, c)
        return carry

    _for_chunks(nkc, score_chunk, 0, unroll=2)
    thr = _threshold(keys_ref, dig_ref, work_ref, alive_ref, nkc, kk)

    m_ref[...] = jnp.full(m_ref.shape, NEG, jnp.float32)
    l_ref[...] = jnp.zeros(l_ref.shape, jnp.float32)
    acc_ref[...] = jnp.zeros(acc_ref.shape, jnp.float32)

    def logits(c, lg_ref):
        lg_ref[...] = jnp.dot(k_ref[c], wq_ref[...], preferred_element_type=jnp.float32)

    def softmax_pv(c, lg_ref, near):
        md = jnp.where(keys_ref[c] >= thr, 0.0, NEG)
        md2 = jnp.concatenate([md, md], axis=1)
        if near:
            delta = [jnp.clip(qi - (c * SUB + j), 0, 2) for j in range(SUB)]
        for pr in range(N_HEADS // 2):
            kv = pr // (GROUP // 2)
            lg = lg_ref[:, 2 * pr * TQ:(2 * pr + 2) * TQ] + md2
            if near:
                lg = lg + jnp.concatenate(
                    [jnp.concatenate([toep_ref[delta[j], 2 * pr], toep_ref[delta[j], 2 * pr + 1]], axis=1)
                     for j in range(SUB)], axis=0)
            m_old = m_ref[pr]
            m_new = jnp.maximum(m_old, jnp.max(lg, axis=0, keepdims=True))
            alpha = jnp.exp2(m_old - m_new)
            p = jnp.exp2(lg - m_new)
            l_ref[pr] = alpha * l_ref[pr] + jnp.sum(p, axis=0, keepdims=True)
            pv = jnp.dot(vt_ref[c, kv * HEAD_DIM:(kv + 1) * HEAD_DIM, :], p.astype(MXU_DTYPE),
                         preferred_element_type=jnp.float32)
            acc_ref[pr] = alpha * acc_ref[pr] + pv
            m_ref[pr] = m_new

    n_far2 = (jnp.maximum(qi - 1, 0) // SUB) // 2

    def far_pair(i, carry):
        logits(2 * i + 1, lg1_ref)
        softmax_pv(2 * i, lg0_ref, near=False)
        logits(2 * i + 2, lg0_ref)
        softmax_pv(2 * i + 1, lg1_ref, near=False)
        return carry

    logits(0, lg0_ref)
    lax.fori_loop(0, n_far2, far_pair, 0)

    c0 = 2 * n_far2
    c_last = k_ref.shape[0] - 1
    logits(jnp.minimum(c0 + 1, c_last), lg1_ref)
    softmax_pv(c0, lg0_ref, near=True)

    @pl.when(c0 + 1 < nkc)
    def _():
        logits(jnp.minimum(c0 + 2, c_last), lg0_ref)
        softmax_pv(c0 + 1, lg1_ref, near=True)

    @pl.when(c0 + 2 < nkc)
    def _():
        softmax_pv(c0 + 2, lg0_ref, near=True)

    for pr in range(N_HEADS // 2):
        o_t = acc_ref[pr] / l_ref[pr]
        pair_t = jnp.concatenate([o_t[:, :TQ], o_t[:, TQ:]], axis=0)
        sl = slice(pr * LANE, (pr + 1) * LANE)
        o_ref[:, sl] = (pair_t.T * _silu(ga_ref[:, sl])).astype(o_ref.dtype)


def _attend(iqt, iwt, qzt, ga, ik, k, vt, toep_t, kk):
    bsz, lq = ga.shape[:2]
    nq = lq // TQ
    nc = k.shape[1]
    lanes = lambda a: pl.BlockSpec(a.shape[:-1] + (TQ,), lambda b, i: (0,) * (a.ndim - 1) + (b * nq + i,))
    whole = lambda a: pl.BlockSpec((None,) + a.shape[1:], lambda b, i: (b,) + (0,) * (a.ndim - 1))
    rows = pl.BlockSpec((None, TQ, N_HEADS * HEAD_DIM), lambda b, i: (b, i, 0))
    return pl.pallas_call(
        functools.partial(_attend_kernel, kk=kk),
        grid=(bsz, nq),
        in_specs=[lanes(iqt), lanes(iwt), lanes(qzt), rows, whole(ik), whole(k), whole(vt),
                  pl.BlockSpec(toep_t.shape, lambda b, i: (0, 0, 0, 0))],
        out_specs=rows,
        out_shape=jax.ShapeDtypeStruct((bsz, lq, N_HEADS * HEAD_DIM), MXU_DTYPE),
        scratch_shapes=[pltpu.VMEM((nc, KC, TQ), jnp.int32),
                        pltpu.VMEM((N_LEVELS, nc, TQ, TQ), jnp.int32), pltpu.VMEM((nc, TQ, TQ), jnp.int32),
                        pltpu.VMEM((nc, TQ, TQ), jnp.int32),
                        pltpu.VMEM((N_HEADS // 2, 1, 2 * TQ), jnp.float32),
                        pltpu.VMEM((N_HEADS // 2, 1, 2 * TQ), jnp.float32),
                        pltpu.VMEM((N_HEADS // 2, HEAD_DIM, 2 * TQ), jnp.float32),
                        pltpu.VMEM((KC, N_HEADS * TQ), jnp.float32), pltpu.VMEM((KC, N_HEADS * TQ), jnp.float32),
                        pltpu.VMEM((LANE, N_HEADS * TQ), MXU_DTYPE)],
        compiler_params=_cparams(2),
        name="attend",
    )(iqt, iwt, qzt, ga, ik, k, vt, toep_t)


def _out_kernel(x_ref, mc_ref, ma_ref, wc_ref, wa_ref, o_ref):
    o_ref[...] = (x_ref[...]
                  + jnp.dot(mc_ref[...], wc_ref[...], preferred_element_type=jnp.float32)
                  + jnp.dot(ma_ref[...], wa_ref[...], preferred_element_type=jnp.float32))


def _out_final_kernel(x_ref, mc_ref, ma_ref, wc_ref, wa_ref, g_ref, o_ref, y_ref):
    x = (x_ref[...]
         + jnp.dot(mc_ref[...], wc_ref[...], preferred_element_type=jnp.float32)
         + jnp.dot(ma_ref[...], wa_ref[...], preferred_element_type=jnp.float32))
    o_ref[...] = x
    ms = jnp.mean(x * x, axis=-1, keepdims=True)
    y_ref[...] = x * lax.rsqrt(ms + EPS) * g_ref[...]


def _out_proj(x, mc, ma, wc, wa, tm, final_g=None):
    r, d = x.shape
    c = mc.shape[1]
    rows = lambda w: pl.BlockSpec((tm, w), lambda i: (i, 0))
    full = lambda a: pl.BlockSpec(a.shape, lambda i: (0, 0))
    if final_g is None:
        return pl.pallas_call(
            _out_kernel, grid=(r // tm,),
            in_specs=[rows(d), rows(c), rows(c), full(wc), full(wa)],
            out_specs=rows(d), out_shape=jax.ShapeDtypeStruct((r, d), jnp.float32),
            compiler_params=_cparams(1), name="out",
        )(x, mc, ma, wc, wa)
    g = final_g.reshape(1, d)
    return pl.pallas_call(
        _out_final_kernel, grid=(r // tm,),
        in_specs=[rows(d), rows(c), rows(c), full(wc), full(wa), full(g)],
        out_specs=[rows(d), rows(d)], out_shape=[jax.ShapeDtypeStruct((r, d), jnp.float32)] * 2,
        compiler_params=_cparams(1), name="out_final",
    )(x, mc, ma, wc, wa, g)


def _sconv_kernel(st_ref, u_ref, gc_ref, w_ref, b_ref, lg_ref, lb_ref, o_ref):
    acc = jnp.zeros(u_ref.shape, jnp.float32)
    for j in range(CONV_W - 1):
        acc = acc + w_ref[j:j + 1, :] * st_ref[j]
    acc = acc + w_ref[CONV_W - 1:CONV_W, :] * u_ref[...]
    y = acc + b_ref[...]
    mu = jnp.mean(y, axis=-1, keepdims=True)
    dev = y - mu
    var = jnp.mean(dev * dev, axis=-1, keepdims=True)
    yn = dev * lax.rsqrt(var + LN_EPS) * lg_ref[...] + lb_ref[...]
    o_ref[...] = (_silu(yn) * _silu(gc_ref[...])).astype(o_ref.dtype)


def _sample_conv(state_t, u, gc, w, b, lg, lb):
    db, c = u.shape
    row = lambda a: a.reshape(1, c)
    full = lambda a: pl.BlockSpec(a.shape, lambda i: (0,) * a.ndim)
    args = (state_t, u, gc, w, row(b), row(lg), row(lb))
    return pl.pallas_call(
        _sconv_kernel, grid=(1,), in_specs=[full(a) for a in args],
        out_specs=pl.BlockSpec((db, c), lambda i: (0, 0)),
        out_shape=jax.ShapeDtypeStruct((db, c), MXU_DTYPE),
        compiler_params=_cparams(1), name="sconv",
    )(*args)


def _dec_score_kernel(pt_ref, iq_ref, iw_ref, ikn_ref, *refs, n_pages, width):
    del pt_ref
    pages, o_ref = refs[:n_pages], refs[n_pages]
    iq = iq_ref[...][:, :IDX_DIM]
    w = iw_ref[...]
    pieces = []
    for p in range(n_pages):
        s = jnp.dot(iq, pages[p][...].astype(MXU_DTYPE), preferred_element_type=jnp.float32)
        pieces.append(jnp.sum(w * jnp.maximum(s, 0.0), axis=0, keepdims=True))
    ikn = ikn_ref[...][:, :IDX_DIM].astype(MXU_DTYPE).astype(jnp.float32)
    sn = jnp.sum(iq.astype(jnp.float32) * ikn, axis=-1, keepdims=True)
    scn = jnp.sum(w * jnp.maximum(sn, 0.0), axis=0, keepdims=True)
    page = pieces[0].shape[1]
    lane = lax.broadcasted_iota(jnp.int32, (1, width - n_pages * page), 1)
    pieces.append(jnp.where(lane == 0, scn, -jnp.inf))
    o_ref[...] = jnp.concatenate(pieces, axis=1)


def _dec_scores(page_table, iq3, iw3, tail3, cache_ik, layer, width):
    db, n_pages = page_table.shape
    di, page = cache_ik.shape[2:]
    seq = lambda a: pl.BlockSpec((None,) + a.shape[1:], lambda b, pt: (b, 0, 0))
    page_spec = lambda p: pl.BlockSpec((None, None, di, page), lambda b, pt: (layer, pt[b, p], 0, 0))
    return pl.pallas_call(
        functools.partial(_dec_score_kernel, n_pages=n_pages, width=width),
        grid_spec=pltpu.PrefetchScalarGridSpec(
            num_scalar_prefetch=1, grid=(db,),
            in_specs=[seq(iq3), seq(iw3), seq(tail3)] + [page_spec(p) for p in range(n_pages)],
            out_specs=pl.BlockSpec((None, 1, width), lambda b, pt: (b, 0, 0))),
        out_shape=jax.ShapeDtypeStruct((db, 1, width), jnp.float32),
        compiler_params=_cparams(1), name="dec_score",
    )(page_table, iq3, iw3, tail3, *([cache_ik] * n_pages))


def _dec_select_kernel(s_ref, o_ref, keys_ref, dig_ref, work_ref, alive_ref, *, kk):
    nc = keys_ref.shape[0]
    row = lax.broadcasted_iota(jnp.int32, (KC, keys_ref.shape[2]), 0)
    for c in range(nc):
        keys_ref[c] = _to_key(s_ref[c * KC:(c + 1) * KC, :], c * KC + row)
        _pack_digits(keys_ref, dig_ref, c)
    thr = _threshold(keys_ref, dig_ref, work_ref, alive_ref, nc, kk)
    for c in range(nc):
        o_ref[c * KC:(c + 1) * KC, :] = jnp.where(keys_ref[c] >= thr, 0.0, NEG)


def _dec_select(scores_t, kk):
    width, db = scores_t.shape
    nc = width // KC
    spec = pl.BlockSpec((width, db), lambda i: (0, 0))
    return pl.pallas_call(
        functools.partial(_dec_select_kernel, kk=kk),
        grid=(1,), in_specs=[spec], out_specs=spec,
        out_shape=jax.ShapeDtypeStruct((width, db), jnp.float32),
        scratch_shapes=[pltpu.VMEM((nc, KC, db), jnp.int32),
                        pltpu.VMEM((N_LEVELS, nc, TQ, db), jnp.int32), pltpu.VMEM((nc, TQ, db), jnp.int32),
                        pltpu.VMEM((nc, TQ, db), jnp.int32)],
        compiler_params=_cparams(1), name="dec_select",
    )(scores_t)


def _dec_attend_kernel(pt_ref, qz_ref, kvn_ref, madd_ref, bias_ref, ga_ref, *refs, n_pages):
    del pt_ref
    kpages, vpages, o_ref = refs[:n_pages], refs[n_pages:2 * n_pages], refs[2 * n_pages]
    qz = qz_ref[...]
    page = kpages[0].shape[1]
    past = n_pages * page
    nt = (((1,), (1,)), ((), ()))
    logits = []
    for p in range(n_pages):
        sl = slice(p * page, (p + 1) * page)
        lg = jnp.dot(qz, kpages[p][...].astype(MXU_DTYPE), preferred_element_type=jnp.float32)
        logits.append(lg + bias_ref[:, sl] + madd_ref[:, sl])
    kvn = kvn_ref[...].astype(MXU_DTYPE).astype(jnp.float32)
    lgn = jnp.sum(qz.astype(jnp.float32) * kvn[:, :LANE], axis=-1, keepdims=True)
    lgn = lgn + bias_ref[:, past:past + 1] + madd_ref[:, past:past + 1]
    m = lgn
    for lg in logits:
        m = jnp.maximum(m, jnp.max(lg, axis=-1, keepdims=True))
    pn = jnp.exp2(lgn - m)
    den = pn
    o = pn.astype(MXU_DTYPE).astype(jnp.float32) * kvn[:, LANE:]
    for p in range(n_pages):
        pp = jnp.exp2(logits[p] - m)
        den = den + jnp.sum(pp, axis=-1, keepdims=True)
        o = o + lax.dot_general(pp.astype(MXU_DTYPE), vpages[p][...].astype(MXU_DTYPE), nt,
                                preferred_element_type=jnp.float32)
    o = o / den
    swapped = pltpu.roll(o, HEAD_DIM, axis=1)
    lane = lax.broadcasted_iota(jnp.int32, (1, LANE), 1)
    pairs = []
    for j in range(N_HEADS // 2):
        first, second = (o, swapped) if 2 * j < GROUP else (swapped, o)
        pairs.append(jnp.where(lane < HEAD_DIM, first[2 * j:2 * j + 1], second[2 * j + 1:2 * j + 2]))
    o_ref[...] = (jnp.concatenate(pairs, axis=0) * _silu(ga_ref[...])).astype(o_ref.dtype)


def _dec_attend(page_table, qz3, kvn3, madd3, bias, ga3, cache_k, cache_v, layer):
    db, n_pages = page_table.shape
    feat, page = cache_k.shape[2:]
    seq = lambda a: pl.BlockSpec((None,) + a.shape[1:], lambda b, pt: (b, 0, 0))
    page_spec = lambda p: pl.BlockSpec((None, None, feat, page), lambda b, pt: (layer, pt[b, p], 0, 0))
    return pl.pallas_call(
        functools.partial(_dec_attend_kernel, n_pages=n_pages),
        grid_spec=pltpu.PrefetchScalarGridSpec(
            num_scalar_prefetch=1, grid=(db,),
            in_specs=[seq(qz3), seq(kvn3), seq(madd3), pl.BlockSpec(bias.shape, lambda b, pt: (0, 0)), seq(ga3)]
                     + [page_spec(p) for p in range(n_pages)] * 2,
            out_specs=pl.BlockSpec((None, GROUP, LANE), lambda b, pt: (b, 0, 0))),
        out_shape=jax.ShapeDtypeStruct((db, GROUP, LANE), MXU_DTYPE),
        compiler_params=_cparams(1), name="dec_attend",
    )(page_table, qz3, kvn3, madd3, bias, ga3, *([cache_k] * n_pages), *([cache_v] * n_pages))


def _round_up(x, m):
    return -(-x // m) * m


def _row_tile(rows):
    for f in (5, 4, 3, 2, 1):
        if rows % (f * TQ) == 0:
            return f * TQ
    return rows


def kernel(x_prompt, x_sample, cache_k, cache_v, cache_idx_k, state_conv, page_table, meta_tokens, rel_bias,
           norm_g, w_in, conv_w, conv_b, conv_ln_g, conv_ln_b, w_out, final_norm_g):
    bsz, seq, d = x_prompt.shape
    depth = w_in.shape[0]
    lp = seq + N_META
    lq = _round_up(lp, TQ)
    lk = _round_up(lp, KC)
    nq, nc = lq // TQ, lk // KC
    kk_p = min(TOPK_MAX, lp // 4)
    db = x_sample.shape[0]
    n_pool, page = cache_k.shape[1:3]
    n_pages = page_table.shape[1]
    past = n_pages * page
    kk_s = min(TOPK_MAX, (past + 1) // 4)
    width_s = _round_up(past + 1, KC)
    feat = N_KV_HEADS * HEAD_DIM
    tm = _row_tile(bsz * lq)

    xp = jnp.concatenate([jnp.broadcast_to(meta_tokens[None].astype(x_prompt.dtype), (bsz, N_META, d)), x_prompt,
                          jnp.zeros((bsz, lq - lp, d), x_prompt.dtype)], axis=1).reshape(bsz * lq, d)
    xs = x_sample.reshape(db, d)
    toep_t = _bias_tiles(rel_bias)
    bias_s = jnp.moveaxis(rel_bias[_t5_bucket(past - jnp.arange(width_s, dtype=jnp.int32))], -1, 0) * LOG2E
    cache_k4 = jnp.transpose(cache_k, (0, 1, 3, 4, 2)).reshape(depth, n_pool, feat, page)
    cache_v4 = jnp.transpose(cache_v, (0, 1, 3, 4, 2)).reshape(depth, n_pool, feat, page)
    cache_ik4 = jnp.swapaxes(cache_idx_k, 2, 3)

    def key_chunks(a):
        return jnp.pad(a, ((0, 0), (0, lk - lq), (0, 0))).astype(MXU_DTYPE).reshape(bsz, nc, KC, LANE)

    def key_chunks_t(a):
        a = jnp.pad(a.reshape(LANE, bsz, lq), ((0, 0), (0, 0), (0, lk - lq)))
        return jnp.transpose(a.reshape(LANE, bsz, nc, KC), (1, 2, 0, 3))

    kp, vp, ikp, cp, ksm, vsm, iks, cs = ([] for _ in range(8))
    yp = ys = None
    for l in range(depth):
        w = _prep_w_in(w_in[l])
        wc = w_out[l, :C_CONV].astype(MXU_DTYPE)
        wa = w_out[l, C_CONV:].astype(MXU_DTYPE)
        last = l == depth - 1

        u, gc, kv, ga, tail, qzt, iqzt, vt, iwt = _project(xp, norm_g[l], w, tm, feature_major=True)
        b3 = lambda a: a.reshape(bsz, lq, a.shape[-1])
        u3, kv3, tail3 = b3(u), b3(kv), b3(tail)
        mixc = _conv_branch(u3, b3(gc), conv_w[l], conv_b[l], conv_ln_g[l], conv_ln_b[l])
        lane = jnp.arange(LANE)
        ik = key_chunks(jnp.where(lane < IDX_DIM, tail3, 0.0))
        mixa = _attend(iqzt.reshape(N_IDX_HEADS, LANE, bsz * lq), iwt, qzt.reshape(N_HEADS, LANE, bsz * lq), b3(ga),
                       ik, key_chunks(kv3[..., :feat]), key_chunks_t(vt), toep_t, kk_p)
        res = _out_proj(xp, mixc.reshape(bsz * lq, C_CONV), mixa.reshape(bsz * lq, -1), wc, wa, tm,
                        final_norm_g if last else None)
        xp, yp = res if last else (res, None)
        kp.append(kv3[:, :lp, :feat].reshape(bsz, lp, N_KV_HEADS, HEAD_DIM))
        vp.append(kv3[:, :lp, feat:].reshape(bsz, lp, N_KV_HEADS, HEAD_DIM))
        ikp.append(tail3[:, :lp, :IDX_DIM])
        cp.append(u3[:, lp - (CONV_W - 1):lp])

        u, gc, kv, ga, tail, q, iq = _project(xs, norm_g[l], w, db, feature_major=False)
        mixc = _sample_conv(jnp.transpose(state_conv[l], (1, 0, 2)), u, gc,
                            conv_w[l], conv_b[l], conv_ln_g[l], conv_ln_b[l])
        iw3 = tail[:, IDX_DIM:IDX_DIM + N_IDX_HEADS].reshape(db, N_IDX_HEADS, 1)
        scores = _dec_scores(page_table, iq.reshape(db, N_IDX_HEADS, IDX_DIM), iw3, tail.reshape(db, 1, LANE),
                             cache_ik4, l, width_s)
        madd = _dec_select(scores.reshape(db, width_s).T, kk_s).T
        q4 = q.reshape(db, N_KV_HEADS, GROUP, HEAD_DIM)
        qz3 = jnp.concatenate([jnp.pad(q4[:, 0], ((0, 0), (0, 0), (0, HEAD_DIM))),
                               jnp.pad(q4[:, 1], ((0, 0), (0, 0), (HEAD_DIM, 0)))], axis=1)
        mixa = _dec_attend(page_table, qz3, kv.reshape(db, 1, 2 * feat),
                           madd.reshape(db, 1, width_s), bias_s, ga.reshape(db, N_HEADS // 2, LANE),
                           cache_k4, cache_v4, l)
        res = _out_proj(xs, mixc, mixa.reshape(db, -1), wc, wa, db, final_norm_g if last else None)
        xs, ys = res if last else (res, None)
        ksm.append(kv[:, :feat].reshape(db, 1, N_KV_HEADS, HEAD_DIM))
        vsm.append(kv[:, feat:].reshape(db, 1, N_KV_HEADS, HEAD_DIM))
        iks.append(tail[:, None, :IDX_DIM])
        cs.append(jnp.concatenate([state_conv[l][:, 1:], u[:, None]], axis=1))

    y_prompt = yp.reshape(bsz, lq, d)[:, N_META:lp]
    y_sample = ys.reshape(db, 1, d)
    return (y_prompt, y_sample, jnp.stack(kp), jnp.stack(vp), jnp.stack(ikp), jnp.stack(cp),
            jnp.stack(ksm), jnp.stack(vsm), jnp.stack(iks), jnp.stack(cs))
```

```python
import functools
import math

import numpy as np
import jax
import jax.numpy as jnp
from jax import lax
from jax.experimental import pallas as pl
from jax.experimental.pallas import tpu as pltpu

N_HEADS = 8
N_KV_HEADS = 2
GROUP = N_HEADS // N_KV_HEADS
HEAD_DIM = 64
N_IDX_HEADS = 4
IDX_DIM = 64
C_CONV = 512
CONV_W = 31
TOPK_MAX = 256
N_META = 16
NUM_BUCKETS = 32
MAX_DISTANCE = 128
EPS = 1e-6
LN_EPS = 1e-5

LANE = 128
ROWS = 8
TQ = 128
KC = 512
SUB = KC // TQ
N_ACC = 4
DIGIT_BITS = 7
DIGIT_MASK = (1 << DIGIT_BITS) - 1
KEY_DIGITS = (7, 7, 7, 7, 4)
KEY_SHIFTS = tuple(32 - sum(KEY_DIGITS[:i + 1]) for i in range(len(KEY_DIGITS)))
N_LEVELS = len(KEY_DIGITS)
V_ROWS = HEAD_DIM + 16
HALO = 32
CONV_SUB = 64
MXU_DTYPE = jnp.bfloat16
NEG = -1e30
LOG2E = 1.4426950408889634
INT_MIN = -2 ** 31
TIE_SPAN = 1 << 14
KEY_NEG_INF = -0x7F800000 - TIE_SPAN
VMEM_LIMIT = 56 * 1024 * 1024

_PROJ_SIZES = (C_CONV, C_CONV, C_CONV, N_HEADS * HEAD_DIM, N_KV_HEADS * HEAD_DIM, N_KV_HEADS * HEAD_DIM,
               N_HEADS * HEAD_DIM, N_IDX_HEADS * IDX_DIM, IDX_DIM, N_IDX_HEADS)
_UB_OFF, _GC_OFF, _Q_OFF, _K_OFF, _V_OFF, _GA_OFF, _IQ_OFF, _IK_OFF, _IW_OFF, _D_IN = (
    int(s) for s in np.cumsum(_PROJ_SIZES))
_W_ROWS = _IK_OFF + LANE


def _cparams(n_axes):
    return pltpu.CompilerParams(dimension_semantics=("arbitrary",) * n_axes, vmem_limit_bytes=VMEM_LIMIT)


def _silu(x):
    return x * jax.nn.sigmoid(x)


def _prep_w_in(w):
    assert w.shape[1] == _D_IN
    return jnp.pad(w.T, ((0, _W_ROWS - _D_IN), (0, 0))).astype(MXU_DTYPE)


def _t5_bucket(rel):
    n = jnp.maximum(rel, 0)
    max_exact = NUM_BUCKETS // 2
    large = max_exact + (jnp.log(jnp.maximum(n, 1).astype(jnp.float32) / max_exact)
                         / math.log(MAX_DISTANCE / max_exact)
                         * (NUM_BUCKETS - max_exact)).astype(jnp.int32)
    large = jnp.minimum(large, NUM_BUCKETS - 1)
    return jnp.where(n < max_exact, n, large)


def _bias_tiles(rel_bias):
    far = rel_bias[NUM_BUCKETS - 1]
    d = jnp.arange(2 * TQ, dtype=jnp.int32)
    dists = (jnp.where(d < TQ, d, 0),
             jnp.where(d < TQ, d + TQ, d - TQ))
    tiles = []
    for dist in dists:
        v = ((rel_bias[_t5_bucket(dist)] - far) * LOG2E).T
        rep = jnp.tile(v, (1, TQ))[:, :TQ * (2 * TQ - 1)].reshape(N_HEADS, TQ, 2 * TQ - 1)
        tiles.append(rep[:, :, :TQ])
    tiles.append(jnp.zeros_like(tiles[0]))
    return jnp.stack(tiles).astype(jnp.float32)


_NT = (((1,), (1,)), ((), ()))
_Q_SCALE = HEAD_DIM ** -0.5 * LOG2E
_IQ_SCALE = IDX_DIM ** -0.5
_IW_SCALE = N_IDX_HEADS ** -0.5
_IW_ROWS = 16


def _proj_kernel(x_ref, g_ref, w_ref, u_ref, gc_ref, kv_ref, ga_ref, tail_ref, *q_refs, feature_major):
    x = x_ref[...]
    ms = jnp.mean(x * x, axis=-1, keepdims=True)
    xn = (x * lax.rsqrt(ms + EPS) * g_ref[...]).astype(MXU_DTYPE)

    def mm(lo, hi):
        return lax.dot_general(xn, w_ref[lo:hi, :], _NT, preferred_element_type=jnp.float32)

    def mm_t(lo, hi):
        return lax.dot_general(w_ref[lo:hi, :], xn, _NT, preferred_element_type=jnp.float32)

    u_ref[...] = mm(0, _UB_OFF) * jax.nn.sigmoid(mm(_UB_OFF, _GC_OFF))
    gc_ref[...] = mm(_GC_OFF, _Q_OFF)
    kv_ref[...] = mm(_K_OFF, _GA_OFF)
    ga_ref[...] = mm(_GA_OFF, _IQ_OFF)
    t = mm(_IK_OFF, _W_ROWS)
    lane = lax.broadcasted_iota(jnp.int32, t.shape, 1)
    tail_ref[...] = t * jnp.where(lane >= IDX_DIM, _IW_SCALE, 1.0)
    if feature_major:
        qzt_ref, iqzt_ref, vt_ref, iwt_ref = q_refs
        qt = (mm_t(_Q_OFF, _K_OFF) * _Q_SCALE).astype(qzt_ref.dtype)
        qzt_ref[...] = jnp.zeros(qzt_ref.shape, qzt_ref.dtype)
        for h in range(N_HEADS):
            lo = h * LANE + (h // GROUP) * HEAD_DIM
            qzt_ref[lo:lo + HEAD_DIM, :] = qt[h * HEAD_DIM:(h + 1) * HEAD_DIM]
        iqt = (mm_t(_IQ_OFF, _IK_OFF) * _IQ_SCALE).astype(iqzt_ref.dtype)
        iqzt_ref[...] = jnp.zeros(iqzt_ref.shape, iqzt_ref.dtype)
        for h in range(N_IDX_HEADS):
            iqzt_ref[h * LANE:h * LANE + IDX_DIM, :] = iqt[h * IDX_DIM:(h + 1) * IDX_DIM]
        vt_ref[...] = mm_t(_V_OFF, _GA_OFF).astype(vt_ref.dtype)
        iwt_ref[...] = mm_t(_IW_OFF, _IW_OFF + _IW_ROWS) * _IW_SCALE
    else:
        q_ref, iq_ref = q_refs
        q_ref[...] = (mm(_Q_OFF, _K_OFF) * _Q_SCALE).astype(q_ref.dtype)
        iq_ref[...] = (mm(_IQ_OFF, _IK_OFF) * _IQ_SCALE).astype(iq_ref.dtype)


def _project(x, g, w, tm, feature_major):
    r, d = x.shape
    assert r % tm == 0
    rows = lambda c: pl.BlockSpec((tm, c), lambda i: (i, 0))
    cols = lambda c: pl.BlockSpec((c, tm), lambda i: (0, i))
    feat = N_KV_HEADS * HEAD_DIM
    specs = [rows(C_CONV), rows(C_CONV), rows(2 * feat), rows(N_HEADS * HEAD_DIM), rows(LANE)]
    shapes = [jax.ShapeDtypeStruct((r, s.block_shape[1]), jnp.float32) for s in specs]
    if feature_major:
        extra = [(N_HEADS * LANE, MXU_DTYPE), (N_IDX_HEADS * LANE, MXU_DTYPE), (feat, MXU_DTYPE),
                 (_IW_ROWS, jnp.float32)]
        specs += [cols(c) for c, _ in extra]
        shapes += [jax.ShapeDtypeStruct((c, r), t) for c, t in extra]
    else:
        extra = [(N_HEADS * HEAD_DIM, MXU_DTYPE), (N_IDX_HEADS * IDX_DIM, MXU_DTYPE)]
        specs += [rows(c) for c, _ in extra]
        shapes += [jax.ShapeDtypeStruct((r, c), t) for c, t in extra]
    return pl.pallas_call(
        functools.partial(_proj_kernel, feature_major=feature_major),
        grid=(r // tm,),
        in_specs=[rows(d), pl.BlockSpec((1, d), lambda i: (0, 0)), pl.BlockSpec((_W_ROWS, d), lambda i: (0, 0))],
        out_specs=specs,
        out_shape=shapes,
        compiler_params=_cparams(1),
        name="proj",
    )(x, g.reshape(1, d), w)


def _conv_kernel(prev_ref, cur_ref, gc_ref, w_ref, b_ref, lg_ref, lb_ref, o_ref, ext_ref):
    i = pl.program_id(1)
    ext_ref[0, 0:HALO, :] = jnp.where(i > 0, prev_ref[...], 0.0)
    ext_ref[0, HALO:HALO + TQ, :] = cur_ref[...]
    n_ext = HALO + TQ
    for s in range(1, ROWS):
        ext_ref[s, 0:n_ext - ROWS, :] = ext_ref[0, s:s + n_ext - ROWS, :]
    off = HALO - (CONV_W - 1)
    for r0 in range(0, TQ, CONV_SUB):
        acc = jnp.zeros((CONV_SUB, C_CONV), jnp.float32)
        for j in range(CONV_W):
            s = (off + j) % ROWS
            lo = r0 + off + j - s
            acc = acc + w_ref[j:j + 1, :] * ext_ref[s, lo:lo + CONV_SUB, :]
        y = acc + b_ref[...]
        mu = jnp.mean(y, axis=-1, keepdims=True)
        dev = y - mu
        var = jnp.mean(dev * dev, axis=-1, keepdims=True)
        yn = dev * lax.rsqrt(var + LN_EPS) * lg_ref[...] + lb_ref[...]
        o_ref[r0:r0 + CONV_SUB, :] = (_silu(yn) * _silu(gc_ref[r0:r0 + CONV_SUB, :])).astype(o_ref.dtype)


def _conv_branch(u, gc, w, b, lg, lb):
    bsz, lq, c = u.shape
    per = TQ // HALO
    row = lambda a: a.reshape(1, c)
    vec = pl.BlockSpec((1, c), lambda bi, i: (0, 0))
    tile = pl.BlockSpec((None, TQ, c), lambda bi, i: (bi, i, 0))
    return pl.pallas_call(
        _conv_kernel,
        grid=(bsz, lq // TQ),
        in_specs=[pl.BlockSpec((None, HALO, c), lambda bi, i: (bi, jnp.maximum(i * per - 1, 0), 0)),
                  tile, tile, pl.BlockSpec((CONV_W, c), lambda bi, i: (0, 0)), vec, vec, vec],
        out_specs=tile,
        out_shape=jax.ShapeDtypeStruct((bsz, lq, c), MXU_DTYPE),
        scratch_shapes=[pltpu.VMEM((ROWS, HALO + TQ, c), jnp.float32)],
        compiler_params=_cparams(2),
        name="conv",
    )(u, u, gc, w, row(b), row(lg), row(lb))


_INT_MIN = np.int32(INT_MIN)
_BYTE_ONES = np.int32(0x01010101)
_BYTE_LOW = np.int32(0x7F7F7F7F)
_GUARD = np.int32(0x80808080 - (1 << 32))
assert SUB == 4 and DIGIT_BITS == 7


def _to_key(s, idx):
    bits = lax.bitcast_convert_type(s, jnp.int32)
    key = jnp.where(bits < 0, (_INT_MIN - bits) - TIE_SPAN, bits)
    return jnp.where((bits == 0) | (bits == _INT_MIN), -1 - idx, key)


def _for_chunks(nkc, body, init, unroll):
    n_main = nkc // unroll

    def main(i, carry):
        for k in range(unroll):
            carry = body(i * unroll + k, carry)
        return carry

    carry = lax.fori_loop(0, n_main, main, init)
    return lax.fori_loop(n_main * unroll, nkc, body, carry)


def _pack_digits(keys_ref, dig_ref, c):
    for lv in range(N_LEVELS):
        word = None
        for j in range(SUB):
            u = keys_ref[c, j * TQ:(j + 1) * TQ, :]
            if lv == 0:
                u = u ^ _INT_MIN
            move = KEY_SHIFTS[lv] - 8 * j
            u = lax.shift_right_logical(u, move) if move >= 0 else lax.shift_left(u, -move)
            field = u & np.int32(((1 << KEY_DIGITS[lv]) - 1) << (8 * j))
            word = field if word is None else word | field
        dig_ref[lv, c] = word


def _count_fields(work_ref, nkc, cand_bytes):
    q = work_ref.shape[2]

    def body(c, accs):
        accs = list(accs)
        for n, r in enumerate(range(0, TQ, ROWS)):
            diff = work_ref[c, r:r + ROWS, :] - cand_bytes
            accs[n % N_ACC] = accs[n % N_ACC] + (lax.shift_right_logical(diff, DIGIT_BITS) & _BYTE_ONES)
        return tuple(accs)

    accs = _for_chunks(nkc, body, (jnp.zeros((ROWS, q), jnp.int32),) * N_ACC, unroll=4)
    total = jnp.zeros((ROWS, q), jnp.int32)
    for a in accs:
        total = total + ((a & 0xFF) + (lax.shift_right_logical(a, 8) & 0xFF)
                         + (lax.shift_right_logical(a, 16) & 0xFF) + lax.shift_right_logical(a, 24))
    return jnp.sum(total.astype(jnp.float32), axis=0, keepdims=True)


def _digit_search(dig_ref, work_ref, alive_ref, nkc, shifts, digits, target):
    q = dig_ref.shape[3]
    assert dig_ref.shape[1] * (TQ // ROWS) <= 255 * N_ACC
    zero = jnp.zeros((1, q), jnp.float32)
    value = jnp.zeros((1, q), jnp.int32)
    above, done, cnt_ge, cnt_gt = zero, zero, target + 1.0, zero

    for lv, (shift, n_bits) in enumerate(zip(shifts, digits)):
        def load_fields(c, carry):
            work_ref[c] = (dig_ref[lv, c] & alive_ref[c]) | _GUARD
            return carry

        _for_chunks(nkc, load_fields, 0, unroll=4)

        def digit_bit(i, state):
            digit, cnt_ge, cnt_gt, cnt_rej, done = state
            cand = digit | lax.shift_left(jnp.int32(1), n_bits - 1 - i)
            cnt_alive = _count_fields(work_ref, nkc, cand * _BYTE_ONES)
            cnt = above + cnt_alive
            take = (cnt >= target) & (done == 0.0)
            drop = (cnt < target) & (done == 0.0)
            return (jnp.where(take, cand, digit), jnp.where(take, cnt, cnt_ge), jnp.where(drop, cnt, cnt_gt),
                    jnp.where(drop, cnt_alive, cnt_rej),
                    jnp.where(take & (cnt == target), 1.0, done))

        digit, cnt_ge, cnt_gt, cnt_rej, done = lax.fori_loop(
            0, n_bits, digit_bit, (jnp.zeros((1, q), jnp.int32), cnt_ge, cnt_gt, zero, done))
        above = above + cnt_rej
        value = value | lax.shift_left(digit, shift)

        if lv + 1 < len(shifts):
            digit_bytes = digit * _BYTE_ONES

            def narrow(c, carry):
                differs = lax.shift_right_logical(((dig_ref[lv, c] ^ digit_bytes) + _BYTE_LOW) & _GUARD, DIGIT_BITS)
                alive_ref[c] = alive_ref[c] & ((_BYTE_ONES - differs) * DIGIT_MASK)
                return carry

            _for_chunks(nkc, narrow, 0, unroll=4)

    return value, cnt_ge, cnt_gt, done


def _fill(ref, nkc, word):
    def body(c, carry):
        ref[c] = jnp.full(ref.shape[1:], word, ref.dtype)
        return carry

    _for_chunks(nkc, body, 0, unroll=4)


def _threshold(keys_ref, dig_ref, work_ref, alive_ref, nkc, kk):
    q = keys_ref.shape[2]
    kf = jnp.full((1, q), kk, jnp.float32)
    _fill(alive_ref, nkc, _BYTE_LOW)
    prefix, cnt_thr, cnt_gt, _ = _digit_search(dig_ref, work_ref, alive_ref, nkc, KEY_SHIFTS, KEY_DIGITS, kf)
    thr = prefix ^ _INT_MIN
    admissible = thr != KEY_NEG_INF
    surplus = (cnt_thr > kf) & admissible

    @pl.when(jnp.max(jnp.where(surplus, 1.0, 0.0)) > 0.0)
    def _():
        assert keys_ref.shape[0] * KC <= 1 << (2 * DIGIT_BITS)
        top = (1 << (2 * DIGIT_BITS)) - 1
        row = lax.broadcasted_iota(jnp.int32, (TQ, q), 0)

        def tie_fields(c, carry):
            hi = lo = tie = None
            for j in range(SUB):
                is_tie = keys_ref[c, j * TQ:(j + 1) * TQ, :] == thr
                rev = top - (c * KC + j * TQ + row)
                fields = [jnp.where(is_tie, f, 0) for f in
                          (lax.shift_right_logical(rev, DIGIT_BITS), rev & DIGIT_MASK, DIGIT_MASK)]
                fields = [f if j == 0 else lax.shift_left(f, 8 * j) for f in fields]
                hi, lo, tie = fields if j == 0 else (hi | fields[0], lo | fields[1], tie | fields[2])
            dig_ref[0, c], dig_ref[1, c], alive_ref[c] = hi, lo, tie
            return carry

        lax.fori_loop(0, nkc, tie_fields, 0)
        cut, _, _, _ = _digit_search(dig_ref, work_ref, alive_ref, nkc, (DIGIT_BITS, 0), (DIGIT_BITS, DIGIT_BITS),
                                     kf - cnt_gt)

        def drop_losers(c, carry):
            for j in range(SUB):
                rows = slice(j * TQ, (j + 1) * TQ)
                x = keys_ref[c, rows, :]
                lost = (x == thr) & (top - (c * KC + j * TQ + row) < cut)
                keys_ref[c, rows, :] = jnp.where(lost, thr - 1, x)
            return carry

        lax.fori_loop(0, nkc, drop_losers, 0)

    return jnp.where(admissible, thr, KEY_NEG_INF + 1)


def _attend_kernel(iqt_ref, iwt_ref, qzt_ref, ga_ref, ik_ref, k_ref, vt_ref, toep_ref, o_ref,
                   keys_ref, dig_ref, work_ref, alive_ref, m_ref, acc_ref, lg0_ref, lg1_ref,
                   wq_ref, *, kk):
    qi = pl.program_id(1)
    nkc = qi // SUB + 1
    qpos = qi * TQ + lax.broadcasted_iota(jnp.int32, (TQ, TQ), 1)
    krow = lax.broadcasted_iota(jnp.int32, (TQ, TQ), 0)
    for h in range(N_HEADS):
        wq_ref[:, h * TQ:(h + 1) * TQ] = qzt_ref[h]

    def score_chunk(c, carry):
        iqt = jnp.concatenate([iqt_ref[h] for h in range(N_IDX_HEADS)], axis=1)
        w = iwt_ref[...]
        for j in range(SUB):
            st = jnp.dot(ik_ref[c, j * TQ:(j + 1) * TQ, :], iqt, preferred_element_type=jnp.float32)
            s = jnp.zeros((TQ, TQ), jnp.float32)
            for h in range(N_IDX_HEADS):
                s = s + w[h:h + 1, :] * jnp.maximum(st[:, h * TQ:(h + 1) * TQ], 0.0)
            kpos = c * KC + j * TQ + krow
            keys_ref[c, j * TQ:(j + 1) * TQ, :] = _to_key(jnp.where(kpos <= qpos, s, -jnp.inf), kpos)
        _pack_digits(keys_ref, dig_ref, c)
        return carry

    _for_chunks(nkc, score_chunk, 0, unroll=2)
    thr = _threshold(keys_ref, dig_ref, work_ref, alive_ref, nkc, kk)

    m_ref[...] = jnp.full(m_ref.shape, NEG, jnp.float32)
    acc_ref[...] = jnp.zeros(acc_ref.shape, jnp.float32)

    def bias(c, h):
        return jnp.concatenate([toep_ref[jnp.clip(qi - (c * SUB + j), 0, 2), h] for j in range(SUB)], axis=0)

    def logits(c, lg_ref, near):
        md = jnp.where(keys_ref[c] >= thr, 0.0, NEG)
        add = [md + bias(c, h) for h in range(N_HEADS)] if near else [md] * N_HEADS
        lg_ref[...] = (jnp.dot(k_ref[c], wq_ref[...], preferred_element_type=jnp.float32)
                       + jnp.concatenate(add, axis=1))

    def add_bias(c, lg_ref):
        for h in range(N_HEADS):
            sl = slice(h * TQ, (h + 1) * TQ)
            lg_ref[:, sl] = lg_ref[:, sl] + bias(c, h)

    def softmax_pv(c, lg_ref):
        for pr in range(N_HEADS // 2):
            kv = pr // (GROUP // 2)
            sl = slice(2 * pr * TQ, (2 * pr + 2) * TQ)
            m_old = m_ref[pr]
            m_new = jnp.maximum(m_old, jnp.max(lg_ref[:, sl], axis=0, keepdims=True))
            alpha = jnp.exp2(m_old - m_new)
            p = jnp.exp2(lg_ref[:, sl] - m_new)
            pv = jnp.dot(vt_ref[c, kv * V_ROWS:(kv + 1) * V_ROWS, :], p.astype(MXU_DTYPE),
                         preferred_element_type=jnp.float32)
            acc_ref[pr] = alpha * acc_ref[pr] + pv
            m_ref[pr] = m_new

    n_far2 = (jnp.maximum(qi - 1, 0) // SUB) // 2

    def far_pair(i, carry):
        logits(2 * i + 1, lg1_ref, near=False)
        softmax_pv(2 * i, lg0_ref)
        logits(2 * i + 2, lg0_ref, near=False)
        softmax_pv(2 * i + 1, lg1_ref)
        return carry

    logits(0, lg0_ref, near=False)
    lax.fori_loop(0, n_far2, far_pair, 0)

    c0 = 2 * n_far2
    add_bias(c0, lg0_ref)

    @pl.when(c0 + 1 < nkc)
    def _():
        logits(c0 + 1, lg1_ref, near=True)
        softmax_pv(c0, lg0_ref)

    @pl.when(c0 + 1 >= nkc)
    def _():
        softmax_pv(c0, lg0_ref)

    @pl.when(c0 + 2 < nkc)
    def _():
        logits(c0 + 2, lg0_ref, near=True)
        softmax_pv(c0 + 1, lg1_ref)
        softmax_pv(c0 + 2, lg0_ref)

    @pl.when((c0 + 1 < nkc) & (c0 + 2 >= nkc))
    def _():
        softmax_pv(c0 + 1, lg1_ref)

    for pr in range(N_HEADS // 2):
        acc = acc_ref[pr]
        o_t = acc[:HEAD_DIM] / acc[HEAD_DIM:HEAD_DIM + 1]
        pair_t = jnp.concatenate([o_t[:, :TQ], o_t[:, TQ:]], axis=0)
        sl = slice(pr * LANE, (pr + 1) * LANE)
        o_ref[:, sl] = (pair_t.T * _silu(ga_ref[:, sl])).astype(o_ref.dtype)


def _attend(iqt, iwt, qzt, ga, ik, k, vt, toep_t, kk):
    bsz, lq = ga.shape[:2]
    nq = lq // TQ
    nc = k.shape[1]
    lanes = lambda a: pl.BlockSpec(a.shape[:-1] + (TQ,), lambda b, i: (0,) * (a.ndim - 1) + (b * nq + i,))
    whole = lambda a: pl.BlockSpec((None,) + a.shape[1:], lambda b, i: (b,) + (0,) * (a.ndim - 1))
    rows = pl.BlockSpec((None, TQ, N_HEADS * HEAD_DIM), lambda b, i: (b, i, 0))
    return pl.pallas_call(
        functools.partial(_attend_kernel, kk=kk),
        grid=(bsz, nq),
        in_specs=[lanes(iqt), lanes(iwt), lanes(qzt), rows, whole(ik), whole(k), whole(vt),
                  pl.BlockSpec(toep_t.shape, lambda b, i: (0, 0, 0, 0))],
        out_specs=rows,
        out_shape=jax.ShapeDtypeStruct((bsz, lq, N_HEADS * HEAD_DIM), MXU_DTYPE),
        scratch_shapes=[pltpu.VMEM((nc, KC, TQ), jnp.int32),
                        pltpu.VMEM((N_LEVELS, nc, TQ, TQ), jnp.int32), pltpu.VMEM((nc, TQ, TQ), jnp.int32),
                        pltpu.VMEM((nc, TQ, TQ), jnp.int32),
                        pltpu.VMEM((N_HEADS // 2, 1, 2 * TQ), jnp.float32),
                        pltpu.VMEM((N_HEADS // 2, V_ROWS, 2 * TQ), jnp.float32),
                        pltpu.VMEM((KC, N_HEADS * TQ), jnp.float32), pltpu.VMEM((KC, N_HEADS * TQ), jnp.float32),
                        pltpu.VMEM((LANE, N_HEADS * TQ), MXU_DTYPE)],
        compiler_params=_cparams(2),
        name="attend",
    )(iqt, iwt, qzt, ga, ik, k, vt, toep_t)


def _out_kernel(x_ref, mc_ref, ma_ref, wc_ref, wa_ref, o_ref):
    o_ref[...] = (x_ref[...]
                  + jnp.dot(mc_ref[...], wc_ref[...], preferred_element_type=jnp.float32)
                  + jnp.dot(ma_ref[...], wa_ref[...], preferred_element_type=jnp.float32))


def _out_final_kernel(x_ref, mc_ref, ma_ref, wc_ref, wa_ref, g_ref, o_ref, y_ref):
    x = (x_ref[...]
         + jnp.dot(mc_ref[...], wc_ref[...], preferred_element_type=jnp.float32)
         + jnp.dot(ma_ref[...], wa_ref[...], preferred_element_type=jnp.float32))
    o_ref[...] = x
    ms = jnp.mean(x * x, axis=-1, keepdims=True)
    y_ref[...] = x * lax.rsqrt(ms + EPS) * g_ref[...]


def _out_proj(x, mc, ma, wc, wa, tm, final_g=None):
    r, d = x.shape
    c = mc.shape[1]
    rows = lambda w: pl.BlockSpec((tm, w), lambda i: (i, 0))
    full = lambda a: pl.BlockSpec(a.shape, lambda i: (0, 0))
    if final_g is None:
        return pl.pallas_call(
            _out_kernel, grid=(r // tm,),
            in_specs=[rows(d), rows(c), rows(c), full(wc), full(wa)],
            out_specs=rows(d), out_shape=jax.ShapeDtypeStruct((r, d), jnp.float32),
            compiler_params=_cparams(1), name="out",
        )(x, mc, ma, wc, wa)
    g = final_g.reshape(1, d)
    return pl.pallas_call(
        _out_final_kernel, grid=(r // tm,),
        in_specs=[rows(d), rows(c), rows(c), full(wc), full(wa), full(g)],
        out_specs=[rows(d), rows(d)], out_shape=[jax.ShapeDtypeStruct((r, d), jnp.float32)] * 2,
        compiler_params=_cparams(1), name="out_final",
    )(x, mc, ma, wc, wa, g)


def _sconv_kernel(st_ref, u_ref, gc_ref, w_ref, b_ref, lg_ref, lb_ref, o_ref):
    acc = jnp.zeros(u_ref.shape, jnp.float32)
    for j in range(CONV_W - 1):
        acc = acc + w_ref[j:j + 1, :] * st_ref[j]
    acc = acc + w_ref[CONV_W - 1:CONV_W, :] * u_ref[...]
    y = acc + b_ref[...]
    mu = jnp.mean(y, axis=-1, keepdims=True)
    dev = y - mu
    var = jnp.mean(dev * dev, axis=-1, keepdims=True)
    yn = dev * lax.rsqrt(var + LN_EPS) * lg_ref[...] + lb_ref[...]
    o_ref[...] = (_silu(yn) * _silu(gc_ref[...])).astype(o_ref.dtype)


def _sample_conv(state_t, u, gc, w, b, lg, lb):
    db, c = u.shape
    row = lambda a: a.reshape(1, c)
    full = lambda a: pl.BlockSpec(a.shape, lambda i: (0,) * a.ndim)
    args = (state_t, u, gc, w, row(b), row(lg), row(lb))
    return pl.pallas_call(
        _sconv_kernel, grid=(1,), in_specs=[full(a) for a in args],
        out_specs=pl.BlockSpec((db, c), lambda i: (0, 0)),
        out_shape=jax.ShapeDtypeStruct((db, c), MXU_DTYPE),
        compiler_params=_cparams(1), name="sconv",
    )(*args)


def _dec_score_kernel(pt_ref, iq_ref, iw_ref, ikn_ref, *refs, n_pages, width):
    del pt_ref
    pages, o_ref = refs[:n_pages], refs[n_pages]
    iq = iq_ref[...][:, :IDX_DIM]
    w = iw_ref[...]
    ikt = jnp.concatenate([r[...].astype(MXU_DTYPE) for r in pages], axis=1)
    s = jnp.dot(iq, ikt, preferred_element_type=jnp.float32)
    sc = jnp.sum(w * jnp.maximum(s, 0.0), axis=0, keepdims=True)
    ikn = ikn_ref[...][:, :IDX_DIM].astype(MXU_DTYPE).astype(jnp.float32)
    sn = jnp.sum(iq.astype(jnp.float32) * ikn, axis=-1, keepdims=True)
    scn = jnp.sum(w * jnp.maximum(sn, 0.0), axis=0, keepdims=True)
    lane = lax.broadcasted_iota(jnp.int32, (1, width - sc.shape[1]), 1)
    o_ref[...] = jnp.concatenate([sc, jnp.where(lane == 0, scn, -jnp.inf)], axis=1)


def _dec_scores(page_table, iq3, iw3, tail3, cache_ik, layer, width):
    db, n_pages = page_table.shape
    di, page = cache_ik.shape[2:]
    seq = lambda a: pl.BlockSpec((None,) + a.shape[1:], lambda b, pt: (b, 0, 0))
    page_spec = lambda p: pl.BlockSpec((None, None, di, page), lambda b, pt: (layer, pt[b, p], 0, 0))
    return pl.pallas_call(
        functools.partial(_dec_score_kernel, n_pages=n_pages, width=width),
        grid_spec=pltpu.PrefetchScalarGridSpec(
            num_scalar_prefetch=1, grid=(db,),
            in_specs=[seq(iq3), seq(iw3), seq(tail3)] + [page_spec(p) for p in range(n_pages)],
            out_specs=pl.BlockSpec((None, 1, width), lambda b, pt: (b, 0, 0))),
        out_shape=jax.ShapeDtypeStruct((db, 1, width), jnp.float32),
        compiler_params=_cparams(1), name="dec_score",
    )(page_table, iq3, iw3, tail3, *([cache_ik] * n_pages))


def _dec_select_kernel(s_ref, o_ref, keys_ref, dig_ref, work_ref, alive_ref, *, kk):
    nc = keys_ref.shape[0]
    row = lax.broadcasted_iota(jnp.int32, (KC, keys_ref.shape[2]), 0)
    for c in range(nc):
        keys_ref[c] = _to_key(s_ref[c * KC:(c + 1) * KC, :], c * KC + row)
        _pack_digits(keys_ref, dig_ref, c)
    thr = _threshold(keys_ref, dig_ref, work_ref, alive_ref, nc, kk)
    for c in range(nc):
        o_ref[c * KC:(c + 1) * KC, :] = jnp.where(keys_ref[c] >= thr, 0.0, NEG)


def _dec_select(scores_t, kk):
    width, db = scores_t.shape
    nc = width // KC
    spec = pl.BlockSpec((width, db), lambda i: (0, 0))
    return pl.pallas_call(
        functools.partial(_dec_select_kernel, kk=kk),
        grid=(1,), in_specs=[spec], out_specs=spec,
        out_shape=jax.ShapeDtypeStruct((width, db), jnp.float32),
        scratch_shapes=[pltpu.VMEM((nc, KC, db), jnp.int32),
                        pltpu.VMEM((N_LEVELS, nc, TQ, db), jnp.int32), pltpu.VMEM((nc, TQ, db), jnp.int32),
                        pltpu.VMEM((nc, TQ, db), jnp.int32)],
        compiler_params=_cparams(1), name="dec_select",
    )(scores_t)


def _dec_attend_kernel(pt_ref, qz_ref, kvn_ref, madd_ref, bias_ref, ga_ref, *refs, n_pages):
    del pt_ref
    kpages, vpages, o_ref = refs[:n_pages], refs[n_pages:2 * n_pages], refs[2 * n_pages]
    qz = qz_ref[...]
    page = kpages[0].shape[1]
    past = n_pages * page
    kt = jnp.concatenate([r[...].astype(MXU_DTYPE) for r in kpages], axis=1)
    vt = jnp.concatenate([r[...].astype(MXU_DTYPE) for r in vpages], axis=1)
    lg = (jnp.dot(qz, kt, preferred_element_type=jnp.float32)
          + bias_ref[:, :past] + madd_ref[:, :past])
    kvn = kvn_ref[...].astype(MXU_DTYPE).astype(jnp.float32)
    lgn = jnp.sum(qz.astype(jnp.float32) * kvn[:, :LANE], axis=-1, keepdims=True)
    lgn = lgn + bias_ref[:, past:past + 1] + madd_ref[:, past:past + 1]
    m = jnp.maximum(lgn, jnp.max(lg, axis=-1, keepdims=True))
    pn = jnp.exp2(lgn - m)
    pp = jnp.exp2(lg - m)
    den = pn + jnp.sum(pp, axis=-1, keepdims=True)
    o = (pn.astype(MXU_DTYPE).astype(jnp.float32) * kvn[:, LANE:]
         + lax.dot_general(pp.astype(MXU_DTYPE), vt, _NT, preferred_element_type=jnp.float32))
    o = o / den
    swapped = pltpu.roll(o, HEAD_DIM, axis=1)
    lane = lax.broadcasted_iota(jnp.int32, (1, LANE), 1)
    pairs = []
    for j in range(N_HEADS // 2):
        first, second = (o, swapped) if 2 * j < GROUP else (swapped, o)
        pairs.append(jnp.where(lane < HEAD_DIM, first[2 * j:2 * j + 1], second[2 * j + 1:2 * j + 2]))
    o_ref[...] = (jnp.concatenate(pairs, axis=0) * _silu(ga_ref[...])).astype(o_ref.dtype)


def _dec_attend(page_table, qz3, kvn3, madd3, bias, ga3, cache_k, cache_v, layer):
    db, n_pages = page_table.shape
    feat, page = cache_k.shape[2:]
    seq = lambda a: pl.BlockSpec((None,) + a.shape[1:], lambda b, pt: (b, 0, 0))
    page_spec = lambda p: pl.BlockSpec((None, None, feat, page), lambda b, pt: (layer, pt[b, p], 0, 0))
    return pl.pallas_call(
        functools.partial(_dec_attend_kernel, n_pages=n_pages),
        grid_spec=pltpu.PrefetchScalarGridSpec(
            num_scalar_prefetch=1, grid=(db,),
            in_specs=[seq(qz3), seq(kvn3), seq(madd3), pl.BlockSpec(bias.shape, lambda b, pt: (0, 0)), seq(ga3)]
                     + [page_spec(p) for p in range(n_pages)] * 2,
            out_specs=pl.BlockSpec((None, GROUP, LANE), lambda b, pt: (b, 0, 0))),
        out_shape=jax.ShapeDtypeStruct((db, GROUP, LANE), MXU_DTYPE),
        compiler_params=_cparams(1), name="dec_attend",
    )(page_table, qz3, kvn3, madd3, bias, ga3, *([cache_k] * n_pages), *([cache_v] * n_pages))


def _round_up(x, m):
    return -(-x // m) * m


def _row_tile(rows):
    for f in (5, 4, 3, 2, 1):
        if rows % (f * TQ) == 0:
            return f * TQ
    return rows


def kernel(x_prompt, x_sample, cache_k, cache_v, cache_idx_k, state_conv, page_table, meta_tokens, rel_bias,
           norm_g, w_in, conv_w, conv_b, conv_ln_g, conv_ln_b, w_out, final_norm_g):
    bsz, seq, d = x_prompt.shape
    depth = w_in.shape[0]
    lp = seq + N_META
    lq = _round_up(lp, TQ)
    lk = _round_up(lp, KC)
    nq, nc = lq // TQ, lk // KC
    kk_p = min(TOPK_MAX, lp // 4)
    db = x_sample.shape[0]
    n_pool, page = cache_k.shape[1:3]
    n_pages = page_table.shape[1]
    past = n_pages * page
    kk_s = min(TOPK_MAX, (past + 1) // 4)
    width_s = _round_up(past + 1, KC)
    feat = N_KV_HEADS * HEAD_DIM
    tm = _row_tile(bsz * lq)

    xp = jnp.concatenate([jnp.broadcast_to(meta_tokens[None].astype(x_prompt.dtype), (bsz, N_META, d)), x_prompt,
                          jnp.zeros((bsz, lq - lp, d), x_prompt.dtype)], axis=1).reshape(bsz * lq, d)
    xs = x_sample.reshape(db, d)
    toep_t = _bias_tiles(rel_bias)
    bias_s = jnp.moveaxis(rel_bias[_t5_bucket(past - jnp.arange(width_s, dtype=jnp.int32))], -1, 0) * LOG2E
    cache_k4 = jnp.transpose(cache_k, (0, 1, 3, 4, 2)).reshape(depth, n_pool, feat, page)
    cache_v4 = jnp.transpose(cache_v, (0, 1, 3, 4, 2)).reshape(depth, n_pool, feat, page)
    cache_ik4 = jnp.swapaxes(cache_idx_k, 2, 3)

    def key_chunks(a):
        return jnp.pad(a, ((0, 0), (0, lk - lq), (0, 0))).astype(MXU_DTYPE).reshape(bsz, nc, KC, LANE)

    def value_chunks_t(a):
        a = jnp.pad(a.reshape(N_KV_HEADS, HEAD_DIM, bsz, lq), ((0, 0), (0, 0), (0, 0), (0, lk - lq)))
        a = jnp.concatenate([a, jnp.ones((N_KV_HEADS, V_ROWS - HEAD_DIM, bsz, lk), a.dtype)], axis=1)
        return jnp.transpose(a.reshape(N_KV_HEADS * V_ROWS, bsz, nc, KC), (1, 2, 0, 3))

    kp, vp, ikp, cp, ksm, vsm, iks, cs = ([] for _ in range(8))
    yp = ys = None
    for l in range(depth):
        w = _prep_w_in(w_in[l])
        wc = w_out[l, :C_CONV].astype(MXU_DTYPE)
        wa = w_out[l, C_CONV:].astype(MXU_DTYPE)
        last = l == depth - 1

        u, gc, kv, ga, tail, qzt, iqzt, vt, iwt = _project(xp, norm_g[l], w, tm, feature_major=True)
        b3 = lambda a: a.reshape(bsz, lq, a.shape[-1])
        u3, kv3, tail3 = b3(u), b3(kv), b3(tail)
        mixc = _conv_branch(u3, b3(gc), conv_w[l], conv_b[l], conv_ln_g[l], conv_ln_b[l])
        lane = jnp.arange(LANE)
        ik = key_chunks(jnp.where(lane < IDX_DIM, tail3, 0.0))
        mixa = _attend(iqzt.reshape(N_IDX_HEADS, LANE, bsz * lq), iwt, qzt.reshape(N_HEADS, LANE, bsz * lq), b3(ga),
                       ik, key_chunks(kv3[..., :feat]), value_chunks_t(vt), toep_t, kk_p)
        res = _out_proj(xp, mixc.reshape(bsz * lq, C_CONV), mixa.reshape(bsz * lq, -1), wc, wa, tm,
                        final_norm_g if last else None)
        xp, yp = res if last else (res, None)
        kp.append(kv3[:, :lp, :feat].reshape(bsz, lp, N_KV_HEADS, HEAD_DIM))
        vp.append(kv3[:, :lp, feat:].reshape(bsz, lp, N_KV_HEADS, HEAD_DIM))
        ikp.append(tail3[:, :lp, :IDX_DIM])
        cp.append(u3[:, lp - (CONV_W - 1):lp])

        u, gc, kv, ga, tail, q, iq = _project(xs, norm_g[l], w, db, feature_major=False)
        mixc = _sample_conv(jnp.transpose(state_conv[l], (1, 0, 2)), u, gc,
                            conv_w[l], conv_b[l], conv_ln_g[l], conv_ln_b[l])
        iw3 = tail[:, IDX_DIM:IDX_DIM + N_IDX_HEADS].reshape(db, N_IDX_HEADS, 1)
        scores = _dec_scores(page_table, iq.reshape(db, N_IDX_HEADS, IDX_DIM), iw3, tail.reshape(db, 1, LANE),
                             cache_ik4, l, width_s)
        madd = _dec_select(scores.reshape(db, width_s).T, kk_s).T
        q4 = q.reshape(db, N_KV_HEADS, GROUP, HEAD_DIM)
        qz3 = jnp.concatenate([jnp.pad(q4[:, 0], ((0, 0), (0, 0), (0, HEAD_DIM))),
                               jnp.pad(q4[:, 1], ((0, 0), (0, 0), (HEAD_DIM, 0)))], axis=1)
        mixa = _dec_attend(page_table, qz3, kv.reshape(db, 1, 2 * feat),
                           madd.reshape(db, 1, width_s), bias_s, ga.reshape(db, N_HEADS // 2, LANE),
                           cache_k4, cache_v4, l)
        res = _out_proj(xs, mixc, mixa.reshape(db, -1), wc, wa, db, final_norm_g if last else None)
        xs, ys = res if last else (res, None)
        ksm.append(kv[:, :feat].reshape(db, 1, N_KV_HEADS, HEAD_DIM))
        vsm.append(kv[:, feat:].reshape(db, 1, N_KV_HEADS, HEAD_DIM))
        iks.append(tail[:, None, :IDX_DIM])
        cs.append(jnp.concatenate([state_conv[l][:, 1:], u[:, None]], axis=1))

    y_prompt = yp.reshape(bsz, lq, d)[:, N_META:lp]
    y_sample = ys.reshape(db, 1, d)
    return (y_prompt, y_sample, jnp.stack(kp), jnp.stack(vp), jnp.stack(ikp), jnp.stack(cp),
            jnp.stack(ksm), jnp.stack(vsm), jnp.stack(iks), jnp.stack(cs))
```

```python
import functools
import math

import numpy as np
import jax
import jax.numpy as jnp
from jax import lax
from jax.experimental import pallas as pl
from jax.experimental.pallas import tpu as pltpu

N_HEADS = 8
N_KV_HEADS = 2
GROUP = N_HEADS // N_KV_HEADS
HEAD_DIM = 64
N_IDX_HEADS = 4
IDX_DIM = 64
C_CONV = 512
CONV_W = 31
TOPK_MAX = 256
N_META = 16
NUM_BUCKETS = 32
MAX_DISTANCE = 128
EPS = 1e-6
LN_EPS = 1e-5

LANE = 128
ROWS = 8
TQ = 128
KC = 512
SUB = KC // TQ
N_ACC = 4
DIGIT_BITS = 7
DIGIT_MASK = (1 << DIGIT_BITS) - 1
KEY_DIGITS = (7, 7, 7, 7, 4)
KEY_SHIFTS = tuple(32 - sum(KEY_DIGITS[:i + 1]) for i in range(len(KEY_DIGITS)))
N_LEVELS = len(KEY_DIGITS)
V_ROWS = HEAD_DIM + 16
HALO = 32
CONV_SUB = 64
MXU_DTYPE = jnp.bfloat16
NEG = -1e30
LOG2E = 1.4426950408889634
INT_MIN = -2 ** 31
TIE_SPAN = 1 << 14
KEY_NEG_INF = -0x7F800000 - TIE_SPAN
VMEM_LIMIT = 56 * 1024 * 1024

_PROJ_SIZES = (C_CONV, C_CONV, C_CONV, N_HEADS * HEAD_DIM, N_KV_HEADS * HEAD_DIM, N_KV_HEADS * HEAD_DIM,
               N_HEADS * HEAD_DIM, N_IDX_HEADS * IDX_DIM, IDX_DIM, N_IDX_HEADS)
_UB_OFF, _GC_OFF, _Q_OFF, _K_OFF, _V_OFF, _GA_OFF, _IQ_OFF, _IK_OFF, _IW_OFF, _D_IN = (
    int(s) for s in np.cumsum(_PROJ_SIZES))
_W_ROWS = _IK_OFF + LANE


def _cparams(n_axes):
    return pltpu.CompilerParams(dimension_semantics=("arbitrary",) * n_axes, vmem_limit_bytes=VMEM_LIMIT)


def _silu(x):
    return x * jax.nn.sigmoid(x)


def _prep_w_in(w):
    assert w.shape[1] == _D_IN
    return jnp.pad(w.T, ((0, _W_ROWS - _D_IN), (0, 0))).astype(MXU_DTYPE)


def _t5_bucket(rel):
    n = jnp.maximum(rel, 0)
    max_exact = NUM_BUCKETS // 2
    large = max_exact + (jnp.log(jnp.maximum(n, 1).astype(jnp.float32) / max_exact)
                         / math.log(MAX_DISTANCE / max_exact)
                         * (NUM_BUCKETS - max_exact)).astype(jnp.int32)
    large = jnp.minimum(large, NUM_BUCKETS - 1)
    return jnp.where(n < max_exact, n, large)


def _bias_tiles(rel_bias):
    far = rel_bias[NUM_BUCKETS - 1]
    d = jnp.arange(2 * TQ, dtype=jnp.int32)
    dists = (jnp.where(d < TQ, d, 0),
             jnp.where(d < TQ, d + TQ, d - TQ))
    tiles = []
    for dist in dists:
        v = ((rel_bias[_t5_bucket(dist)] - far) * LOG2E).T
        rep = jnp.tile(v, (1, TQ))[:, :TQ * (2 * TQ - 1)].reshape(N_HEADS, TQ, 2 * TQ - 1)
        tiles.append(rep[:, :, :TQ])
    tiles.append(jnp.zeros_like(tiles[0]))
    return jnp.stack(tiles).astype(jnp.float32)


_NT = (((1,), (1,)), ((), ()))
_Q_SCALE = HEAD_DIM ** -0.5 * LOG2E
_IQ_SCALE = IDX_DIM ** -0.5
_IW_SCALE = N_IDX_HEADS ** -0.5
_IW_ROWS = 16


def _proj_kernel(x_ref, g_ref, w_ref, u_ref, gc_ref, kv_ref, ga_ref, tail_ref, *q_refs, feature_major):
    x = x_ref[...]
    ms = jnp.mean(x * x, axis=-1, keepdims=True)
    xn = (x * lax.rsqrt(ms + EPS) * g_ref[...]).astype(MXU_DTYPE)

    def mm(lo, hi):
        return lax.dot_general(xn, w_ref[lo:hi, :], _NT, preferred_element_type=jnp.float32)

    def mm_t(lo, hi):
        return lax.dot_general(w_ref[lo:hi, :], xn, _NT, preferred_element_type=jnp.float32)

    u_ref[...] = mm(0, _UB_OFF) * jax.nn.sigmoid(mm(_UB_OFF, _GC_OFF))
    gc_ref[...] = mm(_GC_OFF, _Q_OFF)
    kv_ref[...] = mm(_K_OFF, _GA_OFF)
    ga_ref[...] = mm(_GA_OFF, _IQ_OFF)
    t = mm(_IK_OFF, _W_ROWS)
    lane = lax.broadcasted_iota(jnp.int32, t.shape, 1)
    tail_ref[...] = t * jnp.where(lane >= IDX_DIM, _IW_SCALE, 1.0)
    if feature_major:
        qzt_ref, iqzt_ref, vt_ref, iwt_ref = q_refs
        qt = (mm_t(_Q_OFF, _K_OFF) * _Q_SCALE).astype(qzt_ref.dtype)
        qzt_ref[...] = jnp.zeros(qzt_ref.shape, qzt_ref.dtype)
        for h in range(N_HEADS):
            lo = h * LANE + (h // GROUP) * HEAD_DIM
            qzt_ref[lo:lo + HEAD_DIM, :] = qt[h * HEAD_DIM:(h + 1) * HEAD_DIM]
        iqt = (mm_t(_IQ_OFF, _IK_OFF) * _IQ_SCALE).astype(iqzt_ref.dtype)
        iqzt_ref[...] = jnp.zeros(iqzt_ref.shape, iqzt_ref.dtype)
        for h in range(N_IDX_HEADS):
            iqzt_ref[h * LANE:h * LANE + IDX_DIM, :] = iqt[h * IDX_DIM:(h + 1) * IDX_DIM]
        vt_ref[...] = mm_t(_V_OFF, _GA_OFF).astype(vt_ref.dtype)
        iwt_ref[...] = mm_t(_IW_OFF, _IW_OFF + _IW_ROWS) * _IW_SCALE
    else:
        q_ref, iq_ref = q_refs
        q_ref[...] = (mm(_Q_OFF, _K_OFF) * _Q_SCALE).astype(q_ref.dtype)
        iq_ref[...] = (mm(_IQ_OFF, _IK_OFF) * _IQ_SCALE).astype(iq_ref.dtype)


def _project(x, g, w, tm, feature_major):
    r, d = x.shape
    assert r % tm == 0
    rows = lambda c: pl.BlockSpec((tm, c), lambda i: (i, 0))
    cols = lambda c: pl.BlockSpec((c, tm), lambda i: (0, i))
    feat = N_KV_HEADS * HEAD_DIM
    specs = [rows(C_CONV), rows(C_CONV), rows(2 * feat), rows(N_HEADS * HEAD_DIM), rows(LANE)]
    shapes = [jax.ShapeDtypeStruct((r, s.block_shape[1]), jnp.float32) for s in specs]
    if feature_major:
        extra = [(N_HEADS * LANE, MXU_DTYPE), (N_IDX_HEADS * LANE, MXU_DTYPE), (feat, MXU_DTYPE),
                 (_IW_ROWS, jnp.float32)]
        specs += [cols(c) for c, _ in extra]
        shapes += [jax.ShapeDtypeStruct((c, r), t) for c, t in extra]
    else:
        extra = [(N_HEADS * HEAD_DIM, MXU_DTYPE), (N_IDX_HEADS * IDX_DIM, MXU_DTYPE)]
        specs += [rows(c) for c, _ in extra]
        shapes += [jax.ShapeDtypeStruct((r, c), t) for c, t in extra]
    return pl.pallas_call(
        functools.partial(_proj_kernel, feature_major=feature_major),
        grid=(r // tm,),
        in_specs=[rows(d), pl.BlockSpec((1, d), lambda i: (0, 0)), pl.BlockSpec((_W_ROWS, d), lambda i: (0, 0))],
        out_specs=specs,
        out_shape=shapes,
        compiler_params=_cparams(1),
        name="proj",
    )(x, g.reshape(1, d), w)


def _conv_kernel(prev_ref, cur_ref, gc_ref, w_ref, b_ref, lg_ref, lb_ref, o_ref, ext_ref):
    i = pl.program_id(1)
    ext_ref[0, 0:HALO, :] = jnp.where(i > 0, prev_ref[...], 0.0)
    ext_ref[0, HALO:HALO + TQ, :] = cur_ref[...]
    n_ext = HALO + TQ
    for s in range(1, ROWS):
        ext_ref[s, 0:n_ext - ROWS, :] = ext_ref[0, s:s + n_ext - ROWS, :]
    off = HALO - (CONV_W - 1)
    for r0 in range(0, TQ, CONV_SUB):
        acc = jnp.zeros((CONV_SUB, C_CONV), jnp.float32)
        for j in range(CONV_W):
            s = (off + j) % ROWS
            lo = r0 + off + j - s
            acc = acc + w_ref[j:j + 1, :] * ext_ref[s, lo:lo + CONV_SUB, :]
        y = acc + b_ref[...]
        mu = jnp.mean(y, axis=-1, keepdims=True)
        dev = y - mu
        var = jnp.mean(dev * dev, axis=-1, keepdims=True)
        yn = dev * lax.rsqrt(var + LN_EPS) * lg_ref[...] + lb_ref[...]
        o_ref[r0:r0 + CONV_SUB, :] = (_silu(yn) * _silu(gc_ref[r0:r0 + CONV_SUB, :])).astype(o_ref.dtype)


def _conv_branch(u, gc, w, b, lg, lb):
    bsz, lq, c = u.shape
    per = TQ // HALO
    row = lambda a: a.reshape(1, c)
    vec = pl.BlockSpec((1, c), lambda bi, i: (0, 0))
    tile = pl.BlockSpec((None, TQ, c), lambda bi, i: (bi, i, 0))
    return pl.pallas_call(
        _conv_kernel,
        grid=(bsz, lq // TQ),
        in_specs=[pl.BlockSpec((None, HALO, c), lambda bi, i: (bi, jnp.maximum(i * per - 1, 0), 0)),
                  tile, tile, pl.BlockSpec((CONV_W, c), lambda bi, i: (0, 0)), vec, vec, vec],
        out_specs=tile,
        out_shape=jax.ShapeDtypeStruct((bsz, lq, c), MXU_DTYPE),
        scratch_shapes=[pltpu.VMEM((ROWS, HALO + TQ, c), jnp.float32)],
        compiler_params=_cparams(2),
        name="conv",
    )(u, u, gc, w, row(b), row(lg), row(lb))


_INT_MIN = np.int32(INT_MIN)
_BYTE_ONES = np.int32(0x01010101)
_BYTE_LOW = np.int32(0x7F7F7F7F)
_HALF_LOW_BYTES = np.int32(0x00FF00FF)
_GUARD = np.int32(0x80808080 - (1 << 32))
assert SUB == 4 and DIGIT_BITS == 7


def _to_key(s, idx):
    bits = lax.bitcast_convert_type(s, jnp.int32)
    key = jnp.where(bits < 0, (_INT_MIN - bits) - TIE_SPAN, bits)
    return jnp.where((bits == 0) | (bits == _INT_MIN), -1 - idx, key)


def _for_chunks(nkc, body, init, unroll):
    n_main = nkc // unroll

    def main(i, carry):
        for k in range(unroll):
            carry = body(i * unroll + k, carry)
        return carry

    carry = lax.fori_loop(0, n_main, main, init)
    return lax.fori_loop(n_main * unroll, nkc, body, carry)


def _pack_digits(keys_ref, dig_ref, c):
    for lv in range(N_LEVELS):
        word = None
        for j in range(SUB):
            u = keys_ref[c, j * TQ:(j + 1) * TQ, :]
            if lv == 0:
                u = u ^ _INT_MIN
            move = KEY_SHIFTS[lv] - 8 * j
            u = lax.shift_right_logical(u, move) if move >= 0 else lax.shift_left(u, -move)
            field = u & np.int32(((1 << KEY_DIGITS[lv]) - 1) << (8 * j))
            word = field if word is None else word | field
        dig_ref[lv, c] = word


def _count_fields(work_ref, nkc, cand_bytes):
    q = work_ref.shape[2]

    def body(c, accs):
        accs = list(accs)
        for n, r in enumerate(range(0, TQ, ROWS)):
            diff = work_ref[c, r:r + ROWS, :] - cand_bytes
            accs[n % N_ACC] = accs[n % N_ACC] + (lax.shift_right_logical(diff, DIGIT_BITS) & _BYTE_ONES)
        return tuple(accs)

    accs = _for_chunks(nkc, body, (jnp.zeros((ROWS, q), jnp.int32),) * N_ACC, unroll=4)
    halves = jnp.zeros((ROWS, q), jnp.int32)
    for a in accs:
        halves = halves + (a & _HALF_LOW_BYTES) + (lax.shift_right_logical(a, 8) & _HALF_LOW_BYTES)
    total = (halves & 0xFFFF) + lax.shift_right_logical(halves, 16)
    return jnp.sum(total.astype(jnp.float32), axis=0, keepdims=True)


def _digit_search(dig_ref, work_ref, alive_ref, nkc, shifts, digits, target):
    q = dig_ref.shape[3]
    assert dig_ref.shape[1] * (TQ // ROWS) <= 255 * N_ACC
    zero = jnp.zeros((1, q), jnp.float32)
    value = jnp.zeros((1, q), jnp.int32)
    above, done, cnt_ge, cnt_gt = zero, zero, target + 1.0, zero

    for lv, (shift, n_bits) in enumerate(zip(shifts, digits)):
        def load_fields(c, carry):
            work_ref[c] = (dig_ref[lv, c] & alive_ref[c]) | _GUARD
            return carry

        _for_chunks(nkc, load_fields, 0, unroll=4)

        def digit_bit(i, state):
            digit, cnt_ge, cnt_gt, cnt_rej, done = state
            cand = digit | lax.shift_left(jnp.int32(1), n_bits - 1 - i)
            cnt_alive = _count_fields(work_ref, nkc, cand * _BYTE_ONES)
            cnt = above + cnt_alive
            take = (cnt >= target) & (done == 0.0)
            drop = (cnt < target) & (done == 0.0)
            return (jnp.where(take, cand, digit), jnp.where(take, cnt, cnt_ge), jnp.where(drop, cnt, cnt_gt),
                    jnp.where(drop, cnt_alive, cnt_rej),
                    jnp.where(take & (cnt == target), 1.0, done))

        digit, cnt_ge, cnt_gt, cnt_rej, done = lax.fori_loop(
            0, n_bits, digit_bit, (jnp.zeros((1, q), jnp.int32), cnt_ge, cnt_gt, zero, done))
        above = above + cnt_rej
        value = value | lax.shift_left(digit, shift)

        if lv + 1 < len(shifts):
            digit_bytes = digit * _BYTE_ONES

            def narrow(c, carry):
                differs = lax.shift_right_logical(((dig_ref[lv, c] ^ digit_bytes) + _BYTE_LOW) & _GUARD, DIGIT_BITS)
                alive_ref[c] = alive_ref[c] & ((_BYTE_ONES - differs) * DIGIT_MASK)
                return carry

            _for_chunks(nkc, narrow, 0, unroll=4)

    return value, cnt_ge, cnt_gt, done


def _fill(ref, nkc, word):
    def body(c, carry):
        ref[c] = jnp.full(ref.shape[1:], word, ref.dtype)
        return carry

    _for_chunks(nkc, body, 0, unroll=4)


def _threshold(keys_ref, dig_ref, work_ref, alive_ref, nkc, kk):
    q = keys_ref.shape[2]
    kf = jnp.full((1, q), kk, jnp.float32)
    _fill(alive_ref, nkc, _BYTE_LOW)
    prefix, cnt_thr, cnt_gt, _ = _digit_search(dig_ref, work_ref, alive_ref, nkc, KEY_SHIFTS, KEY_DIGITS, kf)
    thr = prefix ^ _INT_MIN
    admissible = thr != KEY_NEG_INF
    surplus = (cnt_thr > kf) & admissible

    @pl.when(jnp.max(jnp.where(surplus, 1.0, 0.0)) > 0.0)
    def _():
        assert keys_ref.shape[0] * KC <= 1 << (2 * DIGIT_BITS)
        top = (1 << (2 * DIGIT_BITS)) - 1
        row = lax.broadcasted_iota(jnp.int32, (TQ, q), 0)

        def tie_fields(c, carry):
            hi = lo = tie = None
            for j in range(SUB):
                is_tie = keys_ref[c, j * TQ:(j + 1) * TQ, :] == thr
                rev = top - (c * KC + j * TQ + row)
                fields = [jnp.where(is_tie, f, 0) for f in
                          (lax.shift_right_logical(rev, DIGIT_BITS), rev & DIGIT_MASK, DIGIT_MASK)]
                fields = [f if j == 0 else lax.shift_left(f, 8 * j) for f in fields]
                hi, lo, tie = fields if j == 0 else (hi | fields[0], lo | fields[1], tie | fields[2])
            dig_ref[0, c], dig_ref[1, c], alive_ref[c] = hi, lo, tie
            return carry

        lax.fori_loop(0, nkc, tie_fields, 0)
        cut, _, _, _ = _digit_search(dig_ref, work_ref, alive_ref, nkc, (DIGIT_BITS, 0), (DIGIT_BITS, DIGIT_BITS),
                                     kf - cnt_gt)

        def drop_losers(c, carry):
            for j in range(SUB):
                rows = slice(j * TQ, (j + 1) * TQ)
                x = keys_ref[c, rows, :]
                lost = (x == thr) & (top - (c * KC + j * TQ + row) < cut)
                keys_ref[c, rows, :] = jnp.where(lost, thr - 1, x)
            return carry

        lax.fori_loop(0, nkc, drop_losers, 0)

    return jnp.where(admissible, thr, KEY_NEG_INF + 1)


def _attend_kernel(iqt_ref, iwt_ref, qzt_ref, ga_ref, ik_ref, k_ref, vt_ref, toep_ref, o_ref,
                   keys_ref, dig_ref, work_ref, alive_ref, m_ref, acc_ref, lg0_ref, lg1_ref,
                   wq_ref, *, kk):
    qi = pl.program_id(1)
    nkc = qi // SUB + 1
    qpos = qi * TQ + lax.broadcasted_iota(jnp.int32, (TQ, TQ), 1)
    krow = lax.broadcasted_iota(jnp.int32, (TQ, TQ), 0)
    for h in range(N_HEADS):
        wq_ref[:, h * TQ:(h + 1) * TQ] = qzt_ref[h]

    def score_chunk(c, carry, causal):
        iqt = jnp.concatenate([iqt_ref[h] for h in range(N_IDX_HEADS)], axis=1)
        w = iwt_ref[...]
        for j in range(SUB):
            st = jnp.dot(ik_ref[c, j * TQ:(j + 1) * TQ, :], iqt, preferred_element_type=jnp.float32)
            s = jnp.zeros((TQ, TQ), jnp.float32)
            for h in range(N_IDX_HEADS):
                s = s + w[h:h + 1, :] * jnp.maximum(st[:, h * TQ:(h + 1) * TQ], 0.0)
            kpos = c * KC + j * TQ + krow
            if causal:
                s = jnp.where(kpos <= qpos, s, -jnp.inf)
            keys_ref[c, j * TQ:(j + 1) * TQ, :] = _to_key(s, kpos)
        _pack_digits(keys_ref, dig_ref, c)
        return carry

    _for_chunks(nkc - 1, functools.partial(score_chunk, causal=False), 0, unroll=2)
    score_chunk(nkc - 1, 0, causal=True)
    thr = _threshold(keys_ref, dig_ref, work_ref, alive_ref, nkc, kk)

    m_ref[...] = jnp.full(m_ref.shape, NEG, jnp.float32)
    acc_ref[...] = jnp.zeros(acc_ref.shape, jnp.float32)

    def bias(c, h):
        return jnp.concatenate([toep_ref[jnp.clip(qi - (c * SUB + j), 0, 2), h] for j in range(SUB)], axis=0)

    def logits(c, lg_ref, near):
        md = jnp.where(keys_ref[c] >= thr, 0.0, NEG)
        add = [md + bias(c, h) for h in range(N_HEADS)] if near else [md] * N_HEADS
        lg_ref[...] = (jnp.dot(k_ref[c], wq_ref[...], preferred_element_type=jnp.float32)
                       + jnp.concatenate(add, axis=1))

    def add_bias(c, lg_ref):
        for h in range(N_HEADS):
            sl = slice(h * TQ, (h + 1) * TQ)
            lg_ref[:, sl] = lg_ref[:, sl] + bias(c, h)

    def softmax_pv(c, lg_ref):
        for pr in range(N_HEADS // 2):
            kv = pr // (GROUP // 2)
            sl = slice(2 * pr * TQ, (2 * pr + 2) * TQ)
            m_old = m_ref[pr]
            m_new = jnp.maximum(m_old, jnp.max(lg_ref[:, sl], axis=0, keepdims=True))
            alpha = jnp.exp2(m_old - m_new)
            p = jnp.exp2(lg_ref[:, sl] - m_new)
            pv = jnp.dot(vt_ref[c, kv * V_ROWS:(kv + 1) * V_ROWS, :], p.astype(MXU_DTYPE),
                         preferred_element_type=jnp.float32)
            acc_ref[pr] = alpha * acc_ref[pr] + pv
            m_ref[pr] = m_new

    n_far2 = (jnp.maximum(qi - 1, 0) // SUB) // 2

    def far_pair(i, carry):
        logits(2 * i + 1, lg1_ref, near=False)
        softmax_pv(2 * i, lg0_ref)
        logits(2 * i + 2, lg0_ref, near=False)
        softmax_pv(2 * i + 1, lg1_ref)
        return carry

    logits(0, lg0_ref, near=False)
    lax.fori_loop(0, n_far2, far_pair, 0)

    c0 = 2 * n_far2
    add_bias(c0, lg0_ref)

    @pl.when(c0 + 1 < nkc)
    def _():
        logits(c0 + 1, lg1_ref, near=True)
        softmax_pv(c0, lg0_ref)

    @pl.when(c0 + 1 >= nkc)
    def _():
        softmax_pv(c0, lg0_ref)

    @pl.when(c0 + 2 < nkc)
    def _():
        logits(c0 + 2, lg0_ref, near=True)
        softmax_pv(c0 + 1, lg1_ref)
        softmax_pv(c0 + 2, lg0_ref)

    @pl.when((c0 + 1 < nkc) & (c0 + 2 >= nkc))
    def _():
        softmax_pv(c0 + 1, lg1_ref)

    for pr in range(N_HEADS // 2):
        acc = acc_ref[pr]
        o_t = acc[:HEAD_DIM] / acc[HEAD_DIM:HEAD_DIM + 1]
        pair_t = jnp.concatenate([o_t[:, :TQ], o_t[:, TQ:]], axis=0)
        sl = slice(pr * LANE, (pr + 1) * LANE)
        o_ref[:, sl] = (pair_t.T * _silu(ga_ref[:, sl])).astype(o_ref.dtype)


def _attend(iqt, iwt, qzt, ga, ik, k, vt, toep_t, kk):
    bsz, lq = ga.shape[:2]
    nq = lq // TQ
    nc = k.shape[1]
    lanes = lambda a: pl.BlockSpec(a.shape[:-1] + (TQ,), lambda b, i: (0,) * (a.ndim - 1) + (b * nq + i,))
    whole = lambda a: pl.BlockSpec((None,) + a.shape[1:], lambda b, i: (b,) + (0,) * (a.ndim - 1))
    rows = pl.BlockSpec((None, TQ, N_HEADS * HEAD_DIM), lambda b, i: (b, i, 0))
    return pl.pallas_call(
        functools.partial(_attend_kernel, kk=kk),
        grid=(bsz, nq),
        in_specs=[lanes(iqt), lanes(iwt), lanes(qzt), rows, whole(ik), whole(k), whole(vt),
                  pl.BlockSpec(toep_t.shape, lambda b, i: (0, 0, 0, 0))],
        out_specs=rows,
        out_shape=jax.ShapeDtypeStruct((bsz, lq, N_HEADS * HEAD_DIM), MXU_DTYPE),
        scratch_shapes=[pltpu.VMEM((nc, KC, TQ), jnp.int32),
                        pltpu.VMEM((N_LEVELS, nc, TQ, TQ), jnp.int32), pltpu.VMEM((nc, TQ, TQ), jnp.int32),
                        pltpu.VMEM((nc, TQ, TQ), jnp.int32),
                        pltpu.VMEM((N_HEADS // 2, 1, 2 * TQ), jnp.float32),
                        pltpu.VMEM((N_HEADS // 2, V_ROWS, 2 * TQ), jnp.float32),
                        pltpu.VMEM((KC, N_HEADS * TQ), jnp.float32), pltpu.VMEM((KC, N_HEADS * TQ), jnp.float32),
                        pltpu.VMEM((LANE, N_HEADS * TQ), MXU_DTYPE)],
        compiler_params=_cparams(2),
        name="attend",
    )(iqt, iwt, qzt, ga, ik, k, vt, toep_t)


def _out_kernel(x_ref, mc_ref, ma_ref, wc_ref, wa_ref, o_ref):
    o_ref[...] = (x_ref[...]
                  + jnp.dot(mc_ref[...], wc_ref[...], preferred_element_type=jnp.float32)
                  + jnp.dot(ma_ref[...], wa_ref[...], preferred_element_type=jnp.float32))


def _out_final_kernel(x_ref, mc_ref, ma_ref, wc_ref, wa_ref, g_ref, o_ref, y_ref):
    x = (x_ref[...]
         + jnp.dot(mc_ref[...], wc_ref[...], preferred_element_type=jnp.float32)
         + jnp.dot(ma_ref[...], wa_ref[...], preferred_element_type=jnp.float32))
    o_ref[...] = x
    ms = jnp.mean(x * x, axis=-1, keepdims=True)
    y_ref[...] = x * lax.rsqrt(ms + EPS) * g_ref[...]


def _out_proj(x, mc, ma, wc, wa, tm, final_g=None):
    r, d = x.shape
    c = mc.shape[1]
    rows = lambda w: pl.BlockSpec((tm, w), lambda i: (i, 0))
    full = lambda a: pl.BlockSpec(a.shape, lambda i: (0, 0))
    if final_g is None:
        return pl.pallas_call(
            _out_kernel, grid=(r // tm,),
            in_specs=[rows(d), rows(c), rows(c), full(wc), full(wa)],
            out_specs=rows(d), out_shape=jax.ShapeDtypeStruct((r, d), jnp.float32),
            compiler_params=_cparams(1), name="out",
        )(x, mc, ma, wc, wa)
    g = final_g.reshape(1, d)
    return pl.pallas_call(
        _out_final_kernel, grid=(r // tm,),
        in_specs=[rows(d), rows(c), rows(c), full(wc), full(wa), full(g)],
        out_specs=[rows(d), rows(d)], out_shape=[jax.ShapeDtypeStruct((r, d), jnp.float32)] * 2,
        compiler_params=_cparams(1), name="out_final",
    )(x, mc, ma, wc, wa, g)


def _sconv_kernel(st_ref, u_ref, gc_ref, w_ref, b_ref, lg_ref, lb_ref, o_ref):
    acc = jnp.zeros(u_ref.shape, jnp.float32)
    for j in range(CONV_W - 1):
        acc = acc + w_ref[j:j + 1, :] * st_ref[j]
    acc = acc + w_ref[CONV_W - 1:CONV_W, :] * u_ref[...]
    y = acc + b_ref[...]
    mu = jnp.mean(y, axis=-1, keepdims=True)
    dev = y - mu
    var = jnp.mean(dev * dev, axis=-1, keepdims=True)
    yn = dev * lax.rsqrt(var + LN_EPS) * lg_ref[...] + lb_ref[...]
    o_ref[...] = (_silu(yn) * _silu(gc_ref[...])).astype(o_ref.dtype)


def _sample_conv(state_t, u, gc, w, b, lg, lb):
    db, c = u.shape
    row = lambda a: a.reshape(1, c)
    full = lambda a: pl.BlockSpec(a.shape, lambda i: (0,) * a.ndim)
    args = (state_t, u, gc, w, row(b), row(lg), row(lb))
    return pl.pallas_call(
        _sconv_kernel, grid=(1,), in_specs=[full(a) for a in args],
        out_specs=pl.BlockSpec((db, c), lambda i: (0, 0)),
        out_shape=jax.ShapeDtypeStruct((db, c), MXU_DTYPE),
        compiler_params=_cparams(1), name="sconv",
    )(*args)


def _dec_score_kernel(pt_ref, iq_ref, iw_ref, ikn_ref, *refs, n_pages, width):
    del pt_ref
    pages, o_ref = refs[:n_pages], refs[n_pages]
    iq = iq_ref[...][:, :IDX_DIM]
    w = iw_ref[...]
    ikt = jnp.concatenate([r[...].astype(MXU_DTYPE) for r in pages], axis=1)
    s = jnp.dot(iq, ikt, preferred_element_type=jnp.float32)
    sc = jnp.sum(w * jnp.maximum(s, 0.0), axis=0, keepdims=True)
    ikn = ikn_ref[...][:, :IDX_DIM].astype(MXU_DTYPE).astype(jnp.float32)
    sn = jnp.sum(iq.astype(jnp.float32) * ikn, axis=-1, keepdims=True)
    scn = jnp.sum(w * jnp.maximum(sn, 0.0), axis=0, keepdims=True)
    lane = lax.broadcasted_iota(jnp.int32, (1, width - sc.shape[1]), 1)
    o_ref[...] = jnp.concatenate([sc, jnp.where(lane == 0, scn, -jnp.inf)], axis=1)


def _dec_scores(page_table, iq3, iw3, tail3, cache_ik, layer, width):
    db, n_pages = page_table.shape
    di, page = cache_ik.shape[2:]
    seq = lambda a: pl.BlockSpec((None,) + a.shape[1:], lambda b, pt: (b, 0, 0))
    page_spec = lambda p: pl.BlockSpec((None, None, di, page), lambda b, pt: (layer, pt[b, p], 0, 0))
    return pl.pallas_call(
        functools.partial(_dec_score_kernel, n_pages=n_pages, width=width),
        grid_spec=pltpu.PrefetchScalarGridSpec(
            num_scalar_prefetch=1, grid=(db,),
            in_specs=[seq(iq3), seq(iw3), seq(tail3)] + [page_spec(p) for p in range(n_pages)],
            out_specs=pl.BlockSpec((None, 1, width), lambda b, pt: (b, 0, 0))),
        out_shape=jax.ShapeDtypeStruct((db, 1, width), jnp.float32),
        compiler_params=_cparams(1), name="dec_score",
    )(page_table, iq3, iw3, tail3, *([cache_ik] * n_pages))


def _dec_select_kernel(s_ref, o_ref, keys_ref, dig_ref, work_ref, alive_ref, *, kk):
    nc = keys_ref.shape[0]
    row = lax.broadcasted_iota(jnp.int32, (KC, keys_ref.shape[2]), 0)
    for c in range(nc):
        keys_ref[c] = _to_key(s_ref[c * KC:(c + 1) * KC, :], c * KC + row)
        _pack_digits(keys_ref, dig_ref, c)
    thr = _threshold(keys_ref, dig_ref, work_ref, alive_ref, nc, kk)
    for c in range(nc):
        o_ref[c * KC:(c + 1) * KC, :] = jnp.where(keys_ref[c] >= thr, 0.0, NEG)


def _dec_select(scores_t, kk):
    width, db = scores_t.shape
    nc = width // KC
    spec = pl.BlockSpec((width, db), lambda i: (0, 0))
    return pl.pallas_call(
        functools.partial(_dec_select_kernel, kk=kk),
        grid=(1,), in_specs=[spec], out_specs=spec,
        out_shape=jax.ShapeDtypeStruct((width, db), jnp.float32),
        scratch_shapes=[pltpu.VMEM((nc, KC, db), jnp.int32),
                        pltpu.VMEM((N_LEVELS, nc, TQ, db), jnp.int32), pltpu.VMEM((nc, TQ, db), jnp.int32),
                        pltpu.VMEM((nc, TQ, db), jnp.int32)],
        compiler_params=_cparams(1), name="dec_select",
    )(scores_t)


def _dec_attend_kernel(pt_ref, qz_ref, kvn_ref, madd_ref, bias_ref, ga_ref, *refs, n_pages):
    del pt_ref
    kpages, vpages, o_ref = refs[:n_pages], refs[n_pages:2 * n_pages], refs[2 * n_pages]
    qz = qz_ref[...]
    page = kpages[0].shape[1]
    past = n_pages * page
    kt = jnp.concatenate([r[...].astype(MXU_DTYPE) for r in kpages], axis=1)
    vt = jnp.concatenate([r[...].astype(MXU_DTYPE) for r in vpages], axis=1)
    lg = (jnp.dot(qz, kt, preferred_element_type=jnp.float32)
          + bias_ref[:, :past] + madd_ref[:, :past])
    kvn = kvn_ref[...].astype(MXU_DTYPE).astype(jnp.float32)
    lgn = jnp.sum(qz.astype(jnp.float32) * kvn[:, :LANE], axis=-1, keepdims=True)
    lgn = lgn + bias_ref[:, past:past + 1] + madd_ref[:, past:past + 1]
    m = jnp.maximum(lgn, jnp.max(lg, axis=-1, keepdims=True))
    pn = jnp.exp2(lgn - m)
    pp = jnp.exp2(lg - m)
    den = pn + jnp.sum(pp, axis=-1, keepdims=True)
    o = (pn.astype(MXU_DTYPE).astype(jnp.float32) * kvn[:, LANE:]
         + lax.dot_general(pp.astype(MXU_DTYPE), vt, _NT, preferred_element_type=jnp.float32))
    o = o / den
    swapped = pltpu.roll(o, HEAD_DIM, axis=1)
    lane = lax.broadcasted_iota(jnp.int32, (1, LANE), 1)
    pairs = []
    for j in range(N_HEADS // 2):
        first, second = (o, swapped) if 2 * j < GROUP else (swapped, o)
        pairs.append(jnp.where(lane < HEAD_DIM, first[2 * j:2 * j + 1], second[2 * j + 1:2 * j + 2]))
    o_ref[...] = (jnp.concatenate(pairs, axis=0) * _silu(ga_ref[...])).astype(o_ref.dtype)


def _dec_attend(page_table, qz3, kvn3, madd3, bias, ga3, cache_k, cache_v, layer):
    db, n_pages = page_table.shape
    feat, page = cache_k.shape[2:]
    seq = lambda a: pl.BlockSpec((None,) + a.shape[1:], lambda b, pt: (b, 0, 0))
    page_spec = lambda p: pl.BlockSpec((None, None, feat, page), lambda b, pt: (layer, pt[b, p], 0, 0))
    return pl.pallas_call(
        functools.partial(_dec_attend_kernel, n_pages=n_pages),
        grid_spec=pltpu.PrefetchScalarGridSpec(
            num_scalar_prefetch=1, grid=(db,),
            in_specs=[seq(qz3), seq(kvn3), seq(madd3), pl.BlockSpec(bias.shape, lambda b, pt: (0, 0)), seq(ga3)]
                     + [page_spec(p) for p in range(n_pages)] * 2,
            out_specs=pl.BlockSpec((None, GROUP, LANE), lambda b, pt: (b, 0, 0))),
        out_shape=jax.ShapeDtypeStruct((db, GROUP, LANE), MXU_DTYPE),
        compiler_params=_cparams(1), name="dec_attend",
    )(page_table, qz3, kvn3, madd3, bias, ga3, *([cache_k] * n_pages), *([cache_v] * n_pages))


def _round_up(x, m):
    return -(-x // m) * m


def _row_tile(rows):
    for f in (5, 4, 3, 2, 1):
        if rows % (f * TQ) == 0:
            return f * TQ
    return rows


def kernel(x_prompt, x_sample, cache_k, cache_v, cache_idx_k, state_conv, page_table, meta_tokens, rel_bias,
           norm_g, w_in, conv_w, conv_b, conv_ln_g, conv_ln_b, w_out, final_norm_g):
    bsz, seq, d = x_prompt.shape
    depth = w_in.shape[0]
    lp = seq + N_META
    lq = _round_up(lp, TQ)
    lk = _round_up(lp, KC)
    nq, nc = lq // TQ, lk // KC
    kk_p = min(TOPK_MAX, lp // 4)
    db = x_sample.shape[0]
    n_pool, page = cache_k.shape[1:3]
    n_pages = page_table.shape[1]
    past = n_pages * page
    kk_s = min(TOPK_MAX, (past + 1) // 4)
    width_s = _round_up(past + 1, KC)
    feat = N_KV_HEADS * HEAD_DIM
    tm = _row_tile(bsz * lq)

    xp = jnp.concatenate([jnp.broadcast_to(meta_tokens[None].astype(x_prompt.dtype), (bsz, N_META, d)), x_prompt,
                          jnp.zeros((bsz, lq - lp, d), x_prompt.dtype)], axis=1).reshape(bsz * lq, d)
    xs = x_sample.reshape(db, d)
    toep_t = _bias_tiles(rel_bias)
    bias_s = jnp.moveaxis(rel_bias[_t5_bucket(past - jnp.arange(width_s, dtype=jnp.int32))], -1, 0) * LOG2E
    cache_k4 = jnp.transpose(cache_k, (0, 1, 3, 4, 2)).reshape(depth, n_pool, feat, page)
    cache_v4 = jnp.transpose(cache_v, (0, 1, 3, 4, 2)).reshape(depth, n_pool, feat, page)
    cache_ik4 = jnp.swapaxes(cache_idx_k, 2, 3)

    def key_chunks(a):
        return jnp.pad(a, ((0, 0), (0, lk - lq), (0, 0))).astype(MXU_DTYPE).reshape(bsz, nc, KC, LANE)

    def value_chunks_t(a):
        a = jnp.pad(a.reshape(N_KV_HEADS, HEAD_DIM, bsz, lq), ((0, 0), (0, 0), (0, 0), (0, lk - lq)))
        a = jnp.concatenate([a, jnp.ones((N_KV_HEADS, V_ROWS - HEAD_DIM, bsz, lk), a.dtype)], axis=1)
        return jnp.transpose(a.reshape(N_KV_HEADS * V_ROWS, bsz, nc, KC), (1, 2, 0, 3))

    kp, vp, ikp, cp, ksm, vsm, iks, cs = ([] for _ in range(8))
    yp = ys = None
    for l in range(depth):
        w = _prep_w_in(w_in[l])
        wc = w_out[l, :C_CONV].astype(MXU_DTYPE)
        wa = w_out[l, C_CONV:].astype(MXU_DTYPE)
        last = l == depth - 1

        u, gc, kv, ga, tail, qzt, iqzt, vt, iwt = _project(xp, norm_g[l], w, tm, feature_major=True)
        b3 = lambda a: a.reshape(bsz, lq, a.shape[-1])
        u3, kv3, tail3 = b3(u), b3(kv), b3(tail)
        mixc = _conv_branch(u3, b3(gc), conv_w[l], conv_b[l], conv_ln_g[l], conv_ln_b[l])
        lane = jnp.arange(LANE)
        ik = key_chunks(jnp.where(lane < IDX_DIM, tail3, 0.0))
        mixa = _attend(iqzt.reshape(N_IDX_HEADS, LANE, bsz * lq), iwt, qzt.reshape(N_HEADS, LANE, bsz * lq), b3(ga),
                       ik, key_chunks(kv3[..., :feat]), value_chunks_t(vt), toep_t, kk_p)
        res = _out_proj(xp, mixc.reshape(bsz * lq, C_CONV), mixa.reshape(bsz * lq, -1), wc, wa, tm,
                        final_norm_g if last else None)
        xp, yp = res if last else (res, None)
        kp.append(kv3[:, :lp, :feat].reshape(bsz, lp, N_KV_HEADS, HEAD_DIM))
        vp.append(kv3[:, :lp, feat:].reshape(bsz, lp, N_KV_HEADS, HEAD_DIM))
        ikp.append(tail3[:, :lp, :IDX_DIM])
        cp.append(u3[:, lp - (CONV_W - 1):lp])

        u, gc, kv, ga, tail, q, iq = _project(xs, norm_g[l], w, db, feature_major=False)
        mixc = _sample_conv(jnp.transpose(state_conv[l], (1, 0, 2)), u, gc,
                            conv_w[l], conv_b[l], conv_ln_g[l], conv_ln_b[l])
        iw3 = tail[:, IDX_DIM:IDX_DIM + N_IDX_HEADS].reshape(db, N_IDX_HEADS, 1)
        scores = _dec_scores(page_table, iq.reshape(db, N_IDX_HEADS, IDX_DIM), iw3, tail.reshape(db, 1, LANE),
                             cache_ik4, l, width_s)
        madd = _dec_select(scores.reshape(db, width_s).T, kk_s).T
        q4 = q.reshape(db, N_KV_HEADS, GROUP, HEAD_DIM)
        qz3 = jnp.concatenate([jnp.pad(q4[:, 0], ((0, 0), (0, 0), (0, HEAD_DIM))),
                               jnp.pad(q4[:, 1], ((0, 0), (0, 0), (HEAD_DIM, 0)))], axis=1)
        mixa = _dec_attend(page_table, qz3, kv.reshape(db, 1, 2 * feat),
                           madd.reshape(db, 1, width_s), bias_s, ga.reshape(db, N_HEADS // 2, LANE),
                           cache_k4, cache_v4, l)
        res = _out_proj(xs, mixc, mixa.reshape(db, -1), wc, wa, db, final_norm_g if last else None)
        xs, ys = res if last else (res, None)
        ksm.append(kv[:, :feat].reshape(db, 1, N_KV_HEADS, HEAD_DIM))
        vsm.append(kv[:, feat:].reshape(db, 1, N_KV_HEADS, HEAD_DIM))
        iks.append(tail[:, None, :IDX_DIM])
        cs.append(jnp.concatenate([state_conv[l][:, 1:], u[:, None]], axis=1))

    y_prompt = yp.reshape(bsz, lq, d)[:, N_META:lp]
    y_sample = ys.reshape(db, 1, d)
    return (y_prompt, y_sample, jnp.stack(kp), jnp.stack(vp), jnp.stack(ikp), jnp.stack(cp),
            jnp.stack(ksm), jnp.stack(vsm), jnp.stack(iks), jnp.stack(cs))
```

```python
import functools
import math

import numpy as np
import jax
import jax.numpy as jnp
from jax import lax
from jax.experimental import pallas as pl
from jax.experimental.pallas import tpu as pltpu

N_HEADS = 8
N_KV_HEADS = 2
GROUP = N_HEADS // N_KV_HEADS
HEAD_DIM = 64
N_IDX_HEADS = 4
IDX_DIM = 64
C_CONV = 512
CONV_W = 31
TOPK_MAX = 256
N_META = 16
NUM_BUCKETS = 32
MAX_DISTANCE = 128
EPS = 1e-6
LN_EPS = 1e-5

LANE = 128
ROWS = 8
TQ = 128
KC = 512
SUB = KC // TQ
N_ACC = 4
DIGIT_BITS = 7
DIGIT_MASK = (1 << DIGIT_BITS) - 1
KEY_DIGITS = (7, 7, 7, 7, 4)
KEY_SHIFTS = tuple(32 - sum(KEY_DIGITS[:i + 1]) for i in range(len(KEY_DIGITS)))
N_LEVELS = len(KEY_DIGITS)
V_ROWS = HEAD_DIM + 16
HALO = 32
CONV_SUB = 64
MXU_DTYPE = jnp.bfloat16
NEG = -1e30
LOG2E = 1.4426950408889634
INT_MIN = -2 ** 31
TIE_SPAN = 1 << 14
KEY_NEG_INF = -0x7F800000 - TIE_SPAN
VMEM_LIMIT = 56 * 1024 * 1024

_PROJ_SIZES = (C_CONV, C_CONV, C_CONV, N_HEADS * HEAD_DIM, N_KV_HEADS * HEAD_DIM, N_KV_HEADS * HEAD_DIM,
               N_HEADS * HEAD_DIM, N_IDX_HEADS * IDX_DIM, IDX_DIM, N_IDX_HEADS)
_UB_OFF, _GC_OFF, _Q_OFF, _K_OFF, _V_OFF, _GA_OFF, _IQ_OFF, _IK_OFF, _IW_OFF, _D_IN = (
    int(s) for s in np.cumsum(_PROJ_SIZES))
_W_ROWS = _IK_OFF + LANE


def _cparams(n_axes):
    return pltpu.CompilerParams(dimension_semantics=("arbitrary",) * n_axes, vmem_limit_bytes=VMEM_LIMIT)


def _silu(x):
    return x * jax.nn.sigmoid(x)


def _prep_w_in(w):
    assert w.shape[1] == _D_IN
    return jnp.pad(w.T, ((0, _W_ROWS - _D_IN), (0, 0))).astype(MXU_DTYPE)


def _t5_bucket(rel):
    n = jnp.maximum(rel, 0)
    max_exact = NUM_BUCKETS // 2
    large = max_exact + (jnp.log(jnp.maximum(n, 1).astype(jnp.float32) / max_exact)
                         / math.log(MAX_DISTANCE / max_exact)
                         * (NUM_BUCKETS - max_exact)).astype(jnp.int32)
    large = jnp.minimum(large, NUM_BUCKETS - 1)
    return jnp.where(n < max_exact, n, large)


def _bias_tiles(rel_bias):
    far = rel_bias[NUM_BUCKETS - 1]
    d = jnp.arange(2 * TQ, dtype=jnp.int32)
    dists = (jnp.where(d < TQ, d, 0),
             jnp.where(d < TQ, d + TQ, d - TQ))
    tiles = []
    for dist in dists:
        v = ((rel_bias[_t5_bucket(dist)] - far) * LOG2E).T
        rep = jnp.tile(v, (1, TQ))[:, :TQ * (2 * TQ - 1)].reshape(N_HEADS, TQ, 2 * TQ - 1)
        tiles.append(rep[:, :, :TQ])
    tiles.append(jnp.zeros_like(tiles[0]))
    return jnp.stack(tiles).astype(jnp.float32)


_NT = (((1,), (1,)), ((), ()))
_Q_SCALE = HEAD_DIM ** -0.5 * LOG2E
_IQ_SCALE = IDX_DIM ** -0.5
_IW_SCALE = N_IDX_HEADS ** -0.5
_IW_ROWS = 16


def _proj_kernel(x_ref, g_ref, w_ref, u_ref, gc_ref, kv_ref, ga_ref, tail_ref, *q_refs, feature_major):
    x = x_ref[...]
    ms = jnp.mean(x * x, axis=-1, keepdims=True)
    xn = (x * lax.rsqrt(ms + EPS) * g_ref[...]).astype(MXU_DTYPE)

    def mm(lo, hi):
        return lax.dot_general(xn, w_ref[lo:hi, :], _NT, preferred_element_type=jnp.float32)

    def mm_t(lo, hi):
        return lax.dot_general(w_ref[lo:hi, :], xn, _NT, preferred_element_type=jnp.float32)

    u_ref[...] = mm(0, _UB_OFF) * jax.nn.sigmoid(mm(_UB_OFF, _GC_OFF))
    gc_ref[...] = mm(_GC_OFF, _Q_OFF)
    kv_ref[...] = mm(_K_OFF, _GA_OFF)
    ga_ref[...] = mm(_GA_OFF, _IQ_OFF)
    t = mm(_IK_OFF, _W_ROWS)
    lane = lax.broadcasted_iota(jnp.int32, t.shape, 1)
    tail_ref[...] = t * jnp.where(lane >= IDX_DIM, _IW_SCALE, 1.0)
    if feature_major:
        qzt_ref, iqzt_ref, vt_ref, iwt_ref = q_refs
        qt = (mm_t(_Q_OFF, _K_OFF) * _Q_SCALE).astype(qzt_ref.dtype)
        qzt_ref[...] = jnp.zeros(qzt_ref.shape, qzt_ref.dtype)
        for h in range(N_HEADS):
            lo = h * LANE + (h // GROUP) * HEAD_DIM
            qzt_ref[lo:lo + HEAD_DIM, :] = qt[h * HEAD_DIM:(h + 1) * HEAD_DIM]
        iqt = (mm_t(_IQ_OFF, _IK_OFF) * _IQ_SCALE).astype(iqzt_ref.dtype)
        iqzt_ref[...] = jnp.zeros(iqzt_ref.shape, iqzt_ref.dtype)
        for h in range(N_IDX_HEADS):
            iqzt_ref[h * LANE:h * LANE + IDX_DIM, :] = iqt[h * IDX_DIM:(h + 1) * IDX_DIM]
        vt_ref[...] = mm_t(_V_OFF, _GA_OFF).astype(vt_ref.dtype)
        iwt_ref[...] = mm_t(_IW_OFF, _IW_OFF + _IW_ROWS) * _IW_SCALE
    else:
        q_ref, iq_ref = q_refs
        q_ref[...] = (mm(_Q_OFF, _K_OFF) * _Q_SCALE).astype(q_ref.dtype)
        iq_ref[...] = (mm(_IQ_OFF, _IK_OFF) * _IQ_SCALE).astype(iq_ref.dtype)


def _project(x, g, w, tm, feature_major):
    r, d = x.shape
    assert r % tm == 0
    rows = lambda c: pl.BlockSpec((tm, c), lambda i: (i, 0))
    cols = lambda c: pl.BlockSpec((c, tm), lambda i: (0, i))
    feat = N_KV_HEADS * HEAD_DIM
    specs = [rows(C_CONV), rows(C_CONV), rows(2 * feat), rows(N_HEADS * HEAD_DIM), rows(LANE)]
    shapes = [jax.ShapeDtypeStruct((r, s.block_shape[1]), jnp.float32) for s in specs]
    if feature_major:
        extra = [(N_HEADS * LANE, MXU_DTYPE), (N_IDX_HEADS * LANE, MXU_DTYPE), (feat, MXU_DTYPE),
                 (_IW_ROWS, jnp.float32)]
        specs += [cols(c) for c, _ in extra]
        shapes += [jax.ShapeDtypeStruct((c, r), t) for c, t in extra]
    else:
        extra = [(N_HEADS * HEAD_DIM, MXU_DTYPE), (N_IDX_HEADS * IDX_DIM, MXU_DTYPE)]
        specs += [rows(c) for c, _ in extra]
        shapes += [jax.ShapeDtypeStruct((r, c), t) for c, t in extra]
    return pl.pallas_call(
        functools.partial(_proj_kernel, feature_major=feature_major),
        grid=(r // tm,),
        in_specs=[rows(d), pl.BlockSpec((1, d), lambda i: (0, 0)), pl.BlockSpec((_W_ROWS, d), lambda i: (0, 0))],
        out_specs=specs,
        out_shape=shapes,
        compiler_params=_cparams(1),
        name="proj",
    )(x, g.reshape(1, d), w)


def _conv_kernel(prev_ref, cur_ref, gc_ref, w_ref, b_ref, lg_ref, lb_ref, o_ref, ext_ref):
    i = pl.program_id(1)
    ext_ref[0, 0:HALO, :] = jnp.where(i > 0, prev_ref[...], 0.0)
    ext_ref[0, HALO:HALO + TQ, :] = cur_ref[...]
    n_ext = HALO + TQ
    for s in range(1, ROWS):
        ext_ref[s, 0:n_ext - ROWS, :] = ext_ref[0, s:s + n_ext - ROWS, :]
    off = HALO - (CONV_W - 1)
    for r0 in range(0, TQ, CONV_SUB):
        acc = jnp.zeros((CONV_SUB, C_CONV), jnp.float32)
        for j in range(CONV_W):
            s = (off + j) % ROWS
            lo = r0 + off + j - s
            acc = acc + w_ref[j:j + 1, :] * ext_ref[s, lo:lo + CONV_SUB, :]
        y = acc + b_ref[...]
        mu = jnp.mean(y, axis=-1, keepdims=True)
        dev = y - mu
        var = jnp.mean(dev * dev, axis=-1, keepdims=True)
        yn = dev * lax.rsqrt(var + LN_EPS) * lg_ref[...] + lb_ref[...]
        o_ref[r0:r0 + CONV_SUB, :] = (_silu(yn) * _silu(gc_ref[r0:r0 + CONV_SUB, :])).astype(o_ref.dtype)


def _conv_branch(u, gc, w, b, lg, lb):
    bsz, lq, c = u.shape
    per = TQ // HALO
    row = lambda a: a.reshape(1, c)
    vec = pl.BlockSpec((1, c), lambda bi, i: (0, 0))
    tile = pl.BlockSpec((None, TQ, c), lambda bi, i: (bi, i, 0))
    return pl.pallas_call(
        _conv_kernel,
        grid=(bsz, lq // TQ),
        in_specs=[pl.BlockSpec((None, HALO, c), lambda bi, i: (bi, jnp.maximum(i * per - 1, 0), 0)),
                  tile, tile, pl.BlockSpec((CONV_W, c), lambda bi, i: (0, 0)), vec, vec, vec],
        out_specs=tile,
        out_shape=jax.ShapeDtypeStruct((bsz, lq, c), MXU_DTYPE),
        scratch_shapes=[pltpu.VMEM((ROWS, HALO + TQ, c), jnp.float32)],
        compiler_params=_cparams(2),
        name="conv",
    )(u, u, gc, w, row(b), row(lg), row(lb))


_INT_MIN = np.int32(INT_MIN)
_BYTE_ONES = np.int32(0x01010101)
_BYTE_LOW = np.int32(0x7F7F7F7F)
_HALF_LOW_BYTES = np.int32(0x00FF00FF)
_GUARD = np.int32(0x80808080 - (1 << 32))
assert SUB == 4 and DIGIT_BITS == 7


def _to_key(s, idx):
    bits = lax.bitcast_convert_type(s, jnp.int32)
    key = jnp.where(bits < 0, (_INT_MIN - bits) - TIE_SPAN, bits)
    return jnp.where((bits == 0) | (bits == _INT_MIN), -1 - idx, key)


def _for_chunks(nkc, body, init, unroll):
    n_main = nkc // unroll

    def main(i, carry):
        for k in range(unroll):
            carry = body(i * unroll + k, carry)
        return carry

    carry = lax.fori_loop(0, n_main, main, init)
    return lax.fori_loop(n_main * unroll, nkc, body, carry)


def _pack_digits(keys_ref, dig_ref, c):
    for lv in range(N_LEVELS):
        word = None
        for j in range(SUB):
            u = keys_ref[c, j * TQ:(j + 1) * TQ, :]
            if lv == 0:
                u = u ^ _INT_MIN
            move = KEY_SHIFTS[lv] - 8 * j
            u = lax.shift_right_logical(u, move) if move >= 0 else lax.shift_left(u, -move)
            field = u & np.int32(((1 << KEY_DIGITS[lv]) - 1) << (8 * j))
            word = field if word is None else word | field
        dig_ref[lv, c] = word


def _count_fields(work_ref, nkc, cand_bytes):
    q = work_ref.shape[2]

    def body(c, accs):
        accs = list(accs)
        for n, r in enumerate(range(0, TQ, ROWS)):
            diff = work_ref[c, r:r + ROWS, :] - cand_bytes
            accs[n % N_ACC] = accs[n % N_ACC] + (lax.shift_right_logical(diff, DIGIT_BITS) & _BYTE_ONES)
        return tuple(accs)

    accs = _for_chunks(nkc, body, (jnp.zeros((ROWS, q), jnp.int32),) * N_ACC, unroll=4)
    halves = jnp.zeros((ROWS, q), jnp.int32)
    for a in accs:
        halves = halves + (a & _HALF_LOW_BYTES) + (lax.shift_right_logical(a, 8) & _HALF_LOW_BYTES)
    total = (halves & 0xFFFF) + lax.shift_right_logical(halves, 16)
    return jnp.sum(total.astype(jnp.float32), axis=0, keepdims=True)


def _digit_search(dig_ref, work_ref, alive_ref, nkc, shifts, digits, target):
    q = dig_ref.shape[3]
    assert dig_ref.shape[1] * (TQ // ROWS) <= 255 * N_ACC
    zero = jnp.zeros((1, q), jnp.float32)
    value = jnp.zeros((1, q), jnp.int32)
    above, done, cnt_ge, cnt_gt = zero, zero, target + 1.0, zero

    for lv, (shift, n_bits) in enumerate(zip(shifts, digits)):
        def load_fields(c, carry):
            work_ref[c] = (dig_ref[lv, c] & alive_ref[c]) | _GUARD
            return carry

        _for_chunks(nkc, load_fields, 0, unroll=4)

        def digit_bit(i, state):
            digit, cnt_ge, cnt_gt, cnt_rej, done = state
            cand = digit | lax.shift_left(jnp.int32(1), n_bits - 1 - i)
            cnt_alive = _count_fields(work_ref, nkc, cand * _BYTE_ONES)
            cnt = above + cnt_alive
            take = (cnt >= target) & (done == 0.0)
            drop = (cnt < target) & (done == 0.0)
            return (jnp.where(take, cand, digit), jnp.where(take, cnt, cnt_ge), jnp.where(drop, cnt, cnt_gt),
                    jnp.where(drop, cnt_alive, cnt_rej),
                    jnp.where(take & (cnt == target), 1.0, done))

        digit, cnt_ge, cnt_gt, cnt_rej, done = lax.fori_loop(
            0, n_bits, digit_bit, (jnp.zeros((1, q), jnp.int32), cnt_ge, cnt_gt, zero, done))
        above = above + cnt_rej
        value = value | lax.shift_left(digit, shift)

        if lv + 1 < len(shifts):
            digit_bytes = digit * _BYTE_ONES

            def narrow(c, carry):
                differs = lax.shift_right_logical(((dig_ref[lv, c] ^ digit_bytes) + _BYTE_LOW) & _GUARD, DIGIT_BITS)
                alive_ref[c] = alive_ref[c] & ((_BYTE_ONES - differs) * DIGIT_MASK)
                return carry

            _for_chunks(nkc, narrow, 0, unroll=4)

    return value, cnt_ge, cnt_gt, done


def _fill(ref, nkc, word):
    def body(c, carry):
        ref[c] = jnp.full(ref.shape[1:], word, ref.dtype)
        return carry

    _for_chunks(nkc, body, 0, unroll=4)


def _threshold(keys_ref, dig_ref, work_ref, alive_ref, nkc, kk):
    q = keys_ref.shape[2]
    kf = jnp.full((1, q), kk, jnp.float32)
    _fill(alive_ref, nkc, _BYTE_LOW)
    prefix, cnt_thr, cnt_gt, _ = _digit_search(dig_ref, work_ref, alive_ref, nkc, KEY_SHIFTS, KEY_DIGITS, kf)
    thr = prefix ^ _INT_MIN
    admissible = thr != KEY_NEG_INF
    surplus = (cnt_thr > kf) & admissible

    @pl.when(jnp.max(jnp.where(surplus, 1.0, 0.0)) > 0.0)
    def _():
        assert keys_ref.shape[0] * KC <= 1 << (2 * DIGIT_BITS)
        top = (1 << (2 * DIGIT_BITS)) - 1
        row = lax.broadcasted_iota(jnp.int32, (TQ, q), 0)

        def tie_fields(c, carry):
            hi = lo = tie = None
            for j in range(SUB):
                is_tie = keys_ref[c, j * TQ:(j + 1) * TQ, :] == thr
                rev = top - (c * KC + j * TQ + row)
                fields = [jnp.where(is_tie, f, 0) for f in
                          (lax.shift_right_logical(rev, DIGIT_BITS), rev & DIGIT_MASK, DIGIT_MASK)]
                fields = [f if j == 0 else lax.shift_left(f, 8 * j) for f in fields]
                hi, lo, tie = fields if j == 0 else (hi | fields[0], lo | fields[1], tie | fields[2])
            dig_ref[0, c], dig_ref[1, c], alive_ref[c] = hi, lo, tie
            return carry

        lax.fori_loop(0, nkc, tie_fields, 0)
        cut, _, _, _ = _digit_search(dig_ref, work_ref, alive_ref, nkc, (DIGIT_BITS, 0), (DIGIT_BITS, DIGIT_BITS),
                                     kf - cnt_gt)

        def drop_losers(c, carry):
            for j in range(SUB):
                rows = slice(j * TQ, (j + 1) * TQ)
                x = keys_ref[c, rows, :]
                lost = (x == thr) & (top - (c * KC + j * TQ + row) < cut)
                keys_ref[c, rows, :] = jnp.where(lost, thr - 1, x)
            return carry

        lax.fori_loop(0, nkc, drop_losers, 0)

    return jnp.where(admissible, thr, KEY_NEG_INF + 1)


def _attend_kernel(iqt_ref, iwt_ref, qzt_ref, ga_ref, ik_ref, k_ref, vt_ref, toep_ref, o_ref,
                   keys_ref, dig_ref, work_ref, alive_ref, m_ref, acc_ref, lg0_ref, lg1_ref,
                   wq_ref, *, kk):
    qi = pl.program_id(1)
    nkc = qi // SUB + 1
    qpos = qi * TQ + lax.broadcasted_iota(jnp.int32, (TQ, TQ), 1)
    krow = lax.broadcasted_iota(jnp.int32, (TQ, TQ), 0)
    for h in range(N_HEADS):
        wq_ref[:, h * TQ:(h + 1) * TQ] = qzt_ref[h]

    def score_chunk(c, carry, causal):
        iqt = jnp.concatenate([iqt_ref[h] for h in range(N_IDX_HEADS)], axis=1)
        w = iwt_ref[...]
        for j in range(SUB):
            st = jnp.dot(ik_ref[c, j * TQ:(j + 1) * TQ, :], iqt, preferred_element_type=jnp.float32)
            s = jnp.zeros((TQ, TQ), jnp.float32)
            for h in range(N_IDX_HEADS):
                s = s + w[h:h + 1, :] * jnp.maximum(st[:, h * TQ:(h + 1) * TQ], 0.0)
            kpos = c * KC + j * TQ + krow
            if causal:
                s = jnp.where(kpos <= qpos, s, -jnp.inf)
            keys_ref[c, j * TQ:(j + 1) * TQ, :] = _to_key(s, kpos)
        _pack_digits(keys_ref, dig_ref, c)
        return carry

    _for_chunks(nkc - 1, functools.partial(score_chunk, causal=False), 0, unroll=2)
    score_chunk(nkc - 1, 0, causal=True)
    thr = _threshold(keys_ref, dig_ref, work_ref, alive_ref, nkc, kk)

    m_ref[...] = jnp.full(m_ref.shape, NEG, jnp.float32)
    acc_ref[...] = jnp.zeros(acc_ref.shape, jnp.float32)

    def bias(c, h):
        return jnp.concatenate([toep_ref[jnp.clip(qi - (c * SUB + j), 0, 2), h] for j in range(SUB)], axis=0)

    def logits(c, lg_ref, near):
        md = jnp.where(keys_ref[c] >= thr, 0.0, NEG)
        add = [md + bias(c, h) for h in range(N_HEADS)] if near else [md] * N_HEADS
        lg_ref[...] = (jnp.dot(k_ref[c], wq_ref[...], preferred_element_type=jnp.float32)
                       + jnp.concatenate(add, axis=1))

    def add_bias(c, lg_ref):
        for h in range(N_HEADS):
            sl = slice(h * TQ, (h + 1) * TQ)
            lg_ref[:, sl] = lg_ref[:, sl] + bias(c, h)

    def softmax_pv(c, lg_ref):
        for pr in range(N_HEADS // 2):
            kv = pr // (GROUP // 2)
            sl = slice(2 * pr * TQ, (2 * pr + 2) * TQ)
            m_old = m_ref[pr]
            m_new = jnp.maximum(m_old, jnp.max(lg_ref[:, sl], axis=0, keepdims=True))
            alpha = jnp.exp2(m_old - m_new)
            p = jnp.exp2(lg_ref[:, sl] - m_new)
            pv = jnp.dot(vt_ref[c, kv * V_ROWS:(kv + 1) * V_ROWS, :], p.astype(MXU_DTYPE),
                         preferred_element_type=jnp.float32)
            acc_ref[pr] = alpha * acc_ref[pr] + pv
            m_ref[pr] = m_new

    n_far2 = (jnp.maximum(qi - 1, 0) // SUB) // 2

    def far_pair(i, carry):
        logits(2 * i + 1, lg1_ref, near=False)
        softmax_pv(2 * i, lg0_ref)
        logits(2 * i + 2, lg0_ref, near=False)
        softmax_pv(2 * i + 1, lg1_ref)
        return carry

    logits(0, lg0_ref, near=False)
    lax.fori_loop(0, n_far2, far_pair, 0)

    c0 = 2 * n_far2
    add_bias(c0, lg0_ref)

    @pl.when(c0 + 1 < nkc)
    def _():
        logits(c0 + 1, lg1_ref, near=True)
        softmax_pv(c0, lg0_ref)

    @pl.when(c0 + 1 >= nkc)
    def _():
        softmax_pv(c0, lg0_ref)

    @pl.when(c0 + 2 < nkc)
    def _():
        logits(c0 + 2, lg0_ref, near=True)
        softmax_pv(c0 + 1, lg1_ref)
        softmax_pv(c0 + 2, lg0_ref)

    @pl.when((c0 + 1 < nkc) & (c0 + 2 >= nkc))
    def _():
        softmax_pv(c0 + 1, lg1_ref)

    for pr in range(N_HEADS // 2):
        acc = acc_ref[pr]
        o_t = acc[:HEAD_DIM] / acc[HEAD_DIM:HEAD_DIM + 1]
        pair_t = jnp.concatenate([o_t[:, :TQ], o_t[:, TQ:]], axis=0)
        sl = slice(pr * LANE, (pr + 1) * LANE)
        o_ref[:, sl] = (pair_t.T * _silu(ga_ref[:, sl])).astype(o_ref.dtype)


def _attend(iqt, iwt, qzt, ga, ik, k, vt, toep_t, kk):
    bsz, lq = ga.shape[:2]
    nq = lq // TQ
    nc = k.shape[1]
    lanes = lambda a: pl.BlockSpec(a.shape[:-1] + (TQ,), lambda b, i: (0,) * (a.ndim - 1) + (b * nq + i,))
    whole = lambda a: pl.BlockSpec((None,) + a.shape[1:], lambda b, i: (b,) + (0,) * (a.ndim - 1))
    rows = pl.BlockSpec((None, TQ, N_HEADS * HEAD_DIM), lambda b, i: (b, i, 0))
    return pl.pallas_call(
        functools.partial(_attend_kernel, kk=kk),
        grid=(bsz, nq),
        in_specs=[lanes(iqt), lanes(iwt), lanes(qzt), rows, whole(ik), whole(k), whole(vt),
                  pl.BlockSpec(toep_t.shape, lambda b, i: (0, 0, 0, 0))],
        out_specs=rows,
        out_shape=jax.ShapeDtypeStruct((bsz, lq, N_HEADS * HEAD_DIM), MXU_DTYPE),
        scratch_shapes=[pltpu.VMEM((nc, KC, TQ), jnp.int32),
                        pltpu.VMEM((N_LEVELS, nc, TQ, TQ), jnp.int32), pltpu.VMEM((nc, TQ, TQ), jnp.int32),
                        pltpu.VMEM((nc, TQ, TQ), jnp.int32),
                        pltpu.VMEM((N_HEADS // 2, 1, 2 * TQ), jnp.float32),
                        pltpu.VMEM((N_HEADS // 2, V_ROWS, 2 * TQ), jnp.float32),
                        pltpu.VMEM((KC, N_HEADS * TQ), jnp.float32), pltpu.VMEM((KC, N_HEADS * TQ), jnp.float32),
                        pltpu.VMEM((LANE, N_HEADS * TQ), MXU_DTYPE)],
        compiler_params=_cparams(2),
        name="attend",
    )(iqt, iwt, qzt, ga, ik, k, vt, toep_t)


def _out_kernel(x_ref, mc_ref, ma_ref, wc_ref, wa_ref, o_ref):
    o_ref[...] = (x_ref[...]
                  + jnp.dot(mc_ref[...], wc_ref[...], preferred_element_type=jnp.float32)
                  + jnp.dot(ma_ref[...], wa_ref[...], preferred_element_type=jnp.float32))


def _out_final_kernel(x_ref, mc_ref, ma_ref, wc_ref, wa_ref, g_ref, o_ref, y_ref):
    x = (x_ref[...]
         + jnp.dot(mc_ref[...], wc_ref[...], preferred_element_type=jnp.float32)
         + jnp.dot(ma_ref[...], wa_ref[...], preferred_element_type=jnp.float32))
    o_ref[...] = x
    ms = jnp.mean(x * x, axis=-1, keepdims=True)
    y_ref[...] = x * lax.rsqrt(ms + EPS) * g_ref[...]


def _out_proj(x, mc, ma, wc, wa, tm, final_g=None):
    r, d = x.shape
    c = mc.shape[1]
    rows = lambda w: pl.BlockSpec((tm, w), lambda i: (i, 0))
    full = lambda a: pl.BlockSpec(a.shape, lambda i: (0, 0))
    if final_g is None:
        return pl.pallas_call(
            _out_kernel, grid=(r // tm,),
            in_specs=[rows(d), rows(c), rows(c), full(wc), full(wa)],
            out_specs=rows(d), out_shape=jax.ShapeDtypeStruct((r, d), jnp.float32),
            compiler_params=_cparams(1), name="out",
        )(x, mc, ma, wc, wa)
    g = final_g.reshape(1, d)
    return pl.pallas_call(
        _out_final_kernel, grid=(r // tm,),
        in_specs=[rows(d), rows(c), rows(c), full(wc), full(wa), full(g)],
        out_specs=[rows(d), rows(d)], out_shape=[jax.ShapeDtypeStruct((r, d), jnp.float32)] * 2,
        compiler_params=_cparams(1), name="out_final",
    )(x, mc, ma, wc, wa, g)


def _sconv_kernel(st_ref, u_ref, gc_ref, w_ref, b_ref, lg_ref, lb_ref, o_ref):
    acc = jnp.zeros(u_ref.shape, jnp.float32)
    for j in range(CONV_W - 1):
        acc = acc + w_ref[j:j + 1, :] * st_ref[j]
    acc = acc + w_ref[CONV_W - 1:CONV_W, :] * u_ref[...]
    y = acc + b_ref[...]
    mu = jnp.mean(y, axis=-1, keepdims=True)
    dev = y - mu
    var = jnp.mean(dev * dev, axis=-1, keepdims=True)
    yn = dev * lax.rsqrt(var + LN_EPS) * lg_ref[...] + lb_ref[...]
    o_ref[...] = (_silu(yn) * _silu(gc_ref[...])).astype(o_ref.dtype)


def _sample_conv(state_t, u, gc, w, b, lg, lb):
    db, c = u.shape
    row = lambda a: a.reshape(1, c)
    full = lambda a: pl.BlockSpec(a.shape, lambda i: (0,) * a.ndim)
    args = (state_t, u, gc, w, row(b), row(lg), row(lb))
    return pl.pallas_call(
        _sconv_kernel, grid=(1,), in_specs=[full(a) for a in args],
        out_specs=pl.BlockSpec((db, c), lambda i: (0, 0)),
        out_shape=jax.ShapeDtypeStruct((db, c), MXU_DTYPE),
        compiler_params=_cparams(1), name="sconv",
    )(*args)


def _dec_score_kernel(pt_ref, iq_ref, iw_ref, ikn_ref, cik_ref, o_ref, buf_ref, sem_ref, *, layer, width):
    slot = _fetch_pages(pt_ref, (cik_ref,), (buf_ref,), (sem_ref,), layer)
    iq = iq_ref[...][:, :IDX_DIM]
    w = iw_ref[...]
    ikt = buf_ref[slot].astype(MXU_DTYPE)
    s = jnp.dot(iq, ikt, preferred_element_type=jnp.float32)
    sc = jnp.sum(w * jnp.maximum(s, 0.0), axis=0, keepdims=True)
    ikn = ikn_ref[...][:, :IDX_DIM].astype(MXU_DTYPE).astype(jnp.float32)
    sn = jnp.sum(iq.astype(jnp.float32) * ikn, axis=-1, keepdims=True)
    scn = jnp.sum(w * jnp.maximum(sn, 0.0), axis=0, keepdims=True)
    lane = lax.broadcasted_iota(jnp.int32, (1, width - sc.shape[1]), 1)
    o_ref[...] = jnp.concatenate([sc, jnp.where(lane == 0, scn, -jnp.inf)], axis=1)


def _dec_scores(page_table, iq3, iw3, tail3, cache_ik, layer, width):
    db, n_pages = page_table.shape
    di, page = cache_ik.shape[2:]
    seq = lambda a: pl.BlockSpec((None,) + a.shape[1:], lambda b, pt: (b, 0, 0))
    return pl.pallas_call(
        functools.partial(_dec_score_kernel, layer=layer, width=width),
        grid_spec=pltpu.PrefetchScalarGridSpec(
            num_scalar_prefetch=1, grid=(db,),
            in_specs=[seq(iq3), seq(iw3), seq(tail3), pl.BlockSpec(memory_space=pl.ANY)],
            out_specs=pl.BlockSpec((None, 1, width), lambda b, pt: (b, 0, 0)),
            scratch_shapes=[pltpu.VMEM((2, di, n_pages * page), cache_ik.dtype), pltpu.SemaphoreType.DMA((2,))]),
        out_shape=jax.ShapeDtypeStruct((db, 1, width), jnp.float32),
        compiler_params=_cparams(1), name="dec_score",
    )(page_table, iq3, iw3, tail3, cache_ik)


def _dec_select_kernel(s_ref, o_ref, keys_ref, dig_ref, work_ref, alive_ref, *, kk):
    nc = keys_ref.shape[0]
    row = lax.broadcasted_iota(jnp.int32, (KC, keys_ref.shape[2]), 0)
    for c in range(nc):
        keys_ref[c] = _to_key(s_ref[c * KC:(c + 1) * KC, :], c * KC + row)
        _pack_digits(keys_ref, dig_ref, c)
    thr = _threshold(keys_ref, dig_ref, work_ref, alive_ref, nc, kk)
    for c in range(nc):
        o_ref[c * KC:(c + 1) * KC, :] = jnp.where(keys_ref[c] >= thr, 0.0, NEG)


def _dec_select(scores_t, kk):
    width, db = scores_t.shape
    nc = width // KC
    spec = pl.BlockSpec((width, db), lambda i: (0, 0))
    return pl.pallas_call(
        functools.partial(_dec_select_kernel, kk=kk),
        grid=(1,), in_specs=[spec], out_specs=spec,
        out_shape=jax.ShapeDtypeStruct((width, db), jnp.float32),
        scratch_shapes=[pltpu.VMEM((nc, KC, db), jnp.int32),
                        pltpu.VMEM((N_LEVELS, nc, TQ, db), jnp.int32), pltpu.VMEM((nc, TQ, db), jnp.int32),
                        pltpu.VMEM((nc, TQ, db), jnp.int32)],
        compiler_params=_cparams(1), name="dec_select",
    )(scores_t)


def _page_copies(pt_ref, cache_ref, buf_ref, sem_ref, layer, seq, slot):
    page = cache_ref.shape[3]
    n_pages = buf_ref.shape[2] // page
    return [pltpu.make_async_copy(cache_ref.at[layer, pt_ref[seq, p]],
                                  buf_ref.at[slot, :, pl.ds(p * page, page)], sem_ref.at[slot])
            for p in range(n_pages)]


def _fetch_pages(pt_ref, caches, bufs, sems, layer):
    b, nb = pl.program_id(0), pl.num_programs(0)

    def start(seq, slot):
        for cache_ref, buf_ref, sem_ref in zip(caches, bufs, sems):
            for cp in _page_copies(pt_ref, cache_ref, buf_ref, sem_ref, layer, seq, slot):
                cp.start()

    @pl.when(b == 0)
    def _():
        start(0, 0)

    @pl.when(b + 1 < nb)
    def _():
        start(b + 1, (b + 1) % 2)

    slot = b % 2
    for cache_ref, buf_ref, sem_ref in zip(caches, bufs, sems):
        for cp in _page_copies(pt_ref, cache_ref, buf_ref, sem_ref, layer, b, slot):
            cp.wait()
    return slot


def _dec_attend_kernel(pt_ref, qz_ref, kvn_ref, madd_ref, bias_ref, ga_ref, ck_ref, cv_ref, o_ref,
                       kbuf_ref, vbuf_ref, ksem_ref, vsem_ref, *, layer):
    slot = _fetch_pages(pt_ref, (ck_ref, cv_ref), (kbuf_ref, vbuf_ref), (ksem_ref, vsem_ref), layer)
    qz = qz_ref[...]
    past = kbuf_ref.shape[2]
    kt = kbuf_ref[slot].astype(MXU_DTYPE)
    vt = vbuf_ref[slot].astype(MXU_DTYPE)
    lg = (jnp.dot(qz, kt, preferred_element_type=jnp.float32)
          + bias_ref[:, :past] + madd_ref[:, :past])
    kvn = kvn_ref[...].astype(MXU_DTYPE).astype(jnp.float32)
    lgn = jnp.sum(qz.astype(jnp.float32) * kvn[:, :LANE], axis=-1, keepdims=True)
    lgn = lgn + bias_ref[:, past:past + 1] + madd_ref[:, past:past + 1]
    m = jnp.maximum(lgn, jnp.max(lg, axis=-1, keepdims=True))
    pn = jnp.exp2(lgn - m)
    pp = jnp.exp2(lg - m)
    den = pn + jnp.sum(pp, axis=-1, keepdims=True)
    o = (pn.astype(MXU_DTYPE).astype(jnp.float32) * kvn[:, LANE:]
         + lax.dot_general(pp.astype(MXU_DTYPE), vt, _NT, preferred_element_type=jnp.float32))
    o = o / den
    swapped = pltpu.roll(o, HEAD_DIM, axis=1)
    lane = lax.broadcasted_iota(jnp.int32, (1, LANE), 1)
    pairs = []
    for j in range(N_HEADS // 2):
        first, second = (o, swapped) if 2 * j < GROUP else (swapped, o)
        pairs.append(jnp.where(lane < HEAD_DIM, first[2 * j:2 * j + 1], second[2 * j + 1:2 * j + 2]))
    o_ref[...] = (jnp.concatenate(pairs, axis=0) * _silu(ga_ref[...])).astype(o_ref.dtype)


def _dec_attend(page_table, qz3, kvn3, madd3, bias, ga3, cache_k, cache_v, layer):
    db, n_pages = page_table.shape
    feat, page = cache_k.shape[2:]
    seq = lambda a: pl.BlockSpec((None,) + a.shape[1:], lambda b, pt: (b, 0, 0))
    hbm = pl.BlockSpec(memory_space=pl.ANY)
    buf = pltpu.VMEM((2, feat, n_pages * page), cache_k.dtype)
    return pl.pallas_call(
        functools.partial(_dec_attend_kernel, layer=layer),
        grid_spec=pltpu.PrefetchScalarGridSpec(
            num_scalar_prefetch=1, grid=(db,),
            in_specs=[seq(qz3), seq(kvn3), seq(madd3), pl.BlockSpec(bias.shape, lambda b, pt: (0, 0)), seq(ga3),
                      hbm, hbm],
            out_specs=pl.BlockSpec((None, GROUP, LANE), lambda b, pt: (b, 0, 0)),
            scratch_shapes=[buf, buf, pltpu.SemaphoreType.DMA((2,)), pltpu.SemaphoreType.DMA((2,))]),
        out_shape=jax.ShapeDtypeStruct((db, GROUP, LANE), MXU_DTYPE),
        compiler_params=_cparams(1), name="dec_attend",
    )(page_table, qz3, kvn3, madd3, bias, ga3, cache_k, cache_v)


def _round_up(x, m):
    return -(-x // m) * m


def _row_tile(rows):
    for f in (5, 4, 3, 2, 1):
        if rows % (f * TQ) == 0:
            return f * TQ
    return rows


def kernel(x_prompt, x_sample, cache_k, cache_v, cache_idx_k, state_conv, page_table, meta_tokens, rel_bias,
           norm_g, w_in, conv_w, conv_b, conv_ln_g, conv_ln_b, w_out, final_norm_g):
    bsz, seq, d = x_prompt.shape
    depth = w_in.shape[0]
    lp = seq + N_META
    lq = _round_up(lp, TQ)
    lk = _round_up(lp, KC)
    nq, nc = lq // TQ, lk // KC
    kk_p = min(TOPK_MAX, lp // 4)
    db = x_sample.shape[0]
    n_pool, page = cache_k.shape[1:3]
    n_pages = page_table.shape[1]
    past = n_pages * page
    kk_s = min(TOPK_MAX, (past + 1) // 4)
    width_s = _round_up(past + 1, KC)
    feat = N_KV_HEADS * HEAD_DIM
    tm = _row_tile(bsz * lq)

    xp = jnp.concatenate([jnp.broadcast_to(meta_tokens[None].astype(x_prompt.dtype), (bsz, N_META, d)), x_prompt,
                          jnp.zeros((bsz, lq - lp, d), x_prompt.dtype)], axis=1).reshape(bsz * lq, d)
    xs = x_sample.reshape(db, d)
    toep_t = _bias_tiles(rel_bias)
    bias_s = jnp.moveaxis(rel_bias[_t5_bucket(past - jnp.arange(width_s, dtype=jnp.int32))], -1, 0) * LOG2E
    cache_k4 = jnp.transpose(cache_k, (0, 1, 3, 4, 2)).reshape(depth, n_pool, feat, page)
    cache_v4 = jnp.transpose(cache_v, (0, 1, 3, 4, 2)).reshape(depth, n_pool, feat, page)
    cache_ik4 = jnp.swapaxes(cache_idx_k, 2, 3)

    def key_chunks(a):
        return jnp.pad(a, ((0, 0), (0, lk - lq), (0, 0))).astype(MXU_DTYPE).reshape(bsz, nc, KC, LANE)

    def value_chunks_t(a):
        a = jnp.pad(a.reshape(N_KV_HEADS, HEAD_DIM, bsz, lq), ((0, 0), (0, 0), (0, 0), (0, lk - lq)))
        a = jnp.concatenate([a, jnp.ones((N_KV_HEADS, V_ROWS - HEAD_DIM, bsz, lk), a.dtype)], axis=1)
        return jnp.transpose(a.reshape(N_KV_HEADS * V_ROWS, bsz, nc, KC), (1, 2, 0, 3))

    kp, vp, ikp, cp, ksm, vsm, iks, cs = ([] for _ in range(8))
    yp = ys = None
    for l in range(depth):
        w = _prep_w_in(w_in[l])
        wc = w_out[l, :C_CONV].astype(MXU_DTYPE)
        wa = w_out[l, C_CONV:].astype(MXU_DTYPE)
        last = l == depth - 1

        u, gc, kv, ga, tail, qzt, iqzt, vt, iwt = _project(xp, norm_g[l], w, tm, feature_major=True)
        b3 = lambda a: a.reshape(bsz, lq, a.shape[-1])
        u3, kv3, tail3 = b3(u), b3(kv), b3(tail)
        mixc = _conv_branch(u3, b3(gc), conv_w[l], conv_b[l], conv_ln_g[l], conv_ln_b[l])
        lane = jnp.arange(LANE)
        ik = key_chunks(jnp.where(lane < IDX_DIM, tail3, 0.0))
        mixa = _attend(iqzt.reshape(N_IDX_HEADS, LANE, bsz * lq), iwt, qzt.reshape(N_HEADS, LANE, bsz * lq), b3(ga),
                       ik, key_chunks(kv3[..., :feat]), value_chunks_t(vt), toep_t, kk_p)
        res = _out_proj(xp, mixc.reshape(bsz * lq, C_CONV), mixa.reshape(bsz * lq, -1), wc, wa, tm,
                        final_norm_g if last else None)
        xp, yp = res if last else (res, None)
        kp.append(kv3[:, :lp, :feat].reshape(bsz, lp, N_KV_HEADS, HEAD_DIM))
        vp.append(kv3[:, :lp, feat:].reshape(bsz, lp, N_KV_HEADS, HEAD_DIM))
        ikp.append(tail3[:, :lp, :IDX_DIM])
        cp.append(u3[:, lp - (CONV_W - 1):lp])

        u, gc, kv, ga, tail, q, iq = _project(xs, norm_g[l], w, db, feature_major=False)
        mixc = _sample_conv(jnp.transpose(state_conv[l], (1, 0, 2)), u, gc,
                            conv_w[l], conv_b[l], conv_ln_g[l], conv_ln_b[l])
        iw3 = tail[:, IDX_DIM:IDX_DIM + N_IDX_HEADS].reshape(db, N_IDX_HEADS, 1)
        scores = _dec_scores(page_table, iq.reshape(db, N_IDX_HEADS, IDX_DIM), iw3, tail.reshape(db, 1, LANE),
                             cache_ik4, l, width_s)
        madd = _dec_select(scores.reshape(db, width_s).T, kk_s).T
        q4 = q.reshape(db, N_KV_HEADS, GROUP, HEAD_DIM)
        qz3 = jnp.concatenate([jnp.pad(q4[:, 0], ((0, 0), (0, 0), (0, HEAD_DIM))),
                               jnp.pad(q4[:, 1], ((0, 0), (0, 0), (HEAD_DIM, 0)))], axis=1)
        mixa = _dec_attend(page_table, qz3, kv.reshape(db, 1, 2 * feat),
                           madd.reshape(db, 1, width_s), bias_s, ga.reshape(db, N_HEADS // 2, LANE),
                           cache_k4, cache_v4, l)
        res = _out_proj(xs, mixc, mixa.reshape(db, -1), wc, wa, db, final_norm_g if last else None)
        xs, ys = res if last else (res, None)
        ksm.append(kv[:, :feat].reshape(db, 1, N_KV_HEADS, HEAD_DIM))
        vsm.append(kv[:, feat:].reshape(db, 1, N_KV_HEADS, HEAD_DIM))
        iks.append(tail[:, None, :IDX_DIM])
        cs.append(jnp.concatenate([state_conv[l][:, 1:], u[:, None]], axis=1))

    y_prompt = yp.reshape(bsz, lq, d)[:, N_META:lp]
    y_sample = ys.reshape(db, 1, d)
    return (y_prompt, y_sample, jnp.stack(kp), jnp.stack(vp), jnp.stack(ikp), jnp.stack(cp),
            jnp.stack(ksm), jnp.stack(vsm), jnp.stack(iks), jnp.stack(cs))
```

```python
import functools
import math

import numpy as np
import jax
import jax.numpy as jnp
from jax import lax
from jax.experimental import pallas as pl
from jax.experimental.pallas import tpu as pltpu

N_HEADS = 8
N_KV_HEADS = 2
GROUP = N_HEADS // N_KV_HEADS
HEAD_DIM = 64
N_IDX_HEADS = 4
IDX_DIM = 64
C_CONV = 512
CONV_W = 31
TOPK_MAX = 256
N_META = 16
NUM_BUCKETS = 32
MAX_DISTANCE = 128
EPS = 1e-6
LN_EPS = 1e-5

LANE = 128
ROWS = 8
TQ = 128
KC = 512
SUB = KC // TQ
N_ACC = 4
DIGIT_BITS = 7
DIGIT_MASK = (1 << DIGIT_BITS) - 1
KEY_DIGITS = (7, 7, 7, 7, 4)
KEY_SHIFTS = tuple(32 - sum(KEY_DIGITS[:i + 1]) for i in range(len(KEY_DIGITS)))
N_LEVELS = len(KEY_DIGITS)
V_ROWS = HEAD_DIM + 16
PAGE_SLOTS = 4
HALO = 32
CONV_SUB = 64
MXU_DTYPE = jnp.bfloat16
NEG = -1e30
LOG2E = 1.4426950408889634
INT_MIN = -2 ** 31
TIE_SPAN = 1 << 14
KEY_NEG_INF = -0x7F800000 - TIE_SPAN
VMEM_LIMIT = 56 * 1024 * 1024

_PROJ_SIZES = (C_CONV, C_CONV, C_CONV, N_HEADS * HEAD_DIM, N_KV_HEADS * HEAD_DIM, N_KV_HEADS * HEAD_DIM,
               N_HEADS * HEAD_DIM, N_IDX_HEADS * IDX_DIM, IDX_DIM, N_IDX_HEADS)
_UB_OFF, _GC_OFF, _Q_OFF, _K_OFF, _V_OFF, _GA_OFF, _IQ_OFF, _IK_OFF, _IW_OFF, _D_IN = (
    int(s) for s in np.cumsum(_PROJ_SIZES))
_W_ROWS = _IK_OFF + LANE


def _cparams(n_axes):
    return pltpu.CompilerParams(dimension_semantics=("arbitrary",) * n_axes, vmem_limit_bytes=VMEM_LIMIT)


def _silu(x):
    return x * jax.nn.sigmoid(x)


def _prep_w_in(w):
    assert w.shape[1] == _D_IN
    return jnp.pad(w.T, ((0, _W_ROWS - _D_IN), (0, 0))).astype(MXU_DTYPE)


def _t5_bucket(rel):
    n = jnp.maximum(rel, 0)
    max_exact = NUM_BUCKETS // 2
    large = max_exact + (jnp.log(jnp.maximum(n, 1).astype(jnp.float32) / max_exact)
                         / math.log(MAX_DISTANCE / max_exact)
                         * (NUM_BUCKETS - max_exact)).astype(jnp.int32)
    large = jnp.minimum(large, NUM_BUCKETS - 1)
    return jnp.where(n < max_exact, n, large)


def _bias_tiles(rel_bias):
    far = rel_bias[NUM_BUCKETS - 1]
    d = jnp.arange(2 * TQ, dtype=jnp.int32)
    dists = (jnp.where(d < TQ, d, 0),
             jnp.where(d < TQ, d + TQ, d - TQ))
    tiles = []
    for dist in dists:
        v = ((rel_bias[_t5_bucket(dist)] - far) * LOG2E).T
        rep = jnp.tile(v, (1, TQ))[:, :TQ * (2 * TQ - 1)].reshape(N_HEADS, TQ, 2 * TQ - 1)
        tiles.append(rep[:, :, :TQ])
    tiles.append(jnp.zeros_like(tiles[0]))
    return jnp.stack(tiles).astype(jnp.float32)


_NT = (((1,), (1,)), ((), ()))
_Q_SCALE = HEAD_DIM ** -0.5 * LOG2E
_IQ_SCALE = IDX_DIM ** -0.5
_IW_SCALE = N_IDX_HEADS ** -0.5
_IW_ROWS = 16


def _proj_kernel(x_ref, g_ref, w_ref, u_ref, gc_ref, kv_ref, ga_ref, tail_ref, *q_refs, feature_major):
    x = x_ref[...]
    ms = jnp.mean(x * x, axis=-1, keepdims=True)
    xn = (x * lax.rsqrt(ms + EPS) * g_ref[...]).astype(MXU_DTYPE)

    def mm(lo, hi):
        return lax.dot_general(xn, w_ref[lo:hi, :], _NT, preferred_element_type=jnp.float32)

    def mm_t(lo, hi):
        return lax.dot_general(w_ref[lo:hi, :], xn, _NT, preferred_element_type=jnp.float32)

    u_ref[...] = mm(0, _UB_OFF) * jax.nn.sigmoid(mm(_UB_OFF, _GC_OFF))
    gc_ref[...] = mm(_GC_OFF, _Q_OFF)
    kv_ref[...] = mm(_K_OFF, _GA_OFF)
    ga_ref[...] = mm(_GA_OFF, _IQ_OFF)
    t = mm(_IK_OFF, _W_ROWS)
    lane = lax.broadcasted_iota(jnp.int32, t.shape, 1)
    tail_ref[...] = t * jnp.where(lane >= IDX_DIM, _IW_SCALE, 1.0)
    if feature_major:
        qzt_ref, iqzt_ref, vt_ref, iwt_ref = q_refs
        qt = (mm_t(_Q_OFF, _K_OFF) * _Q_SCALE).astype(qzt_ref.dtype)
        qzt_ref[...] = jnp.zeros(qzt_ref.shape, qzt_ref.dtype)
        for h in range(N_HEADS):
            lo = h * LANE + (h // GROUP) * HEAD_DIM
            qzt_ref[lo:lo + HEAD_DIM, :] = qt[h * HEAD_DIM:(h + 1) * HEAD_DIM]
        iqt = (mm_t(_IQ_OFF, _IK_OFF) * _IQ_SCALE).astype(iqzt_ref.dtype)
        iqzt_ref[...] = jnp.zeros(iqzt_ref.shape, iqzt_ref.dtype)
        for h in range(N_IDX_HEADS):
            iqzt_ref[h * LANE:h * LANE + IDX_DIM, :] = iqt[h * IDX_DIM:(h + 1) * IDX_DIM]
        vt_ref[...] = mm_t(_V_OFF, _GA_OFF).astype(vt_ref.dtype)
        iwt_ref[...] = mm_t(_IW_OFF, _IW_OFF + _IW_ROWS) * _IW_SCALE
    else:
        q_ref, iq_ref = q_refs
        q_ref[...] = (mm(_Q_OFF, _K_OFF) * _Q_SCALE).astype(q_ref.dtype)
        iq_ref[...] = (mm(_IQ_OFF, _IK_OFF) * _IQ_SCALE).astype(iq_ref.dtype)


def _project(x, g, w, tm, feature_major):
    r, d = x.shape
    assert r % tm == 0
    rows = lambda c: pl.BlockSpec((tm, c), lambda i: (i, 0))
    cols = lambda c: pl.BlockSpec((c, tm), lambda i: (0, i))
    feat = N_KV_HEADS * HEAD_DIM
    specs = [rows(C_CONV), rows(C_CONV), rows(2 * feat), rows(N_HEADS * HEAD_DIM), rows(LANE)]
    shapes = [jax.ShapeDtypeStruct((r, s.block_shape[1]), jnp.float32) for s in specs]
    if feature_major:
        extra = [(N_HEADS * LANE, MXU_DTYPE), (N_IDX_HEADS * LANE, MXU_DTYPE), (feat, MXU_DTYPE),
                 (_IW_ROWS, jnp.float32)]
        specs += [cols(c) for c, _ in extra]
        shapes += [jax.ShapeDtypeStruct((c, r), t) for c, t in extra]
    else:
        extra = [(N_HEADS * HEAD_DIM, MXU_DTYPE), (N_IDX_HEADS * IDX_DIM, MXU_DTYPE)]
        specs += [rows(c) for c, _ in extra]
        shapes += [jax.ShapeDtypeStruct((r, c), t) for c, t in extra]
    return pl.pallas_call(
        functools.partial(_proj_kernel, feature_major=feature_major),
        grid=(r // tm,),
        in_specs=[rows(d), pl.BlockSpec((1, d), lambda i: (0, 0)), pl.BlockSpec((_W_ROWS, d), lambda i: (0, 0))],
        out_specs=specs,
        out_shape=shapes,
        compiler_params=_cparams(1),
        name="proj",
    )(x, g.reshape(1, d), w)


def _conv_kernel(prev_ref, cur_ref, gc_ref, w_ref, b_ref, lg_ref, lb_ref, o_ref, ext_ref):
    i = pl.program_id(1)
    ext_ref[0, 0:HALO, :] = jnp.where(i > 0, prev_ref[...], 0.0)
    ext_ref[0, HALO:HALO + TQ, :] = cur_ref[...]
    n_ext = HALO + TQ
    for s in range(1, ROWS):
        ext_ref[s, 0:n_ext - ROWS, :] = ext_ref[0, s:s + n_ext - ROWS, :]
    off = HALO - (CONV_W - 1)
    for r0 in range(0, TQ, CONV_SUB):
        acc = jnp.zeros((CONV_SUB, C_CONV), jnp.float32)
        for j in range(CONV_W):
            s = (off + j) % ROWS
            lo = r0 + off + j - s
            acc = acc + w_ref[j:j + 1, :] * ext_ref[s, lo:lo + CONV_SUB, :]
        y = acc + b_ref[...]
        mu = jnp.mean(y, axis=-1, keepdims=True)
        dev = y - mu
        var = jnp.mean(dev * dev, axis=-1, keepdims=True)
        yn = dev * lax.rsqrt(var + LN_EPS) * lg_ref[...] + lb_ref[...]
        o_ref[r0:r0 + CONV_SUB, :] = (_silu(yn) * _silu(gc_ref[r0:r0 + CONV_SUB, :])).astype(o_ref.dtype)


def _conv_branch(u, gc, w, b, lg, lb):
    bsz, lq, c = u.shape
    per = TQ // HALO
    row = lambda a: a.reshape(1, c)
    vec = pl.BlockSpec((1, c), lambda bi, i: (0, 0))
    tile = pl.BlockSpec((None, TQ, c), lambda bi, i: (bi, i, 0))
    return pl.pallas_call(
        _conv_kernel,
        grid=(bsz, lq // TQ),
        in_specs=[pl.BlockSpec((None, HALO, c), lambda bi, i: (bi, jnp.maximum(i * per - 1, 0), 0)),
                  tile, tile, pl.BlockSpec((CONV_W, c), lambda bi, i: (0, 0)), vec, vec, vec],
        out_specs=tile,
        out_shape=jax.ShapeDtypeStruct((bsz, lq, c), MXU_DTYPE),
        scratch_shapes=[pltpu.VMEM((ROWS, HALO + TQ, c), jnp.float32)],
        compiler_params=_cparams(2),
        name="conv",
    )(u, u, gc, w, row(b), row(lg), row(lb))


_INT_MIN = np.int32(INT_MIN)
_BYTE_ONES = np.int32(0x01010101)
_BYTE_LOW = np.int32(0x7F7F7F7F)
_HALF_LOW_BYTES = np.int32(0x00FF00FF)
_GUARD = np.int32(0x80808080 - (1 << 32))
assert SUB == 4 and DIGIT_BITS == 7


def _to_key(s, idx):
    bits = lax.bitcast_convert_type(s, jnp.int32)
    key = jnp.where(bits < 0, (_INT_MIN - bits) - TIE_SPAN, bits)
    return jnp.where((bits == 0) | (bits == _INT_MIN), -1 - idx, key)


def _for_chunks(nkc, body, init, unroll):
    n_main = nkc // unroll

    def main(i, carry):
        for k in range(unroll):
            carry = body(i * unroll + k, carry)
        return carry

    carry = lax.fori_loop(0, n_main, main, init)
    return lax.fori_loop(n_main * unroll, nkc, body, carry)


def _pack_digits(keys_ref, dig_ref, c):
    for lv in range(N_LEVELS):
        word = None
        for j in range(SUB):
            u = keys_ref[c, j * TQ:(j + 1) * TQ, :]
            if lv == 0:
                u = u ^ _INT_MIN
            move = KEY_SHIFTS[lv] - 8 * j
            u = lax.shift_right_logical(u, move) if move >= 0 else lax.shift_left(u, -move)
            field = u & np.int32(((1 << KEY_DIGITS[lv]) - 1) << (8 * j))
            word = field if word is None else word | field
        dig_ref[lv, c] = word


def _count_fields(work_ref, nkc, cand_bytes):
    q = work_ref.shape[2]

    def body(c, accs):
        accs = list(accs)
        for n, r in enumerate(range(0, TQ, ROWS)):
            diff = work_ref[c, r:r + ROWS, :] - cand_bytes
            accs[n % N_ACC] = accs[n % N_ACC] + (lax.shift_right_logical(diff, DIGIT_BITS) & _BYTE_ONES)
        return tuple(accs)

    accs = _for_chunks(nkc, body, (jnp.zeros((ROWS, q), jnp.int32),) * N_ACC, unroll=4)
    halves = jnp.zeros((ROWS, q), jnp.int32)
    for a in accs:
        halves = halves + (a & _HALF_LOW_BYTES) + (lax.shift_right_logical(a, 8) & _HALF_LOW_BYTES)
    total = (halves & 0xFFFF) + lax.shift_right_logical(halves, 16)
    return jnp.sum(total.astype(jnp.float32), axis=0, keepdims=True)


def _digit_search(dig_ref, work_ref, alive_ref, nkc, shifts, digits, target):
    q = dig_ref.shape[3]
    assert dig_ref.shape[1] * (TQ // ROWS) <= 255 * N_ACC
    zero = jnp.zeros((1, q), jnp.float32)
    value = jnp.zeros((1, q), jnp.int32)
    above, done, cnt_ge, cnt_gt = zero, zero, target + 1.0, zero

    for lv, (shift, n_bits) in enumerate(zip(shifts, digits)):
        def load_fields(c, carry):
            work_ref[c] = (dig_ref[lv, c] & alive_ref[c]) | _GUARD
            return carry

        _for_chunks(nkc, load_fields, 0, unroll=4)

        def digit_bit(i, state):
            digit, cnt_ge, cnt_gt, cnt_rej, done = state
            cand = digit | lax.shift_left(jnp.int32(1), n_bits - 1 - i)
            cnt_alive = _count_fields(work_ref, nkc, cand * _BYTE_ONES)
            cnt = above + cnt_alive
            take = (cnt >= target) & (done == 0.0)
            drop = (cnt < target) & (done == 0.0)
            return (jnp.where(take, cand, digit), jnp.where(take, cnt, cnt_ge), jnp.where(drop, cnt, cnt_gt),
                    jnp.where(drop, cnt_alive, cnt_rej),
                    jnp.where(take & (cnt == target), 1.0, done))

        digit, cnt_ge, cnt_gt, cnt_rej, done = lax.fori_loop(
            0, n_bits, digit_bit, (jnp.zeros((1, q), jnp.int32), cnt_ge, cnt_gt, zero, done))
        above = above + cnt_rej
        value = value | lax.shift_left(digit, shift)

        if lv + 1 < len(shifts):
            digit_bytes = digit * _BYTE_ONES

            def narrow(c, carry):
                differs = lax.shift_right_logical(((dig_ref[lv, c] ^ digit_bytes) + _BYTE_LOW) & _GUARD, DIGIT_BITS)
                alive_ref[c] = alive_ref[c] & ((_BYTE_ONES - differs) * DIGIT_MASK)
                return carry

            _for_chunks(nkc, narrow, 0, unroll=4)

    return value, cnt_ge, cnt_gt, done


def _fill(ref, nkc, word):
    def body(c, carry):
        ref[c] = jnp.full(ref.shape[1:], word, ref.dtype)
        return carry

    _for_chunks(nkc, body, 0, unroll=4)


def _threshold(keys_ref, dig_ref, work_ref, alive_ref, nkc, kk):
    q = keys_ref.shape[2]
    kf = jnp.full((1, q), kk, jnp.float32)
    _fill(alive_ref, nkc, _BYTE_LOW)
    prefix, cnt_thr, cnt_gt, _ = _digit_search(dig_ref, work_ref, alive_ref, nkc, KEY_SHIFTS, KEY_DIGITS, kf)
    thr = prefix ^ _INT_MIN
    admissible = thr != KEY_NEG_INF
    surplus = (cnt_thr > kf) & admissible

    @pl.when(jnp.max(jnp.where(surplus, 1.0, 0.0)) > 0.0)
    def _():
        assert keys_ref.shape[0] * KC <= 1 << (2 * DIGIT_BITS)
        top = (1 << (2 * DIGIT_BITS)) - 1
        row = lax.broadcasted_iota(jnp.int32, (TQ, q), 0)

        def tie_fields(c, carry):
            hi = lo = tie = None
            for j in range(SUB):
                is_tie = keys_ref[c, j * TQ:(j + 1) * TQ, :] == thr
                rev = top - (c * KC + j * TQ + row)
                fields = [jnp.where(is_tie, f, 0) for f in
                          (lax.shift_right_logical(rev, DIGIT_BITS), rev & DIGIT_MASK, DIGIT_MASK)]
                fields = [f if j == 0 else lax.shift_left(f, 8 * j) for f in fields]
                hi, lo, tie = fields if j == 0 else (hi | fields[0], lo | fields[1], tie | fields[2])
            dig_ref[0, c], dig_ref[1, c], alive_ref[c] = hi, lo, tie
            return carry

        lax.fori_loop(0, nkc, tie_fields, 0)
        cut, _, _, _ = _digit_search(dig_ref, work_ref, alive_ref, nkc, (DIGIT_BITS, 0), (DIGIT_BITS, DIGIT_BITS),
                                     kf - cnt_gt)

        def drop_losers(c, carry):
            for j in range(SUB):
                rows = slice(j * TQ, (j + 1) * TQ)
                x = keys_ref[c, rows, :]
                lost = (x == thr) & (top - (c * KC + j * TQ + row) < cut)
                keys_ref[c, rows, :] = jnp.where(lost, thr - 1, x)
            return carry

        lax.fori_loop(0, nkc, drop_losers, 0)

    return jnp.where(admissible, thr, KEY_NEG_INF + 1)


def _attend_kernel(iqt_ref, iwt_ref, qzt_ref, ga_ref, ik_ref, k_ref, vt_ref, toep_ref, o_ref,
                   keys_ref, dig_ref, work_ref, alive_ref, m_ref, acc_ref, lg0_ref, lg1_ref,
                   wq_ref, *, kk):
    qi = pl.program_id(1)
    nkc = qi // SUB + 1
    qpos = qi * TQ + lax.broadcasted_iota(jnp.int32, (TQ, TQ), 1)
    krow = lax.broadcasted_iota(jnp.int32, (TQ, TQ), 0)
    for h in range(N_HEADS):
        wq_ref[:, h * TQ:(h + 1) * TQ] = qzt_ref[h]

    def score_chunk(c, carry, causal):
        iqt = jnp.concatenate([iqt_ref[h] for h in range(N_IDX_HEADS)], axis=1)
        w = iwt_ref[...]
        for j in range(SUB):
            st = jnp.dot(ik_ref[c, j * TQ:(j + 1) * TQ, :], iqt, preferred_element_type=jnp.float32)
            s = jnp.zeros((TQ, TQ), jnp.float32)
            for h in range(N_IDX_HEADS):
                s = s + w[h:h + 1, :] * jnp.maximum(st[:, h * TQ:(h + 1) * TQ], 0.0)
            kpos = c * KC + j * TQ + krow
            if causal:
                s = jnp.where(kpos <= qpos, s, -jnp.inf)
            keys_ref[c, j * TQ:(j + 1) * TQ, :] = _to_key(s, kpos)
        _pack_digits(keys_ref, dig_ref, c)
        return carry

    _for_chunks(nkc - 1, functools.partial(score_chunk, causal=False), 0, unroll=2)
    score_chunk(nkc - 1, 0, causal=True)
    thr = _threshold(keys_ref, dig_ref, work_ref, alive_ref, nkc, kk)

    m_ref[...] = jnp.full(m_ref.shape, NEG, jnp.float32)
    acc_ref[...] = jnp.zeros(acc_ref.shape, jnp.float32)

    def bias(c, h):
        return jnp.concatenate([toep_ref[jnp.clip(qi - (c * SUB + j), 0, 2), h] for j in range(SUB)], axis=0)

    def logits(c, lg_ref, near):
        md = jnp.where(keys_ref[c] >= thr, 0.0, NEG)
        add = [md + bias(c, h) for h in range(N_HEADS)] if near else [md] * N_HEADS
        lg_ref[...] = (jnp.dot(k_ref[c], wq_ref[...], preferred_element_type=jnp.float32)
                       + jnp.concatenate(add, axis=1))

    def add_bias(c, lg_ref):
        for h in range(N_HEADS):
            sl = slice(h * TQ, (h + 1) * TQ)
            lg_ref[:, sl] = lg_ref[:, sl] + bias(c, h)

    def softmax_pv(c, lg_ref):
        for pr in range(N_HEADS // 2):
            kv = pr // (GROUP // 2)
            sl = slice(2 * pr * TQ, (2 * pr + 2) * TQ)
            m_old = m_ref[pr]
            m_new = jnp.maximum(m_old, jnp.max(lg_ref[:, sl], axis=0, keepdims=True))
            alpha = jnp.exp2(m_old - m_new)
            p = jnp.exp2(lg_ref[:, sl] - m_new)
            pv = jnp.dot(vt_ref[c, kv * V_ROWS:(kv + 1) * V_ROWS, :], p.astype(MXU_DTYPE),
                         preferred_element_type=jnp.float32)
            acc_ref[pr] = alpha * acc_ref[pr] + pv
            m_ref[pr] = m_new

    n_far2 = (jnp.maximum(qi - 1, 0) // SUB) // 2

    def far_pair(i, carry):
        logits(2 * i + 1, lg1_ref, near=False)
        softmax_pv(2 * i, lg0_ref)
        logits(2 * i + 2, lg0_ref, near=False)
        softmax_pv(2 * i + 1, lg1_ref)
        return carry

    logits(0, lg0_ref, near=False)
    lax.fori_loop(0, n_far2, far_pair, 0)

    c0 = 2 * n_far2
    add_bias(c0, lg0_ref)

    @pl.when(c0 + 1 < nkc)
    def _():
        logits(c0 + 1, lg1_ref, near=True)
        softmax_pv(c0, lg0_ref)

    @pl.when(c0 + 1 >= nkc)
    def _():
        softmax_pv(c0, lg0_ref)

    @pl.when(c0 + 2 < nkc)
    def _():
        logits(c0 + 2, lg0_ref, near=True)
        softmax_pv(c0 + 1, lg1_ref)
        softmax_pv(c0 + 2, lg0_ref)

    @pl.when((c0 + 1 < nkc) & (c0 + 2 >= nkc))
    def _():
        softmax_pv(c0 + 1, lg1_ref)

    for pr in range(N_HEADS // 2):
        acc = acc_ref[pr]
        o_t = acc[:HEAD_DIM] / acc[HEAD_DIM:HEAD_DIM + 1]
        pair_t = jnp.concatenate([o_t[:, :TQ], o_t[:, TQ:]], axis=0)
        sl = slice(pr * LANE, (pr + 1) * LANE)
        o_ref[:, sl] = (pair_t.T * _silu(ga_ref[:, sl])).astype(o_ref.dtype)


def _attend(iqt, iwt, qzt, ga, ik, k, vt, toep_t, kk):
    bsz, lq = ga.shape[:2]
    nq = lq // TQ
    nc = k.shape[1]
    lanes = lambda a: pl.BlockSpec(a.shape[:-1] + (TQ,), lambda b, i: (0,) * (a.ndim - 1) + (b * nq + i,))
    whole = lambda a: pl.BlockSpec((None,) + a.shape[1:], lambda b, i: (b,) + (0,) * (a.ndim - 1))
    rows = pl.BlockSpec((None, TQ, N_HEADS * HEAD_DIM), lambda b, i: (b, i, 0))
    return pl.pallas_call(
        functools.partial(_attend_kernel, kk=kk),
        grid=(bsz, nq),
        in_specs=[lanes(iqt), lanes(iwt), lanes(qzt), rows, whole(ik), whole(k), whole(vt),
                  pl.BlockSpec(toep_t.shape, lambda b, i: (0, 0, 0, 0))],
        out_specs=rows,
        out_shape=jax.ShapeDtypeStruct((bsz, lq, N_HEADS * HEAD_DIM), MXU_DTYPE),
        scratch_shapes=[pltpu.VMEM((nc, KC, TQ), jnp.int32),
                        pltpu.VMEM((N_LEVELS, nc, TQ, TQ), jnp.int32), pltpu.VMEM((nc, TQ, TQ), jnp.int32),
                        pltpu.VMEM((nc, TQ, TQ), jnp.int32),
                        pltpu.VMEM((N_HEADS // 2, 1, 2 * TQ), jnp.float32),
                        pltpu.VMEM((N_HEADS // 2, V_ROWS, 2 * TQ), jnp.float32),
                        pltpu.VMEM((KC, N_HEADS * TQ), jnp.float32), pltpu.VMEM((KC, N_HEADS * TQ), jnp.float32),
                        pltpu.VMEM((LANE, N_HEADS * TQ), MXU_DTYPE)],
        compiler_params=_cparams(2),
        name="attend",
    )(iqt, iwt, qzt, ga, ik, k, vt, toep_t)


def _out_kernel(x_ref, mc_ref, ma_ref, wc_ref, wa_ref, o_ref):
    o_ref[...] = (x_ref[...]
                  + jnp.dot(mc_ref[...], wc_ref[...], preferred_element_type=jnp.float32)
                  + jnp.dot(ma_ref[...], wa_ref[...], preferred_element_type=jnp.float32))


def _out_final_kernel(x_ref, mc_ref, ma_ref, wc_ref, wa_ref, g_ref, o_ref, y_ref):
    x = (x_ref[...]
         + jnp.dot(mc_ref[...], wc_ref[...], preferred_element_type=jnp.float32)
         + jnp.dot(ma_ref[...], wa_ref[...], preferred_element_type=jnp.float32))
    o_ref[...] = x
    ms = jnp.mean(x * x, axis=-1, keepdims=True)
    y_ref[...] = x * lax.rsqrt(ms + EPS) * g_ref[...]


def _out_proj(x, mc, ma, wc, wa, tm, final_g=None):
    r, d = x.shape
    c = mc.shape[1]
    rows = lambda w: pl.BlockSpec((tm, w), lambda i: (i, 0))
    full = lambda a: pl.BlockSpec(a.shape, lambda i: (0, 0))
    if final_g is None:
        return pl.pallas_call(
            _out_kernel, grid=(r // tm,),
            in_specs=[rows(d), rows(c), rows(c), full(wc), full(wa)],
            out_specs=rows(d), out_shape=jax.ShapeDtypeStruct((r, d), jnp.float32),
            compiler_params=_cparams(1), name="out",
        )(x, mc, ma, wc, wa)
    g = final_g.reshape(1, d)
    return pl.pallas_call(
        _out_final_kernel, grid=(r // tm,),
        in_specs=[rows(d), rows(c), rows(c), full(wc), full(wa), full(g)],
        out_specs=[rows(d), rows(d)], out_shape=[jax.ShapeDtypeStruct((r, d), jnp.float32)] * 2,
        compiler_params=_cparams(1), name="out_final",
    )(x, mc, ma, wc, wa, g)


def _sconv_kernel(st_ref, u_ref, gc_ref, w_ref, b_ref, lg_ref, lb_ref, o_ref):
    acc = jnp.zeros(u_ref.shape, jnp.float32)
    for j in range(CONV_W - 1):
        acc = acc + w_ref[j:j + 1, :] * st_ref[j]
    acc = acc + w_ref[CONV_W - 1:CONV_W, :] * u_ref[...]
    y = acc + b_ref[...]
    mu = jnp.mean(y, axis=-1, keepdims=True)
    dev = y - mu
    var = jnp.mean(dev * dev, axis=-1, keepdims=True)
    yn = dev * lax.rsqrt(var + LN_EPS) * lg_ref[...] + lb_ref[...]
    o_ref[...] = (_silu(yn) * _silu(gc_ref[...])).astype(o_ref.dtype)


def _sample_conv(state_t, u, gc, w, b, lg, lb):
    db, c = u.shape
    row = lambda a: a.reshape(1, c)
    full = lambda a: pl.BlockSpec(a.shape, lambda i: (0,) * a.ndim)
    args = (state_t, u, gc, w, row(b), row(lg), row(lb))
    return pl.pallas_call(
        _sconv_kernel, grid=(1,), in_specs=[full(a) for a in args],
        out_specs=pl.BlockSpec((db, c), lambda i: (0, 0)),
        out_shape=jax.ShapeDtypeStruct((db, c), MXU_DTYPE),
        compiler_params=_cparams(1), name="sconv",
    )(*args)


def _dec_score_kernel(pt_ref, iq_ref, iw_ref, ikn_ref, cik_ref, o_ref, buf_ref, sem_ref, *, layer, width):
    slot = _fetch_pages(pt_ref, (cik_ref,), (buf_ref,), (sem_ref,), layer)
    iq = iq_ref[...][:, :IDX_DIM]
    w = iw_ref[...]
    ikt = buf_ref[slot].astype(MXU_DTYPE)
    s = jnp.dot(iq, ikt, preferred_element_type=jnp.float32)
    sc = jnp.sum(w * jnp.maximum(s, 0.0), axis=0, keepdims=True)
    ikn = ikn_ref[...][:, :IDX_DIM].astype(MXU_DTYPE).astype(jnp.float32)
    sn = jnp.sum(iq.astype(jnp.float32) * ikn, axis=-1, keepdims=True)
    scn = jnp.sum(w * jnp.maximum(sn, 0.0), axis=0, keepdims=True)
    lane = lax.broadcasted_iota(jnp.int32, (1, width - sc.shape[1]), 1)
    o_ref[...] = jnp.concatenate([sc, jnp.where(lane == 0, scn, -jnp.inf)], axis=1)


def _dec_scores(page_table, iq3, iw3, tail3, cache_ik, layer, width):
    db, n_pages = page_table.shape
    di, page = cache_ik.shape[2:]
    seq = lambda a: pl.BlockSpec((None,) + a.shape[1:], lambda b, pt: (b, 0, 0))
    return pl.pallas_call(
        functools.partial(_dec_score_kernel, layer=layer, width=width),
        grid_spec=pltpu.PrefetchScalarGridSpec(
            num_scalar_prefetch=1, grid=(db,),
            in_specs=[seq(iq3), seq(iw3), seq(tail3), pl.BlockSpec(memory_space=pl.ANY)],
            out_specs=pl.BlockSpec((None, 1, width), lambda b, pt: (b, 0, 0)),
            scratch_shapes=[pltpu.VMEM((PAGE_SLOTS, di, n_pages * page), cache_ik.dtype),
                            pltpu.SemaphoreType.DMA((PAGE_SLOTS,))]),
        out_shape=jax.ShapeDtypeStruct((db, 1, width), jnp.float32),
        compiler_params=_cparams(1), name="dec_score",
    )(page_table, iq3, iw3, tail3, cache_ik)


def _dec_select_kernel(s_ref, o_ref, keys_ref, dig_ref, work_ref, alive_ref, *, kk):
    nc = keys_ref.shape[0]
    row = lax.broadcasted_iota(jnp.int32, (KC, keys_ref.shape[2]), 0)
    for c in range(nc):
        keys_ref[c] = _to_key(s_ref[c * KC:(c + 1) * KC, :], c * KC + row)
        _pack_digits(keys_ref, dig_ref, c)
    thr = _threshold(keys_ref, dig_ref, work_ref, alive_ref, nc, kk)
    for c in range(nc):
        o_ref[c * KC:(c + 1) * KC, :] = jnp.where(keys_ref[c] >= thr, 0.0, NEG)


def _dec_select(scores_t, kk):
    width, db = scores_t.shape
    nc = width // KC
    spec = pl.BlockSpec((width, db), lambda i: (0, 0))
    return pl.pallas_call(
        functools.partial(_dec_select_kernel, kk=kk),
        grid=(1,), in_specs=[spec], out_specs=spec,
        out_shape=jax.ShapeDtypeStruct((width, db), jnp.float32),
        scratch_shapes=[pltpu.VMEM((nc, KC, db), jnp.int32),
                        pltpu.VMEM((N_LEVELS, nc, TQ, db), jnp.int32), pltpu.VMEM((nc, TQ, db), jnp.int32),
                        pltpu.VMEM((nc, TQ, db), jnp.int32)],
        compiler_params=_cparams(1), name="dec_select",
    )(scores_t)


def _page_copies(pt_ref, cache_ref, buf_ref, sem_ref, layer, seq, slot):
    page = cache_ref.shape[3]
    n_pages = buf_ref.shape[2] // page
    return [pltpu.make_async_copy(cache_ref.at[layer, pt_ref[seq, p]],
                                  buf_ref.at[slot, :, pl.ds(p * page, page)], sem_ref.at[slot])
            for p in range(n_pages)]


def _fetch_pages(pt_ref, caches, bufs, sems, layer):
    b, nb = pl.program_id(0), pl.num_programs(0)
    ahead = PAGE_SLOTS - 1

    def start(seq, slot):
        for cache_ref, buf_ref, sem_ref in zip(caches, bufs, sems):
            for cp in _page_copies(pt_ref, cache_ref, buf_ref, sem_ref, layer, seq, slot):
                cp.start()

    for s in range(ahead):
        @pl.when((b == 0) & (s < nb))
        def _():
            start(s, s)

    @pl.when(b + ahead < nb)
    def _():
        start(b + ahead, (b + ahead) % PAGE_SLOTS)

    slot = b % PAGE_SLOTS
    for cache_ref, buf_ref, sem_ref in zip(caches, bufs, sems):
        for cp in _page_copies(pt_ref, cache_ref, buf_ref, sem_ref, layer, b, slot):
            cp.wait()
    return slot


def _dec_attend_kernel(pt_ref, qz_ref, kvn_ref, madd_ref, bias_ref, ga_ref, ck_ref, cv_ref, o_ref,
                       kbuf_ref, vbuf_ref, ksem_ref, vsem_ref, *, layer):
    slot = _fetch_pages(pt_ref, (ck_ref, cv_ref), (kbuf_ref, vbuf_ref), (ksem_ref, vsem_ref), layer)
    qz = qz_ref[...]
    past = kbuf_ref.shape[2]
    kt = kbuf_ref[slot].astype(MXU_DTYPE)
    vt = vbuf_ref[slot].astype(MXU_DTYPE)
    lg = (jnp.dot(qz, kt, preferred_element_type=jnp.float32)
          + bias_ref[:, :past] + madd_ref[:, :past])
    kvn = kvn_ref[...].astype(MXU_DTYPE).astype(jnp.float32)
    lgn = jnp.sum(qz.astype(jnp.float32) * kvn[:, :LANE], axis=-1, keepdims=True)
    lgn = lgn + bias_ref[:, past:past + 1] + madd_ref[:, past:past + 1]
    m = jnp.maximum(lgn, jnp.max(lg, axis=-1, keepdims=True))
    pn = jnp.exp2(lgn - m)
    pp = jnp.exp2(lg - m)
    den = pn + jnp.sum(pp, axis=-1, keepdims=True)
    o = (pn.astype(MXU_DTYPE).astype(jnp.float32) * kvn[:, LANE:]
         + lax.dot_general(pp.astype(MXU_DTYPE), vt, _NT, preferred_element_type=jnp.float32))
    o = o / den
    swapped = pltpu.roll(o, HEAD_DIM, axis=1)
    lane = lax.broadcasted_iota(jnp.int32, (1, LANE), 1)
    pairs = []
    for j in range(N_HEADS // 2):
        first, second = (o, swapped) if 2 * j < GROUP else (swapped, o)
        pairs.append(jnp.where(lane < HEAD_DIM, first[2 * j:2 * j + 1], second[2 * j + 1:2 * j + 2]))
    o_ref[...] = (jnp.concatenate(pairs, axis=0) * _silu(ga_ref[...])).astype(o_ref.dtype)


def _dec_attend(page_table, qz3, kvn3, madd3, bias, ga3, cache_k, cache_v, layer):
    db, n_pages = page_table.shape
    feat, page = cache_k.shape[2:]
    seq = lambda a: pl.BlockSpec((None,) + a.shape[1:], lambda b, pt: (b, 0, 0))
    hbm = pl.BlockSpec(memory_space=pl.ANY)
    buf = pltpu.VMEM((PAGE_SLOTS, feat, n_pages * page), cache_k.dtype)
    sem = pltpu.SemaphoreType.DMA((PAGE_SLOTS,))
    return pl.pallas_call(
        functools.partial(_dec_attend_kernel, layer=layer),
        grid_spec=pltpu.PrefetchScalarGridSpec(
            num_scalar_prefetch=1, grid=(db,),
            in_specs=[seq(qz3), seq(kvn3), seq(madd3), pl.BlockSpec(bias.shape, lambda b, pt: (0, 0)), seq(ga3),
                      hbm, hbm],
            out_specs=pl.BlockSpec((None, GROUP, LANE), lambda b, pt: (b, 0, 0)),
            scratch_shapes=[buf, buf, sem, sem]),
        out_shape=jax.ShapeDtypeStruct((db, GROUP, LANE), MXU_DTYPE),
        compiler_params=_cparams(1), name="dec_attend",
    )(page_table, qz3, kvn3, madd3, bias, ga3, cache_k, cache_v)


def _round_up(x, m):
    return -(-x // m) * m


def _row_tile(rows):
    for f in (5, 4, 3, 2, 1):
        if rows % (f * TQ) == 0:
            return f * TQ
    return rows


def kernel(x_prompt, x_sample, cache_k, cache_v, cache_idx_k, state_conv, page_table, meta_tokens, rel_bias,
           norm_g, w_in, conv_w, conv_b, conv_ln_g, conv_ln_b, w_out, final_norm_g):
    bsz, seq, d = x_prompt.shape
    depth = w_in.shape[0]
    lp = seq + N_META
    lq = _round_up(lp, TQ)
    lk = _round_up(lp, KC)
    nq, nc = lq // TQ, lk // KC
    kk_p = min(TOPK_MAX, lp // 4)
    db = x_sample.shape[0]
    n_pool, page = cache_k.shape[1:3]
    n_pages = page_table.shape[1]
    past = n_pages * page
    kk_s = min(TOPK_MAX, (past + 1) // 4)
    width_s = _round_up(past + 1, KC)
    feat = N_KV_HEADS * HEAD_DIM
    tm = _row_tile(bsz * lq)

    xp = jnp.concatenate([jnp.broadcast_to(meta_tokens[None].astype(x_prompt.dtype), (bsz, N_META, d)), x_prompt,
                          jnp.zeros((bsz, lq - lp, d), x_prompt.dtype)], axis=1).reshape(bsz * lq, d)
    xs = x_sample.reshape(db, d)
    toep_t = _bias_tiles(rel_bias)
    bias_s = jnp.moveaxis(rel_bias[_t5_bucket(past - jnp.arange(width_s, dtype=jnp.int32))], -1, 0) * LOG2E
    cache_k4 = jnp.transpose(cache_k, (0, 1, 3, 4, 2)).reshape(depth, n_pool, feat, page)
    cache_v4 = jnp.transpose(cache_v, (0, 1, 3, 4, 2)).reshape(depth, n_pool, feat, page)
    cache_ik4 = jnp.swapaxes(cache_idx_k, 2, 3)

    def key_chunks(a):
        return jnp.pad(a, ((0, 0), (0, lk - lq), (0, 0))).astype(MXU_DTYPE).reshape(bsz, nc, KC, LANE)

    def value_chunks_t(a):
        a = jnp.pad(a.reshape(N_KV_HEADS, HEAD_DIM, bsz, lq), ((0, 0), (0, 0), (0, 0), (0, lk - lq)))
        a = jnp.concatenate([a, jnp.ones((N_KV_HEADS, V_ROWS - HEAD_DIM, bsz, lk), a.dtype)], axis=1)
        return jnp.transpose(a.reshape(N_KV_HEADS * V_ROWS, bsz, nc, KC), (1, 2, 0, 3))

    kp, vp, ikp, cp, ksm, vsm, iks, cs = ([] for _ in range(8))
    yp = ys = None
    for l in range(depth):
        w = _prep_w_in(w_in[l])
        wc = w_out[l, :C_CONV].astype(MXU_DTYPE)
        wa = w_out[l, C_CONV:].astype(MXU_DTYPE)
        last = l == depth - 1

        u, gc, kv, ga, tail, qzt, iqzt, vt, iwt = _project(xp, norm_g[l], w, tm, feature_major=True)
        b3 = lambda a: a.reshape(bsz, lq, a.shape[-1])
        u3, kv3, tail3 = b3(u), b3(kv), b3(tail)
        mixc = _conv_branch(u3, b3(gc), conv_w[l], conv_b[l], conv_ln_g[l], conv_ln_b[l])
        lane = jnp.arange(LANE)
        ik = key_chunks(jnp.where(lane < IDX_DIM, tail3, 0.0))
        mixa = _attend(iqzt.reshape(N_IDX_HEADS, LANE, bsz * lq), iwt, qzt.reshape(N_HEADS, LANE, bsz * lq), b3(ga),
                       ik, key_chunks(kv3[..., :feat]), value_chunks_t(vt), toep_t, kk_p)
        res = _out_proj(xp, mixc.reshape(bsz * lq, C_CONV), mixa.reshape(bsz * lq, -1), wc, wa, tm,
                        final_norm_g if last else None)
        xp, yp = res if last else (res, None)
        kp.append(kv3[:, :lp, :feat].reshape(bsz, lp, N_KV_HEADS, HEAD_DIM))
        vp.append(kv3[:, :lp, feat:].reshape(bsz, lp, N_KV_HEADS, HEAD_DIM))
        ikp.append(tail3[:, :lp, :IDX_DIM])
        cp.append(u3[:, lp - (CONV_W - 1):lp])

        u, gc, kv, ga, tail, q, iq = _project(xs, norm_g[l], w, db, feature_major=False)
        mixc = _sample_conv(jnp.transpose(state_conv[l], (1, 0, 2)), u, gc,
                            conv_w[l], conv_b[l], conv_ln_g[l], conv_ln_b[l])
        iw3 = tail[:, IDX_DIM:IDX_DIM + N_IDX_HEADS].reshape(db, N_IDX_HEADS, 1)
        scores = _dec_scores(page_table, iq.reshape(db, N_IDX_HEADS, IDX_DIM), iw3, tail.reshape(db, 1, LANE),
                             cache_ik4, l, width_s)
        madd = _dec_select(scores.reshape(db, width_s).T, kk_s).T
        q4 = q.reshape(db, N_KV_HEADS, GROUP, HEAD_DIM)
        qz3 = jnp.concatenate([jnp.pad(q4[:, 0], ((0, 0), (0, 0), (0, HEAD_DIM))),
                               jnp.pad(q4[:, 1], ((0, 0), (0, 0), (HEAD_DIM, 0)))], axis=1)
        mixa = _dec_attend(page_table, qz3, kv.reshape(db, 1, 2 * feat),
                           madd.reshape(db, 1, width_s), bias_s, ga.reshape(db, N_HEADS // 2, LANE),
                           cache_k4, cache_v4, l)
        res = _out_proj(xs, mixc, mixa.reshape(db, -1), wc, wa, db, final_norm_g if last else None)
        xs, ys = res if last else (res, None)
        ksm.append(kv[:, :feat].reshape(db, 1, N_KV_HEADS, HEAD_DIM))
        vsm.append(kv[:, feat:].reshape(db, 1, N_KV_HEADS, HEAD_DIM))
        iks.append(tail[:, None, :IDX_DIM])
        cs.append(jnp.concatenate([state_conv[l][:, 1:], u[:, None]], axis=1))

    y_prompt = yp.reshape(bsz, lq, d)[:, N_META:lp]
    y_sample = ys.reshape(db, 1, d)
    return (y_prompt, y_sample, jnp.stack(kp), jnp.stack(vp), jnp.stack(ikp), jnp.stack(cp),
            jnp.stack(ksm), jnp.stack(vsm), jnp.stack(iks), jnp.stack(cs))
```

```python
import functools
import math

import numpy as np
import jax
import jax.numpy as jnp
from jax import lax
from jax.experimental import pallas as pl
from jax.experimental.pallas import tpu as pltpu

N_HEADS = 8
N_KV_HEADS = 2
GROUP = N_HEADS // N_KV_HEADS
HEAD_DIM = 64
N_IDX_HEADS = 4
IDX_DIM = 64
C_CONV = 512
CONV_W = 31
TOPK_MAX = 256
N_META = 16
NUM_BUCKETS = 32
MAX_DISTANCE = 128
EPS = 1e-6
LN_EPS = 1e-5

LANE = 128
ROWS = 8
TQ = 128
KC = 512
SUB = KC // TQ
N_ACC = 4
DIGIT_BITS = 7
DIGIT_MASK = (1 << DIGIT_BITS) - 1
KEY_DIGITS = (7, 7, 7, 7, 4)
KEY_SHIFTS = tuple(32 - sum(KEY_DIGITS[:i + 1]) for i in range(len(KEY_DIGITS)))
N_LEVELS = len(KEY_DIGITS)
V_ROWS = HEAD_DIM + 16
PAGE_SLOTS = 4
HALO = 32
CONV_SUB = 64
MXU_DTYPE = jnp.bfloat16
NEG = -1e30
LOG2E = 1.4426950408889634
INT_MIN = -2 ** 31
TIE_SPAN = 1 << 14
KEY_NEG_INF = -0x7F800000 - TIE_SPAN
VMEM_LIMIT = 56 * 1024 * 1024

_PROJ_SIZES = (C_CONV, C_CONV, C_CONV, N_HEADS * HEAD_DIM, N_KV_HEADS * HEAD_DIM, N_KV_HEADS * HEAD_DIM,
               N_HEADS * HEAD_DIM, N_IDX_HEADS * IDX_DIM, IDX_DIM, N_IDX_HEADS)
_UB_OFF, _GC_OFF, _Q_OFF, _K_OFF, _V_OFF, _GA_OFF, _IQ_OFF, _IK_OFF, _IW_OFF, _D_IN = (
    int(s) for s in np.cumsum(_PROJ_SIZES))
_W_ROWS = _IK_OFF + LANE


def _cparams(n_axes):
    return pltpu.CompilerParams(dimension_semantics=("arbitrary",) * n_axes, vmem_limit_bytes=VMEM_LIMIT)


def _silu(x):
    return x * jax.nn.sigmoid(x)


def _prep_w_in(w):
    assert w.shape[1] == _D_IN
    return jnp.pad(w.T, ((0, _W_ROWS - _D_IN), (0, 0))).astype(MXU_DTYPE)


def _t5_bucket(rel):
    n = jnp.maximum(rel, 0)
    max_exact = NUM_BUCKETS // 2
    large = max_exact + (jnp.log(jnp.maximum(n, 1).astype(jnp.float32) / max_exact)
                         / math.log(MAX_DISTANCE / max_exact)
                         * (NUM_BUCKETS - max_exact)).astype(jnp.int32)
    large = jnp.minimum(large, NUM_BUCKETS - 1)
    return jnp.where(n < max_exact, n, large)


def _bias_tiles(rel_bias):
    far = rel_bias[NUM_BUCKETS - 1]
    d = jnp.arange(2 * TQ, dtype=jnp.int32)
    dists = (jnp.where(d < TQ, d, 0),
             jnp.where(d < TQ, d + TQ, d - TQ))
    tiles = []
    for dist in dists:
        v = ((rel_bias[_t5_bucket(dist)] - far) * LOG2E).T
        rep = jnp.tile(v, (1, TQ))[:, :TQ * (2 * TQ - 1)].reshape(N_HEADS, TQ, 2 * TQ - 1)
        tiles.append(rep[:, :, :TQ])
    tiles.append(jnp.zeros_like(tiles[0]))
    return jnp.stack(tiles).astype(jnp.float32)


_NT = (((1,), (1,)), ((), ()))
_Q_SCALE = HEAD_DIM ** -0.5 * LOG2E
_IQ_SCALE = IDX_DIM ** -0.5
_IW_SCALE = N_IDX_HEADS ** -0.5
_IW_ROWS = 16


def _proj_kernel(x_ref, g_ref, w_ref, u_ref, gc_ref, kv_ref, ga_ref, tail_ref, *q_refs, feature_major):
    x = x_ref[...]
    ms = jnp.mean(x * x, axis=-1, keepdims=True)
    xn = (x * lax.rsqrt(ms + EPS) * g_ref[...]).astype(MXU_DTYPE)

    def mm(lo, hi):
        return lax.dot_general(xn, w_ref[lo:hi, :], _NT, preferred_element_type=jnp.float32)

    def mm_t(lo, hi):
        return lax.dot_general(w_ref[lo:hi, :], xn, _NT, preferred_element_type=jnp.float32)

    u_ref[...] = mm(0, _UB_OFF) * jax.nn.sigmoid(mm(_UB_OFF, _GC_OFF))
    gc_ref[...] = mm(_GC_OFF, _Q_OFF)
    kv_ref[...] = mm(_K_OFF, _GA_OFF)
    ga_ref[...] = mm(_GA_OFF, _IQ_OFF)
    t = mm(_IK_OFF, _W_ROWS)
    lane = lax.broadcasted_iota(jnp.int32, t.shape, 1)
    tail_ref[...] = t * jnp.where(lane >= IDX_DIM, _IW_SCALE, 1.0)
    if feature_major:
        qzt_ref, iqzt_ref, vt_ref, iwt_ref = q_refs
        qt = (mm_t(_Q_OFF, _K_OFF) * _Q_SCALE).astype(qzt_ref.dtype)
        qzt_ref[...] = jnp.zeros(qzt_ref.shape, qzt_ref.dtype)
        for h in range(N_HEADS):
            lo = h * LANE + (h // GROUP) * HEAD_DIM
            qzt_ref[lo:lo + HEAD_DIM, :] = qt[h * HEAD_DIM:(h + 1) * HEAD_DIM]
        iqt = (mm_t(_IQ_OFF, _IK_OFF) * _IQ_SCALE).astype(iqzt_ref.dtype)
        iqzt_ref[...] = jnp.zeros(iqzt_ref.shape, iqzt_ref.dtype)
        for h in range(N_IDX_HEADS):
            iqzt_ref[h * LANE:h * LANE + IDX_DIM, :] = iqt[h * IDX_DIM:(h + 1) * IDX_DIM]
        vt_ref[...] = mm_t(_V_OFF, _GA_OFF).astype(vt_ref.dtype)
        iwt_ref[...] = mm_t(_IW_OFF, _IW_OFF + _IW_ROWS) * _IW_SCALE
    else:
        q_ref, iq_ref = q_refs
        q_ref[...] = (mm(_Q_OFF, _K_OFF) * _Q_SCALE).astype(q_ref.dtype)
        iq_ref[...] = (mm(_IQ_OFF, _IK_OFF) * _IQ_SCALE).astype(iq_ref.dtype)


def _project(x, g, w, tm, feature_major):
    r, d = x.shape
    assert r % tm == 0
    rows = lambda c: pl.BlockSpec((tm, c), lambda i: (i, 0))
    cols = lambda c: pl.BlockSpec((c, tm), lambda i: (0, i))
    feat = N_KV_HEADS * HEAD_DIM
    specs = [rows(C_CONV), rows(C_CONV), rows(2 * feat), rows(N_HEADS * HEAD_DIM), rows(LANE)]
    shapes = [jax.ShapeDtypeStruct((r, s.block_shape[1]), jnp.float32) for s in specs]
    if feature_major:
        extra = [(N_HEADS * LANE, MXU_DTYPE), (N_IDX_HEADS * LANE, MXU_DTYPE), (feat, MXU_DTYPE),
                 (_IW_ROWS, jnp.float32)]
        specs += [cols(c) for c, _ in extra]
        shapes += [jax.ShapeDtypeStruct((c, r), t) for c, t in extra]
    else:
        extra = [(N_HEADS * HEAD_DIM, MXU_DTYPE), (N_IDX_HEADS * IDX_DIM, MXU_DTYPE)]
        specs += [rows(c) for c, _ in extra]
        shapes += [jax.ShapeDtypeStruct((r, c), t) for c, t in extra]
    return pl.pallas_call(
        functools.partial(_proj_kernel, feature_major=feature_major),
        grid=(r // tm,),
        in_specs=[rows(d), pl.BlockSpec((1, d), lambda i: (0, 0)), pl.BlockSpec((_W_ROWS, d), lambda i: (0, 0))],
        out_specs=specs,
        out_shape=shapes,
        compiler_params=_cparams(1),
        name="proj",
    )(x, g.reshape(1, d), w)


def _conv_kernel(prev_ref, cur_ref, gc_ref, w_ref, b_ref, lg_ref, lb_ref, o_ref, ext_ref):
    i = pl.program_id(1)
    ext_ref[0, 0:HALO, :] = jnp.where(i > 0, prev_ref[...], 0.0)
    ext_ref[0, HALO:HALO + TQ, :] = cur_ref[...]
    n_ext = HALO + TQ
    for s in range(1, ROWS):
        ext_ref[s, 0:n_ext - ROWS, :] = ext_ref[0, s:s + n_ext - ROWS, :]
    off = HALO - (CONV_W - 1)
    for r0 in range(0, TQ, CONV_SUB):
        acc = jnp.zeros((CONV_SUB, C_CONV), jnp.float32)
        for j in range(CONV_W):
            s = (off + j) % ROWS
            lo = r0 + off + j - s
            acc = acc + w_ref[j:j + 1, :] * ext_ref[s, lo:lo + CONV_SUB, :]
        y = acc + b_ref[...]
        mu = jnp.mean(y, axis=-1, keepdims=True)
        dev = y - mu
        var = jnp.mean(dev * dev, axis=-1, keepdims=True)
        yn = dev * lax.rsqrt(var + LN_EPS) * lg_ref[...] + lb_ref[...]
        o_ref[r0:r0 + CONV_SUB, :] = (_silu(yn) * _silu(gc_ref[r0:r0 + CONV_SUB, :])).astype(o_ref.dtype)


def _conv_branch(u, gc, w, b, lg, lb):
    bsz, lq, c = u.shape
    per = TQ // HALO
    row = lambda a: a.reshape(1, c)
    vec = pl.BlockSpec((1, c), lambda bi, i: (0, 0))
    tile = pl.BlockSpec((None, TQ, c), lambda bi, i: (bi, i, 0))
    return pl.pallas_call(
        _conv_kernel,
        grid=(bsz, lq // TQ),
        in_specs=[pl.BlockSpec((None, HALO, c), lambda bi, i: (bi, jnp.maximum(i * per - 1, 0), 0)),
                  tile, tile, pl.BlockSpec((CONV_W, c), lambda bi, i: (0, 0)), vec, vec, vec],
        out_specs=tile,
        out_shape=jax.ShapeDtypeStruct((bsz, lq, c), MXU_DTYPE),
        scratch_shapes=[pltpu.VMEM((ROWS, HALO + TQ, c), jnp.float32)],
        compiler_params=_cparams(2),
        name="conv",
    )(u, u, gc, w, row(b), row(lg), row(lb))


_INT_MIN = np.int32(INT_MIN)
_BYTE_ONES = np.int32(0x01010101)
_BYTE_LOW = np.int32(0x7F7F7F7F)
_HALF_LOW_BYTES = np.int32(0x00FF00FF)
_GUARD = np.int32(0x80808080 - (1 << 32))
assert SUB == 4 and DIGIT_BITS == 7


def _to_key(s, idx):
    bits = lax.bitcast_convert_type(s, jnp.int32)
    key = jnp.where(bits < 0, (_INT_MIN - bits) - TIE_SPAN, bits)
    return jnp.where((bits == 0) | (bits == _INT_MIN), -1 - idx, key)


def _for_chunks(nkc, body, init, unroll):
    n_main = nkc // unroll

    def main(i, carry):
        for k in range(unroll):
            carry = body(i * unroll + k, carry)
        return carry

    carry = lax.fori_loop(0, n_main, main, init)
    return lax.fori_loop(n_main * unroll, nkc, body, carry)


def _pack_digits(keys_ref, dig_ref, c):
    for lv in range(N_LEVELS):
        word = None
        for j in range(SUB):
            u = keys_ref[c, j * TQ:(j + 1) * TQ, :]
            if lv == 0:
                u = u ^ _INT_MIN
            move = KEY_SHIFTS[lv] - 8 * j
            u = lax.shift_right_logical(u, move) if move >= 0 else lax.shift_left(u, -move)
            field = u & np.int32(((1 << KEY_DIGITS[lv]) - 1) << (8 * j))
            word = field if word is None else word | field
        dig_ref[lv, c] = word


def _count_fields(work_ref, nkc, cand_bytes):
    q = work_ref.shape[2]

    def body(c, accs):
        accs = list(accs)
        for n, r in enumerate(range(0, TQ, ROWS)):
            diff = work_ref[c, r:r + ROWS, :] - cand_bytes
            accs[n % N_ACC] = accs[n % N_ACC] + (lax.shift_right_logical(diff, DIGIT_BITS) & _BYTE_ONES)
        return tuple(accs)

    accs = _for_chunks(nkc, body, (jnp.zeros((ROWS, q), jnp.int32),) * N_ACC, unroll=4)
    halves = jnp.zeros((ROWS, q), jnp.int32)
    for a in accs:
        halves = halves + (a & _HALF_LOW_BYTES) + (lax.shift_right_logical(a, 8) & _HALF_LOW_BYTES)
    total = (halves & 0xFFFF) + lax.shift_right_logical(halves, 16)
    return jnp.sum(total.astype(jnp.float32), axis=0, keepdims=True)


def _digit_search(dig_ref, work_ref, alive_ref, nkc, shifts, digits, target):
    q = dig_ref.shape[3]
    assert dig_ref.shape[1] * (TQ // ROWS) <= 255 * N_ACC
    zero = jnp.zeros((1, q), jnp.float32)
    value = jnp.zeros((1, q), jnp.int32)
    above, done, cnt_ge, cnt_gt = zero, zero, target + 1.0, zero

    for lv, (shift, n_bits) in enumerate(zip(shifts, digits)):
        def load_fields(c, carry):
            work_ref[c] = (dig_ref[lv, c] & alive_ref[c]) | _GUARD
            return carry

        _for_chunks(nkc, load_fields, 0, unroll=4)

        def digit_bit(i, state):
            digit, cnt_ge, cnt_gt, cnt_rej, done = state
            cand = digit | lax.shift_left(jnp.int32(1), n_bits - 1 - i)
            cnt_alive = _count_fields(work_ref, nkc, cand * _BYTE_ONES)
            cnt = above + cnt_alive
            take = (cnt >= target) & (done == 0.0)
            drop = (cnt < target) & (done == 0.0)
            return (jnp.where(take, cand, digit), jnp.where(take, cnt, cnt_ge), jnp.where(drop, cnt, cnt_gt),
                    jnp.where(drop, cnt_alive, cnt_rej),
                    jnp.where(take & (cnt == target), 1.0, done))

        digit, cnt_ge, cnt_gt, cnt_rej, done = lax.fori_loop(
            0, n_bits, digit_bit, (jnp.zeros((1, q), jnp.int32), cnt_ge, cnt_gt, zero, done))
        above = above + cnt_rej
        value = value | lax.shift_left(digit, shift)

        if lv + 1 < len(shifts):
            digit_bytes = digit * _BYTE_ONES

            def narrow(c, carry):
                differs = lax.shift_right_logical(((dig_ref[lv, c] ^ digit_bytes) + _BYTE_LOW) & _GUARD, DIGIT_BITS)
                alive_ref[c] = alive_ref[c] & ((_BYTE_ONES - differs) * DIGIT_MASK)
                return carry

            _for_chunks(nkc, narrow, 0, unroll=4)

    return value, cnt_ge, cnt_gt, done


def _fill(ref, nkc, word):
    def body(c, carry):
        ref[c] = jnp.full(ref.shape[1:], word, ref.dtype)
        return carry

    _for_chunks(nkc, body, 0, unroll=4)


def _threshold(keys_ref, dig_ref, work_ref, alive_ref, nkc, kk):
    q = keys_ref.shape[2]
    kf = jnp.full((1, q), kk, jnp.float32)
    _fill(alive_ref, nkc, _BYTE_LOW)
    prefix, cnt_thr, cnt_gt, _ = _digit_search(dig_ref, work_ref, alive_ref, nkc, KEY_SHIFTS, KEY_DIGITS, kf)
    thr = prefix ^ _INT_MIN
    admissible = thr != KEY_NEG_INF
    surplus = (cnt_thr > kf) & admissible

    @pl.when(jnp.max(jnp.where(surplus, 1.0, 0.0)) > 0.0)
    def _():
        assert keys_ref.shape[0] * KC <= 1 << (2 * DIGIT_BITS)
        top = (1 << (2 * DIGIT_BITS)) - 1
        row = lax.broadcasted_iota(jnp.int32, (TQ, q), 0)

        def tie_fields(c, carry):
            hi = lo = tie = None
            for j in range(SUB):
                is_tie = keys_ref[c, j * TQ:(j + 1) * TQ, :] == thr
                rev = top - (c * KC + j * TQ + row)
                fields = [jnp.where(is_tie, f, 0) for f in
                          (lax.shift_right_logical(rev, DIGIT_BITS), rev & DIGIT_MASK, DIGIT_MASK)]
                fields = [f if j == 0 else lax.shift_left(f, 8 * j) for f in fields]
                hi, lo, tie = fields if j == 0 else (hi | fields[0], lo | fields[1], tie | fields[2])
            dig_ref[0, c], dig_ref[1, c], alive_ref[c] = hi, lo, tie
            return carry

        lax.fori_loop(0, nkc, tie_fields, 0)
        cut, _, _, _ = _digit_search(dig_ref, work_ref, alive_ref, nkc, (DIGIT_BITS, 0), (DIGIT_BITS, DIGIT_BITS),
                                     kf - cnt_gt)

        def drop_losers(c, carry):
            for j in range(SUB):
                rows = slice(j * TQ, (j + 1) * TQ)
                x = keys_ref[c, rows, :]
                lost = (x == thr) & (top - (c * KC + j * TQ + row) < cut)
                keys_ref[c, rows, :] = jnp.where(lost, thr - 1, x)
            return carry

        lax.fori_loop(0, nkc, drop_losers, 0)

    return jnp.where(admissible, thr, KEY_NEG_INF + 1)


def _attend_kernel(iqt_ref, iwt_ref, qzt_ref, ga_ref, ik_ref, k_ref, vt_ref, toep_ref, o_ref,
                   keys_ref, dig_ref, work_ref, alive_ref, m_ref, acc_ref, lg0_ref, lg1_ref,
                   wq_ref, *, kk):
    qi = pl.program_id(1)
    nkc = qi // SUB + 1
    qpos = qi * TQ + lax.broadcasted_iota(jnp.int32, (TQ, TQ), 1)
    krow = lax.broadcasted_iota(jnp.int32, (TQ, TQ), 0)
    for h in range(N_HEADS):
        wq_ref[:, h * TQ:(h + 1) * TQ] = qzt_ref[h]

    def score_chunk(c, carry, causal):
        iqt = jnp.concatenate([iqt_ref[h] for h in range(N_IDX_HEADS)], axis=1)
        w = iwt_ref[...]
        for j in range(SUB):
            st = jnp.dot(ik_ref[c, j * TQ:(j + 1) * TQ, :], iqt, preferred_element_type=jnp.float32)
            s = jnp.zeros((TQ, TQ), jnp.float32)
            for h in range(N_IDX_HEADS):
                s = s + w[h:h + 1, :] * jnp.maximum(st[:, h * TQ:(h + 1) * TQ], 0.0)
            kpos = c * KC + j * TQ + krow
            if causal:
                s = jnp.where(kpos <= qpos, s, -jnp.inf)
            keys_ref[c, j * TQ:(j + 1) * TQ, :] = _to_key(s, kpos)
        _pack_digits(keys_ref, dig_ref, c)
        return carry

    _for_chunks(nkc - 1, functools.partial(score_chunk, causal=False), 0, unroll=2)
    score_chunk(nkc - 1, 0, causal=True)
    thr = _threshold(keys_ref, dig_ref, work_ref, alive_ref, nkc, kk)

    m_ref[...] = jnp.full(m_ref.shape, NEG, jnp.float32)
    acc_ref[...] = jnp.zeros(acc_ref.shape, jnp.float32)

    def bias(c, h):
        return jnp.concatenate([toep_ref[jnp.clip(qi - (c * SUB + j), 0, 2), h] for j in range(SUB)], axis=0)

    def logits(c, lg_ref, near):
        md = jnp.where(keys_ref[c] >= thr, 0.0, NEG)
        add = [md + bias(c, h) for h in range(N_HEADS)] if near else [md] * N_HEADS
        lg_ref[...] = (jnp.dot(k_ref[c], wq_ref[...], preferred_element_type=jnp.float32)
                       + jnp.concatenate(add, axis=1))

    def add_bias(c, lg_ref):
        for h in range(N_HEADS):
            sl = slice(h * TQ, (h + 1) * TQ)
            lg_ref[:, sl] = lg_ref[:, sl] + bias(c, h)

    def softmax_pv(c, lg_ref):
        for pr in range(N_HEADS // 2):
            kv = pr // (GROUP // 2)
            sl = slice(2 * pr * TQ, (2 * pr + 2) * TQ)
            m_old = m_ref[pr]
            m_new = jnp.maximum(m_old, jnp.max(lg_ref[:, sl], axis=0, keepdims=True))
            alpha = jnp.exp2(m_old - m_new)
            p = jnp.exp2(lg_ref[:, sl] - m_new)
            pv = jnp.dot(vt_ref[c, kv * V_ROWS:(kv + 1) * V_ROWS, :], p.astype(MXU_DTYPE),
                         preferred_element_type=jnp.float32)
            acc_ref[pr] = alpha * acc_ref[pr] + pv
            m_ref[pr] = m_new

    n_far2 = (jnp.maximum(qi - 1, 0) // SUB) // 2

    def far_pair(i, carry):
        logits(2 * i + 1, lg1_ref, near=False)
        softmax_pv(2 * i, lg0_ref)
        logits(2 * i + 2, lg0_ref, near=False)
        softmax_pv(2 * i + 1, lg1_ref)
        return carry

    logits(0, lg0_ref, near=False)
    lax.fori_loop(0, n_far2, far_pair, 0)

    c0 = 2 * n_far2
    add_bias(c0, lg0_ref)

    @pl.when(c0 + 1 < nkc)
    def _():
        logits(c0 + 1, lg1_ref, near=True)
        softmax_pv(c0, lg0_ref)

    @pl.when(c0 + 1 >= nkc)
    def _():
        softmax_pv(c0, lg0_ref)

    @pl.when(c0 + 2 < nkc)
    def _():
        logits(c0 + 2, lg0_ref, near=True)
        softmax_pv(c0 + 1, lg1_ref)
        softmax_pv(c0 + 2, lg0_ref)

    @pl.when((c0 + 1 < nkc) & (c0 + 2 >= nkc))
    def _():
        softmax_pv(c0 + 1, lg1_ref)

    for pr in range(N_HEADS // 2):
        acc = acc_ref[pr]
        o_t = acc[:HEAD_DIM] / acc[HEAD_DIM:HEAD_DIM + 1]
        pair_t = jnp.concatenate([o_t[:, :TQ], o_t[:, TQ:]], axis=0)
        sl = slice(pr * LANE, (pr + 1) * LANE)
        o_ref[:, sl] = (pair_t.T * _silu(ga_ref[:, sl])).astype(o_ref.dtype)


def _attend(iqt, iwt, qzt, ga, ik, k, vt, toep_t, kk):
    bsz, lq = ga.shape[:2]
    nq = lq // TQ
    nc = k.shape[1]
    lanes = lambda a: pl.BlockSpec(a.shape[:-1] + (TQ,), lambda b, i: (0,) * (a.ndim - 1) + (b * nq + i,))
    whole = lambda a: pl.BlockSpec((None,) + a.shape[1:], lambda b, i: (b,) + (0,) * (a.ndim - 1))
    rows = pl.BlockSpec((None, TQ, N_HEADS * HEAD_DIM), lambda b, i: (b, i, 0))
    return pl.pallas_call(
        functools.partial(_attend_kernel, kk=kk),
        grid=(bsz, nq),
        in_specs=[lanes(iqt), lanes(iwt), lanes(qzt), rows, whole(ik), whole(k), whole(vt),
                  pl.BlockSpec(toep_t.shape, lambda b, i: (0, 0, 0, 0))],
        out_specs=rows,
        out_shape=jax.ShapeDtypeStruct((bsz, lq, N_HEADS * HEAD_DIM), MXU_DTYPE),
        scratch_shapes=[pltpu.VMEM((nc, KC, TQ), jnp.int32),
                        pltpu.VMEM((N_LEVELS, nc, TQ, TQ), jnp.int32), pltpu.VMEM((nc, TQ, TQ), jnp.int32),
                        pltpu.VMEM((nc, TQ, TQ), jnp.int32),
                        pltpu.VMEM((N_HEADS // 2, 1, 2 * TQ), jnp.float32),
                        pltpu.VMEM((N_HEADS // 2, V_ROWS, 2 * TQ), jnp.float32),
                        pltpu.VMEM((KC, N_HEADS * TQ), jnp.float32), pltpu.VMEM((KC, N_HEADS * TQ), jnp.float32),
                        pltpu.VMEM((LANE, N_HEADS * TQ), MXU_DTYPE)],
        compiler_params=_cparams(2),
        name="attend",
    )(iqt, iwt, qzt, ga, ik, k, vt, toep_t)


def _out_kernel(x_ref, mc_ref, ma_ref, wc_ref, wa_ref, o_ref):
    o_ref[...] = (x_ref[...]
                  + jnp.dot(mc_ref[...], wc_ref[...], preferred_element_type=jnp.float32)
                  + jnp.dot(ma_ref[...], wa_ref[...], preferred_element_type=jnp.float32))


def _out_final_kernel(x_ref, mc_ref, ma_ref, wc_ref, wa_ref, g_ref, o_ref, y_ref):
    x = (x_ref[...]
         + jnp.dot(mc_ref[...], wc_ref[...], preferred_element_type=jnp.float32)
         + jnp.dot(ma_ref[...], wa_ref[...], preferred_element_type=jnp.float32))
    o_ref[...] = x
    ms = jnp.mean(x * x, axis=-1, keepdims=True)
    y_ref[...] = x * lax.rsqrt(ms + EPS) * g_ref[...]


def _out_proj(x, mc, ma, wc, wa, tm, final_g=None):
    r, d = x.shape
    c = mc.shape[1]
    rows = lambda w: pl.BlockSpec((tm, w), lambda i: (i, 0))
    full = lambda a: pl.BlockSpec(a.shape, lambda i: (0, 0))
    if final_g is None:
        return pl.pallas_call(
            _out_kernel, grid=(r // tm,),
            in_specs=[rows(d), rows(c), rows(c), full(wc), full(wa)],
            out_specs=rows(d), out_shape=jax.ShapeDtypeStruct((r, d), jnp.float32),
            compiler_params=_cparams(1), name="out",
        )(x, mc, ma, wc, wa)
    g = final_g.reshape(1, d)
    return pl.pallas_call(
        _out_final_kernel, grid=(r // tm,),
        in_specs=[rows(d), rows(c), rows(c), full(wc), full(wa), full(g)],
        out_specs=[rows(d), rows(d)], out_shape=[jax.ShapeDtypeStruct((r, d), jnp.float32)] * 2,
        compiler_params=_cparams(1), name="out_final",
    )(x, mc, ma, wc, wa, g)


def _sconv_kernel(st_ref, u_ref, gc_ref, w_ref, b_ref, lg_ref, lb_ref, o_ref):
    acc = jnp.zeros(u_ref.shape, jnp.float32)
    for j in range(CONV_W - 1):
        acc = acc + w_ref[j:j + 1, :] * st_ref[j]
    acc = acc + w_ref[CONV_W - 1:CONV_W, :] * u_ref[...]
    y = acc + b_ref[...]
    mu = jnp.mean(y, axis=-1, keepdims=True)
    dev = y - mu
    var = jnp.mean(dev * dev, axis=-1, keepdims=True)
    yn = dev * lax.rsqrt(var + LN_EPS) * lg_ref[...] + lb_ref[...]
    o_ref[...] = (_silu(yn) * _silu(gc_ref[...])).astype(o_ref.dtype)


def _sample_conv(state_t, u, gc, w, b, lg, lb):
    db, c = u.shape
    row = lambda a: a.reshape(1, c)
    full = lambda a: pl.BlockSpec(a.shape, lambda i: (0,) * a.ndim)
    args = (state_t, u, gc, w, row(b), row(lg), row(lb))
    return pl.pallas_call(
        _sconv_kernel, grid=(1,), in_specs=[full(a) for a in args],
        out_specs=pl.BlockSpec((db, c), lambda i: (0, 0)),
        out_shape=jax.ShapeDtypeStruct((db, c), MXU_DTYPE),
        compiler_params=_cparams(1), name="sconv",
    )(*args)


def _dec_score_kernel(pt_ref, iq_ref, iw_ref, ikn_ref, cik_ref, o_ref, buf_ref, sem_ref, *, layer, width):
    slot = _fetch_pages(pt_ref, (cik_ref,), (buf_ref,), (sem_ref,), layer)
    iq = iq_ref[...][:, :IDX_DIM]
    w = iw_ref[...]
    ikt = buf_ref[slot].astype(MXU_DTYPE)
    s = jnp.dot(iq, ikt, preferred_element_type=jnp.float32)
    sc = jnp.sum(w * jnp.maximum(s, 0.0), axis=0, keepdims=True)
    ikn = ikn_ref[...][:, :IDX_DIM].astype(MXU_DTYPE).astype(jnp.float32)
    sn = jnp.sum(iq.astype(jnp.float32) * ikn, axis=-1, keepdims=True)
    scn = jnp.sum(w * jnp.maximum(sn, 0.0), axis=0, keepdims=True)
    lane = lax.broadcasted_iota(jnp.int32, (1, width - sc.shape[1]), 1)
    o_ref[...] = jnp.concatenate([sc, jnp.where(lane == 0, scn, -jnp.inf)], axis=1)


def _dec_scores(page_table, iq3, iw3, tail3, cache_ik, layer, width):
    db, n_pages = page_table.shape
    di, page = cache_ik.shape[2:]
    seq = lambda a: pl.BlockSpec((None,) + a.shape[1:], lambda b, pt: (b, 0, 0))
    return pl.pallas_call(
        functools.partial(_dec_score_kernel, layer=layer, width=width),
        grid_spec=pltpu.PrefetchScalarGridSpec(
            num_scalar_prefetch=1, grid=(db,),
            in_specs=[seq(iq3), seq(iw3), seq(tail3), pl.BlockSpec(memory_space=pl.ANY)],
            out_specs=pl.BlockSpec((None, 1, width), lambda b, pt: (b, 0, 0)),
            scratch_shapes=[pltpu.VMEM((PAGE_SLOTS, di, n_pages * page), cache_ik.dtype),
                            pltpu.SemaphoreType.DMA((PAGE_SLOTS,))]),
        out_shape=jax.ShapeDtypeStruct((db, 1, width), jnp.float32),
        compiler_params=_cparams(1), name="dec_score",
    )(page_table, iq3, iw3, tail3, cache_ik)


def _dec_select_kernel(s_ref, o_ref, keys_ref, dig_ref, work_ref, alive_ref, *, kk):
    nc = keys_ref.shape[0]
    row = lax.broadcasted_iota(jnp.int32, (KC, keys_ref.shape[2]), 0)
    for c in range(nc):
        keys_ref[c] = _to_key(s_ref[c * KC:(c + 1) * KC, :], c * KC + row)
        _pack_digits(keys_ref, dig_ref, c)
    thr = _threshold(keys_ref, dig_ref, work_ref, alive_ref, nc, kk)
    for c in range(nc):
        o_ref[c * KC:(c + 1) * KC, :] = jnp.where(keys_ref[c] >= thr, 0.0, NEG)


def _dec_select(scores_t, kk):
    width, db = scores_t.shape
    nc = width // KC
    spec = pl.BlockSpec((width, db), lambda i: (0, 0))
    return pl.pallas_call(
        functools.partial(_dec_select_kernel, kk=kk),
        grid=(1,), in_specs=[spec], out_specs=spec,
        out_shape=jax.ShapeDtypeStruct((width, db), jnp.float32),
        scratch_shapes=[pltpu.VMEM((nc, KC, db), jnp.int32),
                        pltpu.VMEM((N_LEVELS, nc, TQ, db), jnp.int32), pltpu.VMEM((nc, TQ, db), jnp.int32),
                        pltpu.VMEM((nc, TQ, db), jnp.int32)],
        compiler_params=_cparams(1), name="dec_select",
    )(scores_t)


def _page_copies(pt_ref, cache_ref, buf_ref, sem_ref, layer, seq, slot):
    page = cache_ref.shape[3]
    n_pages = buf_ref.shape[2] // page
    return [pltpu.make_async_copy(cache_ref.at[layer, pt_ref[seq, p]],
                                  buf_ref.at[slot, :, pl.ds(p * page, page)], sem_ref.at[slot])
            for p in range(n_pages)]


def _fetch_pages(pt_ref, caches, bufs, sems, layer):
    b, nb = pl.program_id(0), pl.num_programs(0)
    ahead = PAGE_SLOTS - 1

    def start(seq, slot):
        for k, (cache_ref, buf_ref, sem_ref) in enumerate(zip(caches, bufs, sems)):
            for p, cp in enumerate(_page_copies(pt_ref, cache_ref, buf_ref, sem_ref, layer, seq, slot)):
                cp.start(priority=(k + p) % 2)

    for s in range(ahead):
        @pl.when((b == 0) & (s < nb))
        def _():
            start(s, s)

    @pl.when(b + ahead < nb)
    def _():
        start(b + ahead, (b + ahead) % PAGE_SLOTS)

    slot = b % PAGE_SLOTS
    for cache_ref, buf_ref, sem_ref in zip(caches, bufs, sems):
        for cp in _page_copies(pt_ref, cache_ref, buf_ref, sem_ref, layer, b, slot):
            cp.wait()
    return slot


def _dec_attend_kernel(pt_ref, qz_ref, kvn_ref, madd_ref, bias_ref, ga_ref, ck_ref, cv_ref, o_ref,
                       kbuf_ref, vbuf_ref, ksem_ref, vsem_ref, *, layer):
    slot = _fetch_pages(pt_ref, (ck_ref, cv_ref), (kbuf_ref, vbuf_ref), (ksem_ref, vsem_ref), layer)
    qz = qz_ref[...]
    past = kbuf_ref.shape[2]
    kt = kbuf_ref[slot].astype(MXU_DTYPE)
    vt = vbuf_ref[slot].astype(MXU_DTYPE)
    lg = (jnp.dot(qz, kt, preferred_element_type=jnp.float32)
          + bias_ref[:, :past] + madd_ref[:, :past])
    kvn = kvn_ref[...].astype(MXU_DTYPE).astype(jnp.float32)
    lgn = jnp.sum(qz.astype(jnp.float32) * kvn[:, :LANE], axis=-1, keepdims=True)
    lgn = lgn + bias_ref[:, past:past + 1] + madd_ref[:, past:past + 1]
    m = jnp.maximum(lgn, jnp.max(lg, axis=-1, keepdims=True))
    pn = jnp.exp2(lgn - m)
    pp = jnp.exp2(lg - m)
    den = pn + jnp.sum(pp, axis=-1, keepdims=True)
    o = (pn.astype(MXU_DTYPE).astype(jnp.float32) * kvn[:, LANE:]
         + lax.dot_general(pp.astype(MXU_DTYPE), vt, _NT, preferred_element_type=jnp.float32))
    o = o / den
    swapped = pltpu.roll(o, HEAD_DIM, axis=1)
    lane = lax.broadcasted_iota(jnp.int32, (1, LANE), 1)
    pairs = []
    for j in range(N_HEADS // 2):
        first, second = (o, swapped) if 2 * j < GROUP else (swapped, o)
        pairs.append(jnp.where(lane < HEAD_DIM, first[2 * j:2 * j + 1], second[2 * j + 1:2 * j + 2]))
    o_ref[...] = (jnp.concatenate(pairs, axis=0) * _silu(ga_ref[...])).astype(o_ref.dtype)


def _dec_attend(page_table, qz3, kvn3, madd3, bias, ga3, cache_k, cache_v, layer):
    db, n_pages = page_table.shape
    feat, page = cache_k.shape[2:]
    seq = lambda a: pl.BlockSpec((None,) + a.shape[1:], lambda b, pt: (b, 0, 0))
    hbm = pl.BlockSpec(memory_space=pl.ANY)
    buf = pltpu.VMEM((PAGE_SLOTS, feat, n_pages * page), cache_k.dtype)
    sem = pltpu.SemaphoreType.DMA((PAGE_SLOTS,))
    return pl.pallas_call(
        functools.partial(_dec_attend_kernel, layer=layer),
        grid_spec=pltpu.PrefetchScalarGridSpec(
            num_scalar_prefetch=1, grid=(db,),
            in_specs=[seq(qz3), seq(kvn3), seq(madd3), pl.BlockSpec(bias.shape, lambda b, pt: (0, 0)), seq(ga3),
                      hbm, hbm],
            out_specs=pl.BlockSpec((None, GROUP, LANE), lambda b, pt: (b, 0, 0)),
            scratch_shapes=[buf, buf, sem, sem]),
        out_shape=jax.ShapeDtypeStruct((db, GROUP, LANE), MXU_DTYPE),
        compiler_params=_cparams(1), name="dec_attend",
    )(page_table, qz3, kvn3, madd3, bias, ga3, cache_k, cache_v)


def _round_up(x, m):
    return -(-x // m) * m


def _row_tile(rows):
    for f in (5, 4, 3, 2, 1):
        if rows % (f * TQ) == 0:
            return f * TQ
    return rows


def kernel(x_prompt, x_sample, cache_k, cache_v, cache_idx_k, state_conv, page_table, meta_tokens, rel_bias,
           norm_g, w_in, conv_w, conv_b, conv_ln_g, conv_ln_b, w_out, final_norm_g):
    bsz, seq, d = x_prompt.shape
    depth = w_in.shape[0]
    lp = seq + N_META
    lq = _round_up(lp, TQ)
    lk = _round_up(lp, KC)
    nq, nc = lq // TQ, lk // KC
    kk_p = min(TOPK_MAX, lp // 4)
    db = x_sample.shape[0]
    n_pool, page = cache_k.shape[1:3]
    n_pages = page_table.shape[1]
    past = n_pages * page
    kk_s = min(TOPK_MAX, (past + 1) // 4)
    width_s = _round_up(past + 1, KC)
    feat = N_KV_HEADS * HEAD_DIM
    tm = _row_tile(bsz * lq)

    xp = jnp.concatenate([jnp.broadcast_to(meta_tokens[None].astype(x_prompt.dtype), (bsz, N_META, d)), x_prompt,
                          jnp.zeros((bsz, lq - lp, d), x_prompt.dtype)], axis=1).reshape(bsz * lq, d)
    xs = x_sample.reshape(db, d)
    toep_t = _bias_tiles(rel_bias)
    bias_s = jnp.moveaxis(rel_bias[_t5_bucket(past - jnp.arange(width_s, dtype=jnp.int32))], -1, 0) * LOG2E
    cache_k4 = jnp.transpose(cache_k, (0, 1, 3, 4, 2)).reshape(depth, n_pool, feat, page)
    cache_v4 = jnp.transpose(cache_v, (0, 1, 3, 4, 2)).reshape(depth, n_pool, feat, page)
    cache_ik4 = jnp.swapaxes(cache_idx_k, 2, 3)

    def key_chunks(a):
        return jnp.pad(a, ((0, 0), (0, lk - lq), (0, 0))).astype(MXU_DTYPE).reshape(bsz, nc, KC, LANE)

    def value_chunks_t(a):
        a = jnp.pad(a.reshape(N_KV_HEADS, HEAD_DIM, bsz, lq), ((0, 0), (0, 0), (0, 0), (0, lk - lq)))
        a = jnp.concatenate([a, jnp.ones((N_KV_HEADS, V_ROWS - HEAD_DIM, bsz, lk), a.dtype)], axis=1)
        return jnp.transpose(a.reshape(N_KV_HEADS * V_ROWS, bsz, nc, KC), (1, 2, 0, 3))

    kp, vp, ikp, cp, ksm, vsm, iks, cs = ([] for _ in range(8))
    yp = ys = None
    for l in range(depth):
        w = _prep_w_in(w_in[l])
        wc = w_out[l, :C_CONV].astype(MXU_DTYPE)
        wa = w_out[l, C_CONV:].astype(MXU_DTYPE)
        last = l == depth - 1

        u, gc, kv, ga, tail, qzt, iqzt, vt, iwt = _project(xp, norm_g[l], w, tm, feature_major=True)
        b3 = lambda a: a.reshape(bsz, lq, a.shape[-1])
        u3, kv3, tail3 = b3(u), b3(kv), b3(tail)
        mixc = _conv_branch(u3, b3(gc), conv_w[l], conv_b[l], conv_ln_g[l], conv_ln_b[l])
        lane = jnp.arange(LANE)
        ik = key_chunks(jnp.where(lane < IDX_DIM, tail3, 0.0))
        mixa = _attend(iqzt.reshape(N_IDX_HEADS, LANE, bsz * lq), iwt, qzt.reshape(N_HEADS, LANE, bsz * lq), b3(ga),
                       ik, key_chunks(kv3[..., :feat]), value_chunks_t(vt), toep_t, kk_p)
        res = _out_proj(xp, mixc.reshape(bsz * lq, C_CONV), mixa.reshape(bsz * lq, -1), wc, wa, tm,
                        final_norm_g if last else None)
        xp, yp = res if last else (res, None)
        kp.append(kv3[:, :lp, :feat].reshape(bsz, lp, N_KV_HEADS, HEAD_DIM))
        vp.append(kv3[:, :lp, feat:].reshape(bsz, lp, N_KV_HEADS, HEAD_DIM))
        ikp.append(tail3[:, :lp, :IDX_DIM])
        cp.append(u3[:, lp - (CONV_W - 1):lp])

        u, gc, kv, ga, tail, q, iq = _project(xs, norm_g[l], w, db, feature_major=False)
        mixc = _sample_conv(jnp.transpose(state_conv[l], (1, 0, 2)), u, gc,
                            conv_w[l], conv_b[l], conv_ln_g[l], conv_ln_b[l])
        iw3 = tail[:, IDX_DIM:IDX_DIM + N_IDX_HEADS].reshape(db, N_IDX_HEADS, 1)
        scores = _dec_scores(page_table, iq.reshape(db, N_IDX_HEADS, IDX_DIM), iw3, tail.reshape(db, 1, LANE),
                             cache_ik4, l, width_s)
        madd = _dec_select(scores.reshape(db, width_s).T, kk_s).T
        q4 = q.reshape(db, N_KV_HEADS, GROUP, HEAD_DIM)
        qz3 = jnp.concatenate([jnp.pad(q4[:, 0], ((0, 0), (0, 0), (0, HEAD_DIM))),
                               jnp.pad(q4[:, 1], ((0, 0), (0, 0), (HEAD_DIM, 0)))], axis=1)
        mixa = _dec_attend(page_table, qz3, kv.reshape(db, 1, 2 * feat),
                           madd.reshape(db, 1, width_s), bias_s, ga.reshape(db, N_HEADS // 2, LANE),
                           cache_k4, cache_v4, l)
        res = _out_proj(xs, mixc, mixa.reshape(db, -1), wc, wa, db, final_norm_g if last else None)
        xs, ys = res if last else (res, None)
        ksm.append(kv[:, :feat].reshape(db, 1, N_KV_HEADS, HEAD_DIM))
        vsm.append(kv[:, feat:].reshape(db, 1, N_KV_HEADS, HEAD_DIM))
        iks.append(tail[:, None, :IDX_DIM])
        cs.append(jnp.concatenate([state_conv[l][:, 1:], u[:, None]], axis=1))

    y_prompt = yp.reshape(bsz, lq, d)[:, N_META:lp]
    y_sample = ys.reshape(db, 1, d)
    return (y_prompt, y_sample, jnp.stack(kp), jnp.stack(vp), jnp.stack(ikp), jnp.stack(cp),
            jnp.stack(ksm), jnp.stack(vsm), jnp.stack(iks), jnp.stack(cs))
```

```python
import functools
import math

import numpy as np
import jax
import jax.numpy as jnp
from jax import lax
from jax.experimental import pallas as pl
from jax.experimental.pallas import tpu as pltpu

N_HEADS = 8
N_KV_HEADS = 2
GROUP = N_HEADS // N_KV_HEADS
HEAD_DIM = 64
N_IDX_HEADS = 4
IDX_DIM = 64
C_CONV = 512
CONV_W = 31
TOPK_MAX = 256
N_META = 16
NUM_BUCKETS = 32
MAX_DISTANCE = 128
EPS = 1e-6
LN_EPS = 1e-5

LANE = 128
ROWS = 8
TQ = 128
KC = 512
SUB = KC // TQ
N_ACC = 4
DIGIT_BITS = 7
DIGIT_MASK = (1 << DIGIT_BITS) - 1
KEY_DIGITS = (7, 7, 7, 7, 4)
KEY_SHIFTS = tuple(32 - sum(KEY_DIGITS[:i + 1]) for i in range(len(KEY_DIGITS)))
N_LEVELS = len(KEY_DIGITS)
V_ROWS = HEAD_DIM + 16
PAGE_SLOTS = 4
HALO = 32
CONV_SUB = 64
MXU_DTYPE = jnp.bfloat16
NEG = -1e30
LOG2E = 1.4426950408889634
INT_MIN = -2 ** 31
TIE_SPAN = 1 << 14
KEY_NEG_INF = -0x7F800000 - TIE_SPAN
VMEM_LIMIT = 56 * 1024 * 1024

_PROJ_SIZES = (C_CONV, C_CONV, C_CONV, N_HEADS * HEAD_DIM, N_KV_HEADS * HEAD_DIM, N_KV_HEADS * HEAD_DIM,
               N_HEADS * HEAD_DIM, N_IDX_HEADS * IDX_DIM, IDX_DIM, N_IDX_HEADS)
_UB_OFF, _GC_OFF, _Q_OFF, _K_OFF, _V_OFF, _GA_OFF, _IQ_OFF, _IK_OFF, _IW_OFF, _D_IN = (
    int(s) for s in np.cumsum(_PROJ_SIZES))
_W_ROWS = _IK_OFF + LANE


def _cparams(n_axes):
    return pltpu.CompilerParams(dimension_semantics=("arbitrary",) * n_axes, vmem_limit_bytes=VMEM_LIMIT)


def _silu(x):
    return x * jax.nn.sigmoid(x)


def _prep_w_in(w):
    assert w.shape[1] == _D_IN
    return jnp.pad(w.T, ((0, _W_ROWS - _D_IN), (0, 0))).astype(MXU_DTYPE)


def _t5_bucket(rel):
    n = jnp.maximum(rel, 0)
    max_exact = NUM_BUCKETS // 2
    large = max_exact + (jnp.log(jnp.maximum(n, 1).astype(jnp.float32) / max_exact)
                         / math.log(MAX_DISTANCE / max_exact)
                         * (NUM_BUCKETS - max_exact)).astype(jnp.int32)
    large = jnp.minimum(large, NUM_BUCKETS - 1)
    return jnp.where(n < max_exact, n, large)


def _bias_tiles(rel_bias):
    far = rel_bias[NUM_BUCKETS - 1]
    d = jnp.arange(2 * TQ, dtype=jnp.int32)
    dists = (jnp.where(d < TQ, d, 0),
             jnp.where(d < TQ, d + TQ, d - TQ))
    tiles = []
    for dist in dists:
        v = ((rel_bias[_t5_bucket(dist)] - far) * LOG2E).T
        rep = jnp.tile(v, (1, TQ))[:, :TQ * (2 * TQ - 1)].reshape(N_HEADS, TQ, 2 * TQ - 1)
        tiles.append(rep[:, :, :TQ])
    tiles.append(jnp.zeros_like(tiles[0]))
    return jnp.stack(tiles).astype(jnp.float32)


_NT = (((1,), (1,)), ((), ()))
_Q_SCALE = HEAD_DIM ** -0.5 * LOG2E
_IQ_SCALE = IDX_DIM ** -0.5
_IW_SCALE = N_IDX_HEADS ** -0.5
_IW_ROWS = 16


def _proj_kernel(x_ref, g_ref, w_ref, u_ref, gc_ref, kv_ref, ga_ref, tail_ref, *q_refs, feature_major):
    x = x_ref[...]
    ms = jnp.mean(x * x, axis=-1, keepdims=True)
    xn = (x * lax.rsqrt(ms + EPS) * g_ref[...]).astype(MXU_DTYPE)

    def mm(lo, hi):
        return lax.dot_general(xn, w_ref[lo:hi, :], _NT, preferred_element_type=jnp.float32)

    def mm_t(lo, hi):
        return lax.dot_general(w_ref[lo:hi, :], xn, _NT, preferred_element_type=jnp.float32)

    u_ref[...] = mm(0, _UB_OFF) * jax.nn.sigmoid(mm(_UB_OFF, _GC_OFF))
    gc_ref[...] = mm(_GC_OFF, _Q_OFF)
    kv_ref[...] = mm(_K_OFF, _GA_OFF)
    ga_ref[...] = mm(_GA_OFF, _IQ_OFF)
    t = mm(_IK_OFF, _W_ROWS)
    lane = lax.broadcasted_iota(jnp.int32, t.shape, 1)
    tail_ref[...] = t * jnp.where(lane >= IDX_DIM, _IW_SCALE, 1.0)
    if feature_major:
        qzt_ref, iqzt_ref, vt_ref, iwt_ref = q_refs
        qt = (mm_t(_Q_OFF, _K_OFF) * _Q_SCALE).astype(qzt_ref.dtype)
        qzt_ref[...] = jnp.zeros(qzt_ref.shape, qzt_ref.dtype)
        for h in range(N_HEADS):
            lo = h * LANE + (h // GROUP) * HEAD_DIM
            qzt_ref[lo:lo + HEAD_DIM, :] = qt[h * HEAD_DIM:(h + 1) * HEAD_DIM]
        iqt = (mm_t(_IQ_OFF, _IK_OFF) * _IQ_SCALE).astype(iqzt_ref.dtype)
        iqzt_ref[...] = jnp.zeros(iqzt_ref.shape, iqzt_ref.dtype)
        for h in range(N_IDX_HEADS):
            iqzt_ref[h * LANE:h * LANE + IDX_DIM, :] = iqt[h * IDX_DIM:(h + 1) * IDX_DIM]
        vt_ref[...] = mm_t(_V_OFF, _GA_OFF).astype(vt_ref.dtype)
        iwt_ref[...] = mm_t(_IW_OFF, _IW_OFF + _IW_ROWS) * _IW_SCALE
    else:
        q_ref, iq_ref = q_refs
        q_ref[...] = (mm(_Q_OFF, _K_OFF) * _Q_SCALE).astype(q_ref.dtype)
        iq_ref[...] = (mm(_IQ_OFF, _IK_OFF) * _IQ_SCALE).astype(iq_ref.dtype)


def _project(x, g, w, tm, feature_major):
    r, d = x.shape
    assert r % tm == 0
    rows = lambda c: pl.BlockSpec((tm, c), lambda i: (i, 0))
    cols = lambda c: pl.BlockSpec((c, tm), lambda i: (0, i))
    feat = N_KV_HEADS * HEAD_DIM
    specs = [rows(C_CONV), rows(C_CONV), rows(2 * feat), rows(N_HEADS * HEAD_DIM), rows(LANE)]
    shapes = [jax.ShapeDtypeStruct((r, s.block_shape[1]), jnp.float32) for s in specs]
    if feature_major:
        extra = [(N_HEADS * LANE, MXU_DTYPE), (N_IDX_HEADS * LANE, MXU_DTYPE), (feat, MXU_DTYPE),
                 (_IW_ROWS, jnp.float32)]
        specs += [cols(c) for c, _ in extra]
        shapes += [jax.ShapeDtypeStruct((c, r), t) for c, t in extra]
    else:
        extra = [(N_HEADS * HEAD_DIM, MXU_DTYPE), (N_IDX_HEADS * IDX_DIM, MXU_DTYPE)]
        specs += [rows(c) for c, _ in extra]
        shapes += [jax.ShapeDtypeStruct((r, c), t) for c, t in extra]
    return pl.pallas_call(
        functools.partial(_proj_kernel, feature_major=feature_major),
        grid=(r // tm,),
        in_specs=[rows(d), pl.BlockSpec((1, d), lambda i: (0, 0)), pl.BlockSpec((_W_ROWS, d), lambda i: (0, 0))],
        out_specs=specs,
        out_shape=shapes,
        compiler_params=_cparams(1),
        name="proj",
    )(x, g.reshape(1, d), w)


def _conv_kernel(prev_ref, cur_ref, gc_ref, w_ref, b_ref, lg_ref, lb_ref, o_ref, ext_ref):
    i = pl.program_id(1)
    ext_ref[0, 0:HALO, :] = jnp.where(i > 0, prev_ref[...], 0.0)
    ext_ref[0, HALO:HALO + TQ, :] = cur_ref[...]
    n_ext = HALO + TQ
    for s in range(1, ROWS):
        ext_ref[s, 0:n_ext - ROWS, :] = ext_ref[0, s:s + n_ext - ROWS, :]
    off = HALO - (CONV_W - 1)
    for r0 in range(0, TQ, CONV_SUB):
        acc = jnp.zeros((CONV_SUB, C_CONV), jnp.float32)
        for j in range(CONV_W):
            s = (off + j) % ROWS
            lo = r0 + off + j - s
            acc = acc + w_ref[j:j + 1, :] * ext_ref[s, lo:lo + CONV_SUB, :]
        y = acc + b_ref[...]
        mu = jnp.mean(y, axis=-1, keepdims=True)
        dev = y - mu
        var = jnp.mean(dev * dev, axis=-1, keepdims=True)
        yn = dev * lax.rsqrt(var + LN_EPS) * lg_ref[...] + lb_ref[...]
        o_ref[r0:r0 + CONV_SUB, :] = (_silu(yn) * _silu(gc_ref[r0:r0 + CONV_SUB, :])).astype(o_ref.dtype)


def _conv_branch(u, gc, w, b, lg, lb):
    bsz, lq, c = u.shape
    per = TQ // HALO
    row = lambda a: a.reshape(1, c)
    vec = pl.BlockSpec((1, c), lambda bi, i: (0, 0))
    tile = pl.BlockSpec((None, TQ, c), lambda bi, i: (bi, i, 0))
    return pl.pallas_call(
        _conv_kernel,
        grid=(bsz, lq // TQ),
        in_specs=[pl.BlockSpec((None, HALO, c), lambda bi, i: (bi, jnp.maximum(i * per - 1, 0), 0)),
                  tile, tile, pl.BlockSpec((CONV_W, c), lambda bi, i: (0, 0)), vec, vec, vec],
        out_specs=tile,
        out_shape=jax.ShapeDtypeStruct((bsz, lq, c), MXU_DTYPE),
        scratch_shapes=[pltpu.VMEM((ROWS, HALO + TQ, c), jnp.float32)],
        compiler_params=_cparams(2),
        name="conv",
    )(u, u, gc, w, row(b), row(lg), row(lb))


_INT_MIN = np.int32(INT_MIN)
_BYTE_ONES = np.int32(0x01010101)
_BYTE_LOW = np.int32(0x7F7F7F7F)
_HALF_LOW_BYTES = np.int32(0x00FF00FF)
_GUARD = np.int32(0x80808080 - (1 << 32))
assert SUB == 4 and DIGIT_BITS == 7


def _to_key(s, idx):
    bits = lax.bitcast_convert_type(s, jnp.int32)
    key = jnp.where(bits < 0, (_INT_MIN - bits) - TIE_SPAN, bits)
    return jnp.where((bits == 0) | (bits == _INT_MIN), -1 - idx, key)


def _for_chunks(nkc, body, init, unroll):
    n_main = nkc // unroll

    def main(i, carry):
        for k in range(unroll):
            carry = body(i * unroll + k, carry)
        return carry

    carry = lax.fori_loop(0, n_main, main, init)
    return lax.fori_loop(n_main * unroll, nkc, body, carry)


def _pack_digits(keys_ref, dig_ref, c):
    for lv in range(N_LEVELS):
        word = None
        for j in range(SUB):
            u = keys_ref[c, j * TQ:(j + 1) * TQ, :]
            if lv == 0:
                u = u ^ _INT_MIN
            move = KEY_SHIFTS[lv] - 8 * j
            u = lax.shift_right_logical(u, move) if move >= 0 else lax.shift_left(u, -move)
            field = u & np.int32(((1 << KEY_DIGITS[lv]) - 1) << (8 * j))
            word = field if word is None else word | field
        dig_ref[lv, c] = word


def _count_fields(work_ref, nkc, cand_bytes):
    q = work_ref.shape[2]

    def body(c, accs):
        accs = list(accs)
        for n, r in enumerate(range(0, TQ, ROWS)):
            diff = work_ref[c, r:r + ROWS, :] - cand_bytes
            accs[n % N_ACC] = accs[n % N_ACC] + (lax.shift_right_logical(diff, DIGIT_BITS) & _BYTE_ONES)
        return tuple(accs)

    accs = _for_chunks(nkc, body, (jnp.zeros((ROWS, q), jnp.int32),) * N_ACC, unroll=4)
    halves = jnp.zeros((ROWS, q), jnp.int32)
    for a in accs:
        halves = halves + (a & _HALF_LOW_BYTES) + (lax.shift_right_logical(a, 8) & _HALF_LOW_BYTES)
    total = (halves & 0xFFFF) + lax.shift_right_logical(halves, 16)
    return jnp.sum(total.astype(jnp.float32), axis=0, keepdims=True)


def _digit_search(dig_ref, work_ref, alive_ref, nkc, shifts, digits, target):
    q = dig_ref.shape[3]
    assert dig_ref.shape[1] * (TQ // ROWS) <= 255 * N_ACC
    zero = jnp.zeros((1, q), jnp.float32)
    value = jnp.zeros((1, q), jnp.int32)
    above, done, cnt_ge, cnt_gt = zero, zero, target + 1.0, zero

    def load_fields(c, carry):
        work_ref[c] = (dig_ref[0, c] & alive_ref[c]) | _GUARD
        return carry

    _for_chunks(nkc, load_fields, 0, unroll=4)

    for lv, (shift, n_bits) in enumerate(zip(shifts, digits)):

        def digit_bit(i, state):
            digit, cnt_ge, cnt_gt, cnt_rej, done = state
            cand = digit | lax.shift_left(jnp.int32(1), n_bits - 1 - i)
            cnt_alive = _count_fields(work_ref, nkc, cand * _BYTE_ONES)
            cnt = above + cnt_alive
            take = (cnt >= target) & (done == 0.0)
            drop = (cnt < target) & (done == 0.0)
            return (jnp.where(take, cand, digit), jnp.where(take, cnt, cnt_ge), jnp.where(drop, cnt, cnt_gt),
                    jnp.where(drop, cnt_alive, cnt_rej),
                    jnp.where(take & (cnt == target), 1.0, done))

        digit, cnt_ge, cnt_gt, cnt_rej, done = lax.fori_loop(
            0, n_bits, digit_bit, (jnp.zeros((1, q), jnp.int32), cnt_ge, cnt_gt, zero, done))
        above = above + cnt_rej
        value = value | lax.shift_left(digit, shift)

        if lv + 1 < len(shifts):
            digit_bytes = digit * _BYTE_ONES

            def narrow(c, carry):
                differs = lax.shift_right_logical(((dig_ref[lv, c] ^ digit_bytes) + _BYTE_LOW) & _GUARD, DIGIT_BITS)
                alive = alive_ref[c] & ((_BYTE_ONES - differs) * DIGIT_MASK)
                alive_ref[c] = alive
                work_ref[c] = (dig_ref[lv + 1, c] & alive) | _GUARD
                return carry

            _for_chunks(nkc, narrow, 0, unroll=4)

    return value, cnt_ge, cnt_gt, done


def _fill(ref, nkc, word):
    def body(c, carry):
        ref[c] = jnp.full(ref.shape[1:], word, ref.dtype)
        return carry

    _for_chunks(nkc, body, 0, unroll=4)


def _threshold(keys_ref, dig_ref, work_ref, alive_ref, nkc, kk):
    q = keys_ref.shape[2]
    kf = jnp.full((1, q), kk, jnp.float32)
    _fill(alive_ref, nkc, _BYTE_LOW)
    prefix, cnt_thr, cnt_gt, _ = _digit_search(dig_ref, work_ref, alive_ref, nkc, KEY_SHIFTS, KEY_DIGITS, kf)
    thr = prefix ^ _INT_MIN
    admissible = thr != KEY_NEG_INF
    surplus = (cnt_thr > kf) & admissible

    @pl.when(jnp.max(jnp.where(surplus, 1.0, 0.0)) > 0.0)
    def _():
        assert keys_ref.shape[0] * KC <= 1 << (2 * DIGIT_BITS)
        top = (1 << (2 * DIGIT_BITS)) - 1
        row = lax.broadcasted_iota(jnp.int32, (TQ, q), 0)

        def tie_fields(c, carry):
            hi = lo = tie = None
            for j in range(SUB):
                is_tie = keys_ref[c, j * TQ:(j + 1) * TQ, :] == thr
                rev = top - (c * KC + j * TQ + row)
                fields = [jnp.where(is_tie, f, 0) for f in
                          (lax.shift_right_logical(rev, DIGIT_BITS), rev & DIGIT_MASK, DIGIT_MASK)]
                fields = [f if j == 0 else lax.shift_left(f, 8 * j) for f in fields]
                hi, lo, tie = fields if j == 0 else (hi | fields[0], lo | fields[1], tie | fields[2])
            dig_ref[0, c], dig_ref[1, c], alive_ref[c] = hi, lo, tie
            return carry

        lax.fori_loop(0, nkc, tie_fields, 0)
        cut, _, _, _ = _digit_search(dig_ref, work_ref, alive_ref, nkc, (DIGIT_BITS, 0), (DIGIT_BITS, DIGIT_BITS),
                                     kf - cnt_gt)

        def drop_losers(c, carry):
            for j in range(SUB):
                rows = slice(j * TQ, (j + 1) * TQ)
                x = keys_ref[c, rows, :]
                lost = (x == thr) & (top - (c * KC + j * TQ + row) < cut)
                keys_ref[c, rows, :] = jnp.where(lost, thr - 1, x)
            return carry

        lax.fori_loop(0, nkc, drop_losers, 0)

    return jnp.where(admissible, thr, KEY_NEG_INF + 1)


def _attend_kernel(iqt_ref, iwt_ref, qzt_ref, ga_ref, ik_ref, k_ref, vt_ref, toep_ref, o_ref,
                   keys_ref, dig_ref, work_ref, alive_ref, m_ref, acc_ref, lg0_ref, lg1_ref,
                   wq_ref, *, kk):
    qi = pl.program_id(1)
    nkc = qi // SUB + 1
    qpos = qi * TQ + lax.broadcasted_iota(jnp.int32, (TQ, TQ), 1)
    krow = lax.broadcasted_iota(jnp.int32, (TQ, TQ), 0)
    for h in range(N_HEADS):
        wq_ref[:, h * TQ:(h + 1) * TQ] = qzt_ref[h]

    def score_chunk(c, carry, causal):
        iqt = jnp.concatenate([iqt_ref[h] for h in range(N_IDX_HEADS)], axis=1)
        w = iwt_ref[...]
        for j in range(SUB):
            st = jnp.dot(ik_ref[c, j * TQ:(j + 1) * TQ, :], iqt, preferred_element_type=jnp.float32)
            s = jnp.zeros((TQ, TQ), jnp.float32)
            for h in range(N_IDX_HEADS):
                s = s + w[h:h + 1, :] * jnp.maximum(st[:, h * TQ:(h + 1) * TQ], 0.0)
            kpos = c * KC + j * TQ + krow
            if causal:
                s = jnp.where(kpos <= qpos, s, -jnp.inf)
            keys_ref[c, j * TQ:(j + 1) * TQ, :] = _to_key(s, kpos)
        _pack_digits(keys_ref, dig_ref, c)
        return carry

    _for_chunks(nkc - 1, functools.partial(score_chunk, causal=False), 0, unroll=2)
    score_chunk(nkc - 1, 0, causal=True)
    thr = _threshold(keys_ref, dig_ref, work_ref, alive_ref, nkc, kk)

    m_ref[...] = jnp.full(m_ref.shape, NEG, jnp.float32)
    acc_ref[...] = jnp.zeros(acc_ref.shape, jnp.float32)

    def bias(c, h):
        return jnp.concatenate([toep_ref[jnp.clip(qi - (c * SUB + j), 0, 2), h] for j in range(SUB)], axis=0)

    def logits(c, lg_ref, near):
        md = jnp.where(keys_ref[c] >= thr, 0.0, NEG)
        add = [md + bias(c, h) for h in range(N_HEADS)] if near else [md] * N_HEADS
        lg_ref[...] = (jnp.dot(k_ref[c], wq_ref[...], preferred_element_type=jnp.float32)
                       + jnp.concatenate(add, axis=1))

    def add_bias(c, lg_ref):
        for h in range(N_HEADS):
            sl = slice(h * TQ, (h + 1) * TQ)
            lg_ref[:, sl] = lg_ref[:, sl] + bias(c, h)

    def softmax_pv(c, lg_ref):
        for pr in range(N_HEADS // 2):
            kv = pr // (GROUP // 2)
            sl = slice(2 * pr * TQ, (2 * pr + 2) * TQ)
            m_old = m_ref[pr]
            m_new = jnp.maximum(m_old, jnp.max(lg_ref[:, sl], axis=0, keepdims=True))
            alpha = jnp.exp2(m_old - m_new)
            p = jnp.exp2(lg_ref[:, sl] - m_new)
            pv = jnp.dot(vt_ref[c, kv * V_ROWS:(kv + 1) * V_ROWS, :], p.astype(MXU_DTYPE),
                         preferred_element_type=jnp.float32)
            acc_ref[pr] = alpha * acc_ref[pr] + pv
            m_ref[pr] = m_new

    n_far2 = (jnp.maximum(qi - 1, 0) // SUB) // 2

    def far_pair(i, carry):
        logits(2 * i + 1, lg1_ref, near=False)
        softmax_pv(2 * i, lg0_ref)
        logits(2 * i + 2, lg0_ref, near=False)
        softmax_pv(2 * i + 1, lg1_ref)
        return carry

    logits(0, lg0_ref, near=False)
    lax.fori_loop(0, n_far2, far_pair, 0)

    c0 = 2 * n_far2
    add_bias(c0, lg0_ref)

    @pl.when(c0 + 1 < nkc)
    def _():
        logits(c0 + 1, lg1_ref, near=True)
        softmax_pv(c0, lg0_ref)

    @pl.when(c0 + 1 >= nkc)
    def _():
        softmax_pv(c0, lg0_ref)

    @pl.when(c0 + 2 < nkc)
    def _():
        logits(c0 + 2, lg0_ref, near=True)
        softmax_pv(c0 + 1, lg1_ref)
        softmax_pv(c0 + 2, lg0_ref)

    @pl.when((c0 + 1 < nkc) & (c0 + 2 >= nkc))
    def _():
        softmax_pv(c0 + 1, lg1_ref)

    for pr in range(N_HEADS // 2):
        acc = acc_ref[pr]
        o_t = acc[:HEAD_DIM] / acc[HEAD_DIM:HEAD_DIM + 1]
        pair_t = jnp.concatenate([o_t[:, :TQ], o_t[:, TQ:]], axis=0)
        sl = slice(pr * LANE, (pr + 1) * LANE)
        o_ref[:, sl] = (pair_t.T * _silu(ga_ref[:, sl])).astype(o_ref.dtype)


def _attend(iqt, iwt, qzt, ga, ik, k, vt, toep_t, kk):
    bsz, lq = ga.shape[:2]
    nq = lq // TQ
    nc = k.shape[1]
    lanes = lambda a: pl.BlockSpec(a.shape[:-1] + (TQ,), lambda b, i: (0,) * (a.ndim - 1) + (b * nq + i,))
    whole = lambda a: pl.BlockSpec((None,) + a.shape[1:], lambda b, i: (b,) + (0,) * (a.ndim - 1))
    rows = pl.BlockSpec((None, TQ, N_HEADS * HEAD_DIM), lambda b, i: (b, i, 0))
    return pl.pallas_call(
        functools.partial(_attend_kernel, kk=kk),
        grid=(bsz, nq),
        in_specs=[lanes(iqt), lanes(iwt), lanes(qzt), rows, whole(ik), whole(k), whole(vt),
                  pl.BlockSpec(toep_t.shape, lambda b, i: (0, 0, 0, 0))],
        out_specs=rows,
        out_shape=jax.ShapeDtypeStruct((bsz, lq, N_HEADS * HEAD_DIM), MXU_DTYPE),
        scratch_shapes=[pltpu.VMEM((nc, KC, TQ), jnp.int32),
                        pltpu.VMEM((N_LEVELS, nc, TQ, TQ), jnp.int32), pltpu.VMEM((nc, TQ, TQ), jnp.int32),
                        pltpu.VMEM((nc, TQ, TQ), jnp.int32),
                        pltpu.VMEM((N_HEADS // 2, 1, 2 * TQ), jnp.float32),
                        pltpu.VMEM((N_HEADS // 2, V_ROWS, 2 * TQ), jnp.float32),
                        pltpu.VMEM((KC, N_HEADS * TQ), jnp.float32), pltpu.VMEM((KC, N_HEADS * TQ), jnp.float32),
                        pltpu.VMEM((LANE, N_HEADS * TQ), MXU_DTYPE)],
        compiler_params=_cparams(2),
        name="attend",
    )(iqt, iwt, qzt, ga, ik, k, vt, toep_t)


def _out_kernel(x_ref, mc_ref, ma_ref, wc_ref, wa_ref, o_ref):
    o_ref[...] = (x_ref[...]
                  + jnp.dot(mc_ref[...], wc_ref[...], preferred_element_type=jnp.float32)
                  + jnp.dot(ma_ref[...], wa_ref[...], preferred_element_type=jnp.float32))


def _out_final_kernel(x_ref, mc_ref, ma_ref, wc_ref, wa_ref, g_ref, o_ref, y_ref):
    x = (x_ref[...]
         + jnp.dot(mc_ref[...], wc_ref[...], preferred_element_type=jnp.float32)
         + jnp.dot(ma_ref[...], wa_ref[...], preferred_element_type=jnp.float32))
    o_ref[...] = x
    ms = jnp.mean(x * x, axis=-1, keepdims=True)
    y_ref[...] = x * lax.rsqrt(ms + EPS) * g_ref[...]


def _out_proj(x, mc, ma, wc, wa, tm, final_g=None):
    r, d = x.shape
    c = mc.shape[1]
    rows = lambda w: pl.BlockSpec((tm, w), lambda i: (i, 0))
    full = lambda a: pl.BlockSpec(a.shape, lambda i: (0, 0))
    if final_g is None:
        return pl.pallas_call(
            _out_kernel, grid=(r // tm,),
            in_specs=[rows(d), rows(c), rows(c), full(wc), full(wa)],
            out_specs=rows(d), out_shape=jax.ShapeDtypeStruct((r, d), jnp.float32),
            compiler_params=_cparams(1), name="out",
        )(x, mc, ma, wc, wa)
    g = final_g.reshape(1, d)
    return pl.pallas_call(
        _out_final_kernel, grid=(r // tm,),
        in_specs=[rows(d), rows(c), rows(c), full(wc), full(wa), full(g)],
        out_specs=[rows(d), rows(d)], out_shape=[jax.ShapeDtypeStruct((r, d), jnp.float32)] * 2,
        compiler_params=_cparams(1), name="out_final",
    )(x, mc, ma, wc, wa, g)


def _sconv_kernel(st_ref, u_ref, gc_ref, w_ref, b_ref, lg_ref, lb_ref, o_ref):
    acc = jnp.zeros(u_ref.shape, jnp.float32)
    for j in range(CONV_W - 1):
        acc = acc + w_ref[j:j + 1, :] * st_ref[j]
    acc = acc + w_ref[CONV_W - 1:CONV_W, :] * u_ref[...]
    y = acc + b_ref[...]
    mu = jnp.mean(y, axis=-1, keepdims=True)
    dev = y - mu
    var = jnp.mean(dev * dev, axis=-1, keepdims=True)
    yn = dev * lax.rsqrt(var + LN_EPS) * lg_ref[...] + lb_ref[...]
    o_ref[...] = (_silu(yn) * _silu(gc_ref[...])).astype(o_ref.dtype)


def _sample_conv(state_t, u, gc, w, b, lg, lb):
    db, c = u.shape
    row = lambda a: a.reshape(1, c)
    full = lambda a: pl.BlockSpec(a.shape, lambda i: (0,) * a.ndim)
    args = (state_t, u, gc, w, row(b), row(lg), row(lb))
    return pl.pallas_call(
        _sconv_kernel, grid=(1,), in_specs=[full(a) for a in args],
        out_specs=pl.BlockSpec((db, c), lambda i: (0, 0)),
        out_shape=jax.ShapeDtypeStruct((db, c), MXU_DTYPE),
        compiler_params=_cparams(1), name="sconv",
    )(*args)


def _dec_score_kernel(pt_ref, iq_ref, iw_ref, ikn_ref, cik_ref, o_ref, buf_ref, sem_ref, *, layer, width):
    slot = _fetch_pages(pt_ref, (cik_ref,), (buf_ref,), (sem_ref,), layer)
    iq = iq_ref[...][:, :IDX_DIM]
    w = iw_ref[...]
    ikt = buf_ref[slot].astype(MXU_DTYPE)
    s = jnp.dot(iq, ikt, preferred_element_type=jnp.float32)
    sc = jnp.sum(w * jnp.maximum(s, 0.0), axis=0, keepdims=True)
    ikn = ikn_ref[...][:, :IDX_DIM].astype(MXU_DTYPE).astype(jnp.float32)
    sn = jnp.sum(iq.astype(jnp.float32) * ikn, axis=-1, keepdims=True)
    scn = jnp.sum(w * jnp.maximum(sn, 0.0), axis=0, keepdims=True)
    lane = lax.broadcasted_iota(jnp.int32, (1, width - sc.shape[1]), 1)
    o_ref[...] = jnp.concatenate([sc, jnp.where(lane == 0, scn, -jnp.inf)], axis=1)


def _dec_scores(page_table, iq3, iw3, tail3, cache_ik, layer, width):
    db, n_pages = page_table.shape
    di, page = cache_ik.shape[2:]
    seq = lambda a: pl.BlockSpec((None,) + a.shape[1:], lambda b, pt: (b, 0, 0))
    return pl.pallas_call(
        functools.partial(_dec_score_kernel, layer=layer, width=width),
        grid_spec=pltpu.PrefetchScalarGridSpec(
            num_scalar_prefetch=1, grid=(db,),
            in_specs=[seq(iq3), seq(iw3), seq(tail3), pl.BlockSpec(memory_space=pl.ANY)],
            out_specs=pl.BlockSpec((None, 1, width), lambda b, pt: (b, 0, 0)),
            scratch_shapes=[pltpu.VMEM((PAGE_SLOTS, di, n_pages * page), cache_ik.dtype),
                            pltpu.SemaphoreType.DMA((PAGE_SLOTS,))]),
        out_shape=jax.ShapeDtypeStruct((db, 1, width), jnp.float32),
        compiler_params=_cparams(1), name="dec_score",
    )(page_table, iq3, iw3, tail3, cache_ik)


def _dec_select_kernel(s_ref, o_ref, keys_ref, dig_ref, work_ref, alive_ref, *, kk):
    nc = keys_ref.shape[0]
    row = lax.broadcasted_iota(jnp.int32, (KC, keys_ref.shape[2]), 0)
    for c in range(nc):
        keys_ref[c] = _to_key(s_ref[c * KC:(c + 1) * KC, :], c * KC + row)
        _pack_digits(keys_ref, dig_ref, c)
    thr = _threshold(keys_ref, dig_ref, work_ref, alive_ref, nc, kk)
    for c in range(nc):
        o_ref[c * KC:(c + 1) * KC, :] = jnp.where(keys_ref[c] >= thr, 0.0, NEG)


def _dec_select(scores_t, kk):
    width, db = scores_t.shape
    nc = width // KC
    spec = pl.BlockSpec((width, db), lambda i: (0, 0))
    return pl.pallas_call(
        functools.partial(_dec_select_kernel, kk=kk),
        grid=(1,), in_specs=[spec], out_specs=spec,
        out_shape=jax.ShapeDtypeStruct((width, db), jnp.float32),
        scratch_shapes=[pltpu.VMEM((nc, KC, db), jnp.int32),
                        pltpu.VMEM((N_LEVELS, nc, TQ, db), jnp.int32), pltpu.VMEM((nc, TQ, db), jnp.int32),
                        pltpu.VMEM((nc, TQ, db), jnp.int32)],
        compiler_params=_cparams(1), name="dec_select",
    )(scores_t)


def _page_copies(pt_ref, cache_ref, buf_ref, sem_ref, layer, seq, slot):
    page = cache_ref.shape[3]
    n_pages = buf_ref.shape[2] // page
    return [pltpu.make_async_copy(cache_ref.at[layer, pt_ref[seq, p]],
                                  buf_ref.at[slot, :, pl.ds(p * page, page)], sem_ref.at[slot])
            for p in range(n_pages)]


def _fetch_pages(pt_ref, caches, bufs, sems, layer):
    b, nb = pl.program_id(0), pl.num_programs(0)
    ahead = PAGE_SLOTS - 1

    def start(seq, slot):
        for cache_ref, buf_ref, sem_ref in zip(caches, bufs, sems):
            for cp in _page_copies(pt_ref, cache_ref, buf_ref, sem_ref, layer, seq, slot):
                cp.start()

    for s in range(ahead):
        @pl.when((b == 0) & (s < nb))
        def _():
            start(s, s)

    @pl.when(b + ahead < nb)
    def _():
        start(b + ahead, (b + ahead) % PAGE_SLOTS)

    slot = b % PAGE_SLOTS
    for cache_ref, buf_ref, sem_ref in zip(caches, bufs, sems):
        for cp in _page_copies(pt_ref, cache_ref, buf_ref, sem_ref, layer, b, slot):
            cp.wait()
    return slot


def _dec_attend_kernel(pt_ref, qz_ref, kvn_ref, madd_ref, bias_ref, ga_ref, ck_ref, cv_ref, o_ref,
                       kbuf_ref, vbuf_ref, ksem_ref, vsem_ref, *, layer):
    slot = _fetch_pages(pt_ref, (ck_ref, cv_ref), (kbuf_ref, vbuf_ref), (ksem_ref, vsem_ref), layer)
    qz = qz_ref[...]
    past = kbuf_ref.shape[2]
    kt = kbuf_ref[slot].astype(MXU_DTYPE)
    vt = vbuf_ref[slot].astype(MXU_DTYPE)
    lg = (jnp.dot(qz, kt, preferred_element_type=jnp.float32)
          + bias_ref[:, :past] + madd_ref[:, :past])
    kvn = kvn_ref[...].astype(MXU_DTYPE).astype(jnp.float32)
    lgn = jnp.sum(qz.astype(jnp.float32) * kvn[:, :LANE], axis=-1, keepdims=True)
    lgn = lgn + bias_ref[:, past:past + 1] + madd_ref[:, past:past + 1]
    m = jnp.maximum(lgn, jnp.max(lg, axis=-1, keepdims=True))
    pn = jnp.exp2(lgn - m)
    pp = jnp.exp2(lg - m)
    den = pn + jnp.sum(pp, axis=-1, keepdims=True)
    o = (pn.astype(MXU_DTYPE).astype(jnp.float32) * kvn[:, LANE:]
         + lax.dot_general(pp.astype(MXU_DTYPE), vt, _NT, preferred_element_type=jnp.float32))
    o = o / den
    swapped = pltpu.roll(o, HEAD_DIM, axis=1)
    lane = lax.broadcasted_iota(jnp.int32, (1, LANE), 1)
    pairs = []
    for j in range(N_HEADS // 2):
        first, second = (o, swapped) if 2 * j < GROUP else (swapped, o)
        pairs.append(jnp.where(lane < HEAD_DIM, first[2 * j:2 * j + 1], second[2 * j + 1:2 * j + 2]))
    o_ref[...] = (jnp.concatenate(pairs, axis=0) * _silu(ga_ref[...])).astype(o_ref.dtype)


def _dec_attend(page_table, qz3, kvn3, madd3, bias, ga3, cache_k, cache_v, layer):
    db, n_pages = page_table.shape
    feat, page = cache_k.shape[2:]
    seq = lambda a: pl.BlockSpec((None,) + a.shape[1:], lambda b, pt: (b, 0, 0))
    hbm = pl.BlockSpec(memory_space=pl.ANY)
    buf = pltpu.VMEM((PAGE_SLOTS, feat, n_pages * page), cache_k.dtype)
    sem = pltpu.SemaphoreType.DMA((PAGE_SLOTS,))
    return pl.pallas_call(
        functools.partial(_dec_attend_kernel, layer=layer),
        grid_spec=pltpu.PrefetchScalarGridSpec(
            num_scalar_prefetch=1, grid=(db,),
            in_specs=[seq(qz3), seq(kvn3), seq(madd3), pl.BlockSpec(bias.shape, lambda b, pt: (0, 0)), seq(ga3),
                      hbm, hbm],
            out_specs=pl.BlockSpec((None, GROUP, LANE), lambda b, pt: (b, 0, 0)),
            scratch_shapes=[buf, buf, sem, sem]),
        out_shape=jax.ShapeDtypeStruct((db, GROUP, LANE), MXU_DTYPE),
        compiler_params=_cparams(1), name="dec_attend",
    )(page_table, qz3, kvn3, madd3, bias, ga3, cache_k, cache_v)


def _round_up(x, m):
    return -(-x // m) * m


def _row_tile(rows):
    for f in (5, 4, 3, 2, 1):
        if rows % (f * TQ) == 0:
            return f * TQ
    return rows


def kernel(x_prompt, x_sample, cache_k, cache_v, cache_idx_k, state_conv, page_table, meta_tokens, rel_bias,
           norm_g, w_in, conv_w, conv_b, conv_ln_g, conv_ln_b, w_out, final_norm_g):
    bsz, seq, d = x_prompt.shape
    depth = w_in.shape[0]
    lp = seq + N_META
    lq = _round_up(lp, TQ)
    lk = _round_up(lp, KC)
    nq, nc = lq // TQ, lk // KC
    kk_p = min(TOPK_MAX, lp // 4)
    db = x_sample.shape[0]
    n_pool, page = cache_k.shape[1:3]
    n_pages = page_table.shape[1]
    past = n_pages * page
    kk_s = min(TOPK_MAX, (past + 1) // 4)
    width_s = _round_up(past + 1, KC)
    feat = N_KV_HEADS * HEAD_DIM
    tm = _row_tile(bsz * lq)

    xp = jnp.concatenate([jnp.broadcast_to(meta_tokens[None].astype(x_prompt.dtype), (bsz, N_META, d)), x_prompt,
                          jnp.zeros((bsz, lq - lp, d), x_prompt.dtype)], axis=1).reshape(bsz * lq, d)
    xs = x_sample.reshape(db, d)
    toep_t = _bias_tiles(rel_bias)
    bias_s = jnp.moveaxis(rel_bias[_t5_bucket(past - jnp.arange(width_s, dtype=jnp.int32))], -1, 0) * LOG2E
    cache_k4 = jnp.transpose(cache_k, (0, 1, 3, 4, 2)).reshape(depth, n_pool, feat, page)
    cache_v4 = jnp.transpose(cache_v, (0, 1, 3, 4, 2)).reshape(depth, n_pool, feat, page)
    cache_ik4 = jnp.swapaxes(cache_idx_k, 2, 3)

    def key_chunks(a):
        return jnp.pad(a, ((0, 0), (0, lk - lq), (0, 0))).astype(MXU_DTYPE).reshape(bsz, nc, KC, LANE)

    def value_chunks_t(a):
        a = jnp.pad(a.reshape(N_KV_HEADS, HEAD_DIM, bsz, lq), ((0, 0), (0, 0), (0, 0), (0, lk - lq)))
        a = jnp.concatenate([a, jnp.ones((N_KV_HEADS, V_ROWS - HEAD_DIM, bsz, lk), a.dtype)], axis=1)
        return jnp.transpose(a.reshape(N_KV_HEADS * V_ROWS, bsz, nc, KC), (1, 2, 0, 3))

    kp, vp, ikp, cp, ksm, vsm, iks, cs = ([] for _ in range(8))
    yp = ys = None
    for l in range(depth):
        w = _prep_w_in(w_in[l])
        wc = w_out[l, :C_CONV].astype(MXU_DTYPE)
        wa = w_out[l, C_CONV:].astype(MXU_DTYPE)
        last = l == depth - 1

        u, gc, kv, ga, tail, qzt, iqzt, vt, iwt = _project(xp, norm_g[l], w, tm, feature_major=True)
        b3 = lambda a: a.reshape(bsz, lq, a.shape[-1])
        u3, kv3, tail3 = b3(u), b3(kv), b3(tail)
        mixc = _conv_branch(u3, b3(gc), conv_w[l], conv_b[l], conv_ln_g[l], conv_ln_b[l])
        lane = jnp.arange(LANE)
        ik = key_chunks(jnp.where(lane < IDX_DIM, tail3, 0.0))
        mixa = _attend(iqzt.reshape(N_IDX_HEADS, LANE, bsz * lq), iwt, qzt.reshape(N_HEADS, LANE, bsz * lq), b3(ga),
                       ik, key_chunks(kv3[..., :feat]), value_chunks_t(vt), toep_t, kk_p)
        res = _out_proj(xp, mixc.reshape(bsz * lq, C_CONV), mixa.reshape(bsz * lq, -1), wc, wa, tm,
                        final_norm_g if last else None)
        xp, yp = res if last else (res, None)
        kp.append(kv3[:, :lp, :feat].reshape(bsz, lp, N_KV_HEADS, HEAD_DIM))
        vp.append(kv3[:, :lp, feat:].reshape(bsz, lp, N_KV_HEADS, HEAD_DIM))
        ikp.append(tail3[:, :lp, :IDX_DIM])
        cp.append(u3[:, lp - (CONV_W - 1):lp])

        u, gc, kv, ga, tail, q, iq = _project(xs, norm_g[l], w, db, feature_major=False)
        mixc = _sample_conv(jnp.transpose(state_conv[l], (1, 0, 2)), u, gc,
                            conv_w[l], conv_b[l], conv_ln_g[l], conv_ln_b[l])
        iw3 = tail[:, IDX_DIM:IDX_DIM + N_IDX_HEADS].reshape(db, N_IDX_HEADS, 1)
        scores = _dec_scores(page_table, iq.reshape(db, N_IDX_HEADS, IDX_DIM), iw3, tail.reshape(db, 1, LANE),
                             cache_ik4, l, width_s)
        madd = _dec_select(scores.reshape(db, width_s).T, kk_s).T
        q4 = q.reshape(db, N_KV_HEADS, GROUP, HEAD_DIM)
        qz3 = jnp.concatenate([jnp.pad(q4[:, 0], ((0, 0), (0, 0), (0, HEAD_DIM))),
                               jnp.pad(q4[:, 1], ((0, 0), (0, 0), (HEAD_DIM, 0)))], axis=1)
        mixa = _dec_attend(page_table, qz3, kv.reshape(db, 1, 2 * feat),
                           madd.reshape(db, 1, width_s), bias_s, ga.reshape(db, N_HEADS // 2, LANE),
                           cache_k4, cache_v4, l)
        res = _out_proj(xs, mixc, mixa.reshape(db, -1), wc, wa, db, final_norm_g if last else None)
        xs, ys = res if last else (res, None)
        ksm.append(kv[:, :feat].reshape(db, 1, N_KV_HEADS, HEAD_DIM))
        vsm.append(kv[:, feat:].reshape(db, 1, N_KV_HEADS, HEAD_DIM))
        iks.append(tail[:, None, :IDX_DIM])
        cs.append(jnp.concatenate([state_conv[l][:, 1:], u[:, None]], axis=1))

    y_prompt = yp.reshape(bsz, lq, d)[:, N_META:lp]
    y_sample = ys.reshape(db, 1, d)
    return (y_prompt, y_sample, jnp.stack(kp), jnp.stack(vp), jnp.stack(ikp), jnp.stack(cp),
            jnp.stack(ksm), jnp.stack(vsm), jnp.stack(iks), jnp.stack(cs))
```

```python
import functools
import math

import numpy as np
import jax
import jax.numpy as jnp
from jax import lax
from jax.experimental import pallas as pl
from jax.experimental.pallas import tpu as pltpu

N_HEADS = 8
N_KV_HEADS = 2
GROUP = N_HEADS // N_KV_HEADS
HEAD_DIM = 64
N_IDX_HEADS = 4
IDX_DIM = 64
C_CONV = 512
CONV_W = 31
TOPK_MAX = 256
N_META = 16
NUM_BUCKETS = 32
MAX_DISTANCE = 128
EPS = 1e-6
LN_EPS = 1e-5

LANE = 128
ROWS = 8
TQ = 128
KC = 512
SUB = KC // TQ
N_ACC = 4
DIGIT_BITS = 7
DIGIT_MASK = (1 << DIGIT_BITS) - 1
KEY_DIGITS = (7, 7, 7, 7, 4)
KEY_SHIFTS = tuple(32 - sum(KEY_DIGITS[:i + 1]) for i in range(len(KEY_DIGITS)))
N_LEVELS = len(KEY_DIGITS)
V_ROWS = HEAD_DIM + 16
PAGE_SLOTS = 4
HALO = 32
CONV_SUB = 64
MXU_DTYPE = jnp.bfloat16
NEG = -1e30
LOG2E = 1.4426950408889634
INT_MIN = -2 ** 31
TIE_SPAN = 1 << 14
KEY_NEG_INF = -0x7F800000 - TIE_SPAN
VMEM_LIMIT = 56 * 1024 * 1024

_PROJ_SIZES = (C_CONV, C_CONV, C_CONV, N_HEADS * HEAD_DIM, N_KV_HEADS * HEAD_DIM, N_KV_HEADS * HEAD_DIM,
               N_HEADS * HEAD_DIM, N_IDX_HEADS * IDX_DIM, IDX_DIM, N_IDX_HEADS)
_UB_OFF, _GC_OFF, _Q_OFF, _K_OFF, _V_OFF, _GA_OFF, _IQ_OFF, _IK_OFF, _IW_OFF, _D_IN = (
    int(s) for s in np.cumsum(_PROJ_SIZES))
_W_ROWS = _IK_OFF + LANE


def _cparams(n_axes):
    return pltpu.CompilerParams(dimension_semantics=("arbitrary",) * n_axes, vmem_limit_bytes=VMEM_LIMIT)


def _silu(x):
    return x * jax.nn.sigmoid(x)


def _prep_w_in(w):
    assert w.shape[1] == _D_IN
    return jnp.pad(w.T, ((0, _W_ROWS - _D_IN), (0, 0))).astype(MXU_DTYPE)


def _t5_bucket(rel):
    n = jnp.maximum(rel, 0)
    max_exact = NUM_BUCKETS // 2
    large = max_exact + (jnp.log(jnp.maximum(n, 1).astype(jnp.float32) / max_exact)
                         / math.log(MAX_DISTANCE / max_exact)
                         * (NUM_BUCKETS - max_exact)).astype(jnp.int32)
    large = jnp.minimum(large, NUM_BUCKETS - 1)
    return jnp.where(n < max_exact, n, large)


def _bias_tiles(rel_bias):
    far = rel_bias[NUM_BUCKETS - 1]
    d = jnp.arange(2 * TQ, dtype=jnp.int32)
    dists = (jnp.where(d < TQ, d, 0),
             jnp.where(d < TQ, d + TQ, d - TQ))
    tiles = []
    for dist in dists:
        v = ((rel_bias[_t5_bucket(dist)] - far) * LOG2E).T
        rep = jnp.tile(v, (1, TQ))[:, :TQ * (2 * TQ - 1)].reshape(N_HEADS, TQ, 2 * TQ - 1)
        tiles.append(rep[:, :, :TQ])
    tiles.append(jnp.zeros_like(tiles[0]))
    return jnp.stack(tiles).astype(jnp.float32)


_NT = (((1,), (1,)), ((), ()))
_Q_SCALE = HEAD_DIM ** -0.5 * LOG2E
_IQ_SCALE = IDX_DIM ** -0.5
_IW_SCALE = N_IDX_HEADS ** -0.5
_IW_ROWS = 16


def _proj_kernel(x_ref, g_ref, w_ref, u_ref, gc_ref, kv_ref, ga_ref, tail_ref, *q_refs, feature_major):
    x = x_ref[...]
    ms = jnp.mean(x * x, axis=-1, keepdims=True)
    xn = (x * lax.rsqrt(ms + EPS) * g_ref[...]).astype(MXU_DTYPE)

    def mm(lo, hi):
        return lax.dot_general(xn, w_ref[lo:hi, :], _NT, preferred_element_type=jnp.float32)

    def mm_t(lo, hi):
        return lax.dot_general(w_ref[lo:hi, :], xn, _NT, preferred_element_type=jnp.float32)

    u_ref[...] = mm(0, _UB_OFF) * jax.nn.sigmoid(mm(_UB_OFF, _GC_OFF))
    gc_ref[...] = mm(_GC_OFF, _Q_OFF)
    kv_ref[...] = mm(_K_OFF, _GA_OFF)
    ga_ref[...] = mm(_GA_OFF, _IQ_OFF)
    t = mm(_IK_OFF, _W_ROWS)
    lane = lax.broadcasted_iota(jnp.int32, t.shape, 1)
    tail_ref[...] = t * jnp.where(lane >= IDX_DIM, _IW_SCALE, 1.0)
    if feature_major:
        qzt_ref, iqzt_ref, vt_ref, iwt_ref = q_refs
        qt = (mm_t(_Q_OFF, _K_OFF) * _Q_SCALE).astype(qzt_ref.dtype)
        qzt_ref[...] = jnp.zeros(qzt_ref.shape, qzt_ref.dtype)
        for h in range(N_HEADS):
            lo = h * LANE + (h // GROUP) * HEAD_DIM
            qzt_ref[lo:lo + HEAD_DIM, :] = qt[h * HEAD_DIM:(h + 1) * HEAD_DIM]
        iqt = (mm_t(_IQ_OFF, _IK_OFF) * _IQ_SCALE).astype(iqzt_ref.dtype)
        iqzt_ref[...] = jnp.zeros(iqzt_ref.shape, iqzt_ref.dtype)
        for h in range(N_IDX_HEADS):
            iqzt_ref[h * LANE:h * LANE + IDX_DIM, :] = iqt[h * IDX_DIM:(h + 1) * IDX_DIM]
        vt_ref[...] = mm_t(_V_OFF, _GA_OFF).astype(vt_ref.dtype)
        iwt_ref[...] = mm_t(_IW_OFF, _IW_OFF + _IW_ROWS) * _IW_SCALE
    else:
        q_ref, iq_ref = q_refs
        q_ref[...] = (mm(_Q_OFF, _K_OFF) * _Q_SCALE).astype(q_ref.dtype)
        iq_ref[...] = (mm(_IQ_OFF, _IK_OFF) * _IQ_SCALE).astype(iq_ref.dtype)


def _project(x, g, w, tm, feature_major):
    r, d = x.shape
    assert r % tm == 0
    rows = lambda c: pl.BlockSpec((tm, c), lambda i: (i, 0))
    cols = lambda c: pl.BlockSpec((c, tm), lambda i: (0, i))
    feat = N_KV_HEADS * HEAD_DIM
    specs = [rows(C_CONV), rows(C_CONV), rows(2 * feat), rows(N_HEADS * HEAD_DIM), rows(LANE)]
    shapes = [jax.ShapeDtypeStruct((r, s.block_shape[1]), jnp.float32) for s in specs]
    if feature_major:
        extra = [(N_HEADS * LANE, MXU_DTYPE), (N_IDX_HEADS * LANE, MXU_DTYPE), (feat, MXU_DTYPE),
                 (_IW_ROWS, jnp.float32)]
        specs += [cols(c) for c, _ in extra]
        shapes += [jax.ShapeDtypeStruct((c, r), t) for c, t in extra]
    else:
        extra = [(N_HEADS * HEAD_DIM, MXU_DTYPE), (N_IDX_HEADS * IDX_DIM, MXU_DTYPE)]
        specs += [rows(c) for c, _ in extra]
        shapes += [jax.ShapeDtypeStruct((r, c), t) for c, t in extra]
    return pl.pallas_call(
        functools.partial(_proj_kernel, feature_major=feature_major),
        grid=(r // tm,),
        in_specs=[rows(d), pl.BlockSpec((1, d), lambda i: (0, 0)), pl.BlockSpec((_W_ROWS, d), lambda i: (0, 0))],
        out_specs=specs,
        out_shape=shapes,
        compiler_params=_cparams(1),
        name="proj",
    )(x, g.reshape(1, d), w)


def _conv_kernel(prev_ref, cur_ref, gc_ref, w_ref, b_ref, lg_ref, lb_ref, o_ref, ext_ref):
    i = pl.program_id(1)
    ext_ref[0, 0:HALO, :] = jnp.where(i > 0, prev_ref[...], 0.0)
    ext_ref[0, HALO:HALO + TQ, :] = cur_ref[...]
    n_ext = HALO + TQ
    for s in range(1, ROWS):
        ext_ref[s, 0:n_ext - ROWS, :] = ext_ref[0, s:s + n_ext - ROWS, :]
    off = HALO - (CONV_W - 1)
    for r0 in range(0, TQ, CONV_SUB):
        acc = jnp.zeros((CONV_SUB, C_CONV), jnp.float32)
        for j in range(CONV_W):
            s = (off + j) % ROWS
            lo = r0 + off + j - s
            acc = acc + w_ref[j:j + 1, :] * ext_ref[s, lo:lo + CONV_SUB, :]
        y = acc + b_ref[...]
        mu = jnp.mean(y, axis=-1, keepdims=True)
        dev = y - mu
        var = jnp.mean(dev * dev, axis=-1, keepdims=True)
        yn = dev * lax.rsqrt(var + LN_EPS) * lg_ref[...] + lb_ref[...]
        o_ref[r0:r0 + CONV_SUB, :] = (_silu(yn) * _silu(gc_ref[r0:r0 + CONV_SUB, :])).astype(o_ref.dtype)


def _conv_branch(u, gc, w, b, lg, lb):
    bsz, lq, c = u.shape
    per = TQ // HALO
    row = lambda a: a.reshape(1, c)
    vec = pl.BlockSpec((1, c), lambda bi, i: (0, 0))
    tile = pl.BlockSpec((None, TQ, c), lambda bi, i: (bi, i, 0))
    return pl.pallas_call(
        _conv_kernel,
        grid=(bsz, lq // TQ),
        in_specs=[pl.BlockSpec((None, HALO, c), lambda bi, i: (bi, jnp.maximum(i * per - 1, 0), 0)),
                  tile, tile, pl.BlockSpec((CONV_W, c), lambda bi, i: (0, 0)), vec, vec, vec],
        out_specs=tile,
        out_shape=jax.ShapeDtypeStruct((bsz, lq, c), MXU_DTYPE),
        scratch_shapes=[pltpu.VMEM((ROWS, HALO + TQ, c), jnp.float32)],
        compiler_params=_cparams(2),
        name="conv",
    )(u, u, gc, w, row(b), row(lg), row(lb))


_INT_MIN = np.int32(INT_MIN)
_BYTE_ONES = np.int32(0x01010101)
_BYTE_LOW = np.int32(0x7F7F7F7F)
_HALF_LOW_BYTES = np.int32(0x00FF00FF)
_GUARD = np.int32(0x80808080 - (1 << 32))
assert SUB == 4 and DIGIT_BITS == 7


def _to_key(s, idx):
    bits = lax.bitcast_convert_type(s, jnp.int32)
    key = jnp.where(bits < 0, (_INT_MIN - bits) - TIE_SPAN, bits)
    return jnp.where((bits == 0) | (bits == _INT_MIN), -1 - idx, key)


def _for_chunks(nkc, body, init, unroll):
    n_main = nkc // unroll

    def main(i, carry):
        for k in range(unroll):
            carry = body(i * unroll + k, carry)
        return carry

    carry = lax.fori_loop(0, n_main, main, init)
    return lax.fori_loop(n_main * unroll, nkc, body, carry)


def _pack_digits(keys_ref, dig_ref, c):
    for lv in range(N_LEVELS):
        word = None
        for j in range(SUB):
            u = keys_ref[c, j * TQ:(j + 1) * TQ, :]
            if lv == 0:
                u = u ^ _INT_MIN
            move = KEY_SHIFTS[lv] - 8 * j
            u = lax.shift_right_logical(u, move) if move >= 0 else lax.shift_left(u, -move)
            field = u & np.int32(((1 << KEY_DIGITS[lv]) - 1) << (8 * j))
            word = field if word is None else word | field
        dig_ref[lv, c] = word


def _count_fields(work_ref, nkc, cand_bytes):
    q = work_ref.shape[2]

    def body(c, accs):
        accs = list(accs)
        for n, r in enumerate(range(0, TQ, ROWS)):
            diff = work_ref[c, r:r + ROWS, :] - cand_bytes
            accs[n % N_ACC] = accs[n % N_ACC] + (lax.shift_right_logical(diff, DIGIT_BITS) & _BYTE_ONES)
        return tuple(accs)

    accs = _for_chunks(nkc, body, (jnp.zeros((ROWS, q), jnp.int32),) * N_ACC, unroll=4)
    halves = jnp.zeros((ROWS, q), jnp.int32)
    for a in accs:
        halves = halves + (a & _HALF_LOW_BYTES) + (lax.shift_right_logical(a, 8) & _HALF_LOW_BYTES)
    total = (halves & 0xFFFF) + lax.shift_right_logical(halves, 16)
    return jnp.sum(total.astype(jnp.float32), axis=0, keepdims=True)


def _digit_search(dig_ref, work_ref, alive_ref, nkc, shifts, digits, target, all_take_part=False):
    q = dig_ref.shape[3]
    assert dig_ref.shape[1] * (TQ // ROWS) <= 255 * N_ACC
    zero = jnp.zeros((1, q), jnp.float32)
    value = jnp.zeros((1, q), jnp.int32)
    above, done, cnt_ge, cnt_gt = zero, zero, target + 1.0, zero

    def load_fields(c, carry):
        word = dig_ref[0, c]
        work_ref[c] = (word if all_take_part else word & alive_ref[c]) | _GUARD
        return carry

    _for_chunks(nkc, load_fields, 0, unroll=4)

    for lv, (shift, n_bits) in enumerate(zip(shifts, digits)):

        def digit_bit(i, state):
            digit, cnt_ge, cnt_gt, cnt_rej, done = state
            cand = digit | lax.shift_left(jnp.int32(1), n_bits - 1 - i)
            cnt_alive = _count_fields(work_ref, nkc, cand * _BYTE_ONES)
            cnt = above + cnt_alive
            take = (cnt >= target) & (done == 0.0)
            drop = (cnt < target) & (done == 0.0)
            return (jnp.where(take, cand, digit), jnp.where(take, cnt, cnt_ge), jnp.where(drop, cnt, cnt_gt),
                    jnp.where(drop, cnt_alive, cnt_rej),
                    jnp.where(take & (cnt == target), 1.0, done))

        digit, cnt_ge, cnt_gt, cnt_rej, done = lax.fori_loop(
            0, n_bits, digit_bit, (jnp.zeros((1, q), jnp.int32), cnt_ge, cnt_gt, zero, done))
        above = above + cnt_rej
        value = value | lax.shift_left(digit, shift)

        if lv + 1 < len(shifts):
            digit_bytes = digit * _BYTE_ONES

            def narrow(c, carry):
                differs = lax.shift_right_logical(((dig_ref[lv, c] ^ digit_bytes) + _BYTE_LOW) & _GUARD, DIGIT_BITS)
                alive = (_BYTE_ONES - differs) * DIGIT_MASK
                if lv > 0 or not all_take_part:
                    alive = alive & alive_ref[c]
                alive_ref[c] = alive
                work_ref[c] = (dig_ref[lv + 1, c] & alive) | _GUARD
                return carry

            _for_chunks(nkc, narrow, 0, unroll=4)

    return value, cnt_ge, cnt_gt, done


def _threshold(keys_ref, dig_ref, work_ref, alive_ref, nkc, kk):
    q = keys_ref.shape[2]
    kf = jnp.full((1, q), kk, jnp.float32)
    prefix, cnt_thr, cnt_gt, _ = _digit_search(dig_ref, work_ref, alive_ref, nkc, KEY_SHIFTS, KEY_DIGITS, kf,
                                               all_take_part=True)
    thr = prefix ^ _INT_MIN
    admissible = thr != KEY_NEG_INF
    surplus = (cnt_thr > kf) & admissible

    @pl.when(jnp.max(jnp.where(surplus, 1.0, 0.0)) > 0.0)
    def _():
        assert keys_ref.shape[0] * KC <= 1 << (2 * DIGIT_BITS)
        top = (1 << (2 * DIGIT_BITS)) - 1
        row = lax.broadcasted_iota(jnp.int32, (TQ, q), 0)

        def tie_fields(c, carry):
            hi = lo = tie = None
            for j in range(SUB):
                is_tie = keys_ref[c, j * TQ:(j + 1) * TQ, :] == thr
                rev = top - (c * KC + j * TQ + row)
                fields = [jnp.where(is_tie, f, 0) for f in
                          (lax.shift_right_logical(rev, DIGIT_BITS), rev & DIGIT_MASK, DIGIT_MASK)]
                fields = [f if j == 0 else lax.shift_left(f, 8 * j) for f in fields]
                hi, lo, tie = fields if j == 0 else (hi | fields[0], lo | fields[1], tie | fields[2])
            dig_ref[0, c], dig_ref[1, c], alive_ref[c] = hi, lo, tie
            return carry

        lax.fori_loop(0, nkc, tie_fields, 0)
        cut, _, _, _ = _digit_search(dig_ref, work_ref, alive_ref, nkc, (DIGIT_BITS, 0), (DIGIT_BITS, DIGIT_BITS),
                                     kf - cnt_gt)

        def drop_losers(c, carry):
            for j in range(SUB):
                rows = slice(j * TQ, (j + 1) * TQ)
                x = keys_ref[c, rows, :]
                lost = (x == thr) & (top - (c * KC + j * TQ + row) < cut)
                keys_ref[c, rows, :] = jnp.where(lost, thr - 1, x)
            return carry

        lax.fori_loop(0, nkc, drop_losers, 0)

    return jnp.where(admissible, thr, KEY_NEG_INF + 1)


def _attend_kernel(iqt_ref, iwt_ref, qzt_ref, ga_ref, ik_ref, k_ref, vt_ref, toep_ref, o_ref,
                   keys_ref, dig_ref, work_ref, alive_ref, m_ref, acc_ref, lg0_ref, lg1_ref,
                   wq_ref, *, kk):
    qi = pl.program_id(1)
    nkc = qi // SUB + 1
    qpos = qi * TQ + lax.broadcasted_iota(jnp.int32, (TQ, TQ), 1)
    krow = lax.broadcasted_iota(jnp.int32, (TQ, TQ), 0)
    for h in range(N_HEADS):
        wq_ref[:, h * TQ:(h + 1) * TQ] = qzt_ref[h]

    def score_chunk(c, carry, causal):
        iqt = jnp.concatenate([iqt_ref[h] for h in range(N_IDX_HEADS)], axis=1)
        w = iwt_ref[...]
        for j in range(SUB):
            st = jnp.dot(ik_ref[c, j * TQ:(j + 1) * TQ, :], iqt, preferred_element_type=jnp.float32)
            s = jnp.zeros((TQ, TQ), jnp.float32)
            for h in range(N_IDX_HEADS):
                s = s + w[h:h + 1, :] * jnp.maximum(st[:, h * TQ:(h + 1) * TQ], 0.0)
            kpos = c * KC + j * TQ + krow
            if causal:
                s = jnp.where(kpos <= qpos, s, -jnp.inf)
            keys_ref[c, j * TQ:(j + 1) * TQ, :] = _to_key(s, kpos)
        _pack_digits(keys_ref, dig_ref, c)
        return carry

    _for_chunks(nkc - 1, functools.partial(score_chunk, causal=False), 0, unroll=2)
    score_chunk(nkc - 1, 0, causal=True)
    thr = _threshold(keys_ref, dig_ref, work_ref, alive_ref, nkc, kk)

    m_ref[...] = jnp.full(m_ref.shape, NEG, jnp.float32)
    acc_ref[...] = jnp.zeros(acc_ref.shape, jnp.float32)

    def bias(c, h):
        return jnp.concatenate([toep_ref[jnp.clip(qi - (c * SUB + j), 0, 2), h] for j in range(SUB)], axis=0)

    def logits(c, lg_ref, near):
        md = jnp.where(keys_ref[c] >= thr, 0.0, NEG)
        add = [md + bias(c, h) for h in range(N_HEADS)] if near else [md] * N_HEADS
        lg_ref[...] = (jnp.dot(k_ref[c], wq_ref[...], preferred_element_type=jnp.float32)
                       + jnp.concatenate(add, axis=1))

    def add_bias(c, lg_ref):
        for h in range(N_HEADS):
            sl = slice(h * TQ, (h + 1) * TQ)
            lg_ref[:, sl] = lg_ref[:, sl] + bias(c, h)

    def softmax_pv(c, lg_ref):
        for pr in range(N_HEADS // 2):
            kv = pr // (GROUP // 2)
            sl = slice(2 * pr * TQ, (2 * pr + 2) * TQ)
            m_old = m_ref[pr]
            m_new = jnp.maximum(m_old, jnp.max(lg_ref[:, sl], axis=0, keepdims=True))
            alpha = jnp.exp2(m_old - m_new)
            p = jnp.exp2(lg_ref[:, sl] - m_new)
            pv = jnp.dot(vt_ref[c, kv * V_ROWS:(kv + 1) * V_ROWS, :], p.astype(MXU_DTYPE),
                         preferred_element_type=jnp.float32)
            acc_ref[pr] = alpha * acc_ref[pr] + pv
            m_ref[pr] = m_new

    n_far2 = (jnp.maximum(qi - 1, 0) // SUB) // 2

    def far_pair(i, carry):
        logits(2 * i + 1, lg1_ref, near=False)
        softmax_pv(2 * i, lg0_ref)
        logits(2 * i + 2, lg0_ref, near=False)
        softmax_pv(2 * i + 1, lg1_ref)
        return carry

    logits(0, lg0_ref, near=False)
    lax.fori_loop(0, n_far2, far_pair, 0)

    c0 = 2 * n_far2
    add_bias(c0, lg0_ref)

    @pl.when(c0 + 1 < nkc)
    def _():
        logits(c0 + 1, lg1_ref, near=True)
        softmax_pv(c0, lg0_ref)

    @pl.when(c0 + 1 >= nkc)
    def _():
        softmax_pv(c0, lg0_ref)

    @pl.when(c0 + 2 < nkc)
    def _():
        logits(c0 + 2, lg0_ref, near=True)
        softmax_pv(c0 + 1, lg1_ref)
        softmax_pv(c0 + 2, lg0_ref)

    @pl.when((c0 + 1 < nkc) & (c0 + 2 >= nkc))
    def _():
        softmax_pv(c0 + 1, lg1_ref)

    for pr in range(N_HEADS // 2):
        acc = acc_ref[pr]
        o_t = acc[:HEAD_DIM] / acc[HEAD_DIM:HEAD_DIM + 1]
        pair_t = jnp.concatenate([o_t[:, :TQ], o_t[:, TQ:]], axis=0)
        sl = slice(pr * LANE, (pr + 1) * LANE)
        o_ref[:, sl] = (pair_t.T * _silu(ga_ref[:, sl])).astype(o_ref.dtype)


def _attend(iqt, iwt, qzt, ga, ik, k, vt, toep_t, kk):
    bsz, lq = ga.shape[:2]
    nq = lq // TQ
    nc = k.shape[1]
    lanes = lambda a: pl.BlockSpec(a.shape[:-1] + (TQ,), lambda b, i: (0,) * (a.ndim - 1) + (b * nq + i,))
    whole = lambda a: pl.BlockSpec((None,) + a.shape[1:], lambda b, i: (b,) + (0,) * (a.ndim - 1))
    rows = pl.BlockSpec((None, TQ, N_HEADS * HEAD_DIM), lambda b, i: (b, i, 0))
    return pl.pallas_call(
        functools.partial(_attend_kernel, kk=kk),
        grid=(bsz, nq),
        in_specs=[lanes(iqt), lanes(iwt), lanes(qzt), rows, whole(ik), whole(k), whole(vt),
                  pl.BlockSpec(toep_t.shape, lambda b, i: (0, 0, 0, 0))],
        out_specs=rows,
        out_shape=jax.ShapeDtypeStruct((bsz, lq, N_HEADS * HEAD_DIM), MXU_DTYPE),
        scratch_shapes=[pltpu.VMEM((nc, KC, TQ), jnp.int32),
                        pltpu.VMEM((N_LEVELS, nc, TQ, TQ), jnp.int32), pltpu.VMEM((nc, TQ, TQ), jnp.int32),
                        pltpu.VMEM((nc, TQ, TQ), jnp.int32),
                        pltpu.VMEM((N_HEADS // 2, 1, 2 * TQ), jnp.float32),
                        pltpu.VMEM((N_HEADS // 2, V_ROWS, 2 * TQ), jnp.float32),
                        pltpu.VMEM((KC, N_HEADS * TQ), jnp.float32), pltpu.VMEM((KC, N_HEADS * TQ), jnp.float32),
                        pltpu.VMEM((LANE, N_HEADS * TQ), MXU_DTYPE)],
        compiler_params=_cparams(2),
        name="attend",
    )(iqt, iwt, qzt, ga, ik, k, vt, toep_t)


def _out_kernel(x_ref, mc_ref, ma_ref, wc_ref, wa_ref, o_ref):
    o_ref[...] = (x_ref[...]
                  + jnp.dot(mc_ref[...], wc_ref[...], preferred_element_type=jnp.float32)
                  + jnp.dot(ma_ref[...], wa_ref[...], preferred_element_type=jnp.float32))


def _out_final_kernel(x_ref, mc_ref, ma_ref, wc_ref, wa_ref, g_ref, o_ref, y_ref):
    x = (x_ref[...]
         + jnp.dot(mc_ref[...], wc_ref[...], preferred_element_type=jnp.float32)
         + jnp.dot(ma_ref[...], wa_ref[...], preferred_element_type=jnp.float32))
    o_ref[...] = x
    ms = jnp.mean(x * x, axis=-1, keepdims=True)
    y_ref[...] = x * lax.rsqrt(ms + EPS) * g_ref[...]


def _out_proj(x, mc, ma, wc, wa, tm, final_g=None):
    r, d = x.shape
    c = mc.shape[1]
    rows = lambda w: pl.BlockSpec((tm, w), lambda i: (i, 0))
    full = lambda a: pl.BlockSpec(a.shape, lambda i: (0, 0))
    if final_g is None:
        return pl.pallas_call(
            _out_kernel, grid=(r // tm,),
            in_specs=[rows(d), rows(c), rows(c), full(wc), full(wa)],
            out_specs=rows(d), out_shape=jax.ShapeDtypeStruct((r, d), jnp.float32),
            compiler_params=_cparams(1), name="out",
        )(x, mc, ma, wc, wa)
    g = final_g.reshape(1, d)
    return pl.pallas_call(
        _out_final_kernel, grid=(r // tm,),
        in_specs=[rows(d), rows(c), rows(c), full(wc), full(wa), full(g)],
        out_specs=[rows(d), rows(d)], out_shape=[jax.ShapeDtypeStruct((r, d), jnp.float32)] * 2,
        compiler_params=_cparams(1), name="out_final",
    )(x, mc, ma, wc, wa, g)


def _sconv_kernel(st_ref, u_ref, gc_ref, w_ref, b_ref, lg_ref, lb_ref, o_ref):
    acc = jnp.zeros(u_ref.shape, jnp.float32)
    for j in range(CONV_W - 1):
        acc = acc + w_ref[j:j + 1, :] * st_ref[j]
    acc = acc + w_ref[CONV_W - 1:CONV_W, :] * u_ref[...]
    y = acc + b_ref[...]
    mu = jnp.mean(y, axis=-1, keepdims=True)
    dev = y - mu
    var = jnp.mean(dev * dev, axis=-1, keepdims=True)
    yn = dev * lax.rsqrt(var + LN_EPS) * lg_ref[...] + lb_ref[...]
    o_ref[...] = (_silu(yn) * _silu(gc_ref[...])).astype(o_ref.dtype)


def _sample_conv(state_t, u, gc, w, b, lg, lb):
    db, c = u.shape
    row = lambda a: a.reshape(1, c)
    full = lambda a: pl.BlockSpec(a.shape, lambda i: (0,) * a.ndim)
    args = (state_t, u, gc, w, row(b), row(lg), row(lb))
    return pl.pallas_call(
        _sconv_kernel, grid=(1,), in_specs=[full(a) for a in args],
        out_specs=pl.BlockSpec((db, c), lambda i: (0, 0)),
        out_shape=jax.ShapeDtypeStruct((db, c), MXU_DTYPE),
        compiler_params=_cparams(1), name="sconv",
    )(*args)


def _dec_score_kernel(pt_ref, iq_ref, iw_ref, ikn_ref, cik_ref, o_ref, buf_ref, sem_ref, *, layer, width):
    slot = _fetch_pages(pt_ref, (cik_ref,), (buf_ref,), (sem_ref,), layer)
    iq = iq_ref[...][:, :IDX_DIM]
    w = iw_ref[...]
    ikt = buf_ref[slot].astype(MXU_DTYPE)
    s = jnp.dot(iq, ikt, preferred_element_type=jnp.float32)
    sc = jnp.sum(w * jnp.maximum(s, 0.0), axis=0, keepdims=True)
    ikn = ikn_ref[...][:, :IDX_DIM].astype(MXU_DTYPE).astype(jnp.float32)
    sn = jnp.sum(iq.astype(jnp.float32) * ikn, axis=-1, keepdims=True)
    scn = jnp.sum(w * jnp.maximum(sn, 0.0), axis=0, keepdims=True)
    lane = lax.broadcasted_iota(jnp.int32, (1, width - sc.shape[1]), 1)
    o_ref[...] = jnp.concatenate([sc, jnp.where(lane == 0, scn, -jnp.inf)], axis=1)


def _dec_scores(page_table, iq3, iw3, tail3, cache_ik, layer, width):
    db, n_pages = page_table.shape
    di, page = cache_ik.shape[2:]
    seq = lambda a: pl.BlockSpec((None,) + a.shape[1:], lambda b, pt: (b, 0, 0))
    return pl.pallas_call(
        functools.partial(_dec_score_kernel, layer=layer, width=width),
        grid_spec=pltpu.PrefetchScalarGridSpec(
            num_scalar_prefetch=1, grid=(db,),
            in_specs=[seq(iq3), seq(iw3), seq(tail3), pl.BlockSpec(memory_space=pl.ANY)],
            out_specs=pl.BlockSpec((None, 1, width), lambda b, pt: (b, 0, 0)),
            scratch_shapes=[pltpu.VMEM((PAGE_SLOTS, di, n_pages * page), cache_ik.dtype),
                            pltpu.SemaphoreType.DMA((PAGE_SLOTS,))]),
        out_shape=jax.ShapeDtypeStruct((db, 1, width), jnp.float32),
        compiler_params=_cparams(1), name="dec_score",
    )(page_table, iq3, iw3, tail3, cache_ik)


def _dec_select_kernel(s_ref, o_ref, keys_ref, dig_ref, work_ref, alive_ref, *, kk):
    nc = keys_ref.shape[0]
    row = lax.broadcasted_iota(jnp.int32, (KC, keys_ref.shape[2]), 0)
    for c in range(nc):
        keys_ref[c] = _to_key(s_ref[c * KC:(c + 1) * KC, :], c * KC + row)
        _pack_digits(keys_ref, dig_ref, c)
    thr = _threshold(keys_ref, dig_ref, work_ref, alive_ref, nc, kk)
    for c in range(nc):
        o_ref[c * KC:(c + 1) * KC, :] = jnp.where(keys_ref[c] >= thr, 0.0, NEG)


def _dec_select(scores_t, kk):
    width, db = scores_t.shape
    nc = width // KC
    spec = pl.BlockSpec((width, db), lambda i: (0, 0))
    return pl.pallas_call(
        functools.partial(_dec_select_kernel, kk=kk),
        grid=(1,), in_specs=[spec], out_specs=spec,
        out_shape=jax.ShapeDtypeStruct((width, db), jnp.float32),
        scratch_shapes=[pltpu.VMEM((nc, KC, db), jnp.int32),
                        pltpu.VMEM((N_LEVELS, nc, TQ, db), jnp.int32), pltpu.VMEM((nc, TQ, db), jnp.int32),
                        pltpu.VMEM((nc, TQ, db), jnp.int32)],
        compiler_params=_cparams(1), name="dec_select",
    )(scores_t)


def _page_copies(pt_ref, cache_ref, buf_ref, sem_ref, layer, seq, slot):
    page = cache_ref.shape[3]
    n_pages = buf_ref.shape[2] // page
    return [pltpu.make_async_copy(cache_ref.at[layer, pt_ref[seq, p]],
                                  buf_ref.at[slot, :, pl.ds(p * page, page)], sem_ref.at[slot])
            for p in range(n_pages)]


def _fetch_pages(pt_ref, caches, bufs, sems, layer):
    b, nb = pl.program_id(0), pl.num_programs(0)
    ahead = PAGE_SLOTS - 1

    def start(seq, slot):
        for cache_ref, buf_ref, sem_ref in zip(caches, bufs, sems):
            for cp in _page_copies(pt_ref, cache_ref, buf_ref, sem_ref, layer, seq, slot):
                cp.start()

    for s in range(ahead):
        @pl.when((b == 0) & (s < nb))
        def _():
            start(s, s)

    @pl.when(b + ahead < nb)
    def _():
        start(b + ahead, (b + ahead) % PAGE_SLOTS)

    slot = b % PAGE_SLOTS
    for cache_ref, buf_ref, sem_ref in zip(caches, bufs, sems):
        for cp in _page_copies(pt_ref, cache_ref, buf_ref, sem_ref, layer, b, slot):
            cp.wait()
    return slot


def _dec_attend_kernel(pt_ref, qz_ref, kvn_ref, madd_ref, bias_ref, ga_ref, ck_ref, cv_ref, o_ref,
                       kbuf_ref, vbuf_ref, ksem_ref, vsem_ref, *, layer):
    slot = _fetch_pages(pt_ref, (ck_ref, cv_ref), (kbuf_ref, vbuf_ref), (ksem_ref, vsem_ref), layer)
    qz = qz_ref[...]
    past = kbuf_ref.shape[2]
    kt = kbuf_ref[slot].astype(MXU_DTYPE)
    vt = vbuf_ref[slot].astype(MXU_DTYPE)
    lg = (jnp.dot(qz, kt, preferred_element_type=jnp.float32)
          + bias_ref[:, :past] + madd_ref[:, :past])
    kvn = kvn_ref[...].astype(MXU_DTYPE).astype(jnp.float32)
    lgn = jnp.sum(qz.astype(jnp.float32) * kvn[:, :LANE], axis=-1, keepdims=True)
    lgn = lgn + bias_ref[:, past:past + 1] + madd_ref[:, past:past + 1]
    m = jnp.maximum(lgn, jnp.max(lg, axis=-1, keepdims=True))
    pn = jnp.exp2(lgn - m)
    pp = jnp.exp2(lg - m)
    den = pn + jnp.sum(pp, axis=-1, keepdims=True)
    o = (pn.astype(MXU_DTYPE).astype(jnp.float32) * kvn[:, LANE:]
         + lax.dot_general(pp.astype(MXU_DTYPE), vt, _NT, preferred_element_type=jnp.float32))
    o = o / den
    swapped = pltpu.roll(o, HEAD_DIM, axis=1)
    lane = lax.broadcasted_iota(jnp.int32, (1, LANE), 1)
    pairs = []
    for j in range(N_HEADS // 2):
        first, second = (o, swapped) if 2 * j < GROUP else (swapped, o)
        pairs.append(jnp.where(lane < HEAD_DIM, first[2 * j:2 * j + 1], second[2 * j + 1:2 * j + 2]))
    o_ref[...] = (jnp.concatenate(pairs, axis=0) * _silu(ga_ref[...])).astype(o_ref.dtype)


def _dec_attend(page_table, qz3, kvn3, madd3, bias, ga3, cache_k, cache_v, layer):
    db, n_pages = page_table.shape
    feat, page = cache_k.shape[2:]
    seq = lambda a: pl.BlockSpec((None,) + a.shape[1:], lambda b, pt: (b, 0, 0))
    hbm = pl.BlockSpec(memory_space=pl.ANY)
    buf = pltpu.VMEM((PAGE_SLOTS, feat, n_pages * page), cache_k.dtype)
    sem = pltpu.SemaphoreType.DMA((PAGE_SLOTS,))
    return pl.pallas_call(
        functools.partial(_dec_attend_kernel, layer=layer),
        grid_spec=pltpu.PrefetchScalarGridSpec(
            num_scalar_prefetch=1, grid=(db,),
            in_specs=[seq(qz3), seq(kvn3), seq(madd3), pl.BlockSpec(bias.shape, lambda b, pt: (0, 0)), seq(ga3),
                      hbm, hbm],
            out_specs=pl.BlockSpec((None, GROUP, LANE), lambda b, pt: (b, 0, 0)),
            scratch_shapes=[buf, buf, sem, sem]),
        out_shape=jax.ShapeDtypeStruct((db, GROUP, LANE), MXU_DTYPE),
        compiler_params=_cparams(1), name="dec_attend",
    )(page_table, qz3, kvn3, madd3, bias, ga3, cache_k, cache_v)


def _round_up(x, m):
    return -(-x // m) * m


def _row_tile(rows):
    for f in (5, 4, 3, 2, 1):
        if rows % (f * TQ) == 0:
            return f * TQ
    return rows


def kernel(x_prompt, x_sample, cache_k, cache_v, cache_idx_k, state_conv, page_table, meta_tokens, rel_bias,
           norm_g, w_in, conv_w, conv_b, conv_ln_g, conv_ln_b, w_out, final_norm_g):
    bsz, seq, d = x_prompt.shape
    depth = w_in.shape[0]
    lp = seq + N_META
    lq = _round_up(lp, TQ)
    lk = _round_up(lp, KC)
    nq, nc = lq // TQ, lk // KC
    kk_p = min(TOPK_MAX, lp // 4)
    db = x_sample.shape[0]
    n_pool, page = cache_k.shape[1:3]
    n_pages = page_table.shape[1]
    past = n_pages * page
    kk_s = min(TOPK_MAX, (past + 1) // 4)
    width_s = _round_up(past + 1, KC)
    feat = N_KV_HEADS * HEAD_DIM
    tm = _row_tile(bsz * lq)

    xp = jnp.concatenate([jnp.broadcast_to(meta_tokens[None].astype(x_prompt.dtype), (bsz, N_META, d)), x_prompt,
                          jnp.zeros((bsz, lq - lp, d), x_prompt.dtype)], axis=1).reshape(bsz * lq, d)
    xs = x_sample.reshape(db, d)
    toep_t = _bias_tiles(rel_bias)
    bias_s = jnp.moveaxis(rel_bias[_t5_bucket(past - jnp.arange(width_s, dtype=jnp.int32))], -1, 0) * LOG2E
    cache_k4 = jnp.transpose(cache_k, (0, 1, 3, 4, 2)).reshape(depth, n_pool, feat, page)
    cache_v4 = jnp.transpose(cache_v, (0, 1, 3, 4, 2)).reshape(depth, n_pool, feat, page)
    cache_ik4 = jnp.swapaxes(cache_idx_k, 2, 3)

    def key_chunks(a):
        return jnp.pad(a, ((0, 0), (0, lk - lq), (0, 0))).astype(MXU_DTYPE).reshape(bsz, nc, KC, LANE)

    def value_chunks_t(a):
        a = jnp.pad(a.reshape(N_KV_HEADS, HEAD_DIM, bsz, lq), ((0, 0), (0, 0), (0, 0), (0, lk - lq)))
        a = jnp.concatenate([a, jnp.ones((N_KV_HEADS, V_ROWS - HEAD_DIM, bsz, lk), a.dtype)], axis=1)
        return jnp.transpose(a.reshape(N_KV_HEADS * V_ROWS, bsz, nc, KC), (1, 2, 0, 3))

    kp, vp, ikp, cp, ksm, vsm, iks, cs = ([] for _ in range(8))
    yp = ys = None
    for l in range(depth):
        w = _prep_w_in(w_in[l])
        wc = w_out[l, :C_CONV].astype(MXU_DTYPE)
        wa = w_out[l, C_CONV:].astype(MXU_DTYPE)
        last = l == depth - 1

        u, gc, kv, ga, tail, qzt, iqzt, vt, iwt = _project(xp, norm_g[l], w, tm, feature_major=True)
        b3 = lambda a: a.reshape(bsz, lq, a.shape[-1])
        u3, kv3, tail3 = b3(u), b3(kv), b3(tail)
        mixc = _conv_branch(u3, b3(gc), conv_w[l], conv_b[l], conv_ln_g[l], conv_ln_b[l])
        lane = jnp.arange(LANE)
        ik = key_chunks(jnp.where(lane < IDX_DIM, tail3, 0.0))
        mixa = _attend(iqzt.reshape(N_IDX_HEADS, LANE, bsz * lq), iwt, qzt.reshape(N_HEADS, LANE, bsz * lq), b3(ga),
                       ik, key_chunks(kv3[..., :feat]), value_chunks_t(vt), toep_t, kk_p)
        res = _out_proj(xp, mixc.reshape(bsz * lq, C_CONV), mixa.reshape(bsz * lq, -1), wc, wa, tm,
                        final_norm_g if last else None)
        xp, yp = res if last else (res, None)
        kp.append(kv3[:, :lp, :feat].reshape(bsz, lp, N_KV_HEADS, HEAD_DIM))
        vp.append(kv3[:, :lp, feat:].reshape(bsz, lp, N_KV_HEADS, HEAD_DIM))
        ikp.append(tail3[:, :lp, :IDX_DIM])
        cp.append(u3[:, lp - (CONV_W - 1):lp])

        u, gc, kv, ga, tail, q, iq = _project(xs, norm_g[l], w, db, feature_major=False)
        mixc = _sample_conv(jnp.transpose(state_conv[l], (1, 0, 2)), u, gc,
                            conv_w[l], conv_b[l], conv_ln_g[l], conv_ln_b[l])
        iw3 = tail[:, IDX_DIM:IDX_DIM + N_IDX_HEADS].reshape(db, N_IDX_HEADS, 1)
        scores = _dec_scores(page_table, iq.reshape(db, N_IDX_HEADS, IDX_DIM), iw3, tail.reshape(db, 1, LANE),
                             cache_ik4, l, width_s)
        madd = _dec_select(scores.reshape(db, width_s).T, kk_s).T
        q4 = q.reshape(db, N_KV_HEADS, GROUP, HEAD_DIM)
        qz3 = jnp.concatenate([jnp.pad(q4[:, 0], ((0, 0), (0, 0), (0, HEAD_DIM))),
                               jnp.pad(q4[:, 1], ((0, 0), (0, 0), (HEAD_DIM, 0)))], axis=1)
        mixa = _dec_attend(page_table, qz3, kv.reshape(db, 1, 2 * feat),
                           madd.reshape(db, 1, width_s), bias_s, ga.reshape(db, N_HEADS // 2, LANE),
                           cache_k4, cache_v4, l)
        res = _out_proj(xs, mixc, mixa.reshape(db, -1), wc, wa, db, final_norm_g if last else None)
        xs, ys = res if last else (res, None)
        ksm.append(kv[:, :feat].reshape(db, 1, N_KV_HEADS, HEAD_DIM))
        vsm.append(kv[:, feat:].reshape(db, 1, N_KV_HEADS, HEAD_DIM))
        iks.append(tail[:, None, :IDX_DIM])
        cs.append(jnp.concatenate([state_conv[l][:, 1:], u[:, None]], axis=1))

    y_prompt = yp.reshape(bsz, lq, d)[:, N_META:lp]
    y_sample = ys.reshape(db, 1, d)
    return (y_prompt, y_sample, jnp.stack(kp), jnp.stack(vp), jnp.stack(ikp), jnp.stack(cp),
            jnp.stack(ksm), jnp.stack(vsm), jnp.stack(iks), jnp.stack(cs))
```

```python
import functools
import math

import numpy as np
import jax
import jax.numpy as jnp
from jax import lax
from jax.experimental import pallas as pl
from jax.experimental.pallas import tpu as pltpu

N_HEADS = 8
N_KV_HEADS = 2
GROUP = N_HEADS // N_KV_HEADS
HEAD_DIM = 64
N_IDX_HEADS = 4
IDX_DIM = 64
C_CONV = 512
CONV_W = 31
TOPK_MAX = 256
N_META = 16
NUM_BUCKETS = 32
MAX_DISTANCE = 128
EPS = 1e-6
LN_EPS = 1e-5

LANE = 128
ROWS = 8
TQ = 128
KC = 512
SUB = KC // TQ
N_ACC = 4
DIGIT_BITS = 7
DIGIT_MASK = (1 << DIGIT_BITS) - 1
KEY_DIGITS = (7, 7, 7, 7, 4)
KEY_SHIFTS = tuple(32 - sum(KEY_DIGITS[:i + 1]) for i in range(len(KEY_DIGITS)))
N_LEVELS = len(KEY_DIGITS)
V_ROWS = HEAD_DIM + 16
PAGE_SLOTS = 4
HALO = 32
CONV_SUB = 64
MXU_DTYPE = jnp.bfloat16
NEG = -1e30
LOG2E = 1.4426950408889634
INT_MIN = -2 ** 31
TIE_SPAN = 1 << 14
KEY_NEG_INF = -0x7F800000 - TIE_SPAN
VMEM_LIMIT = 56 * 1024 * 1024

_PROJ_SIZES = (C_CONV, C_CONV, C_CONV, N_HEADS * HEAD_DIM, N_KV_HEADS * HEAD_DIM, N_KV_HEADS * HEAD_DIM,
               N_HEADS * HEAD_DIM, N_IDX_HEADS * IDX_DIM, IDX_DIM, N_IDX_HEADS)
_UB_OFF, _GC_OFF, _Q_OFF, _K_OFF, _V_OFF, _GA_OFF, _IQ_OFF, _IK_OFF, _IW_OFF, _D_IN = (
    int(s) for s in np.cumsum(_PROJ_SIZES))
_W_ROWS = _IK_OFF + LANE


def _cparams(n_axes):
    return pltpu.CompilerParams(dimension_semantics=("arbitrary",) * n_axes, vmem_limit_bytes=VMEM_LIMIT)


def _silu(x):
    return x * jax.nn.sigmoid(x)


def _prep_w_in(w):
    assert w.shape[1] == _D_IN
    return jnp.pad(w.T, ((0, _W_ROWS - _D_IN), (0, 0))).astype(MXU_DTYPE)


def _t5_bucket(rel):
    n = jnp.maximum(rel, 0)
    max_exact = NUM_BUCKETS // 2
    large = max_exact + (jnp.log(jnp.maximum(n, 1).astype(jnp.float32) / max_exact)
                         / math.log(MAX_DISTANCE / max_exact)
                         * (NUM_BUCKETS - max_exact)).astype(jnp.int32)
    large = jnp.minimum(large, NUM_BUCKETS - 1)
    return jnp.where(n < max_exact, n, large)


def _bias_tiles(rel_bias):
    far = rel_bias[NUM_BUCKETS - 1]
    d = jnp.arange(2 * TQ, dtype=jnp.int32)
    dists = (jnp.where(d < TQ, d, 0),
             jnp.where(d < TQ, d + TQ, d - TQ))
    tiles = []
    for dist in dists:
        v = ((rel_bias[_t5_bucket(dist)] - far) * LOG2E).T
        rep = jnp.tile(v, (1, TQ))[:, :TQ * (2 * TQ - 1)].reshape(N_HEADS, TQ, 2 * TQ - 1)
        tiles.append(rep[:, :, :TQ])
    tiles.append(jnp.zeros_like(tiles[0]))
    return jnp.stack(tiles).astype(jnp.float32)


_NT = (((1,), (1,)), ((), ()))
_Q_SCALE = HEAD_DIM ** -0.5 * LOG2E
_IQ_SCALE = IDX_DIM ** -0.5
_IW_SCALE = N_IDX_HEADS ** -0.5
_IW_ROWS = 16


def _proj_kernel(x_ref, g_ref, w_ref, u_ref, gc_ref, kv_ref, ga_ref, tail_ref, *q_refs, feature_major):
    x = x_ref[...]
    ms = jnp.mean(x * x, axis=-1, keepdims=True)
    xn = (x * lax.rsqrt(ms + EPS) * g_ref[...]).astype(MXU_DTYPE)

    def mm(lo, hi):
        return lax.dot_general(xn, w_ref[lo:hi, :], _NT, preferred_element_type=jnp.float32)

    def mm_t(lo, hi):
        return lax.dot_general(w_ref[lo:hi, :], xn, _NT, preferred_element_type=jnp.float32)

    u_ref[...] = mm(0, _UB_OFF) * jax.nn.sigmoid(mm(_UB_OFF, _GC_OFF))
    gc_ref[...] = mm(_GC_OFF, _Q_OFF)
    kv_ref[...] = mm(_K_OFF, _GA_OFF)
    ga_ref[...] = mm(_GA_OFF, _IQ_OFF)
    t = mm(_IK_OFF, _W_ROWS)
    lane = lax.broadcasted_iota(jnp.int32, t.shape, 1)
    tail_ref[...] = t * jnp.where(lane >= IDX_DIM, _IW_SCALE, 1.0)
    if feature_major:
        qzt_ref, iqzt_ref, vt_ref, iwt_ref = q_refs
        qt = (mm_t(_Q_OFF, _K_OFF) * _Q_SCALE).astype(qzt_ref.dtype)
        qzt_ref[...] = jnp.zeros(qzt_ref.shape, qzt_ref.dtype)
        for h in range(N_HEADS):
            lo = h * LANE + (h // GROUP) * HEAD_DIM
            qzt_ref[lo:lo + HEAD_DIM, :] = qt[h * HEAD_DIM:(h + 1) * HEAD_DIM]
        iqt = (mm_t(_IQ_OFF, _IK_OFF) * _IQ_SCALE).astype(iqzt_ref.dtype)
        iqzt_ref[...] = jnp.zeros(iqzt_ref.shape, iqzt_ref.dtype)
        for h in range(N_IDX_HEADS):
            iqzt_ref[h * LANE:h * LANE + IDX_DIM, :] = iqt[h * IDX_DIM:(h + 1) * IDX_DIM]
        vt_ref[...] = mm_t(_V_OFF, _GA_OFF).astype(vt_ref.dtype)
        iwt_ref[...] = mm_t(_IW_OFF, _IW_OFF + _IW_ROWS) * _IW_SCALE
    else:
        q_ref, iq_ref = q_refs
        q_ref[...] = (mm(_Q_OFF, _K_OFF) * _Q_SCALE).astype(q_ref.dtype)
        iq_ref[...] = (mm(_IQ_OFF, _IK_OFF) * _IQ_SCALE).astype(iq_ref.dtype)


def _project(x, g, w, tm, feature_major):
    r, d = x.shape
    assert r % tm == 0
    rows = lambda c: pl.BlockSpec((tm, c), lambda i: (i, 0))
    cols = lambda c: pl.BlockSpec((c, tm), lambda i: (0, i))
    feat = N_KV_HEADS * HEAD_DIM
    specs = [rows(C_CONV), rows(C_CONV), rows(2 * feat), rows(N_HEADS * HEAD_DIM), rows(LANE)]
    shapes = [jax.ShapeDtypeStruct((r, s.block_shape[1]), jnp.float32) for s in specs]
    if feature_major:
        extra = [(N_HEADS * LANE, MXU_DTYPE), (N_IDX_HEADS * LANE, MXU_DTYPE), (feat, MXU_DTYPE),
                 (_IW_ROWS, jnp.float32)]
        specs += [cols(c) for c, _ in extra]
        shapes += [jax.ShapeDtypeStruct((c, r), t) for c, t in extra]
    else:
        extra = [(N_HEADS * HEAD_DIM, MXU_DTYPE), (N_IDX_HEADS * IDX_DIM, MXU_DTYPE)]
        specs += [rows(c) for c, _ in extra]
        shapes += [jax.ShapeDtypeStruct((r, c), t) for c, t in extra]
    return pl.pallas_call(
        functools.partial(_proj_kernel, feature_major=feature_major),
        grid=(r // tm,),
        in_specs=[rows(d), pl.BlockSpec((1, d), lambda i: (0, 0)), pl.BlockSpec((_W_ROWS, d), lambda i: (0, 0))],
        out_specs=specs,
        out_shape=shapes,
        compiler_params=_cparams(1),
        name="proj",
    )(x, g.reshape(1, d), w)


def _conv_kernel(prev_ref, cur_ref, gc_ref, w_ref, b_ref, lg_ref, lb_ref, o_ref, ext_ref):
    i = pl.program_id(1)
    ext_ref[0, 0:HALO, :] = jnp.where(i > 0, prev_ref[...], 0.0)
    ext_ref[0, HALO:HALO + TQ, :] = cur_ref[...]
    n_ext = HALO + TQ
    for s in range(1, ROWS):
        ext_ref[s, 0:n_ext - ROWS, :] = ext_ref[0, s:s + n_ext - ROWS, :]
    off = HALO - (CONV_W - 1)
    for r0 in range(0, TQ, CONV_SUB):
        acc = jnp.zeros((CONV_SUB, C_CONV), jnp.float32)
        for j in range(CONV_W):
            s = (off + j) % ROWS
            lo = r0 + off + j - s
            acc = acc + w_ref[j:j + 1, :] * ext_ref[s, lo:lo + CONV_SUB, :]
        y = acc + b_ref[...]
        mu = jnp.mean(y, axis=-1, keepdims=True)
        dev = y - mu
        var = jnp.mean(dev * dev, axis=-1, keepdims=True)
        yn = dev * lax.rsqrt(var + LN_EPS) * lg_ref[...] + lb_ref[...]
        o_ref[r0:r0 + CONV_SUB, :] = (_silu(yn) * _silu(gc_ref[r0:r0 + CONV_SUB, :])).astype(o_ref.dtype)


def _conv_branch(u, gc, w, b, lg, lb):
    bsz, lq, c = u.shape
    per = TQ // HALO
    row = lambda a: a.reshape(1, c)
    vec = pl.BlockSpec((1, c), lambda bi, i: (0, 0))
    tile = pl.BlockSpec((None, TQ, c), lambda bi, i: (bi, i, 0))
    return pl.pallas_call(
        _conv_kernel,
        grid=(bsz, lq // TQ),
        in_specs=[pl.BlockSpec((None, HALO, c), lambda bi, i: (bi, jnp.maximum(i * per - 1, 0), 0)),
                  tile, tile, pl.BlockSpec((CONV_W, c), lambda bi, i: (0, 0)), vec, vec, vec],
        out_specs=tile,
        out_shape=jax.ShapeDtypeStruct((bsz, lq, c), MXU_DTYPE),
        scratch_shapes=[pltpu.VMEM((ROWS, HALO + TQ, c), jnp.float32)],
        compiler_params=_cparams(2),
        name="conv",
    )(u, u, gc, w, row(b), row(lg), row(lb))


_INT_MIN = np.int32(INT_MIN)
_BYTE_ONES = np.int32(0x01010101)
_BYTE_LOW = np.int32(0x7F7F7F7F)
_HALF_LOW_BYTES = np.int32(0x00FF00FF)
_GUARD = np.int32(0x80808080 - (1 << 32))
assert SUB == 4 and DIGIT_BITS == 7


def _to_key(s, idx):
    bits = lax.bitcast_convert_type(s, jnp.int32)
    key = jnp.where(bits < 0, (_INT_MIN - bits) - TIE_SPAN, bits)
    return jnp.where((bits == 0) | (bits == _INT_MIN), -1 - idx, key)


def _for_chunks(nkc, body, init, unroll):
    n_main = nkc // unroll

    def main(i, carry):
        for k in range(unroll):
            carry = body(i * unroll + k, carry)
        return carry

    carry = lax.fori_loop(0, n_main, main, init)
    return lax.fori_loop(n_main * unroll, nkc, body, carry)


def _pack_digits(keys_ref, dig_ref, c):
    for lv in range(N_LEVELS):
        word = None
        for j in range(SUB):
            u = keys_ref[c, j * TQ:(j + 1) * TQ, :]
            if lv == 0:
                u = u ^ _INT_MIN
            move = KEY_SHIFTS[lv] - 8 * j
            u = lax.shift_right_logical(u, move) if move >= 0 else lax.shift_left(u, -move)
            field = u & np.int32(((1 << KEY_DIGITS[lv]) - 1) << (8 * j))
            word = field if word is None else word | field
        dig_ref[lv, c] = word


def _count_fields(work_ref, nkc, cand_bytes):
    q = work_ref.shape[2]

    def body(c, accs):
        accs = list(accs)
        for n, r in enumerate(range(0, TQ, ROWS)):
            diff = work_ref[c, r:r + ROWS, :] - cand_bytes
            accs[n % N_ACC] = accs[n % N_ACC] + (lax.shift_right_logical(diff, DIGIT_BITS) & _BYTE_ONES)
        return tuple(accs)

    accs = _for_chunks(nkc, body, (jnp.zeros((ROWS, q), jnp.int32),) * N_ACC, unroll=4)
    halves = jnp.zeros((ROWS, q), jnp.int32)
    for a in accs:
        halves = halves + (a & _HALF_LOW_BYTES) + (lax.shift_right_logical(a, 8) & _HALF_LOW_BYTES)
    total = (halves & 0xFFFF) + lax.shift_right_logical(halves, 16)
    return jnp.sum(total.astype(jnp.float32), axis=0, keepdims=True)


def _digit_search(dig_ref, work_ref, alive_ref, nkc, shifts, digits, target, all_take_part=False):
    q = dig_ref.shape[3]
    assert dig_ref.shape[1] * (TQ // ROWS) <= 255 * N_ACC
    zero = jnp.zeros((1, q), jnp.float32)
    value = jnp.zeros((1, q), jnp.int32)
    above, done, cnt_ge, cnt_gt = zero, zero, target + 1.0, zero

    def load_fields(c, carry):
        word = dig_ref[0, c]
        work_ref[c] = (word if all_take_part else word & alive_ref[c]) | _GUARD
        return carry

    _for_chunks(nkc, load_fields, 0, unroll=4)

    for lv, (shift, n_bits) in enumerate(zip(shifts, digits)):

        def digit_bit(i, state):
            digit, cnt_ge, cnt_gt, cnt_rej, done = state
            cand = digit | lax.shift_left(jnp.int32(1), n_bits - 1 - i)
            cnt_alive = _count_fields(work_ref, nkc, cand * _BYTE_ONES)
            cnt = above + cnt_alive
            take = (cnt >= target) & (done == 0.0)
            drop = (cnt < target) & (done == 0.0)
            return (jnp.where(take, cand, digit), jnp.where(take, cnt, cnt_ge), jnp.where(drop, cnt, cnt_gt),
                    jnp.where(drop, cnt_alive, cnt_rej),
                    jnp.where(take & (cnt == target), 1.0, done))

        digit, cnt_ge, cnt_gt, cnt_rej, done = lax.fori_loop(
            0, n_bits, digit_bit, (jnp.zeros((1, q), jnp.int32), cnt_ge, cnt_gt, zero, done))
        above = above + cnt_rej
        value = value | lax.shift_left(digit, shift)

        if lv + 1 < len(shifts):
            digit_bytes = digit * _BYTE_ONES

            def narrow(c, carry):
                differs = lax.shift_right_logical(((dig_ref[lv, c] ^ digit_bytes) + _BYTE_LOW) & _GUARD, DIGIT_BITS)
                alive = (_BYTE_ONES - differs) * DIGIT_MASK
                if lv > 0 or not all_take_part:
                    alive = alive & alive_ref[c]
                alive_ref[c] = alive
                work_ref[c] = (dig_ref[lv + 1, c] & alive) | _GUARD
                return carry

            _for_chunks(nkc, narrow, 0, unroll=4)

    return value, cnt_ge, cnt_gt, done


def _threshold(keys_ref, dig_ref, work_ref, alive_ref, nkc, kk):
    q = keys_ref.shape[2]
    kf = jnp.full((1, q), kk, jnp.float32)
    prefix, cnt_thr, cnt_gt, _ = _digit_search(dig_ref, work_ref, alive_ref, nkc, KEY_SHIFTS, KEY_DIGITS, kf,
                                               all_take_part=True)
    thr = prefix ^ _INT_MIN
    admissible = thr != KEY_NEG_INF
    surplus = (cnt_thr > kf) & admissible

    @pl.when(jnp.max(jnp.where(surplus, 1.0, 0.0)) > 0.0)
    def _():
        assert keys_ref.shape[0] * KC <= 1 << (2 * DIGIT_BITS)
        top = (1 << (2 * DIGIT_BITS)) - 1
        row = lax.broadcasted_iota(jnp.int32, (TQ, q), 0)

        def tie_fields(c, carry):
            hi = lo = tie = None
            for j in range(SUB):
                is_tie = keys_ref[c, j * TQ:(j + 1) * TQ, :] == thr
                rev = top - (c * KC + j * TQ + row)
                fields = [jnp.where(is_tie, f, 0) for f in
                          (lax.shift_right_logical(rev, DIGIT_BITS), rev & DIGIT_MASK, DIGIT_MASK)]
                fields = [f if j == 0 else lax.shift_left(f, 8 * j) for f in fields]
                hi, lo, tie = fields if j == 0 else (hi | fields[0], lo | fields[1], tie | fields[2])
            dig_ref[0, c], dig_ref[1, c], alive_ref[c] = hi, lo, tie
            return carry

        lax.fori_loop(0, nkc, tie_fields, 0)
        cut, _, _, _ = _digit_search(dig_ref, work_ref, alive_ref, nkc, (DIGIT_BITS, 0), (DIGIT_BITS, DIGIT_BITS),
                                     kf - cnt_gt)

        def drop_losers(c, carry):
            for j in range(SUB):
                rows = slice(j * TQ, (j + 1) * TQ)
                x = keys_ref[c, rows, :]
                lost = (x == thr) & (top - (c * KC + j * TQ + row) < cut)
                keys_ref[c, rows, :] = jnp.where(lost, thr - 1, x)
            return carry

        lax.fori_loop(0, nkc, drop_losers, 0)

    return jnp.where(admissible, thr, KEY_NEG_INF + 1)


def _attend_kernel(iqt_ref, iwt_ref, qzt_ref, ga_ref, ik_ref, k_ref, vt_ref, toep_ref, o_ref,
                   keys_ref, dig_ref, work_ref, alive_ref, m_ref, acc_ref, lg0_ref, lg1_ref,
                   wq_ref, *, kk):
    qi = pl.program_id(1)
    nkc = qi // SUB + 1
    qpos = qi * TQ + lax.broadcasted_iota(jnp.int32, (TQ, TQ), 1)
    krow = lax.broadcasted_iota(jnp.int32, (TQ, TQ), 0)
    for h in range(N_HEADS):
        wq_ref[:, h * TQ:(h + 1) * TQ] = qzt_ref[h]

    def score_chunk(c, carry, causal):
        iqt = jnp.concatenate([iqt_ref[h] for h in range(N_IDX_HEADS)], axis=1)
        w = iwt_ref[...]
        for j in range(SUB):
            st = jnp.dot(ik_ref[c, j * TQ:(j + 1) * TQ, :], iqt, preferred_element_type=jnp.float32)
            s = jnp.zeros((TQ, TQ), jnp.float32)
            for h in range(N_IDX_HEADS):
                s = s + w[h:h + 1, :] * jnp.maximum(st[:, h * TQ:(h + 1) * TQ], 0.0)
            kpos = c * KC + j * TQ + krow
            if causal:
                s = jnp.where(kpos <= qpos, s, -jnp.inf)
            keys_ref[c, j * TQ:(j + 1) * TQ, :] = _to_key(s, kpos)
        _pack_digits(keys_ref, dig_ref, c)
        return carry

    _for_chunks(nkc - 1, functools.partial(score_chunk, causal=False), 0, unroll=4)
    score_chunk(nkc - 1, 0, causal=True)
    thr = _threshold(keys_ref, dig_ref, work_ref, alive_ref, nkc, kk)

    m_ref[...] = jnp.full(m_ref.shape, NEG, jnp.float32)
    acc_ref[...] = jnp.zeros(acc_ref.shape, jnp.float32)

    def bias(c, h):
        return jnp.concatenate([toep_ref[jnp.clip(qi - (c * SUB + j), 0, 2), h] for j in range(SUB)], axis=0)

    def logits(c, lg_ref, near):
        md = jnp.where(keys_ref[c] >= thr, 0.0, NEG)
        add = [md + bias(c, h) for h in range(N_HEADS)] if near else [md] * N_HEADS
        lg_ref[...] = (jnp.dot(k_ref[c], wq_ref[...], preferred_element_type=jnp.float32)
                       + jnp.concatenate(add, axis=1))

    def add_bias(c, lg_ref):
        for h in range(N_HEADS):
            sl = slice(h * TQ, (h + 1) * TQ)
            lg_ref[:, sl] = lg_ref[:, sl] + bias(c, h)

    def softmax_pv(c, lg_ref):
        for pr in range(N_HEADS // 2):
            kv = pr // (GROUP // 2)
            sl = slice(2 * pr * TQ, (2 * pr + 2) * TQ)
            m_old = m_ref[pr]
            m_new = jnp.maximum(m_old, jnp.max(lg_ref[:, sl], axis=0, keepdims=True))
            alpha = jnp.exp2(m_old - m_new)
            p = jnp.exp2(lg_ref[:, sl] - m_new)
            pv = jnp.dot(vt_ref[c, kv * V_ROWS:(kv + 1) * V_ROWS, :], p.astype(MXU_DTYPE),
                         preferred_element_type=jnp.float32)
            acc_ref[pr] = alpha * acc_ref[pr] + pv
            m_ref[pr] = m_new

    n_far2 = (jnp.maximum(qi - 1, 0) // SUB) // 2

    def far_pair(i, carry):
        logits(2 * i + 1, lg1_ref, near=False)
        softmax_pv(2 * i, lg0_ref)
        logits(2 * i + 2, lg0_ref, near=False)
        softmax_pv(2 * i + 1, lg1_ref)
        return carry

    logits(0, lg0_ref, near=False)
    lax.fori_loop(0, n_far2, far_pair, 0)

    c0 = 2 * n_far2
    add_bias(c0, lg0_ref)

    @pl.when(c0 + 1 < nkc)
    def _():
        logits(c0 + 1, lg1_ref, near=True)
        softmax_pv(c0, lg0_ref)

    @pl.when(c0 + 1 >= nkc)
    def _():
        softmax_pv(c0, lg0_ref)

    @pl.when(c0 + 2 < nkc)
    def _():
        logits(c0 + 2, lg0_ref, near=True)
        softmax_pv(c0 + 1, lg1_ref)
        softmax_pv(c0 + 2, lg0_ref)

    @pl.when((c0 + 1 < nkc) & (c0 + 2 >= nkc))
    def _():
        softmax_pv(c0 + 1, lg1_ref)

    for pr in range(N_HEADS // 2):
        acc = acc_ref[pr]
        o_t = acc[:HEAD_DIM] / acc[HEAD_DIM:HEAD_DIM + 1]
        pair_t = jnp.concatenate([o_t[:, :TQ], o_t[:, TQ:]], axis=0)
        sl = slice(pr * LANE, (pr + 1) * LANE)
        o_ref[:, sl] = (pair_t.T * _silu(ga_ref[:, sl])).astype(o_ref.dtype)


def _attend(iqt, iwt, qzt, ga, ik, k, vt, toep_t, kk):
    bsz, lq = ga.shape[:2]
    nq = lq // TQ
    nc = k.shape[1]
    lanes = lambda a: pl.BlockSpec(a.shape[:-1] + (TQ,), lambda b, i: (0,) * (a.ndim - 1) + (b * nq + i,))
    whole = lambda a: pl.BlockSpec((None,) + a.shape[1:], lambda b, i: (b,) + (0,) * (a.ndim - 1))
    rows = pl.BlockSpec((None, TQ, N_HEADS * HEAD_DIM), lambda b, i: (b, i, 0))
    return pl.pallas_call(
        functools.partial(_attend_kernel, kk=kk),
        grid=(bsz, nq),
        in_specs=[lanes(iqt), lanes(iwt), lanes(qzt), rows, whole(ik), whole(k), whole(vt),
                  pl.BlockSpec(toep_t.shape, lambda b, i: (0, 0, 0, 0))],
        out_specs=rows,
        out_shape=jax.ShapeDtypeStruct((bsz, lq, N_HEADS * HEAD_DIM), MXU_DTYPE),
        scratch_shapes=[pltpu.VMEM((nc, KC, TQ), jnp.int32),
                        pltpu.VMEM((N_LEVELS, nc, TQ, TQ), jnp.int32), pltpu.VMEM((nc, TQ, TQ), jnp.int32),
                        pltpu.VMEM((nc, TQ, TQ), jnp.int32),
                        pltpu.VMEM((N_HEADS // 2, 1, 2 * TQ), jnp.float32),
                        pltpu.VMEM((N_HEADS // 2, V_ROWS, 2 * TQ), jnp.float32),
                        pltpu.VMEM((KC, N_HEADS * TQ), jnp.float32), pltpu.VMEM((KC, N_HEADS * TQ), jnp.float32),
                        pltpu.VMEM((LANE, N_HEADS * TQ), MXU_DTYPE)],
        compiler_params=_cparams(2),
        name="attend",
    )(iqt, iwt, qzt, ga, ik, k, vt, toep_t)


def _out_kernel(x_ref, mc_ref, ma_ref, wc_ref, wa_ref, o_ref):
    o_ref[...] = (x_ref[...]
                  + jnp.dot(mc_ref[...], wc_ref[...], preferred_element_type=jnp.float32)
                  + jnp.dot(ma_ref[...], wa_ref[...], preferred_element_type=jnp.float32))


def _out_final_kernel(x_ref, mc_ref, ma_ref, wc_ref, wa_ref, g_ref, o_ref, y_ref):
    x = (x_ref[...]
         + jnp.dot(mc_ref[...], wc_ref[...], preferred_element_type=jnp.float32)
         + jnp.dot(ma_ref[...], wa_ref[...], preferred_element_type=jnp.float32))
    o_ref[...] = x
    ms = jnp.mean(x * x, axis=-1, keepdims=True)
    y_ref[...] = x * lax.rsqrt(ms + EPS) * g_ref[...]


def _out_proj(x, mc, ma, wc, wa, tm, final_g=None):
    r, d = x.shape
    c = mc.shape[1]
    rows = lambda w: pl.BlockSpec((tm, w), lambda i: (i, 0))
    full = lambda a: pl.BlockSpec(a.shape, lambda i: (0, 0))
    if final_g is None:
        return pl.pallas_call(
            _out_kernel, grid=(r // tm,),
            in_specs=[rows(d), rows(c), rows(c), full(wc), full(wa)],
            out_specs=rows(d), out_shape=jax.ShapeDtypeStruct((r, d), jnp.float32),
            compiler_params=_cparams(1), name="out",
        )(x, mc, ma, wc, wa)
    g = final_g.reshape(1, d)
    return pl.pallas_call(
        _out_final_kernel, grid=(r // tm,),
        in_specs=[rows(d), rows(c), rows(c), full(wc), full(wa), full(g)],
        out_specs=[rows(d), rows(d)], out_shape=[jax.ShapeDtypeStruct((r, d), jnp.float32)] * 2,
        compiler_params=_cparams(1), name="out_final",
    )(x, mc, ma, wc, wa, g)


def _sconv_kernel(st_ref, u_ref, gc_ref, w_ref, b_ref, lg_ref, lb_ref, o_ref):
    acc = jnp.zeros(u_ref.shape, jnp.float32)
    for j in range(CONV_W - 1):
        acc = acc + w_ref[j:j + 1, :] * st_ref[j]
    acc = acc + w_ref[CONV_W - 1:CONV_W, :] * u_ref[...]
    y = acc + b_ref[...]
    mu = jnp.mean(y, axis=-1, keepdims=True)
    dev = y - mu
    var = jnp.mean(dev * dev, axis=-1, keepdims=True)
    yn = dev * lax.rsqrt(var + LN_EPS) * lg_ref[...] + lb_ref[...]
    o_ref[...] = (_silu(yn) * _silu(gc_ref[...])).astype(o_ref.dtype)


def _sample_conv(state_t, u, gc, w, b, lg, lb):
    db, c = u.shape
    row = lambda a: a.reshape(1, c)
    full = lambda a: pl.BlockSpec(a.shape, lambda i: (0,) * a.ndim)
    args = (state_t, u, gc, w, row(b), row(lg), row(lb))
    return pl.pallas_call(
        _sconv_kernel, grid=(1,), in_specs=[full(a) for a in args],
        out_specs=pl.BlockSpec((db, c), lambda i: (0, 0)),
        out_shape=jax.ShapeDtypeStruct((db, c), MXU_DTYPE),
        compiler_params=_cparams(1), name="sconv",
    )(*args)


def _dec_score_kernel(pt_ref, iq_ref, iw_ref, ikn_ref, cik_ref, o_ref, buf_ref, sem_ref, *, layer, width):
    slot = _fetch_pages(pt_ref, (cik_ref,), (buf_ref,), (sem_ref,), layer)
    iq = iq_ref[...][:, :IDX_DIM]
    w = iw_ref[...]
    ikt = buf_ref[slot].astype(MXU_DTYPE)
    s = jnp.dot(iq, ikt, preferred_element_type=jnp.float32)
    sc = jnp.sum(w * jnp.maximum(s, 0.0), axis=0, keepdims=True)
    ikn = ikn_ref[...][:, :IDX_DIM].astype(MXU_DTYPE).astype(jnp.float32)
    sn = jnp.sum(iq.astype(jnp.float32) * ikn, axis=-1, keepdims=True)
    scn = jnp.sum(w * jnp.maximum(sn, 0.0), axis=0, keepdims=True)
    lane = lax.broadcasted_iota(jnp.int32, (1, width - sc.shape[1]), 1)
    o_ref[...] = jnp.concatenate([sc, jnp.where(lane == 0, scn, -jnp.inf)], axis=1)


def _dec_scores(page_table, iq3, iw3, tail3, cache_ik, layer, width):
    db, n_pages = page_table.shape
    di, page = cache_ik.shape[2:]
    seq = lambda a: pl.BlockSpec((None,) + a.shape[1:], lambda b, pt: (b, 0, 0))
    return pl.pallas_call(
        functools.partial(_dec_score_kernel, layer=layer, width=width),
        grid_spec=pltpu.PrefetchScalarGridSpec(
            num_scalar_prefetch=1, grid=(db,),
            in_specs=[seq(iq3), seq(iw3), seq(tail3), pl.BlockSpec(memory_space=pl.ANY)],
            out_specs=pl.BlockSpec((None, 1, width), lambda b, pt: (b, 0, 0)),
            scratch_shapes=[pltpu.VMEM((PAGE_SLOTS, di, n_pages * page), cache_ik.dtype),
                            pltpu.SemaphoreType.DMA((PAGE_SLOTS,))]),
        out_shape=jax.ShapeDtypeStruct((db, 1, width), jnp.float32),
        compiler_params=_cparams(1), name="dec_score",
    )(page_table, iq3, iw3, tail3, cache_ik)


def _dec_select_kernel(s_ref, o_ref, keys_ref, dig_ref, work_ref, alive_ref, *, kk):
    nc = keys_ref.shape[0]
    row = lax.broadcasted_iota(jnp.int32, (KC, keys_ref.shape[2]), 0)
    for c in range(nc):
        keys_ref[c] = _to_key(s_ref[c * KC:(c + 1) * KC, :], c * KC + row)
        _pack_digits(keys_ref, dig_ref, c)
    thr = _threshold(keys_ref, dig_ref, work_ref, alive_ref, nc, kk)
    for c in range(nc):
        o_ref[c * KC:(c + 1) * KC, :] = jnp.where(keys_ref[c] >= thr, 0.0, NEG)


def _dec_select(scores_t, kk):
    width, db = scores_t.shape
    nc = width // KC
    spec = pl.BlockSpec((width, db), lambda i: (0, 0))
    return pl.pallas_call(
        functools.partial(_dec_select_kernel, kk=kk),
        grid=(1,), in_specs=[spec], out_specs=spec,
        out_shape=jax.ShapeDtypeStruct((width, db), jnp.float32),
        scratch_shapes=[pltpu.VMEM((nc, KC, db), jnp.int32),
                        pltpu.VMEM((N_LEVELS, nc, TQ, db), jnp.int32), pltpu.VMEM((nc, TQ, db), jnp.int32),
                        pltpu.VMEM((nc, TQ, db), jnp.int32)],
        compiler_params=_cparams(1), name="dec_select",
    )(scores_t)


def _page_copies(pt_ref, cache_ref, buf_ref, sem_ref, layer, seq, slot):
    page = cache_ref.shape[3]
    n_pages = buf_ref.shape[2] // page
    return [pltpu.make_async_copy(cache_ref.at[layer, pt_ref[seq, p]],
                                  buf_ref.at[slot, :, pl.ds(p * page, page)], sem_ref.at[slot])
            for p in range(n_pages)]


def _fetch_pages(pt_ref, caches, bufs, sems, layer):
    b, nb = pl.program_id(0), pl.num_programs(0)
    ahead = PAGE_SLOTS - 1

    def start(seq, slot):
        for cache_ref, buf_ref, sem_ref in zip(caches, bufs, sems):
            for cp in _page_copies(pt_ref, cache_ref, buf_ref, sem_ref, layer, seq, slot):
                cp.start()

    for s in range(ahead):
        @pl.when((b == 0) & (s < nb))
        def _():
            start(s, s)

    @pl.when(b + ahead < nb)
    def _():
        start(b + ahead, (b + ahead) % PAGE_SLOTS)

    slot = b % PAGE_SLOTS
    for cache_ref, buf_ref, sem_ref in zip(caches, bufs, sems):
        for cp in _page_copies(pt_ref, cache_ref, buf_ref, sem_ref, layer, b, slot):
            cp.wait()
    return slot


def _dec_attend_kernel(pt_ref, qz_ref, kvn_ref, madd_ref, bias_ref, ga_ref, ck_ref, cv_ref, o_ref,
                       kbuf_ref, vbuf_ref, ksem_ref, vsem_ref, *, layer):
    slot = _fetch_pages(pt_ref, (ck_ref, cv_ref), (kbuf_ref, vbuf_ref), (ksem_ref, vsem_ref), layer)
    qz = qz_ref[...]
    past = kbuf_ref.shape[2]
    kt = kbuf_ref[slot].astype(MXU_DTYPE)
    vt = vbuf_ref[slot].astype(MXU_DTYPE)
    lg = (jnp.dot(qz, kt, preferred_element_type=jnp.float32)
          + bias_ref[:, :past] + madd_ref[:, :past])
    kvn = kvn_ref[...].astype(MXU_DTYPE).astype(jnp.float32)
    lgn = jnp.sum(qz.astype(jnp.float32) * kvn[:, :LANE], axis=-1, keepdims=True)
    lgn = lgn + bias_ref[:, past:past + 1] + madd_ref[:, past:past + 1]
    m = jnp.maximum(lgn, jnp.max(lg, axis=-1, keepdims=True))
    pn = jnp.exp2(lgn - m)
    pp = jnp.exp2(lg - m)
    den = pn + jnp.sum(pp, axis=-1, keepdims=True)
    o = (pn.astype(MXU_DTYPE).astype(jnp.float32) * kvn[:, LANE:]
         + lax.dot_general(pp.astype(MXU_DTYPE), vt, _NT, preferred_element_type=jnp.float32))
    o = o / den
    swapped = pltpu.roll(o, HEAD_DIM, axis=1)
    lane = lax.broadcasted_iota(jnp.int32, (1, LANE), 1)
    pairs = []
    for j in range(N_HEADS // 2):
        first, second = (o, swapped) if 2 * j < GROUP else (swapped, o)
        pairs.append(jnp.where(lane < HEAD_DIM, first[2 * j:2 * j + 1], second[2 * j + 1:2 * j + 2]))
    o_ref[...] = (jnp.concatenate(pairs, axis=0) * _silu(ga_ref[...])).astype(o_ref.dtype)


def _dec_attend(page_table, qz3, kvn3, madd3, bias, ga3, cache_k, cache_v, layer):
    db, n_pages = page_table.shape
    feat, page = cache_k.shape[2:]
    seq = lambda a: pl.BlockSpec((None,) + a.shape[1:], lambda b, pt: (b, 0, 0))
    hbm = pl.BlockSpec(memory_space=pl.ANY)
    buf = pltpu.VMEM((PAGE_SLOTS, feat, n_pages * page), cache_k.dtype)
    sem = pltpu.SemaphoreType.DMA((PAGE_SLOTS,))
    return pl.pallas_call(
        functools.partial(_dec_attend_kernel, layer=layer),
        grid_spec=pltpu.PrefetchScalarGridSpec(
            num_scalar_prefetch=1, grid=(db,),
            in_specs=[seq(qz3), seq(kvn3), seq(madd3), pl.BlockSpec(bias.shape, lambda b, pt: (0, 0)), seq(ga3),
                      hbm, hbm],
            out_specs=pl.BlockSpec((None, GROUP, LANE), lambda b, pt: (b, 0, 0)),
            scratch_shapes=[buf, buf, sem, sem]),
        out_shape=jax.ShapeDtypeStruct((db, GROUP, LANE), MXU_DTYPE),
        compiler_params=_cparams(1), name="dec_attend",
    )(page_table, qz3, kvn3, madd3, bias, ga3, cache_k, cache_v)


def _round_up(x, m):
    return -(-x // m) * m


def _row_tile(rows):
    for f in (5, 4, 3, 2, 1):
        if rows % (f * TQ) == 0:
            return f * TQ
    return rows


def kernel(x_prompt, x_sample, cache_k, cache_v, cache_idx_k, state_conv, page_table, meta_tokens, rel_bias,
           norm_g, w_in, conv_w, conv_b, conv_ln_g, conv_ln_b, w_out, final_norm_g):
    bsz, seq, d = x_prompt.shape
    depth = w_in.shape[0]
    lp = seq + N_META
    lq = _round_up(lp, TQ)
    lk = _round_up(lp, KC)
    nq, nc = lq // TQ, lk // KC
    kk_p = min(TOPK_MAX, lp // 4)
    db = x_sample.shape[0]
    n_pool, page = cache_k.shape[1:3]
    n_pages = page_table.shape[1]
    past = n_pages * page
    kk_s = min(TOPK_MAX, (past + 1) // 4)
    width_s = _round_up(past + 1, KC)
    feat = N_KV_HEADS * HEAD_DIM
    tm = _row_tile(bsz * lq)

    xp = jnp.concatenate([jnp.broadcast_to(meta_tokens[None].astype(x_prompt.dtype), (bsz, N_META, d)), x_prompt,
                          jnp.zeros((bsz, lq - lp, d), x_prompt.dtype)], axis=1).reshape(bsz * lq, d)
    xs = x_sample.reshape(db, d)
    toep_t = _bias_tiles(rel_bias)
    bias_s = jnp.moveaxis(rel_bias[_t5_bucket(past - jnp.arange(width_s, dtype=jnp.int32))], -1, 0) * LOG2E
    cache_k4 = jnp.transpose(cache_k, (0, 1, 3, 4, 2)).reshape(depth, n_pool, feat, page)
    cache_v4 = jnp.transpose(cache_v, (0, 1, 3, 4, 2)).reshape(depth, n_pool, feat, page)
    cache_ik4 = jnp.swapaxes(cache_idx_k, 2, 3)

    def key_chunks(a):
        return jnp.pad(a, ((0, 0), (0, lk - lq), (0, 0))).astype(MXU_DTYPE).reshape(bsz, nc, KC, LANE)

    def value_chunks_t(a):
        a = jnp.pad(a.reshape(N_KV_HEADS, HEAD_DIM, bsz, lq), ((0, 0), (0, 0), (0, 0), (0, lk - lq)))
        a = jnp.concatenate([a, jnp.ones((N_KV_HEADS, V_ROWS - HEAD_DIM, bsz, lk), a.dtype)], axis=1)
        return jnp.transpose(a.reshape(N_KV_HEADS * V_ROWS, bsz, nc, KC), (1, 2, 0, 3))

    kp, vp, ikp, cp, ksm, vsm, iks, cs = ([] for _ in range(8))
    yp = ys = None
    for l in range(depth):
        w = _prep_w_in(w_in[l])
        wc = w_out[l, :C_CONV].astype(MXU_DTYPE)
        wa = w_out[l, C_CONV:].astype(MXU_DTYPE)
        last = l == depth - 1

        u, gc, kv, ga, tail, qzt, iqzt, vt, iwt = _project(xp, norm_g[l], w, tm, feature_major=True)
        b3 = lambda a: a.reshape(bsz, lq, a.shape[-1])
        u3, kv3, tail3 = b3(u), b3(kv), b3(tail)
        mixc = _conv_branch(u3, b3(gc), conv_w[l], conv_b[l], conv_ln_g[l], conv_ln_b[l])
        lane = jnp.arange(LANE)
        ik = key_chunks(jnp.where(lane < IDX_DIM, tail3, 0.0))
        mixa = _attend(iqzt.reshape(N_IDX_HEADS, LANE, bsz * lq), iwt, qzt.reshape(N_HEADS, LANE, bsz * lq), b3(ga),
                       ik, key_chunks(kv3[..., :feat]), value_chunks_t(vt), toep_t, kk_p)
        res = _out_proj(xp, mixc.reshape(bsz * lq, C_CONV), mixa.reshape(bsz * lq, -1), wc, wa, tm,
                        final_norm_g if last else None)
        xp, yp = res if last else (res, None)
        kp.append(kv3[:, :lp, :feat].reshape(bsz, lp, N_KV_HEADS, HEAD_DIM))
        vp.append(kv3[:, :lp, feat:].reshape(bsz, lp, N_KV_HEADS, HEAD_DIM))
        ikp.append(tail3[:, :lp, :IDX_DIM])
        cp.append(u3[:, lp - (CONV_W - 1):lp])

        u, gc, kv, ga, tail, q, iq = _project(xs, norm_g[l], w, db, feature_major=False)
        mixc = _sample_conv(jnp.transpose(state_conv[l], (1, 0, 2)), u, gc,
                            conv_w[l], conv_b[l], conv_ln_g[l], conv_ln_b[l])
        iw3 = tail[:, IDX_DIM:IDX_DIM + N_IDX_HEADS].reshape(db, N_IDX_HEADS, 1)
        scores = _dec_scores(page_table, iq.reshape(db, N_IDX_HEADS, IDX_DIM), iw3, tail.reshape(db, 1, LANE),
                             cache_ik4, l, width_s)
        madd = _dec_select(scores.reshape(db, width_s).T, kk_s).T
        q4 = q.reshape(db, N_KV_HEADS, GROUP, HEAD_DIM)
        qz3 = jnp.concatenate([jnp.pad(q4[:, 0], ((0, 0), (0, 0), (0, HEAD_DIM))),
                               jnp.pad(q4[:, 1], ((0, 0), (0, 0), (HEAD_DIM, 0)))], axis=1)
        mixa = _dec_attend(page_table, qz3, kv.reshape(db, 1, 2 * feat),
                           madd.reshape(db, 1, width_s), bias_s, ga.reshape(db, N_HEADS // 2, LANE),
                           cache_k4, cache_v4, l)
        res = _out_proj(xs, mixc, mixa.reshape(db, -1), wc, wa, db, final_norm_g if last else None)
        xs, ys = res if last else (res, None)
        ksm.append(kv[:, :feat].reshape(db, 1, N_KV_HEADS, HEAD_DIM))
        vsm.append(kv[:, feat:].reshape(db, 1, N_KV_HEADS, HEAD_DIM))
        iks.append(tail[:, None, :IDX_DIM])
        cs.append(jnp.concatenate([state_conv[l][:, 1:], u[:, None]], axis=1))

    y_prompt = yp.reshape(bsz, lq, d)[:, N_META:lp]
    y_sample = ys.reshape(db, 1, d)
    return (y_prompt, y_sample, jnp.stack(kp), jnp.stack(vp), jnp.stack(ikp), jnp.stack(cp),
            jnp.stack(ksm), jnp.stack(vsm), jnp.stack(iks), jnp.stack(cs))
```
